```python
import math
import jax
import jax.numpy as jnp
from jax import lax
import numpy as np


D_MODEL = 1024
BATCH = 8
SEQ = 2048
DEPTH = 2

CHUNK = 64
Q_BLOCK = 128
MEM_LEN = 256
BRANCH_D = D_MODEL // 2
N_BRANCH = 3
CONV_D = BRANCH_D
CONV_K = 31
DIFF_HEADS = 4
DIFF_DH = D_MODEL // 16
DIFF_VD = 2 * DIFF_DH
DIFF_D = DIFF_HEADS * DIFF_VD
GMLP_D = BRANCH_D
GMLP_GROUPS = 4
GMLP_GD = GMLP_D // GMLP_GROUPS
GMLP_CHUNK = 128
N_REL_BUCKETS = 32
REL_MAX_DIST = 128
XATTN_HEADS = 4
XATTN_DH = D_MODEL // 8
XATTN_D = XATTN_HEADS * XATTN_DH
D_FF = 2816
FFN_K = 3
EPS = 1e-6
NEG_INF = -1e30

COL_A = 2 * CONV_D
COL_QK = DIFF_HEADS * 2 * DIFF_DH
COL_V = DIFF_D
COL_C = 2 * GMLP_D
COL_G = N_BRANCH * D_MODEL
IN_COLS = COL_A + 2 * COL_QK + COL_V + COL_C + COL_G
SPLITS = (COL_A, COL_A + COL_QK, COL_A + 2 * COL_QK, COL_A + 2 * COL_QK + COL_V, COL_A + 2 * COL_QK + COL_V + COL_C)

kernel_name = 'hybrid_chunk_causal_encoder'


def rms_norm(x, g):
    xf = x.astype(jnp.float32)
    y = xf * lax.rsqrt(jnp.mean(xf * xf, axis=-1, keepdims=True) + EPS)
    return (y * g.astype(jnp.float32)).astype(x.dtype)


def layer_norm(x, g, b):
    xf = x.astype(jnp.float32)
    mu = jnp.mean(xf, axis=-1, keepdims=True)
    var = jnp.mean(jnp.square(xf - mu), axis=-1, keepdims=True)
    y = (xf - mu) * lax.rsqrt(var + EPS)
    return (y * g.astype(jnp.float32) + b.astype(jnp.float32)).astype(x.dtype)


def causal_dwconv(x, w, b):
    k, c = w.shape
    y = lax.conv_general_dilated(x, w[:, None, :], window_strides=(1,), padding=((k - 1, 0),),
                                 dimension_numbers=('NWC', 'WIO', 'NWC'), feature_group_count=c)
    return y + b


def t5_bucket(rel):
    half = N_REL_BUCKETS // 2
    max_exact = half // 2
    n = jnp.abs(rel)
    log_ratio = jnp.log(jnp.maximum(n, 1).astype(jnp.float32) / max_exact) / math.log(REL_MAX_DIST / max_exact)
    large = jnp.minimum(max_exact + (log_ratio * (half - max_exact)).astype(jnp.int32), half - 1)
    return jnp.where(rel > 0, half, 0) + jnp.where(n < max_exact, n, large)


def conformer_branch(pa, conv_w, conv_b, ln_g, ln_b):
    glu = pa[..., :CONV_D] * jax.nn.sigmoid(pa[..., CONV_D:])
    c = causal_dwconv(glu, conv_w, conv_b)
    return jax.nn.silu(layer_norm(c, ln_g, ln_b))


def diff_attention_branch(pq, pk, pv, rel_bias, lam_q1, lam_k1, lam_q2, lam_k2, subln_g, lambda_init):
    b, s, _ = pq.shape
    n_blk = s // Q_BLOCK
    q = pq.reshape(b, n_blk, Q_BLOCK, DIFF_HEADS, 2, DIFF_DH).transpose(1, 0, 2, 3, 4, 5)
    k = pk.reshape(b, s, DIFF_HEADS, 2, DIFF_DH)
    v = pv.reshape(b, s, DIFF_HEADS, DIFF_VD)
    f32 = jnp.float32
    lam = (jnp.exp(jnp.sum(lam_q1.astype(f32) * lam_k1.astype(f32)))
           - jnp.exp(jnp.sum(lam_q2.astype(f32) * lam_k2.astype(f32))) + lambda_init)
    scale = DIFF_DH ** -0.5
    k_pos = jnp.arange(s)
    table = rel_bias.astype(f32)

    def one_block(args):
        q_blk, blk = args
        q_pos = blk * Q_BLOCK + jnp.arange(Q_BLOCK)
        rel = k_pos[None, :] - q_pos[:, None]
        bias = table[t5_bucket(rel)].transpose(2, 0, 1)
        visible = (k_pos[None, :] // CHUNK) <= (q_pos[:, None] // CHUNK)
        logits = jnp.einsum('bqhmd,bkhmd->bhmqk', q_blk, k).astype(f32) * scale + bias[None, :, None]
        logits = jnp.where(visible, logits, NEG_INF)
        p = jax.nn.softmax(logits, axis=-1)
        attn = p[:, :, 0] - lam * p[:, :, 1]
        return jnp.einsum('bhqk,bkhe->bqhe', attn.astype(v.dtype), v)

    o = lax.map(one_block, (q, jnp.arange(n_blk)))
    o = o.transpose(1, 0, 2, 3, 4).reshape(b, s, DIFF_HEADS, DIFF_VD)
    o = rms_norm(o, subln_g) * (1.0 - lambda_init)
    return o.reshape(b, s, DIFF_D)


def gmlp_branch(pc, ln_g, ln_b, w_s, b_s):
    b, s, _ = pc.shape
    z = jax.nn.gelu(pc)
    u, v = z[..., :GMLP_D], z[..., GMLP_D:]
    v = layer_norm(v, ln_g, ln_b).reshape(b, s // GMLP_CHUNK, GMLP_CHUNK, GMLP_GROUPS, GMLP_GD)
    causal = jnp.tril(jnp.ones((GMLP_CHUNK, GMLP_CHUNK), dtype=w_s.dtype))
    sv = jnp.einsum('gts,bnsgc->bntgc', w_s * causal, v) + b_s.T[:, :, None]
    return u * sv.reshape(b, s, GMLP_D)


def hybrid_mixer(h, w_in, gate_b, conv_w, conv_b, conv_ln_g, conv_ln_b, rel_bias,
                 lam_q1, lam_k1, lam_q2, lam_k2, subln_g, gmlp_ln_g, gmlp_ln_b, w_s, b_s,
                 w_br, w_out, lambda_init):
    b, s, _ = h.shape
    proj = h @ w_in
    pa, pq, pk, pv, pc, pg = jnp.split(proj, SPLITS, axis=-1)
    ya = conformer_branch(pa, conv_w, conv_b, conv_ln_g, conv_ln_b)
    yb = diff_attention_branch(pq, pk, pv, rel_bias, lam_q1, lam_k1, lam_q2, lam_k2, subln_g, lambda_init)
    yc = gmlp_branch(pc, gmlp_ln_g, gmlp_ln_b, w_s, b_s)
    gates = jax.nn.sigmoid(pg.reshape(b, s, N_BRANCH, D_MODEL) + gate_b)
    merged = (gates[:, :, 0] * (ya @ w_br[0]) + gates[:, :, 1] * (yb @ w_br[1])
              + gates[:, :, 2] * (yc @ w_br[2]))
    return merged @ w_out


def mem_cross_attention(h, mem, norm_mem_g, w_xq, w_xkv, w_xo):
    b, s, _ = h.shape
    m = mem.shape[1]
    q = (h @ w_xq).reshape(b, s, XATTN_HEADS, XATTN_DH)
    kv = (rms_norm(mem, norm_mem_g) @ w_xkv).reshape(b, m, 2, XATTN_HEADS, XATTN_DH)
    k, v = kv[:, :, 0], kv[:, :, 1]
    logits = jnp.einsum('bqhd,bkhd->bhqk', q, k).astype(jnp.float32) * (XATTN_DH ** -0.5)
    p = jax.nn.softmax(logits, axis=-1)
    o = jnp.einsum('bhqk,bkhd->bqhd', p.astype(v.dtype), v).reshape(b, s, XATTN_D)
    return o @ w_xo


def conv_ffn(h, w_up, ffn_conv_w, ffn_conv_b, w_down):
    up = causal_dwconv(h @ w_up, ffn_conv_w, ffn_conv_b)
    return (jax.nn.silu(up[..., :D_FF]) * up[..., D_FF:]) @ w_down


def setup_inputs(seed: int = 0) -> dict:
    key = jax.random.key(seed)
    ks = jax.random.split(key, 40)
    f32 = jnp.float32

    def nrm(k, shape, scale):
        return jax.random.normal(k, shape, f32) * scale

    def gain(k, shape):
        return 1.0 + nrm(k, shape, 0.05)

    return {
        'x': nrm(ks[0], (BATCH, SEQ, D_MODEL), 1.0),
        'mem': nrm(ks[1], (BATCH, MEM_LEN, D_MODEL), 1.0),
        'rel_bias': nrm(ks[2], (N_REL_BUCKETS, DIFF_HEADS), 0.5),
        'norm_mix_g': gain(ks[3], (DEPTH, D_MODEL)),
        'w_in': nrm(ks[4], (DEPTH, D_MODEL, IN_COLS), D_MODEL ** -0.5),
        'gate_b': nrm(ks[5], (DEPTH, N_BRANCH, D_MODEL), 0.1),
        'conv_w': nrm(ks[6], (DEPTH, CONV_K, CONV_D), CONV_K ** -0.5),
        'conv_b': nrm(ks[7], (DEPTH, CONV_D), 0.02),
        'conv_ln_g': gain(ks[8], (DEPTH, CONV_D)),
        'conv_ln_b': nrm(ks[9], (DEPTH, CONV_D), 0.02),
        'lam_q1': nrm(ks[10], (DEPTH, DIFF_DH), 0.1),
        'lam_k1': nrm(ks[11], (DEPTH, DIFF_DH), 0.1),
        'lam_q2': nrm(ks[12], (DEPTH, DIFF_DH), 0.1),
        'lam_k2': nrm(ks[13], (DEPTH, DIFF_DH), 0.1),
        'subln_g': gain(ks[14], (DEPTH, DIFF_VD)),
        'gmlp_ln_g': gain(ks[15], (DEPTH, GMLP_D)),
        'gmlp_ln_b': nrm(ks[16], (DEPTH, GMLP_D), 0.02),
        'w_s': nrm(ks[17], (DEPTH, GMLP_GROUPS, GMLP_CHUNK, GMLP_CHUNK), GMLP_CHUNK ** -0.5),
        'b_s': gain(ks[18], (DEPTH, GMLP_GROUPS, GMLP_CHUNK)),
        'w_br': nrm(ks[19], (DEPTH, N_BRANCH, BRANCH_D, D_MODEL), BRANCH_D ** -0.5),
        'w_out': nrm(ks[20], (DEPTH, D_MODEL, D_MODEL), D_MODEL ** -0.5),
        'norm_xattn_g': gain(ks[21], (DEPTH, D_MODEL)),
        'norm_mem_g': gain(ks[22], (DEPTH, D_MODEL)),
        'w_xq': nrm(ks[23], (DEPTH, D_MODEL, XATTN_D), D_MODEL ** -0.5),
        'w_xkv': nrm(ks[24], (DEPTH, D_MODEL, 2 * XATTN_D), D_MODEL ** -0.5),
        'w_xo': nrm(ks[25], (DEPTH, XATTN_D, D_MODEL), XATTN_D ** -0.5),
        'norm_ffn_g': gain(ks[26], (DEPTH, D_MODEL)),
        'w_up': nrm(ks[27], (DEPTH, D_MODEL, 2 * D_FF), D_MODEL ** -0.5),
        'ffn_conv_w': nrm(ks[28], (DEPTH, FFN_K, 2 * D_FF), FFN_K ** -0.5),
        'ffn_conv_b': nrm(ks[29], (DEPTH, 2 * D_FF), 0.02),
        'w_down': nrm(ks[30], (DEPTH, D_FF, D_MODEL), D_FF ** -0.5),
        'norm_final_g': gain(ks[31], (D_MODEL,)),
    }


def reference(x, mem, rel_bias, norm_mix_g, w_in, gate_b, conv_w, conv_b, conv_ln_g, conv_ln_b,
              lam_q1, lam_k1, lam_q2, lam_k2, subln_g, gmlp_ln_g, gmlp_ln_b, w_s, b_s, w_br, w_out,
              norm_xattn_g, norm_mem_g, w_xq, w_xkv, w_xo, norm_ffn_g, w_up, ffn_conv_w, ffn_conv_b,
              w_down, norm_final_g):
    for l in range(DEPTH):
        lambda_init = 0.8 - 0.6 * math.exp(-0.3 * l)
        x = x + hybrid_mixer(rms_norm(x, norm_mix_g[l]), w_in[l], gate_b[l], conv_w[l], conv_b[l],
                             conv_ln_g[l], conv_ln_b[l], rel_bias, lam_q1[l], lam_k1[l], lam_q2[l], lam_k2[l],
                             subln_g[l], gmlp_ln_g[l], gmlp_ln_b[l], w_s[l], b_s[l], w_br[l], w_out[l],
                             lambda_init)
        x = x + mem_cross_attention(rms_norm(x, norm_xattn_g[l]), mem, norm_mem_g[l], w_xq[l], w_xkv[l], w_xo[l])
        x = x + conv_ffn(rms_norm(x, norm_ffn_g[l]), w_up[l], ffn_conv_w[l], ffn_conv_b[l], w_down[l])
    return rms_norm(x, norm_final_g)
```

```python
import functools
import math

import jax
import jax.numpy as jnp
from jax import lax
from jax.experimental import pallas as pl
from jax.experimental.pallas import tpu as pltpu

F32 = jnp.float32
BF16 = jnp.bfloat16

D_MODEL = 1024
DEPTH = 2
CHUNK = 64
BRANCH_D = D_MODEL // 2
N_BRANCH = 3
CONV_K = 31
DIFF_HEADS = 4
DIFF_DH = D_MODEL // 16
DIFF_VD = 2 * DIFF_DH
GMLP_GROUPS = 4
GMLP_GD = BRANCH_D // GMLP_GROUPS
GMLP_CHUNK = 128
N_REL_BUCKETS = 32
REL_MAX_DIST = 128
XATTN_HEADS = 4
XATTN_DH = D_MODEL // 8
XATTN_D = XATTN_HEADS * XATTN_DH
D_FF = 2816
FFN_K = 3
EPS = 1e-6
NEG_INF = -1e30

COL_A = 2 * BRANCH_D
COL_QK = DIFF_HEADS * 2 * DIFF_DH
COL_V = DIFF_HEADS * DIFF_VD
COL_C = 2 * BRANCH_D
COL_G = N_BRANCH * D_MODEL
IN_SPLIT_WIDTHS = (COL_A, COL_QK, COL_QK, COL_V, COL_C, COL_G)

LANES = 128
MXU_WIDTH = 256
BF16_SUBLANES = 16
VMEM_LIMIT_BYTES = 56 * 1024 * 1024

ROW_TILE = 512
PROJ_COL_CHUNK = 512
ATTN_TILE = 256
CONV_HALO = 32
CONV_ROW_CHUNK = 64
FFN_HALO = BF16_SUBLANES
FFN_COL_CHUNK = MXU_WIDTH

assert CONV_HALO >= CONV_K - 1 and FFN_HALO >= FFN_K - 1
assert D_FF % FFN_COL_CHUNK == 0
assert ATTN_TILE % CHUNK == 0 and ATTN_TILE > REL_MAX_DIST


def _params(*semantics):
    return pltpu.CompilerParams(dimension_semantics=semantics, vmem_limit_bytes=VMEM_LIMIT_BYTES)


def _resident(shape):
    zeros = (0,) * len(shape)
    return pl.BlockSpec(shape, lambda *_: zeros, pipeline_mode=pl.Buffered(1))


def _rms_norm(x, g):
    return x * lax.rsqrt(jnp.mean(x * x, axis=-1, keepdims=True) + EPS) * g


def _layer_norm(x, g, b):
    mu = jnp.mean(x, axis=-1, keepdims=True)
    xc = x - mu
    var = jnp.mean(xc * xc, axis=-1, keepdims=True)
    return xc * lax.rsqrt(var + EPS) * g + b


def _dot(a, b):
    return jnp.dot(a, b, preferred_element_type=F32)


def _dot_nt(a, b):
    return lax.dot_general(a, b, (((1,), (1,)), ((), ())), preferred_element_type=F32)


def _norm_proj_kernel(x_ref, g_ref, w_ref, *out_refs, widths):
    xn = _rms_norm(x_ref[...], g_ref[...]).astype(BF16)
    col = 0
    for o_ref, width in zip(out_refs, widths):
        for c in range(0, width, PROJ_COL_CHUNK):
            o_ref[:, c:c + PROJ_COL_CHUNK] = _dot(xn, w_ref[:, col + c:col + c + PROJ_COL_CHUNK]).astype(BF16)
        col += width


def _norm_proj(x, g, w, widths, row_tile=ROW_TILE):
    rows, d = x.shape
    assert w.shape == (d, sum(widths)) and all(wd % PROJ_COL_CHUNK == 0 for wd in widths)
    return pl.pallas_call(
        functools.partial(_norm_proj_kernel, widths=widths),
        grid=(rows // row_tile,),
        in_specs=[pl.BlockSpec((row_tile, d), lambda i: (i, 0)), _resident((1, d)), _resident(w.shape)],
        out_specs=[pl.BlockSpec((row_tile, wd), lambda i: (i, 0)) for wd in widths],
        out_shape=[jax.ShapeDtypeStruct((rows, wd), BF16) for wd in widths],
        compiler_params=_params("parallel"),
        name="norm_proj",
    )(x, g.reshape(1, d), w)


def _conformer_kernel(pa_ref, halo_ref, cw_ref, cb_ref, lng_ref, lnb_ref, o_ref, glu_scr):
    rows = o_ref.shape[1]

    def glu(p):
        return p[:, :BRANCH_D].astype(F32) * jax.nn.sigmoid(p[:, BRANCH_D:].astype(F32))

    halo = glu(halo_ref[0])
    glu_scr[:CONV_HALO, :] = jnp.where(pl.program_id(1) > 0, halo, 0.0)
    glu_scr[CONV_HALO:, :] = glu(pa_ref[0])

    first_tap = CONV_HALO - (CONV_K - 1)
    for r in range(0, rows, CONV_ROW_CHUNK):
        acc = jnp.broadcast_to(cb_ref[...], (CONV_ROW_CHUNK, BRANCH_D))
        for k in range(CONV_K):
            start = r + first_tap + k
            acc = acc + cw_ref[k:k + 1, :] * glu_scr[start:start + CONV_ROW_CHUNK, :]
        o_ref[0, r:r + CONV_ROW_CHUNK, :] = jax.nn.silu(_layer_norm(acc, lng_ref[...], lnb_ref[...])).astype(BF16)


def _conformer_branch(pa, conv_w, conv_b, ln_g, ln_b, row_tile=ROW_TILE):
    b, s, _ = pa.shape
    halo_blocks = row_tile // CONV_HALO
    vec = lambda a: a.reshape(1, BRANCH_D)
    return pl.pallas_call(
        _conformer_kernel,
        grid=(b, s // row_tile),
        in_specs=[
            pl.BlockSpec((1, row_tile, COL_A), lambda bi, i: (bi, i, 0)),
            pl.BlockSpec((1, CONV_HALO, COL_A), lambda bi, i: (bi, jnp.maximum(i * halo_blocks - 1, 0), 0)),
            _resident((CONV_K, BRANCH_D)), _resident((1, BRANCH_D)), _resident((1, BRANCH_D)),
            _resident((1, BRANCH_D)),
        ],
        out_specs=pl.BlockSpec((1, row_tile, BRANCH_D), lambda bi, i: (bi, i, 0)),
        out_shape=jax.ShapeDtypeStruct((b, s, BRANCH_D), BF16),
        scratch_shapes=[pltpu.VMEM((CONV_HALO + row_tile, BRANCH_D), F32)],
        compiler_params=_params("parallel", "parallel"),
        name="conformer_branch",
    )(pa, pa, conv_w, vec(conv_b), vec(ln_g), vec(ln_b))


def _rel_bucket(rel):
    half = N_REL_BUCKETS // 2
    max_exact = half // 2
    n = jnp.abs(rel)
    log_ratio = jnp.log(jnp.maximum(n, 1).astype(F32) / max_exact) / math.log(REL_MAX_DIST / max_exact)
    large = jnp.minimum(max_exact + (log_ratio * (half - max_exact)).astype(jnp.int32), half - 1)
    return jnp.where(rel > 0, half, 0) + jnp.where(n < max_exact, n, large)


FAR_BUCKET = N_REL_BUCKETS // 2 - 1
MASKED_BUCKET = -1


def _near_buckets():
    r = jnp.arange(ATTN_TILE)[:, None]
    c = jnp.arange(ATTN_TILE)[None, :]
    visible = (c // CHUNK) <= (r // CHUNK)
    diag = jnp.where(visible, _rel_bucket(c - r), MASKED_BUCKET)
    sub = _rel_bucket(c - r - ATTN_TILE)
    return jnp.stack([diag, sub]).astype(jnp.int32)


def _bias_tiles_kernel(table_ref, bucket_ref, o_ref):
    h = pl.program_id(0)
    bucket = bucket_ref[...]
    far = table_ref[FAR_BUCKET, h]
    out = jnp.zeros(bucket.shape, F32)
    for b in range(N_REL_BUCKETS):
        out = jnp.where(bucket == b, table_ref[b, h] - far, out)
    o_ref[0] = jnp.where(bucket == MASKED_BUCKET, NEG_INF, out)


def _bias_tiles(rel_bias):
    buckets = _near_buckets()
    return pl.pallas_call(
        _bias_tiles_kernel,
        grid=(DIFF_HEADS,),
        in_specs=[pl.BlockSpec(memory_space=pltpu.SMEM), _resident(buckets.shape)],
        out_specs=pl.BlockSpec((1,) + buckets.shape, lambda h: (h, 0, 0, 0)),
        out_shape=jax.ShapeDtypeStruct((DIFF_HEADS,) + buckets.shape, F32),
        compiler_params=_params("arbitrary"),
        name="rel_bias_tiles",
    )(rel_bias.astype(F32), buckets)


def _diff_attn_kernel(q_ref, k_ref, v_ref, bias_ref, lq1_ref, lk1_ref, lq2_ref, lk2_ref, g_ref, o_ref,
                      q2_scr, m_scr, l_scr, acc_scr, *, lambda_init):
    t = ATTN_TILE
    i = pl.program_id(2)

    q = q_ref[0] * (DIFF_DH ** -0.5)
    lane = lax.broadcasted_iota(jnp.int32, q.shape, 1)
    q2_scr[:t, :] = jnp.where(lane < DIFF_DH, q, 0).astype(BF16)
    q2_scr[t:, :] = jnp.where(lane >= DIFF_DH, q, 0).astype(BF16)
    m_scr[...] = jnp.full(m_scr.shape, NEG_INF, F32)
    l_scr[...] = jnp.zeros(l_scr.shape, F32)
    acc_scr[...] = jnp.zeros(acc_scr.shape, F32)

    def key_tile(j, bias):
        start = pl.multiple_of(j * t, t)
        s = _dot_nt(q2_scr[...], k_ref[0, pl.ds(start, t), :])
        if bias is not None:
            s = s + jnp.concatenate([bias, bias], axis=0)
        m_old = m_scr[...]
        m_new = jnp.maximum(m_old, jnp.max(s, axis=-1, keepdims=True))
        alpha = jnp.exp(m_old - m_new)
        p = jnp.exp(s - m_new)
        l_scr[...] = alpha * l_scr[...] + jnp.sum(p, axis=-1, keepdims=True)
        acc_scr[...] = alpha * acc_scr[...] + _dot(p.astype(BF16), v_ref[0, pl.ds(start, t), :])
        m_scr[...] = m_new

    def far_body(j, carry):
        key_tile(j, None)
        return carry

    lax.fori_loop(0, jnp.maximum(i - 1, 0), far_body, 0)

    @pl.when(i > 0)
    def _():
        key_tile(i - 1, bias_ref[0, 1])

    key_tile(i, bias_ref[0, 0])

    lam = (jnp.exp(jnp.sum(lq1_ref[...] * lk1_ref[...])) - jnp.exp(jnp.sum(lq2_ref[...] * lk2_ref[...]))
           + lambda_init)
    o = acc_scr[:t, :] / l_scr[:t, :] - lam * (acc_scr[t:, :] / l_scr[t:, :])
    o_ref[0] = (_rms_norm(o, g_ref[...]) * (1.0 - lambda_init)).astype(BF16)


def _diff_attention(pq, pk, pv, bias_tiles, lam_q1, lam_k1, lam_q2, lam_k2, subln_g, lambda_init):
    b, s, _ = pq.shape
    t = ATTN_TILE
    vec = lambda a: a.reshape(1, -1).astype(F32)
    head_rows = pl.BlockSpec((1, s, DIFF_VD), lambda bi, h, i: (bi, 0, h))
    lam_spec = _resident((1, DIFF_DH))
    return pl.pallas_call(
        functools.partial(_diff_attn_kernel, lambda_init=lambda_init),
        grid=(b, DIFF_HEADS, s // t),
        in_specs=[
            pl.BlockSpec((1, t, DIFF_VD), lambda bi, h, i: (bi, i, h)),
            head_rows, head_rows,
            pl.BlockSpec((1, 2, t, t), lambda bi, h, i: (h, 0, 0, 0)),
            lam_spec, lam_spec, lam_spec, lam_spec, _resident((1, DIFF_VD)),
        ],
        out_specs=pl.BlockSpec((1, t, DIFF_VD), lambda bi, h, i: (bi, i, h)),
        out_shape=jax.ShapeDtypeStruct((b, s, DIFF_HEADS * DIFF_VD), BF16),
        scratch_shapes=[
            pltpu.VMEM((2 * t, DIFF_VD), BF16),
            pltpu.VMEM((2 * t, 1), F32),
            pltpu.VMEM((2 * t, 1), F32),
            pltpu.VMEM((2 * t, DIFF_VD), F32),
        ],
        compiler_params=_params("parallel", "parallel", "parallel"),
        name="diff_attention",
    )(pq, pk, pv, bias_tiles, vec(lam_q1), vec(lam_k1), vec(lam_q2), vec(lam_k2), vec(subln_g))


def _gmlp_kernel(pc_ref, lng_ref, lnb_ref, ws_ref, bs_ref, o_ref):
    rows = o_ref.shape[0]
    z = jax.nn.gelu(pc_ref[...].astype(F32))
    u = z[:, :BRANCH_D]
    v = _layer_norm(z[:, BRANCH_D:], lng_ref[...], lnb_ref[...]).astype(BF16)
    t_idx = lax.broadcasted_iota(jnp.int32, (GMLP_CHUNK, GMLP_CHUNK), 0)
    s_idx = lax.broadcasted_iota(jnp.int32, (GMLP_CHUNK, GMLP_CHUNK), 1)
    for g in range(GMLP_GROUPS):
        w = jnp.where(s_idx <= t_idx, ws_ref[g], 0.0).astype(BF16)
        cols = slice(g * GMLP_GD, (g + 1) * GMLP_GD)
        for r in range(0, rows, GMLP_CHUNK):
            sv = _dot(w, v[r:r + GMLP_CHUNK, cols]) + bs_ref[g]
            o_ref[r:r + GMLP_CHUNK, cols] = (u[r:r + GMLP_CHUNK, cols] * sv).astype(BF16)


def _gmlp_branch(pc, ln_g, ln_b, w_s, b_s, row_tile=ROW_TILE):
    rows = pc.shape[0]
    vec = lambda a: a.reshape(1, BRANCH_D)
    return pl.pallas_call(
        _gmlp_kernel,
        grid=(rows // row_tile,),
        in_specs=[
            pl.BlockSpec((row_tile, COL_C), lambda i: (i, 0)),
            _resident((1, BRANCH_D)), _resident((1, BRANCH_D)),
            _resident(w_s.shape), _resident((GMLP_GROUPS, GMLP_CHUNK, 1)),
        ],
        out_specs=pl.BlockSpec((row_tile, BRANCH_D), lambda i: (i, 0)),
        out_shape=jax.ShapeDtypeStruct((rows, BRANCH_D), BF16),
        compiler_params=_params("parallel"),
        name="gmlp_branch",
    )(pc, vec(ln_g), vec(ln_b), w_s, b_s.reshape(GMLP_GROUPS, GMLP_CHUNK, 1))


def _merge_kernel(x_ref, ya_ref, yb_ref, yc_ref, pg_ref, gb_ref, wbr_ref, wout_ref, o_ref):
    merged = None
    for n, y_ref in enumerate((ya_ref, yb_ref, yc_ref)):
        logits = pg_ref[:, n * D_MODEL:(n + 1) * D_MODEL].astype(F32) + gb_ref[n:n + 1, :]
        term = jax.nn.sigmoid(logits) * _dot(y_ref[...], wbr_ref[n])
        merged = term if merged is None else merged + term
    o_ref[...] = x_ref[...] + _dot(merged.astype(BF16), wout_ref[...])


def _merge(x, ya, yb, yc, pg, gate_b, w_br, w_out, row_tile=ROW_TILE):
    rows = x.shape[0]
    row_spec = lambda width: pl.BlockSpec((row_tile, width), lambda i: (i, 0))
    return pl.pallas_call(
        _merge_kernel,
        grid=(rows // row_tile,),
        in_specs=[row_spec(D_MODEL), row_spec(BRANCH_D), row_spec(BRANCH_D), row_spec(BRANCH_D), row_spec(COL_G),
                  _resident(gate_b.shape), _resident(w_br.shape), _resident(w_out.shape)],
        out_specs=row_spec(D_MODEL),
        out_shape=jax.ShapeDtypeStruct(x.shape, F32),
        compiler_params=_params("parallel"),
        name="branch_merge",
    )(x, ya, yb, yc, pg, gate_b, w_br, w_out)


def _xattn_kernel(x_ref, g_ref, wq_ref, kv_ref, wo_ref, o_ref):
    x = x_ref[0]
    q = _dot(_rms_norm(x, g_ref[...]).astype(BF16), wq_ref[...]).astype(BF16)
    heads = []
    for h in range(XATTN_HEADS):
        cols = slice(h * XATTN_DH, (h + 1) * XATTN_DH)
        k = kv_ref[0, :, cols]
        v = kv_ref[0, :, XATTN_D + h * XATTN_DH:XATTN_D + (h + 1) * XATTN_DH]
        s = _dot_nt(q[:, cols], k) * (XATTN_DH ** -0.5)
        p = jnp.exp(s - jnp.max(s, axis=-1, keepdims=True))
        p = p / jnp.sum(p, axis=-1, keepdims=True)
        heads.append(_dot(p.astype(BF16), v).astype(BF16))
    o_ref[0] = x + _dot(jnp.concatenate(heads, axis=-1), wo_ref[...])


def _mem_cross_attention(x, kv, norm_g, w_xq, w_xo, row_tile=ROW_TILE):
    b, s, d = x.shape
    m = kv.shape[1]
    return pl.pallas_call(
        _xattn_kernel,
        grid=(b, s // row_tile),
        in_specs=[
            pl.BlockSpec((1, row_tile, d), lambda bi, i: (bi, i, 0)),
            _resident((1, d)), _resident(w_xq.shape),
            pl.BlockSpec((1, m, 2 * XATTN_D), lambda bi, i: (bi, 0, 0)),
            _resident(w_xo.shape),
        ],
        out_specs=pl.BlockSpec((1, row_tile, d), lambda bi, i: (bi, i, 0)),
        out_shape=jax.ShapeDtypeStruct(x.shape, F32),
        compiler_params=_params("parallel", "parallel"),
        name="mem_cross_attention",
    )(x, norm_g.reshape(1, d), w_xq, kv, w_xo)


def _ffn_kernel(x_ref, halo_ref, g_ref, wup_ref, cw_ref, cb_ref, wdn_ref, gf_ref, o_ref,
                h_scr, ua_scr, ub_scr, acc_scr, *, final_norm):
    rows = o_ref.shape[1]
    g = g_ref[...]
    halo = _rms_norm(halo_ref[0], g)
    h_scr[:FFN_HALO, :] = jnp.where(pl.program_id(1) > 0, halo, 0.0).astype(BF16)
    h_scr[FFN_HALO:, :] = _rms_norm(x_ref[0], g).astype(BF16)

    first_tap = FFN_HALO - (FFN_K - 1)

    def conv(up_scr, col):
        out = cb_ref[:, col:col + FFN_COL_CHUNK]
        for k in range(FFN_K):
            out = out + cw_ref[k:k + 1, col:col + FFN_COL_CHUNK] * up_scr[first_tap + k:first_tap + k + rows, :]
        return out

    for c in range(0, D_FF, FFN_COL_CHUNK):
        h = h_scr[...]
        ua_scr[...] = _dot(h, wup_ref[:, c:c + FFN_COL_CHUNK])
        ub_scr[...] = _dot(h, wup_ref[:, D_FF + c:D_FF + c + FFN_COL_CHUNK])
        act = (jax.nn.silu(conv(ua_scr, c)) * conv(ub_scr, D_FF + c)).astype(BF16)
        contrib = _dot(act, wdn_ref[c:c + FFN_COL_CHUNK, :])
        if c == 0:
            acc_scr[...] = contrib
        else:
            acc_scr[...] += contrib

    out = x_ref[0] + acc_scr[...]
    if final_norm:
        out = _rms_norm(out, gf_ref[...])
    o_ref[0] = out


def _conv_ffn(x, norm_g, w_up, conv_w, conv_b, w_down, final_g, final_norm, row_tile=ROW_TILE):
    b, s, d = x.shape
    halo_blocks = row_tile // FFN_HALO
    return pl.pallas_call(
        functools.partial(_ffn_kernel, final_norm=final_norm),
        grid=(b, s // row_tile),
        in_specs=[
            pl.BlockSpec((1, row_tile, d), lambda bi, i: (bi, i, 0)),
            pl.BlockSpec((1, FFN_HALO, d), lambda bi, i: (bi, jnp.maximum(i * halo_blocks - 1, 0), 0)),
            _resident((1, d)), _resident(w_up.shape), _resident(conv_w.shape), _resident((1, 2 * D_FF)),
            _resident(w_down.shape), _resident((1, d)),
        ],
        out_specs=pl.BlockSpec((1, row_tile, d), lambda bi, i: (bi, i, 0)),
        out_shape=jax.ShapeDtypeStruct(x.shape, F32),
        scratch_shapes=[
            pltpu.VMEM((FFN_HALO + row_tile, d), BF16),
            pltpu.VMEM((FFN_HALO + row_tile, FFN_COL_CHUNK), F32),
            pltpu.VMEM((FFN_HALO + row_tile, FFN_COL_CHUNK), F32),
            pltpu.VMEM((row_tile, d), F32),
        ],
        compiler_params=_params("parallel", "parallel"),
        name="conv_ffn",
    )(x, x, norm_g.reshape(1, d), w_up, conv_w, conv_b.reshape(1, 2 * D_FF), w_down, final_g.reshape(1, d))


def kernel(x, mem, rel_bias, norm_mix_g, w_in, gate_b, conv_w, conv_b, conv_ln_g, conv_ln_b, lam_q1, lam_k1, lam_q2, lam_k2, subln_g, gmlp_ln_g, gmlp_ln_b, w_s, b_s, w_br, w_out, norm_xattn_g, norm_mem_g, w_xq, w_xkv, w_xo, norm_ffn_g, w_up, ffn_conv_w, ffn_conv_b, w_down, norm_final_g):
    b, s, d = x.shape
    m = mem.shape[1]
    rows = b * s
    bias_tiles = _bias_tiles(rel_bias)
    for l in range(DEPTH):
        lambda_init = 0.8 - 0.6 * math.exp(-0.3 * l)
        pa, pq, pk, pv, pc, pg = _norm_proj(x.reshape(rows, d), norm_mix_g[l], w_in[l].astype(BF16),
                                            IN_SPLIT_WIDTHS)
        ya = _conformer_branch(pa.reshape(b, s, COL_A), conv_w[l], conv_b[l], conv_ln_g[l], conv_ln_b[l])
        yb = _diff_attention(pq.reshape(b, s, COL_QK), pk.reshape(b, s, COL_QK), pv.reshape(b, s, COL_V),
                             bias_tiles, lam_q1[l], lam_k1[l], lam_q2[l], lam_k2[l], subln_g[l], lambda_init)
        yc = _gmlp_branch(pc, gmlp_ln_g[l], gmlp_ln_b[l], w_s[l], b_s[l])
        x = _merge(x.reshape(rows, d), ya.reshape(rows, BRANCH_D), yb.reshape(rows, BRANCH_D), yc, pg,
                   gate_b[l], w_br[l].astype(BF16), w_out[l].astype(BF16)).reshape(b, s, d)
        (kv,) = _norm_proj(mem.reshape(b * m, d), norm_mem_g[l], w_xkv[l].astype(BF16), (2 * XATTN_D,))
        x = _mem_cross_attention(x, kv.reshape(b, m, 2 * XATTN_D), norm_xattn_g[l], w_xq[l].astype(BF16),
                                 w_xo[l].astype(BF16))
        x = _conv_ffn(x, norm_ffn_g[l], w_up[l].astype(BF16), ffn_conv_w[l], ffn_conv_b[l],
                      w_down[l].astype(BF16), norm_final_g, final_norm=(l == DEPTH - 1))
    return x
```

```python
import functools
import math

import jax
import jax.numpy as jnp
from jax import lax
from jax.experimental import pallas as pl
from jax.experimental.pallas import tpu as pltpu

F32 = jnp.float32
BF16 = jnp.bfloat16

D_MODEL = 1024
DEPTH = 2
CHUNK = 64
BRANCH_D = D_MODEL // 2
N_BRANCH = 3
CONV_K = 31
DIFF_HEADS = 4
DIFF_DH = D_MODEL // 16
DIFF_VD = 2 * DIFF_DH
GMLP_GROUPS = 4
GMLP_GD = BRANCH_D // GMLP_GROUPS
GMLP_CHUNK = 128
N_REL_BUCKETS = 32
REL_MAX_DIST = 128
XATTN_HEADS = 4
XATTN_DH = D_MODEL // 8
XATTN_D = XATTN_HEADS * XATTN_DH
D_FF = 2816
FFN_K = 3
EPS = 1e-6
NEG_INF = -1e30

COL_A = 2 * BRANCH_D
COL_QK = DIFF_HEADS * 2 * DIFF_DH
COL_V = DIFF_HEADS * DIFF_VD
COL_C = 2 * BRANCH_D
COL_G = N_BRANCH * D_MODEL
IN_SPLIT_WIDTHS = (COL_A, COL_QK, COL_QK, COL_V, COL_C, COL_G)

LANES = 128
MXU_WIDTH = 256
BF16_SUBLANES = 16
VMEM_LIMIT_BYTES = 56 * 1024 * 1024

ROW_TILE = 512
PROJ_COL_CHUNK = 512
ATTN_TILE = 256
CONV_HALO = 32
CONV_ROW_CHUNK = 64
FFN_HALO = BF16_SUBLANES
FFN_COL_CHUNK = MXU_WIDTH

assert CONV_HALO >= CONV_K - 1 and FFN_HALO >= FFN_K - 1
assert D_FF % FFN_COL_CHUNK == 0
assert ATTN_TILE % CHUNK == 0 and ATTN_TILE > REL_MAX_DIST


def _params(*semantics):
    return pltpu.CompilerParams(dimension_semantics=semantics, vmem_limit_bytes=VMEM_LIMIT_BYTES)


def _resident(shape):
    zeros = (0,) * len(shape)
    return pl.BlockSpec(shape, lambda *_: zeros, pipeline_mode=pl.Buffered(1))


def _rms_norm(x, g):
    return x * lax.rsqrt(jnp.mean(x * x, axis=-1, keepdims=True) + EPS) * g


def _layer_norm(x, g, b):
    mu = jnp.mean(x, axis=-1, keepdims=True)
    xc = x - mu
    var = jnp.mean(xc * xc, axis=-1, keepdims=True)
    return xc * lax.rsqrt(var + EPS) * g + b


def _dot(a, b):
    return jnp.dot(a, b, preferred_element_type=F32)


def _dot_nt(a, b):
    return lax.dot_general(a, b, (((1,), (1,)), ((), ())), preferred_element_type=F32)


def _norm_proj_kernel(x_ref, g_ref, w_ref, *out_refs, widths):
    xn = _rms_norm(x_ref[...], g_ref[...]).astype(BF16)
    col = 0
    for o_ref, width in zip(out_refs, widths):
        for c in range(0, width, PROJ_COL_CHUNK):
            o_ref[:, c:c + PROJ_COL_CHUNK] = _dot(xn, w_ref[:, col + c:col + c + PROJ_COL_CHUNK]).astype(BF16)
        col += width


def _norm_proj(x, g, w, widths, row_tile=ROW_TILE):
    rows, d = x.shape
    assert w.shape == (d, sum(widths)) and all(wd % PROJ_COL_CHUNK == 0 for wd in widths)
    return pl.pallas_call(
        functools.partial(_norm_proj_kernel, widths=widths),
        grid=(rows // row_tile,),
        in_specs=[pl.BlockSpec((row_tile, d), lambda i: (i, 0)), _resident((1, d)), _resident(w.shape)],
        out_specs=[pl.BlockSpec((row_tile, wd), lambda i: (i, 0)) for wd in widths],
        out_shape=[jax.ShapeDtypeStruct((rows, wd), BF16) for wd in widths],
        compiler_params=_params("parallel"),
        name="norm_proj",
    )(x, g.reshape(1, d), w)


def _conformer_kernel(pa_ref, halo_ref, cw_ref, cb_ref, lng_ref, lnb_ref, o_ref, glu_scr):
    rows = o_ref.shape[1]

    def glu(p):
        return p[:, :BRANCH_D].astype(F32) * jax.nn.sigmoid(p[:, BRANCH_D:].astype(F32))

    halo = glu(halo_ref[0])
    glu_scr[:CONV_HALO, :] = jnp.where(pl.program_id(1) > 0, halo, 0.0)
    glu_scr[CONV_HALO:, :] = glu(pa_ref[0])

    first_tap = CONV_HALO - (CONV_K - 1)
    for r in range(0, rows, CONV_ROW_CHUNK):
        acc = jnp.broadcast_to(cb_ref[...], (CONV_ROW_CHUNK, BRANCH_D))
        for k in range(CONV_K):
            start = r + first_tap + k
            acc = acc + cw_ref[k:k + 1, :] * glu_scr[start:start + CONV_ROW_CHUNK, :]
        o_ref[0, r:r + CONV_ROW_CHUNK, :] = jax.nn.silu(_layer_norm(acc, lng_ref[...], lnb_ref[...])).astype(BF16)


def _conformer_branch(pa, conv_w, conv_b, ln_g, ln_b, row_tile=ROW_TILE):
    b, s, _ = pa.shape
    halo_blocks = row_tile // CONV_HALO
    vec = lambda a: a.reshape(1, BRANCH_D)
    return pl.pallas_call(
        _conformer_kernel,
        grid=(b, s // row_tile),
        in_specs=[
            pl.BlockSpec((1, row_tile, COL_A), lambda bi, i: (bi, i, 0)),
            pl.BlockSpec((1, CONV_HALO, COL_A), lambda bi, i: (bi, jnp.maximum(i * halo_blocks - 1, 0), 0)),
            _resident((CONV_K, BRANCH_D)), _resident((1, BRANCH_D)), _resident((1, BRANCH_D)),
            _resident((1, BRANCH_D)),
        ],
        out_specs=pl.BlockSpec((1, row_tile, BRANCH_D), lambda bi, i: (bi, i, 0)),
        out_shape=jax.ShapeDtypeStruct((b, s, BRANCH_D), BF16),
        scratch_shapes=[pltpu.VMEM((CONV_HALO + row_tile, BRANCH_D), F32)],
        compiler_params=_params("parallel", "parallel"),
        name="conformer_branch",
    )(pa, pa, conv_w, vec(conv_b), vec(ln_g), vec(ln_b))


def _rel_bucket(rel):
    half = N_REL_BUCKETS // 2
    max_exact = half // 2
    n = jnp.abs(rel)
    log_ratio = jnp.log(jnp.maximum(n, 1).astype(F32) / max_exact) / math.log(REL_MAX_DIST / max_exact)
    large = jnp.minimum(max_exact + (log_ratio * (half - max_exact)).astype(jnp.int32), half - 1)
    return jnp.where(rel > 0, half, 0) + jnp.where(n < max_exact, n, large)


FAR_BUCKET = N_REL_BUCKETS // 2 - 1
MASKED_BUCKET = -1


def _near_buckets():
    r = jnp.arange(ATTN_TILE)[:, None]
    c = jnp.arange(ATTN_TILE)[None, :]
    visible = (c // CHUNK) <= (r // CHUNK)
    diag = jnp.where(visible, _rel_bucket(c - r), MASKED_BUCKET)
    sub = _rel_bucket(c - r - ATTN_TILE)
    return jnp.stack([diag, sub]).astype(jnp.int32)


def _bias_tiles_kernel(table_ref, bucket_ref, o_ref):
    h = pl.program_id(0)
    bucket = bucket_ref[...]
    far = table_ref[FAR_BUCKET, h]
    out = jnp.zeros(bucket.shape, F32)
    for b in range(N_REL_BUCKETS):
        out = jnp.where(bucket == b, table_ref[b, h] - far, out)
    o_ref[0] = jnp.where(bucket == MASKED_BUCKET, NEG_INF, out)


def _bias_tiles(rel_bias):
    buckets = _near_buckets()
    return pl.pallas_call(
        _bias_tiles_kernel,
        grid=(DIFF_HEADS,),
        in_specs=[pl.BlockSpec(memory_space=pltpu.SMEM), _resident(buckets.shape)],
        out_specs=pl.BlockSpec((1,) + buckets.shape, lambda h: (h, 0, 0, 0)),
        out_shape=jax.ShapeDtypeStruct((DIFF_HEADS,) + buckets.shape, F32),
        compiler_params=_params("arbitrary"),
        name="rel_bias_tiles",
    )(rel_bias.astype(F32), buckets)


def _diff_attn_kernel(q_ref, k_ref, v_ref, bias_ref, lq1_ref, lk1_ref, lq2_ref, lk2_ref, g_ref, o_ref,
                      q2_scr, s_scr, m_scr, l_scr, acc_scr, *, lambda_init):
    t = ATTN_TILE
    i = pl.program_id(1)
    heads = range(DIFF_HEADS)

    def head_cols(h):
        return slice(h * DIFF_VD, (h + 1) * DIFF_VD)

    q = q_ref[0] * (DIFF_DH ** -0.5)
    lane = lax.broadcasted_iota(jnp.int32, (t, DIFF_VD), 1)
    for h in heads:
        qh = q[:, head_cols(h)]
        q2_scr[h, :t, :] = jnp.where(lane < DIFF_DH, qh, 0).astype(BF16)
        q2_scr[h, t:, :] = jnp.where(lane >= DIFF_DH, qh, 0).astype(BF16)
    m_scr[...] = jnp.full(m_scr.shape, NEG_INF, F32)
    l_scr[...] = jnp.zeros(l_scr.shape, F32)
    acc_scr[...] = jnp.zeros(acc_scr.shape, F32)

    def logits_tile(j, near):
        start = pl.multiple_of(j * t, t)
        for h in heads:
            s = _dot_nt(q2_scr[h], k_ref[0, pl.ds(start, t), head_cols(h)])
            if near is not None:
                bias = bias_ref[h, near]
                s = s + jnp.concatenate([bias, bias], axis=0)
            s_scr[j, h] = s
            m_scr[h] = jnp.maximum(m_scr[h], jnp.maximum(s[:, :LANES], s[:, LANES:]))

    def far_body(j, carry):
        logits_tile(j, None)
        return carry

    lax.fori_loop(0, jnp.maximum(i - 1, 0), far_body, 0)

    @pl.when(i > 0)
    def _():
        logits_tile(i - 1, 1)

    logits_tile(i, 0)

    for h in heads:
        m_scr[h] = jnp.broadcast_to(jnp.max(m_scr[h], axis=-1, keepdims=True), (2 * t, LANES))

    def pv_body(j, carry):
        start = pl.multiple_of(j * t, t)
        for h in heads:
            m = m_scr[h]
            p_lo = jnp.exp(s_scr[j, h, :, :LANES] - m)
            p_hi = jnp.exp(s_scr[j, h, :, LANES:] - m)
            l_scr[h] += p_lo + p_hi
            p = jnp.concatenate([p_lo, p_hi], axis=1).astype(BF16)
            acc_scr[h] += _dot(p, v_ref[0, pl.ds(start, t), head_cols(h)])
        return carry

    lax.fori_loop(0, i + 1, pv_body, 0)

    lam = (jnp.exp(jnp.sum(lq1_ref[...] * lk1_ref[...])) - jnp.exp(jnp.sum(lq2_ref[...] * lk2_ref[...]))
           + lambda_init)
    for h in heads:
        l = jnp.sum(l_scr[h], axis=-1, keepdims=True)
        o = acc_scr[h, :t, :] / l[:t] - lam * (acc_scr[h, t:, :] / l[t:])
        o_ref[0, :, head_cols(h)] = (_rms_norm(o, g_ref[...]) * (1.0 - lambda_init)).astype(BF16)


def _diff_attention(pq, pk, pv, bias_tiles, lam_q1, lam_k1, lam_q2, lam_k2, subln_g, lambda_init):
    b, s, width = pq.shape
    t = ATTN_TILE
    vec = lambda a: a.reshape(1, -1).astype(F32)
    all_keys = pl.BlockSpec((1, s, width), lambda bi, i: (bi, 0, 0))
    lam_spec = _resident((1, DIFF_DH))
    return pl.pallas_call(
        functools.partial(_diff_attn_kernel, lambda_init=lambda_init),
        grid=(b, s // t),
        in_specs=[
            pl.BlockSpec((1, t, width), lambda bi, i: (bi, i, 0)),
            all_keys, all_keys, _resident(bias_tiles.shape),
            lam_spec, lam_spec, lam_spec, lam_spec, _resident((1, DIFF_VD)),
        ],
        out_specs=pl.BlockSpec((1, t, width), lambda bi, i: (bi, i, 0)),
        out_shape=jax.ShapeDtypeStruct((b, s, width), BF16),
        scratch_shapes=[
            pltpu.VMEM((DIFF_HEADS, 2 * t, DIFF_VD), BF16),
            pltpu.VMEM((s // t, DIFF_HEADS, 2 * t, t), F32),
            pltpu.VMEM((DIFF_HEADS, 2 * t, LANES), F32),
            pltpu.VMEM((DIFF_HEADS, 2 * t, LANES), F32),
            pltpu.VMEM((DIFF_HEADS, 2 * t, DIFF_VD), F32),
        ],
        compiler_params=_params("parallel", "parallel"),
        name="diff_attention",
    )(pq, pk, pv, bias_tiles, vec(lam_q1), vec(lam_k1), vec(lam_q2), vec(lam_k2), vec(subln_g))


def _gmlp_kernel(pc_ref, lng_ref, lnb_ref, ws_ref, bs_ref, o_ref):
    rows = o_ref.shape[0]
    z = jax.nn.gelu(pc_ref[...].astype(F32))
    u = z[:, :BRANCH_D]
    v = _layer_norm(z[:, BRANCH_D:], lng_ref[...], lnb_ref[...]).astype(BF16)
    t_idx = lax.broadcasted_iota(jnp.int32, (GMLP_CHUNK, GMLP_CHUNK), 0)
    s_idx = lax.broadcasted_iota(jnp.int32, (GMLP_CHUNK, GMLP_CHUNK), 1)
    for g in range(GMLP_GROUPS):
        w = jnp.where(s_idx <= t_idx, ws_ref[g], 0.0).astype(BF16)
        cols = slice(g * GMLP_GD, (g + 1) * GMLP_GD)
        for r in range(0, rows, GMLP_CHUNK):
            sv = _dot(w, v[r:r + GMLP_CHUNK, cols]) + bs_ref[g]
            o_ref[r:r + GMLP_CHUNK, cols] = (u[r:r + GMLP_CHUNK, cols] * sv).astype(BF16)


def _gmlp_branch(pc, ln_g, ln_b, w_s, b_s, row_tile=ROW_TILE):
    rows = pc.shape[0]
    vec = lambda a: a.reshape(1, BRANCH_D)
    return pl.pallas_call(
        _gmlp_kernel,
        grid=(rows // row_tile,),
        in_specs=[
            pl.BlockSpec((row_tile, COL_C), lambda i: (i, 0)),
            _resident((1, BRANCH_D)), _resident((1, BRANCH_D)),
            _resident(w_s.shape), _resident((GMLP_GROUPS, GMLP_CHUNK, 1)),
        ],
        out_specs=pl.BlockSpec((row_tile, BRANCH_D), lambda i: (i, 0)),
        out_shape=jax.ShapeDtypeStruct((rows, BRANCH_D), BF16),
        compiler_params=_params("parallel"),
        name="gmlp_branch",
    )(pc, vec(ln_g), vec(ln_b), w_s, b_s.reshape(GMLP_GROUPS, GMLP_CHUNK, 1))


def _merge_kernel(x_ref, ya_ref, yb_ref, yc_ref, pg_ref, gb_ref, wbr_ref, wout_ref, o_ref):
    merged = None
    for n, y_ref in enumerate((ya_ref, yb_ref, yc_ref)):
        logits = pg_ref[:, n * D_MODEL:(n + 1) * D_MODEL].astype(F32) + gb_ref[n:n + 1, :]
        term = jax.nn.sigmoid(logits) * _dot(y_ref[...], wbr_ref[n])
        merged = term if merged is None else merged + term
    o_ref[...] = x_ref[...] + _dot(merged.astype(BF16), wout_ref[...])


def _merge(x, ya, yb, yc, pg, gate_b, w_br, w_out, row_tile=ROW_TILE):
    rows = x.shape[0]
    row_spec = lambda width: pl.BlockSpec((row_tile, width), lambda i: (i, 0))
    return pl.pallas_call(
        _merge_kernel,
        grid=(rows // row_tile,),
        in_specs=[row_spec(D_MODEL), row_spec(BRANCH_D), row_spec(BRANCH_D), row_spec(BRANCH_D), row_spec(COL_G),
                  _resident(gate_b.shape), _resident(w_br.shape), _resident(w_out.shape)],
        out_specs=row_spec(D_MODEL),
        out_shape=jax.ShapeDtypeStruct(x.shape, F32),
        compiler_params=_params("parallel"),
        name="branch_merge",
    )(x, ya, yb, yc, pg, gate_b, w_br, w_out)


def _xattn_kernel(x_ref, g_ref, wq_ref, kv_ref, wo_ref, o_ref):
    x = x_ref[0]
    q = _dot(_rms_norm(x, g_ref[...]).astype(BF16), wq_ref[...]).astype(BF16)
    heads = []
    for h in range(XATTN_HEADS):
        cols = slice(h * XATTN_DH, (h + 1) * XATTN_DH)
        k = kv_ref[0, :, cols]
        v = kv_ref[0, :, XATTN_D + h * XATTN_DH:XATTN_D + (h + 1) * XATTN_DH]
        s = _dot_nt(q[:, cols], k) * (XATTN_DH ** -0.5)
        p = jnp.exp(s - jnp.max(s, axis=-1, keepdims=True))
        p = p / jnp.sum(p, axis=-1, keepdims=True)
        heads.append(_dot(p.astype(BF16), v).astype(BF16))
    o_ref[0] = x + _dot(jnp.concatenate(heads, axis=-1), wo_ref[...])


def _mem_cross_attention(x, kv, norm_g, w_xq, w_xo, row_tile=ROW_TILE):
    b, s, d = x.shape
    m = kv.shape[1]
    return pl.pallas_call(
        _xattn_kernel,
        grid=(b, s // row_tile),
        in_specs=[
            pl.BlockSpec((1, row_tile, d), lambda bi, i: (bi, i, 0)),
            _resident((1, d)), _resident(w_xq.shape),
            pl.BlockSpec((1, m, 2 * XATTN_D), lambda bi, i: (bi, 0, 0)),
            _resident(w_xo.shape),
        ],
        out_specs=pl.BlockSpec((1, row_tile, d), lambda bi, i: (bi, i, 0)),
        out_shape=jax.ShapeDtypeStruct(x.shape, F32),
        compiler_params=_params("parallel", "parallel"),
        name="mem_cross_attention",
    )(x, norm_g.reshape(1, d), w_xq, kv, w_xo)


def _ffn_kernel(x_ref, halo_ref, g_ref, wup_ref, cw_ref, cb_ref, wdn_ref, gf_ref, o_ref,
                h_scr, ua_scr, ub_scr, acc_scr, *, final_norm):
    rows = o_ref.shape[1]
    g = g_ref[...]
    halo = _rms_norm(halo_ref[0], g)
    h_scr[:FFN_HALO, :] = jnp.where(pl.program_id(1) > 0, halo, 0.0).astype(BF16)
    h_scr[FFN_HALO:, :] = _rms_norm(x_ref[0], g).astype(BF16)

    first_tap = FFN_HALO - (FFN_K - 1)

    def conv(up_scr, col):
        out = cb_ref[:, col:col + FFN_COL_CHUNK]
        for k in range(FFN_K):
            out = out + cw_ref[k:k + 1, col:col + FFN_COL_CHUNK] * up_scr[first_tap + k:first_tap + k + rows, :]
        return out

    for c in range(0, D_FF, FFN_COL_CHUNK):
        h = h_scr[...]
        ua_scr[...] = _dot(h, wup_ref[:, c:c + FFN_COL_CHUNK])
        ub_scr[...] = _dot(h, wup_ref[:, D_FF + c:D_FF + c + FFN_COL_CHUNK])
        act = (jax.nn.silu(conv(ua_scr, c)) * conv(ub_scr, D_FF + c)).astype(BF16)
        contrib = _dot(act, wdn_ref[c:c + FFN_COL_CHUNK, :])
        if c == 0:
            acc_scr[...] = contrib
        else:
            acc_scr[...] += contrib

    out = x_ref[0] + acc_scr[...]
    if final_norm:
        out = _rms_norm(out, gf_ref[...])
    o_ref[0] = out


def _conv_ffn(x, norm_g, w_up, conv_w, conv_b, w_down, final_g, final_norm, row_tile=ROW_TILE):
    b, s, d = x.shape
    halo_blocks = row_tile // FFN_HALO
    return pl.pallas_call(
        functools.partial(_ffn_kernel, final_norm=final_norm),
        grid=(b, s // row_tile),
        in_specs=[
            pl.BlockSpec((1, row_tile, d), lambda bi, i: (bi, i, 0)),
            pl.BlockSpec((1, FFN_HALO, d), lambda bi, i: (bi, jnp.maximum(i * halo_blocks - 1, 0), 0)),
            _resident((1, d)), _resident(w_up.shape), _resident(conv_w.shape), _resident((1, 2 * D_FF)),
            _resident(w_down.shape), _resident((1, d)),
        ],
        out_specs=pl.BlockSpec((1, row_tile, d), lambda bi, i: (bi, i, 0)),
        out_shape=jax.ShapeDtypeStruct(x.shape, F32),
        scratch_shapes=[
            pltpu.VMEM((FFN_HALO + row_tile, d), BF16),
            pltpu.VMEM((FFN_HALO + row_tile, FFN_COL_CHUNK), F32),
            pltpu.VMEM((FFN_HALO + row_tile, FFN_COL_CHUNK), F32),
            pltpu.VMEM((row_tile, d), F32),
        ],
        compiler_params=_params("parallel", "parallel"),
        name="conv_ffn",
    )(x, x, norm_g.reshape(1, d), w_up, conv_w, conv_b.reshape(1, 2 * D_FF), w_down, final_g.reshape(1, d))


def kernel(x, mem, rel_bias, norm_mix_g, w_in, gate_b, conv_w, conv_b, conv_ln_g, conv_ln_b, lam_q1, lam_k1, lam_q2, lam_k2, subln_g, gmlp_ln_g, gmlp_ln_b, w_s, b_s, w_br, w_out, norm_xattn_g, norm_mem_g, w_xq, w_xkv, w_xo, norm_ffn_g, w_up, ffn_conv_w, ffn_conv_b, w_down, norm_final_g):
    b, s, d = x.shape
    m = mem.shape[1]
    rows = b * s
    bias_tiles = _bias_tiles(rel_bias)
    for l in range(DEPTH):
        lambda_init = 0.8 - 0.6 * math.exp(-0.3 * l)
        pa, pq, pk, pv, pc, pg = _norm_proj(x.reshape(rows, d), norm_mix_g[l], w_in[l].astype(BF16),
                                            IN_SPLIT_WIDTHS)
        ya = _conformer_branch(pa.reshape(b, s, COL_A), conv_w[l], conv_b[l], conv_ln_g[l], conv_ln_b[l])
        yb = _diff_attention(pq.reshape(b, s, COL_QK), pk.reshape(b, s, COL_QK), pv.reshape(b, s, COL_V),
                             bias_tiles, lam_q1[l], lam_k1[l], lam_q2[l], lam_k2[l], subln_g[l], lambda_init)
        yc = _gmlp_branch(pc, gmlp_ln_g[l], gmlp_ln_b[l], w_s[l], b_s[l])
        x = _merge(x.reshape(rows, d), ya.reshape(rows, BRANCH_D), yb.reshape(rows, BRANCH_D), yc, pg,
                   gate_b[l], w_br[l].astype(BF16), w_out[l].astype(BF16)).reshape(b, s, d)
        (kv,) = _norm_proj(mem.reshape(b * m, d), norm_mem_g[l], w_xkv[l].astype(BF16), (2 * XATTN_D,))
        x = _mem_cross_attention(x, kv.reshape(b, m, 2 * XATTN_D), norm_xattn_g[l], w_xq[l].astype(BF16),
                                 w_xo[l].astype(BF16))
        x = _conv_ffn(x, norm_ffn_g[l], w_up[l].astype(BF16), ffn_conv_w[l], ffn_conv_b[l],
                      w_down[l].astype(BF16), norm_final_g, final_norm=(l == DEPTH - 1))
    return x
```

```python
import functools
import math

import jax
import jax.numpy as jnp
from jax import lax
from jax.experimental import pallas as pl
from jax.experimental.pallas import tpu as pltpu

F32 = jnp.float32
BF16 = jnp.bfloat16

D_MODEL = 1024
DEPTH = 2
CHUNK = 64
BRANCH_D = D_MODEL // 2
N_BRANCH = 3
CONV_K = 31
DIFF_HEADS = 4
DIFF_DH = D_MODEL // 16
DIFF_VD = 2 * DIFF_DH
GMLP_GROUPS = 4
GMLP_GD = BRANCH_D // GMLP_GROUPS
GMLP_CHUNK = 128
N_REL_BUCKETS = 32
REL_MAX_DIST = 128
XATTN_HEADS = 4
XATTN_DH = D_MODEL // 8
XATTN_D = XATTN_HEADS * XATTN_DH
D_FF = 2816
FFN_K = 3
EPS = 1e-6
NEG_INF = -1e30

COL_A = 2 * BRANCH_D
COL_QK = DIFF_HEADS * 2 * DIFF_DH
COL_V = DIFF_HEADS * DIFF_VD
COL_C = 2 * BRANCH_D
COL_G = N_BRANCH * D_MODEL
IN_SPLIT_WIDTHS = (COL_A, COL_QK, COL_QK, COL_V, COL_C, COL_G)
LOG2_E = math.log2(math.e)
IN_SPLIT_SCALES = (1.0, DIFF_DH ** -0.5 * LOG2_E, 1.0, 1.0, 1.0, 1.0)

LANES = 128
MXU_WIDTH = 256
F32_SUBLANES = 8
BF16_SUBLANES = 16
VMEM_LIMIT_BYTES = 56 * 1024 * 1024

ROW_TILE = 512
PROJ_COL_CHUNK = 512
ATTN_TILE = 256
CONV_HALO = 32
CONV_ROW_CHUNK = 64
FFN_HALO = BF16_SUBLANES
FFN_COL_CHUNK = MXU_WIDTH

assert CONV_HALO >= CONV_K - 1 and FFN_HALO >= FFN_K - 1
assert D_FF % FFN_COL_CHUNK == 0
assert ATTN_TILE % CHUNK == 0 and ATTN_TILE > REL_MAX_DIST


def _params(*semantics):
    return pltpu.CompilerParams(dimension_semantics=semantics, vmem_limit_bytes=VMEM_LIMIT_BYTES)


def _resident(shape):
    zeros = (0,) * len(shape)
    return pl.BlockSpec(shape, lambda *_: zeros, pipeline_mode=pl.Buffered(1))


def _rms_norm(x, g):
    return x * lax.rsqrt(jnp.mean(x * x, axis=-1, keepdims=True) + EPS) * g


def _layer_norm(x, g, b):
    mu = jnp.mean(x, axis=-1, keepdims=True)
    xc = x - mu
    var = jnp.mean(xc * xc, axis=-1, keepdims=True)
    return xc * lax.rsqrt(var + EPS) * g + b


def _dot(a, b):
    return jnp.dot(a, b, preferred_element_type=F32)


def _dot_nt(a, b):
    return lax.dot_general(a, b, (((1,), (1,)), ((), ())), preferred_element_type=F32)


def _norm_proj_kernel(x_ref, g_ref, w_ref, *out_refs, widths, scales):
    xn = _rms_norm(x_ref[...], g_ref[...]).astype(BF16)
    col = 0
    for o_ref, width, scale in zip(out_refs, widths, scales):
        for c in range(0, width, PROJ_COL_CHUNK):
            out = _dot(xn, w_ref[:, col + c:col + c + PROJ_COL_CHUNK])
            if scale != 1.0:
                out = out * scale
            o_ref[:, c:c + PROJ_COL_CHUNK] = out.astype(BF16)
        col += width


def _norm_proj(x, g, w, widths, scales=None, row_tile=ROW_TILE):
    rows, d = x.shape
    scales = scales or (1.0,) * len(widths)
    assert w.shape == (d, sum(widths)) and all(wd % PROJ_COL_CHUNK == 0 for wd in widths)
    return pl.pallas_call(
        functools.partial(_norm_proj_kernel, widths=widths, scales=scales),
        grid=(rows // row_tile,),
        in_specs=[pl.BlockSpec((row_tile, d), lambda i: (i, 0)), _resident((1, d)), _resident(w.shape)],
        out_specs=[pl.BlockSpec((row_tile, wd), lambda i: (i, 0)) for wd in widths],
        out_shape=[jax.ShapeDtypeStruct((rows, wd), BF16) for wd in widths],
        compiler_params=_params("parallel"),
        name="norm_proj",
    )(x, g.reshape(1, d), w)


def _conformer_kernel(pa_ref, halo_ref, cw_ref, cb_ref, lng_ref, lnb_ref, o_ref, xs_scr):
    rows = o_ref.shape[1]

    def glu(p):
        return p[:, :BRANCH_D].astype(F32) * jax.nn.sigmoid(p[:, BRANCH_D:].astype(F32))

    halo = glu(halo_ref[0])
    xs_scr[0, :CONV_HALO, :] = jnp.where(pl.program_id(1) > 0, halo, 0.0)
    xs_scr[0, CONV_HALO:, :] = glu(pa_ref[0])
    shifted_rows = CONV_HALO + rows - F32_SUBLANES
    for b in range(1, F32_SUBLANES):
        xs_scr[b, :shifted_rows, :] = xs_scr[0, b:b + shifted_rows, :]

    first_tap = CONV_HALO - (CONV_K - 1)
    for r in range(0, rows, CONV_ROW_CHUNK):
        acc = jnp.broadcast_to(cb_ref[...], (CONV_ROW_CHUNK, BRANCH_D))
        for k in range(CONV_K):
            b = (first_tap + k) % F32_SUBLANES
            start = r + first_tap + k - b
            acc = acc + cw_ref[k:k + 1, :] * xs_scr[b, start:start + CONV_ROW_CHUNK, :]
        o_ref[0, r:r + CONV_ROW_CHUNK, :] = jax.nn.silu(_layer_norm(acc, lng_ref[...], lnb_ref[...])).astype(BF16)


def _conformer_branch(pa, conv_w, conv_b, ln_g, ln_b, row_tile=ROW_TILE):
    b, s, _ = pa.shape
    halo_blocks = row_tile // CONV_HALO
    vec = lambda a: a.reshape(1, BRANCH_D)
    return pl.pallas_call(
        _conformer_kernel,
        grid=(b, s // row_tile),
        in_specs=[
            pl.BlockSpec((1, row_tile, COL_A), lambda bi, i: (bi, i, 0)),
            pl.BlockSpec((1, CONV_HALO, COL_A), lambda bi, i: (bi, jnp.maximum(i * halo_blocks - 1, 0), 0)),
            _resident((CONV_K, BRANCH_D)), _resident((1, BRANCH_D)), _resident((1, BRANCH_D)),
            _resident((1, BRANCH_D)),
        ],
        out_specs=pl.BlockSpec((1, row_tile, BRANCH_D), lambda bi, i: (bi, i, 0)),
        out_shape=jax.ShapeDtypeStruct((b, s, BRANCH_D), BF16),
        scratch_shapes=[pltpu.VMEM((F32_SUBLANES, CONV_HALO + row_tile, BRANCH_D), F32)],
        compiler_params=_params("parallel", "parallel"),
        name="conformer_branch",
    )(pa, pa, conv_w, vec(conv_b), vec(ln_g), vec(ln_b))


def _rel_bucket(rel):
    half = N_REL_BUCKETS // 2
    max_exact = half // 2
    n = jnp.abs(rel)
    log_ratio = jnp.log(jnp.maximum(n, 1).astype(F32) / max_exact) / math.log(REL_MAX_DIST / max_exact)
    large = jnp.minimum(max_exact + (log_ratio * (half - max_exact)).astype(jnp.int32), half - 1)
    return jnp.where(rel > 0, half, 0) + jnp.where(n < max_exact, n, large)


FAR_BUCKET = N_REL_BUCKETS // 2 - 1
MASKED_BUCKET = -1


def _near_buckets():
    r = jnp.arange(ATTN_TILE)[:, None]
    c = jnp.arange(ATTN_TILE)[None, :]
    visible = (c // CHUNK) <= (r // CHUNK)
    diag = jnp.where(visible, _rel_bucket(c - r), MASKED_BUCKET)
    sub = _rel_bucket(c - r - ATTN_TILE)
    return jnp.stack([diag, sub]).astype(jnp.int32)


def _bias_tiles_kernel(table_ref, bucket_ref, o_ref):
    h = pl.program_id(0)
    bucket = bucket_ref[...]
    far = table_ref[FAR_BUCKET, h]
    out = jnp.zeros(bucket.shape, F32)
    for b in range(N_REL_BUCKETS):
        out = jnp.where(bucket == b, (table_ref[b, h] - far) * LOG2_E, out)
    o_ref[0] = jnp.where(bucket == MASKED_BUCKET, NEG_INF, out)


def _bias_tiles(rel_bias):
    buckets = _near_buckets()
    return pl.pallas_call(
        _bias_tiles_kernel,
        grid=(DIFF_HEADS,),
        in_specs=[pl.BlockSpec(memory_space=pltpu.SMEM), _resident(buckets.shape)],
        out_specs=pl.BlockSpec((1,) + buckets.shape, lambda h: (h, 0, 0, 0)),
        out_shape=jax.ShapeDtypeStruct((DIFF_HEADS,) + buckets.shape, F32),
        compiler_params=_params("arbitrary"),
        name="rel_bias_tiles",
    )(rel_bias.astype(F32), buckets)


def _diff_attn_kernel(q_ref, k_ref, v_ref, bias_ref, lq1_ref, lk1_ref, lq2_ref, lk2_ref, g_ref, o_ref,
                      q2_scr, s_scr, m_scr, l_scr, acc_scr, *, lambda_init):
    t = ATTN_TILE
    i = pl.program_id(1)
    heads = range(DIFF_HEADS)

    def head_cols(h):
        return slice(h * DIFF_VD, (h + 1) * DIFF_VD)

    q = q_ref[0]
    lane = lax.broadcasted_iota(jnp.int32, (t, DIFF_VD), 1)
    for h in heads:
        qh = q[:, head_cols(h)]
        q2_scr[h, :t, :] = jnp.where(lane < DIFF_DH, qh, 0).astype(BF16)
        q2_scr[h, t:, :] = jnp.where(lane >= DIFF_DH, qh, 0).astype(BF16)
    m_scr[...] = jnp.full(m_scr.shape, NEG_INF, F32)
    l_scr[...] = jnp.zeros(l_scr.shape, F32)
    acc_scr[...] = jnp.zeros(acc_scr.shape, F32)

    def logits_tile(j, near):
        start = pl.multiple_of(j * t, t)
        for h in heads:
            s = _dot_nt(q2_scr[h], k_ref[0, pl.ds(start, t), head_cols(h)])
            if near is not None:
                bias = bias_ref[h, near]
                s = s + jnp.concatenate([bias, bias], axis=0)
            s_scr[j, h] = s
            m_scr[h] = jnp.maximum(m_scr[h], jnp.maximum(s[:, :LANES], s[:, LANES:]))

    def far_body(j, carry):
        logits_tile(j, None)
        return carry

    lax.fori_loop(0, jnp.maximum(i - 1, 0), far_body, 0)

    @pl.when(i > 0)
    def _():
        logits_tile(i - 1, 1)

    logits_tile(i, 0)

    for h in heads:
        m_scr[h] = jnp.broadcast_to(jnp.max(m_scr[h], axis=-1, keepdims=True), (2 * t, LANES))

    def pv_body(j, carry):
        start = pl.multiple_of(j * t, t)
        for h in heads:
            m = m_scr[h]
            p_lo = jnp.exp2(s_scr[j, h, :, :LANES] - m)
            p_hi = jnp.exp2(s_scr[j, h, :, LANES:] - m)
            l_scr[h] += p_lo + p_hi
            p = jnp.concatenate([p_lo, p_hi], axis=1).astype(BF16)
            acc_scr[h] += _dot(p, v_ref[0, pl.ds(start, t), head_cols(h)])
        return carry

    lax.fori_loop(0, i + 1, pv_body, 0)

    lam = (jnp.exp(jnp.sum(lq1_ref[...] * lk1_ref[...])) - jnp.exp(jnp.sum(lq2_ref[...] * lk2_ref[...]))
           + lambda_init)
    for h in heads:
        l = jnp.sum(l_scr[h], axis=-1, keepdims=True)
        o = acc_scr[h, :t, :] / l[:t] - lam * (acc_scr[h, t:, :] / l[t:])
        o_ref[0, :, head_cols(h)] = (_rms_norm(o, g_ref[...]) * (1.0 - lambda_init)).astype(BF16)


def _diff_attention(pq, pk, pv, bias_tiles, lam_q1, lam_k1, lam_q2, lam_k2, subln_g, lambda_init):
    b, s, width = pq.shape
    t = ATTN_TILE
    vec = lambda a: a.reshape(1, -1).astype(F32)
    all_keys = pl.BlockSpec((1, s, width), lambda bi, i: (bi, 0, 0))
    lam_spec = _resident((1, DIFF_DH))
    return pl.pallas_call(
        functools.partial(_diff_attn_kernel, lambda_init=lambda_init),
        grid=(b, s // t),
        in_specs=[
            pl.BlockSpec((1, t, width), lambda bi, i: (bi, i, 0)),
            all_keys, all_keys, _resident(bias_tiles.shape),
            lam_spec, lam_spec, lam_spec, lam_spec, _resident((1, DIFF_VD)),
        ],
        out_specs=pl.BlockSpec((1, t, width), lambda bi, i: (bi, i, 0)),
        out_shape=jax.ShapeDtypeStruct((b, s, width), BF16),
        scratch_shapes=[
            pltpu.VMEM((DIFF_HEADS, 2 * t, DIFF_VD), BF16),
            pltpu.VMEM((s // t, DIFF_HEADS, 2 * t, t), F32),
            pltpu.VMEM((DIFF_HEADS, 2 * t, LANES), F32),
            pltpu.VMEM((DIFF_HEADS, 2 * t, LANES), F32),
            pltpu.VMEM((DIFF_HEADS, 2 * t, DIFF_VD), F32),
        ],
        compiler_params=_params("parallel", "parallel"),
        name="diff_attention",
    )(pq, pk, pv, bias_tiles, vec(lam_q1), vec(lam_k1), vec(lam_q2), vec(lam_k2), vec(subln_g))


def _gmlp_kernel(pc_ref, lng_ref, lnb_ref, ws_ref, bs_ref, o_ref):
    rows = o_ref.shape[0]
    z = jax.nn.gelu(pc_ref[...].astype(F32))
    u = z[:, :BRANCH_D]
    v = _layer_norm(z[:, BRANCH_D:], lng_ref[...], lnb_ref[...]).astype(BF16)
    t_idx = lax.broadcasted_iota(jnp.int32, (GMLP_CHUNK, GMLP_CHUNK), 0)
    s_idx = lax.broadcasted_iota(jnp.int32, (GMLP_CHUNK, GMLP_CHUNK), 1)
    for g in range(GMLP_GROUPS):
        w = jnp.where(s_idx <= t_idx, ws_ref[g], 0.0).astype(BF16)
        cols = slice(g * GMLP_GD, (g + 1) * GMLP_GD)
        for r in range(0, rows, GMLP_CHUNK):
            sv = _dot(w, v[r:r + GMLP_CHUNK, cols]) + bs_ref[g]
            o_ref[r:r + GMLP_CHUNK, cols] = (u[r:r + GMLP_CHUNK, cols] * sv).astype(BF16)


def _gmlp_branch(pc, ln_g, ln_b, w_s, b_s, row_tile=ROW_TILE):
    rows = pc.shape[0]
    vec = lambda a: a.reshape(1, BRANCH_D)
    return pl.pallas_call(
        _gmlp_kernel,
        grid=(rows // row_tile,),
        in_specs=[
            pl.BlockSpec((row_tile, COL_C), lambda i: (i, 0)),
            _resident((1, BRANCH_D)), _resident((1, BRANCH_D)),
            _resident(w_s.shape), _resident((GMLP_GROUPS, GMLP_CHUNK, 1)),
        ],
        out_specs=pl.BlockSpec((row_tile, BRANCH_D), lambda i: (i, 0)),
        out_shape=jax.ShapeDtypeStruct((rows, BRANCH_D), BF16),
        compiler_params=_params("parallel"),
        name="gmlp_branch",
    )(pc, vec(ln_g), vec(ln_b), w_s, b_s.reshape(GMLP_GROUPS, GMLP_CHUNK, 1))


def _merge_kernel(x_ref, ya_ref, yb_ref, yc_ref, pg_ref, gb_ref, wbr_ref, wout_ref, o_ref):
    merged = None
    for n, y_ref in enumerate((ya_ref, yb_ref, yc_ref)):
        logits = pg_ref[:, n * D_MODEL:(n + 1) * D_MODEL].astype(F32) + gb_ref[n:n + 1, :]
        term = jax.nn.sigmoid(logits) * _dot(y_ref[...], wbr_ref[n])
        merged = term if merged is None else merged + term
    o_ref[...] = x_ref[...] + _dot(merged.astype(BF16), wout_ref[...])


def _merge(x, ya, yb, yc, pg, gate_b, w_br, w_out, row_tile=ROW_TILE):
    rows = x.shape[0]
    row_spec = lambda width: pl.BlockSpec((row_tile, width), lambda i: (i, 0))
    return pl.pallas_call(
        _merge_kernel,
        grid=(rows // row_tile,),
        in_specs=[row_spec(D_MODEL), row_spec(BRANCH_D), row_spec(BRANCH_D), row_spec(BRANCH_D), row_spec(COL_G),
                  _resident(gate_b.shape), _resident(w_br.shape), _resident(w_out.shape)],
        out_specs=row_spec(D_MODEL),
        out_shape=jax.ShapeDtypeStruct(x.shape, F32),
        compiler_params=_params("parallel"),
        name="branch_merge",
    )(x, ya, yb, yc, pg, gate_b, w_br, w_out)


def _xattn_kernel(x_ref, g_ref, wq_ref, kv_ref, wo_ref, o_ref):
    x = x_ref[0]
    q = _dot(_rms_norm(x, g_ref[...]).astype(BF16), wq_ref[...]).astype(BF16)
    heads = []
    for h in range(XATTN_HEADS):
        cols = slice(h * XATTN_DH, (h + 1) * XATTN_DH)
        k = kv_ref[0, :, cols]
        v = kv_ref[0, :, XATTN_D + h * XATTN_DH:XATTN_D + (h + 1) * XATTN_DH]
        s = _dot_nt(q[:, cols], k) * (XATTN_DH ** -0.5)
        p = jnp.exp(s - jnp.max(s, axis=-1, keepdims=True))
        p = p / jnp.sum(p, axis=-1, keepdims=True)
        heads.append(_dot(p.astype(BF16), v).astype(BF16))
    o_ref[0] = x + _dot(jnp.concatenate(heads, axis=-1), wo_ref[...])


def _mem_cross_attention(x, kv, norm_g, w_xq, w_xo, row_tile=ROW_TILE):
    b, s, d = x.shape
    m = kv.shape[1]
    return pl.pallas_call(
        _xattn_kernel,
        grid=(b, s // row_tile),
        in_specs=[
            pl.BlockSpec((1, row_tile, d), lambda bi, i: (bi, i, 0)),
            _resident((1, d)), _resident(w_xq.shape),
            pl.BlockSpec((1, m, 2 * XATTN_D), lambda bi, i: (bi, 0, 0)),
            _resident(w_xo.shape),
        ],
        out_specs=pl.BlockSpec((1, row_tile, d), lambda bi, i: (bi, i, 0)),
        out_shape=jax.ShapeDtypeStruct(x.shape, F32),
        compiler_params=_params("parallel", "parallel"),
        name="mem_cross_attention",
    )(x, norm_g.reshape(1, d), w_xq, kv, w_xo)


def _ffn_kernel(x_ref, halo_ref, g_ref, wup_ref, cw_ref, cb_ref, wdn_ref, gf_ref, o_ref,
                h_scr, up_scr, act_scr, *, final_norm):
    rows = o_ref.shape[1]
    g = g_ref[...]
    halo = _rms_norm(halo_ref[0], g)
    h_scr[:FFN_HALO, :] = jnp.where(pl.program_id(1) > 0, halo, 0.0).astype(BF16)
    h_scr[FFN_HALO:, :] = _rms_norm(x_ref[0], g).astype(BF16)

    first_tap = FFN_HALO - (FFN_K - 1)
    n_chunks = D_FF // FFN_COL_CHUNK

    def cols(c, half):
        start = half * D_FF + c * FFN_COL_CHUNK
        return slice(start, start + FFN_COL_CHUNK)

    def up_proj(c, slot):
        h = h_scr[...]
        for half in range(2):
            up_scr[slot, half] = _dot(h, wup_ref[:, cols(c, half)])

    def conv(c, slot, half):
        out = cb_ref[:, cols(c, half)]
        for k in range(FFN_K):
            window = up_scr[slot, half, first_tap + k:first_tap + k + rows, :]
            out = out + cw_ref[k:k + 1, cols(c, half)] * window
        return out

    def gate(c, slot):
        act_scr[slot] = (jax.nn.silu(conv(c, slot, 0)) * conv(c, slot, 1)).astype(BF16)

    def down_proj(c, slot):
        o_ref[0] += _dot(act_scr[slot], wdn_ref[cols(c, 0), :])

    def stage(c, slot):
        up_proj(c + 1, 1 - slot)
        down_proj(c - 1, 1 - slot)
        gate(c, slot)

    o_ref[0] = x_ref[0]
    up_proj(0, 0)
    up_proj(1, 1)
    gate(0, 0)
    for c in range(1, n_chunks - 1):
        stage(c, c % 2)
    down_proj(n_chunks - 2, (n_chunks - 2) % 2)
    gate(n_chunks - 1, (n_chunks - 1) % 2)
    down_proj(n_chunks - 1, (n_chunks - 1) % 2)

    if final_norm:
        o_ref[0] = _rms_norm(o_ref[0], gf_ref[...])


def _conv_ffn(x, norm_g, w_up, conv_w, conv_b, w_down, final_g, final_norm, row_tile=ROW_TILE):
    b, s, d = x.shape
    halo_blocks = row_tile // FFN_HALO
    conv_b = conv_b.reshape(1, 2 * D_FF)
    return pl.pallas_call(
        functools.partial(_ffn_kernel, final_norm=final_norm),
        grid=(b, s // row_tile),
        in_specs=[
            pl.BlockSpec((1, row_tile, d), lambda bi, i: (bi, i, 0)),
            pl.BlockSpec((1, FFN_HALO, d), lambda bi, i: (bi, jnp.maximum(i * halo_blocks - 1, 0), 0)),
            _resident((1, d)), _resident(w_up.shape), _resident(conv_w.shape), _resident(conv_b.shape),
            _resident(w_down.shape), _resident((1, d)),
        ],
        out_specs=pl.BlockSpec((1, row_tile, d), lambda bi, i: (bi, i, 0)),
        out_shape=jax.ShapeDtypeStruct(x.shape, F32),
        scratch_shapes=[
            pltpu.VMEM((FFN_HALO + row_tile, d), BF16),
            pltpu.VMEM((2, 2, FFN_HALO + row_tile, FFN_COL_CHUNK), F32),
            pltpu.VMEM((2, row_tile, FFN_COL_CHUNK), BF16),
        ],
        compiler_params=_params("parallel", "parallel"),
        name="conv_ffn",
    )(x, x, norm_g.reshape(1, d), w_up, conv_w, conv_b, w_down, final_g.reshape(1, d))


def kernel(x, mem, rel_bias, norm_mix_g, w_in, gate_b, conv_w, conv_b, conv_ln_g, conv_ln_b, lam_q1, lam_k1, lam_q2, lam_k2, subln_g, gmlp_ln_g, gmlp_ln_b, w_s, b_s, w_br, w_out, norm_xattn_g, norm_mem_g, w_xq, w_xkv, w_xo, norm_ffn_g, w_up, ffn_conv_w, ffn_conv_b, w_down, norm_final_g):
    b, s, d = x.shape
    m = mem.shape[1]
    rows = b * s
    bias_tiles = _bias_tiles(rel_bias)
    for l in range(DEPTH):
        lambda_init = 0.8 - 0.6 * math.exp(-0.3 * l)
        pa, pq, pk, pv, pc, pg = _norm_proj(x.reshape(rows, d), norm_mix_g[l], w_in[l].astype(BF16),
                                            IN_SPLIT_WIDTHS, IN_SPLIT_SCALES)
        ya = _conformer_branch(pa.reshape(b, s, COL_A), conv_w[l], conv_b[l], conv_ln_g[l], conv_ln_b[l])
        yb = _diff_attention(pq.reshape(b, s, COL_QK), pk.reshape(b, s, COL_QK), pv.reshape(b, s, COL_V),
                             bias_tiles, lam_q1[l], lam_k1[l], lam_q2[l], lam_k2[l], subln_g[l], lambda_init)
        yc = _gmlp_branch(pc, gmlp_ln_g[l], gmlp_ln_b[l], w_s[l], b_s[l])
        x = _merge(x.reshape(rows, d), ya.reshape(rows, BRANCH_D), yb.reshape(rows, BRANCH_D), yc, pg,
                   gate_b[l], w_br[l].astype(BF16), w_out[l].astype(BF16)).reshape(b, s, d)
        (kv,) = _norm_proj(mem.reshape(b * m, d), norm_mem_g[l], w_xkv[l].astype(BF16), (2 * XATTN_D,))
        x = _mem_cross_attention(x, kv.reshape(b, m, 2 * XATTN_D), norm_xattn_g[l], w_xq[l].astype(BF16),
                                 w_xo[l].astype(BF16))
        x = _conv_ffn(x, norm_ffn_g[l], w_up[l].astype(BF16), ffn_conv_w[l], ffn_conv_b[l],
                      w_down[l].astype(BF16), norm_final_g, final_norm=(l == DEPTH - 1))
    return x
```

```python
import functools
import math

import jax
import jax.numpy as jnp
from jax import lax
from jax.experimental import pallas as pl
from jax.experimental.pallas import tpu as pltpu

F32 = jnp.float32
BF16 = jnp.bfloat16

D_MODEL = 1024
DEPTH = 2
CHUNK = 64
BRANCH_D = D_MODEL // 2
N_BRANCH = 3
CONV_K = 31
DIFF_HEADS = 4
DIFF_DH = D_MODEL // 16
DIFF_VD = 2 * DIFF_DH
GMLP_GROUPS = 4
GMLP_GD = BRANCH_D // GMLP_GROUPS
GMLP_CHUNK = 128
N_REL_BUCKETS = 32
REL_MAX_DIST = 128
XATTN_HEADS = 4
XATTN_DH = D_MODEL // 8
XATTN_D = XATTN_HEADS * XATTN_DH
D_FF = 2816
FFN_K = 3
EPS = 1e-6
NEG_INF = -1e30

COL_A = 2 * BRANCH_D
COL_QK = DIFF_HEADS * 2 * DIFF_DH
COL_V = DIFF_HEADS * DIFF_VD
COL_C = 2 * BRANCH_D
COL_G = N_BRANCH * D_MODEL
IN_SPLIT_WIDTHS = (COL_A, COL_QK, COL_QK, COL_V, COL_C, COL_G)
LOG2_E = math.log2(math.e)
IN_SPLIT_SCALES = (1.0, DIFF_DH ** -0.5 * LOG2_E, 1.0, 1.0, 1.0, 1.0)

LANES = 128
MXU_WIDTH = 256
F32_SUBLANES = 8
BF16_SUBLANES = 16
VMEM_LIMIT_BYTES = 56 * 1024 * 1024

ROW_TILE = 512
FFN_ROW_TILE = 1024
PROJ_COL_CHUNK = 512
ATTN_TILE = 256
CONV_HALO = 32
CONV_ROW_CHUNK = 64
FFN_HALO = BF16_SUBLANES
FFN_COL_CHUNK = MXU_WIDTH

assert CONV_HALO >= CONV_K - 1 and FFN_HALO == F32_SUBLANES * (FFN_K - 1)
assert D_FF % FFN_COL_CHUNK == 0
assert ATTN_TILE % CHUNK == 0 and ATTN_TILE > REL_MAX_DIST


def _params(*semantics):
    return pltpu.CompilerParams(dimension_semantics=semantics, vmem_limit_bytes=VMEM_LIMIT_BYTES)


def _resident(shape):
    zeros = (0,) * len(shape)
    return pl.BlockSpec(shape, lambda *_: zeros, pipeline_mode=pl.Buffered(1))


def _rms_norm(x, g):
    return x * lax.rsqrt(jnp.mean(x * x, axis=-1, keepdims=True) + EPS) * g


def _layer_norm(x, g, b):
    mu = jnp.mean(x, axis=-1, keepdims=True)
    xc = x - mu
    var = jnp.mean(xc * xc, axis=-1, keepdims=True)
    return xc * lax.rsqrt(var + EPS) * g + b


def _dot(a, b):
    return jnp.dot(a, b, preferred_element_type=F32)


def _dot_nt(a, b):
    return lax.dot_general(a, b, (((1,), (1,)), ((), ())), preferred_element_type=F32)


def _norm_proj_kernel(x_ref, g_ref, w_ref, *out_refs, widths, scales):
    xn = _rms_norm(x_ref[...], g_ref[...]).astype(BF16)
    col = 0
    for o_ref, width, scale in zip(out_refs, widths, scales):
        for c in range(0, width, PROJ_COL_CHUNK):
            out = _dot(xn, w_ref[:, col + c:col + c + PROJ_COL_CHUNK])
            if scale != 1.0:
                out = out * scale
            o_ref[:, c:c + PROJ_COL_CHUNK] = out.astype(BF16)
        col += width


def _norm_proj(x, g, w, widths, scales=None, row_tile=ROW_TILE):
    rows, d = x.shape
    scales = scales or (1.0,) * len(widths)
    assert w.shape == (d, sum(widths)) and all(wd % PROJ_COL_CHUNK == 0 for wd in widths)
    return pl.pallas_call(
        functools.partial(_norm_proj_kernel, widths=widths, scales=scales),
        grid=(rows // row_tile,),
        in_specs=[pl.BlockSpec((row_tile, d), lambda i: (i, 0)), _resident((1, d)), _resident(w.shape)],
        out_specs=[pl.BlockSpec((row_tile, wd), lambda i: (i, 0)) for wd in widths],
        out_shape=[jax.ShapeDtypeStruct((rows, wd), BF16) for wd in widths],
        compiler_params=_params("parallel"),
        name="norm_proj",
    )(x, g.reshape(1, d), w)


def _conformer_kernel(pa_ref, halo_ref, cw_ref, cb_ref, lng_ref, lnb_ref, o_ref, xs_scr):
    rows = o_ref.shape[1]

    def glu(p):
        return p[:, :BRANCH_D].astype(F32) * jax.nn.sigmoid(p[:, BRANCH_D:].astype(F32))

    halo = glu(halo_ref[0])
    xs_scr[0, :CONV_HALO, :] = jnp.where(pl.program_id(1) > 0, halo, 0.0)
    xs_scr[0, CONV_HALO:, :] = glu(pa_ref[0])
    shifted_rows = CONV_HALO + rows - F32_SUBLANES
    for b in range(1, F32_SUBLANES):
        xs_scr[b, :shifted_rows, :] = xs_scr[0, b:b + shifted_rows, :]

    first_tap = CONV_HALO - (CONV_K - 1)
    for r in range(0, rows, CONV_ROW_CHUNK):
        acc = jnp.broadcast_to(cb_ref[...], (CONV_ROW_CHUNK, BRANCH_D))
        for k in range(CONV_K):
            b = (first_tap + k) % F32_SUBLANES
            start = r + first_tap + k - b
            acc = acc + cw_ref[k:k + 1, :] * xs_scr[b, start:start + CONV_ROW_CHUNK, :]
        o_ref[0, r:r + CONV_ROW_CHUNK, :] = jax.nn.silu(_layer_norm(acc, lng_ref[...], lnb_ref[...])).astype(BF16)


def _conformer_branch(pa, conv_w, conv_b, ln_g, ln_b, row_tile=ROW_TILE):
    b, s, _ = pa.shape
    halo_blocks = row_tile // CONV_HALO
    vec = lambda a: a.reshape(1, BRANCH_D)
    return pl.pallas_call(
        _conformer_kernel,
        grid=(b, s // row_tile),
        in_specs=[
            pl.BlockSpec((1, row_tile, COL_A), lambda bi, i: (bi, i, 0)),
            pl.BlockSpec((1, CONV_HALO, COL_A), lambda bi, i: (bi, jnp.maximum(i * halo_blocks - 1, 0), 0)),
            _resident((CONV_K, BRANCH_D)), _resident((1, BRANCH_D)), _resident((1, BRANCH_D)),
            _resident((1, BRANCH_D)),
        ],
        out_specs=pl.BlockSpec((1, row_tile, BRANCH_D), lambda bi, i: (bi, i, 0)),
        out_shape=jax.ShapeDtypeStruct((b, s, BRANCH_D), BF16),
        scratch_shapes=[pltpu.VMEM((F32_SUBLANES, CONV_HALO + row_tile, BRANCH_D), F32)],
        compiler_params=_params("parallel", "parallel"),
        name="conformer_branch",
    )(pa, pa, conv_w, vec(conv_b), vec(ln_g), vec(ln_b))


def _rel_bucket(rel):
    half = N_REL_BUCKETS // 2
    max_exact = half // 2
    n = jnp.abs(rel)
    log_ratio = jnp.log(jnp.maximum(n, 1).astype(F32) / max_exact) / math.log(REL_MAX_DIST / max_exact)
    large = jnp.minimum(max_exact + (log_ratio * (half - max_exact)).astype(jnp.int32), half - 1)
    return jnp.where(rel > 0, half, 0) + jnp.where(n < max_exact, n, large)


FAR_BUCKET = N_REL_BUCKETS // 2 - 1
MASKED_BUCKET = -1


def _near_buckets():
    r = jnp.arange(ATTN_TILE)[:, None]
    c = jnp.arange(ATTN_TILE)[None, :]
    visible = (c // CHUNK) <= (r // CHUNK)
    diag = jnp.where(visible, _rel_bucket(c - r), MASKED_BUCKET)
    sub = _rel_bucket(c - r - ATTN_TILE)
    return jnp.stack([diag, sub]).astype(jnp.int32)


def _bias_tiles_kernel(table_ref, bucket_ref, o_ref):
    h = pl.program_id(0)
    bucket = bucket_ref[...]
    far = table_ref[FAR_BUCKET, h]
    out = jnp.zeros(bucket.shape, F32)
    for b in range(N_REL_BUCKETS):
        out = jnp.where(bucket == b, (table_ref[b, h] - far) * LOG2_E, out)
    o_ref[0] = jnp.where(bucket == MASKED_BUCKET, NEG_INF, out)


def _bias_tiles(rel_bias):
    buckets = _near_buckets()
    return pl.pallas_call(
        _bias_tiles_kernel,
        grid=(DIFF_HEADS,),
        in_specs=[pl.BlockSpec(memory_space=pltpu.SMEM), _resident(buckets.shape)],
        out_specs=pl.BlockSpec((1,) + buckets.shape, lambda h: (h, 0, 0, 0)),
        out_shape=jax.ShapeDtypeStruct((DIFF_HEADS,) + buckets.shape, F32),
        compiler_params=_params("arbitrary"),
        name="rel_bias_tiles",
    )(rel_bias.astype(F32), buckets)


def _diff_attn_kernel(q_ref, k_ref, v_ref, bias_ref, lq1_ref, lk1_ref, lq2_ref, lk2_ref, g_ref, o_ref,
                      q2_scr, s_scr, m_scr, l_scr, acc_scr, *, lambda_init):
    t = ATTN_TILE
    i = pl.program_id(1)
    heads = range(DIFF_HEADS)

    def head_cols(h):
        return slice(h * DIFF_VD, (h + 1) * DIFF_VD)

    q = q_ref[0]
    lane = lax.broadcasted_iota(jnp.int32, (t, DIFF_VD), 1)
    for h in heads:
        qh = q[:, head_cols(h)]
        q2_scr[h, :t, :] = jnp.where(lane < DIFF_DH, qh, 0).astype(BF16)
        q2_scr[h, t:, :] = jnp.where(lane >= DIFF_DH, qh, 0).astype(BF16)
    m_scr[...] = jnp.full(m_scr.shape, NEG_INF, F32)
    l_scr[...] = jnp.zeros(l_scr.shape, F32)
    acc_scr[...] = jnp.zeros(acc_scr.shape, F32)

    def logits_tile(j, near):
        start = pl.multiple_of(j * t, t)
        for h in heads:
            s = _dot_nt(q2_scr[h], k_ref[0, pl.ds(start, t), head_cols(h)])
            if near is not None:
                bias = bias_ref[h, near]
                s = s + jnp.concatenate([bias, bias], axis=0)
            s_scr[j, h] = s
            m_scr[h] = jnp.maximum(m_scr[h], jnp.maximum(s[:, :LANES], s[:, LANES:]))

    def far_body(j, carry):
        logits_tile(j, None)
        return carry

    lax.fori_loop(0, jnp.maximum(i - 1, 0), far_body, 0)

    @pl.when(i > 0)
    def _():
        logits_tile(i - 1, 1)

    logits_tile(i, 0)

    for h in heads:
        m_scr[h] = jnp.broadcast_to(jnp.max(m_scr[h], axis=-1, keepdims=True), (2 * t, LANES))

    def pv_body(j, carry):
        start = pl.multiple_of(j * t, t)
        for h in heads:
            m = m_scr[h]
            p_lo = jnp.exp2(s_scr[j, h, :, :LANES] - m)
            p_hi = jnp.exp2(s_scr[j, h, :, LANES:] - m)
            l_scr[h] += p_lo + p_hi
            p = jnp.concatenate([p_lo, p_hi], axis=1).astype(BF16)
            acc_scr[h] += _dot(p, v_ref[0, pl.ds(start, t), head_cols(h)])
        return carry

    lax.fori_loop(0, i + 1, pv_body, 0)

    lam = (jnp.exp(jnp.sum(lq1_ref[...] * lk1_ref[...])) - jnp.exp(jnp.sum(lq2_ref[...] * lk2_ref[...]))
           + lambda_init)
    for h in heads:
        l = jnp.sum(l_scr[h], axis=-1, keepdims=True)
        o = acc_scr[h, :t, :] / l[:t] - lam * (acc_scr[h, t:, :] / l[t:])
        o_ref[0, :, head_cols(h)] = (_rms_norm(o, g_ref[...]) * (1.0 - lambda_init)).astype(BF16)


def _diff_attention(pq, pk, pv, bias_tiles, lam_q1, lam_k1, lam_q2, lam_k2, subln_g, lambda_init):
    b, s, width = pq.shape
    t = ATTN_TILE
    vec = lambda a: a.reshape(1, -1).astype(F32)
    all_keys = pl.BlockSpec((1, s, width), lambda bi, i: (bi, 0, 0))
    lam_spec = _resident((1, DIFF_DH))
    return pl.pallas_call(
        functools.partial(_diff_attn_kernel, lambda_init=lambda_init),
        grid=(b, s // t),
        in_specs=[
            pl.BlockSpec((1, t, width), lambda bi, i: (bi, i, 0)),
            all_keys, all_keys, _resident(bias_tiles.shape),
            lam_spec, lam_spec, lam_spec, lam_spec, _resident((1, DIFF_VD)),
        ],
        out_specs=pl.BlockSpec((1, t, width), lambda bi, i: (bi, i, 0)),
        out_shape=jax.ShapeDtypeStruct((b, s, width), BF16),
        scratch_shapes=[
            pltpu.VMEM((DIFF_HEADS, 2 * t, DIFF_VD), BF16),
            pltpu.VMEM((s // t, DIFF_HEADS, 2 * t, t), F32),
            pltpu.VMEM((DIFF_HEADS, 2 * t, LANES), F32),
            pltpu.VMEM((DIFF_HEADS, 2 * t, LANES), F32),
            pltpu.VMEM((DIFF_HEADS, 2 * t, DIFF_VD), F32),
        ],
        compiler_params=_params("parallel", "parallel"),
        name="diff_attention",
    )(pq, pk, pv, bias_tiles, vec(lam_q1), vec(lam_k1), vec(lam_q2), vec(lam_k2), vec(subln_g))


def _gmlp_kernel(pc_ref, lng_ref, lnb_ref, ws_ref, bs_ref, o_ref):
    rows = o_ref.shape[0]
    z = jax.nn.gelu(pc_ref[...].astype(F32))
    u = z[:, :BRANCH_D]
    v = _layer_norm(z[:, BRANCH_D:], lng_ref[...], lnb_ref[...]).astype(BF16)
    t_idx = lax.broadcasted_iota(jnp.int32, (GMLP_CHUNK, GMLP_CHUNK), 0)
    s_idx = lax.broadcasted_iota(jnp.int32, (GMLP_CHUNK, GMLP_CHUNK), 1)
    for g in range(GMLP_GROUPS):
        w = jnp.where(s_idx <= t_idx, ws_ref[g], 0.0).astype(BF16)
        cols = slice(g * GMLP_GD, (g + 1) * GMLP_GD)
        for r in range(0, rows, GMLP_CHUNK):
            sv = _dot(w, v[r:r + GMLP_CHUNK, cols]) + bs_ref[g]
            o_ref[r:r + GMLP_CHUNK, cols] = (u[r:r + GMLP_CHUNK, cols] * sv).astype(BF16)


def _gmlp_branch(pc, ln_g, ln_b, w_s, b_s, row_tile=ROW_TILE):
    rows = pc.shape[0]
    vec = lambda a: a.reshape(1, BRANCH_D)
    return pl.pallas_call(
        _gmlp_kernel,
        grid=(rows // row_tile,),
        in_specs=[
            pl.BlockSpec((row_tile, COL_C), lambda i: (i, 0)),
            _resident((1, BRANCH_D)), _resident((1, BRANCH_D)),
            _resident(w_s.shape), _resident((GMLP_GROUPS, GMLP_CHUNK, 1)),
        ],
        out_specs=pl.BlockSpec((row_tile, BRANCH_D), lambda i: (i, 0)),
        out_shape=jax.ShapeDtypeStruct((rows, BRANCH_D), BF16),
        compiler_params=_params("parallel"),
        name="gmlp_branch",
    )(pc, vec(ln_g), vec(ln_b), w_s, b_s.reshape(GMLP_GROUPS, GMLP_CHUNK, 1))


def _merge_kernel(x_ref, ya_ref, yb_ref, yc_ref, pg_ref, gb_ref, wbr_ref, wout_ref, o_ref):
    merged = None
    for n, y_ref in enumerate((ya_ref, yb_ref, yc_ref)):
        logits = pg_ref[:, n * D_MODEL:(n + 1) * D_MODEL].astype(F32) + gb_ref[n:n + 1, :]
        term = jax.nn.sigmoid(logits) * _dot(y_ref[...], wbr_ref[n])
        merged = term if merged is None else merged + term
    o_ref[...] = x_ref[...] + _dot(merged.astype(BF16), wout_ref[...])


def _merge(x, ya, yb, yc, pg, gate_b, w_br, w_out, row_tile=ROW_TILE):
    rows = x.shape[0]
    row_spec = lambda width: pl.BlockSpec((row_tile, width), lambda i: (i, 0))
    return pl.pallas_call(
        _merge_kernel,
        grid=(rows // row_tile,),
        in_specs=[row_spec(D_MODEL), row_spec(BRANCH_D), row_spec(BRANCH_D), row_spec(BRANCH_D), row_spec(COL_G),
                  _resident(gate_b.shape), _resident(w_br.shape), _resident(w_out.shape)],
        out_specs=row_spec(D_MODEL),
        out_shape=jax.ShapeDtypeStruct(x.shape, F32),
        compiler_params=_params("parallel"),
        name="branch_merge",
    )(x, ya, yb, yc, pg, gate_b, w_br, w_out)


def _xattn_kernel(x_ref, g_ref, wq_ref, kv_ref, wo_ref, o_ref):
    x = x_ref[0]
    q = _dot(_rms_norm(x, g_ref[...]).astype(BF16), wq_ref[...]).astype(BF16)
    heads = []
    for h in range(XATTN_HEADS):
        cols = slice(h * XATTN_DH, (h + 1) * XATTN_DH)
        k = kv_ref[0, :, cols]
        v = kv_ref[0, :, XATTN_D + h * XATTN_DH:XATTN_D + (h + 1) * XATTN_DH]
        s = _dot_nt(q[:, cols], k) * (XATTN_DH ** -0.5)
        p = jnp.exp(s - jnp.max(s, axis=-1, keepdims=True))
        p = p / jnp.sum(p, axis=-1, keepdims=True)
        heads.append(_dot(p.astype(BF16), v).astype(BF16))
    o_ref[0] = x + _dot(jnp.concatenate(heads, axis=-1), wo_ref[...])


def _mem_cross_attention(x, kv, norm_g, w_xq, w_xo, row_tile=ROW_TILE):
    b, s, d = x.shape
    m = kv.shape[1]
    return pl.pallas_call(
        _xattn_kernel,
        grid=(b, s // row_tile),
        in_specs=[
            pl.BlockSpec((1, row_tile, d), lambda bi, i: (bi, i, 0)),
            _resident((1, d)), _resident(w_xq.shape),
            pl.BlockSpec((1, m, 2 * XATTN_D), lambda bi, i: (bi, 0, 0)),
            _resident(w_xo.shape),
        ],
        out_specs=pl.BlockSpec((1, row_tile, d), lambda bi, i: (bi, i, 0)),
        out_shape=jax.ShapeDtypeStruct(x.shape, F32),
        compiler_params=_params("parallel", "parallel"),
        name="mem_cross_attention",
    )(x, norm_g.reshape(1, d), w_xq, kv, w_xo)


def _ffn_kernel(x_ref, halo_ref, g_ref, wup_ref, cw_ref, cb_ref, wdn_ref, gf_ref, o_ref,
                slab_scr, h_scr, up_scr, act_scr, acc_scr, *, final_norm):
    tile_rows = o_ref.shape[1]
    rows = acc_scr.shape[0]
    groups = rows // F32_SUBLANES
    slabs = slab_scr.shape[0]

    def lanes(j):
        return slice(j * LANES, (j + 1) * LANES)

    for j in range(slabs):
        slab_scr[j, :tile_rows, :] = x_ref[0, :, lanes(j)]
    slab_scr[:, tile_rows:, :] = jnp.zeros((slabs, rows - tile_rows, LANES), F32)
    x = jnp.concatenate(
        [jnp.concatenate([slab_scr[j, pl.ds(gi, F32_SUBLANES, stride=groups), :] for j in range(slabs)], axis=1)
         for gi in range(groups)], axis=0)

    g = g_ref[...]
    halo = _rms_norm(halo_ref[0], g)
    h_scr[:FFN_HALO, :] = jnp.where(pl.program_id(1) > 0, halo, 0.0).astype(BF16)
    h_scr[FFN_HALO:, :] = _rms_norm(x, g).astype(BF16)
    acc_scr[...] = x

    n_chunks = D_FF // FFN_COL_CHUNK
    first_sublane = lax.broadcasted_iota(jnp.int32, (F32_SUBLANES, FFN_COL_CHUNK), 0) == 0

    def cols(c, half):
        start = half * D_FF + c * FFN_COL_CHUNK
        return slice(start, start + FFN_COL_CHUNK)

    def up_proj(c, slot):
        h = h_scr[...]
        for half in range(2):
            up = up_scr.at[slot, half]
            up[...] = _dot(h, wup_ref[:, cols(c, half)])
            prev_tile = up[FFN_HALO - F32_SUBLANES:FFN_HALO, :]
            for shift in range(1, FFN_K):
                wrapped = up[FFN_HALO + rows - shift * F32_SUBLANES:FFN_HALO + rows - (shift - 1) * F32_SUBLANES, :]
                block = jnp.where(first_sublane, pltpu.roll(prev_tile, shift, 0), pltpu.roll(wrapped, 1, 0))
                up[FFN_HALO - shift * F32_SUBLANES:FFN_HALO - (shift - 1) * F32_SUBLANES, :] = block

    def conv(c, slot, half):
        out = cb_ref[:, cols(c, half)]
        for k in range(FFN_K):
            window = up_scr[slot, half, k * F32_SUBLANES:k * F32_SUBLANES + rows, :]
            out = out + cw_ref[k:k + 1, cols(c, half)] * window
        return out

    def gate(c, slot):
        act_scr[slot] = (jax.nn.silu(conv(c, slot, 0)) * conv(c, slot, 1)).astype(BF16)

    def down_proj(c, slot):
        acc_scr[...] += _dot(act_scr[slot], wdn_ref[cols(c, 0), :])

    def stage(c, slot):
        up_proj(c + 1, 1 - slot)
        down_proj(c - 1, 1 - slot)
        gate(c, slot)

    up_proj(0, 0)
    up_proj(1, 1)
    gate(0, 0)
    for c in range(1, n_chunks - 1):
        stage(c, c % 2)
    down_proj(n_chunks - 2, (n_chunks - 2) % 2)
    gate(n_chunks - 1, (n_chunks - 1) % 2)
    down_proj(n_chunks - 1, (n_chunks - 1) % 2)

    for gi in range(groups):
        out = acc_scr[gi * F32_SUBLANES:(gi + 1) * F32_SUBLANES, :]
        if final_norm:
            out = _rms_norm(out, gf_ref[...])
        for j in range(slabs):
            slab_scr[j, pl.ds(gi, F32_SUBLANES, stride=groups), :] = out[:, lanes(j)]
    for j in range(slabs):
        o_ref[0, :, lanes(j)] = slab_scr[j, :tile_rows, :]


def _conv_ffn(x, norm_g, w_up, conv_w, conv_b, w_down, final_g, final_norm, row_tile=FFN_ROW_TILE):
    b, s, d = x.shape
    halo_blocks = row_tile // FFN_HALO
    conv_b = conv_b.reshape(1, 2 * D_FF)
    perm_rows = row_tile + 2 * F32_SUBLANES
    return pl.pallas_call(
        functools.partial(_ffn_kernel, final_norm=final_norm),
        grid=(b, s // row_tile),
        in_specs=[
            pl.BlockSpec((1, row_tile, d), lambda bi, i: (bi, i, 0)),
            pl.BlockSpec((1, FFN_HALO, d), lambda bi, i: (bi, jnp.maximum(i * halo_blocks - 1, 0), 0)),
            _resident((1, d)), _resident(w_up.shape), _resident(conv_w.shape), _resident(conv_b.shape),
            _resident(w_down.shape), _resident((1, d)),
        ],
        out_specs=pl.BlockSpec((1, row_tile, d), lambda bi, i: (bi, i, 0)),
        out_shape=jax.ShapeDtypeStruct(x.shape, F32),
        scratch_shapes=[
            pltpu.VMEM((d // LANES, perm_rows, LANES), F32),
            pltpu.VMEM((FFN_HALO + perm_rows, d), BF16),
            pltpu.VMEM((2, 2, FFN_HALO + perm_rows, FFN_COL_CHUNK), F32),
            pltpu.VMEM((2, perm_rows, FFN_COL_CHUNK), BF16),
            pltpu.VMEM((perm_rows, d), F32),
        ],
        compiler_params=_params("parallel", "parallel"),
        name="conv_ffn",
    )(x, x, norm_g.reshape(1, d), w_up, conv_w, conv_b, w_down, final_g.reshape(1, d))


def kernel(x, mem, rel_bias, norm_mix_g, w_in, gate_b, conv_w, conv_b, conv_ln_g, conv_ln_b, lam_q1, lam_k1, lam_q2, lam_k2, subln_g, gmlp_ln_g, gmlp_ln_b, w_s, b_s, w_br, w_out, norm_xattn_g, norm_mem_g, w_xq, w_xkv, w_xo, norm_ffn_g, w_up, ffn_conv_w, ffn_conv_b, w_down, norm_final_g):
    b, s, d = x.shape
    m = mem.shape[1]
    rows = b * s
    bias_tiles = _bias_tiles(rel_bias)
    for l in range(DEPTH):
        lambda_init = 0.8 - 0.6 * math.exp(-0.3 * l)
        pa, pq, pk, pv, pc, pg = _norm_proj(x.reshape(rows, d), norm_mix_g[l], w_in[l].astype(BF16),
                                            IN_SPLIT_WIDTHS, IN_SPLIT_SCALES)
        ya = _conformer_branch(pa.reshape(b, s, COL_A), conv_w[l], conv_b[l], conv_ln_g[l], conv_ln_b[l])
        yb = _diff_attention(pq.reshape(b, s, COL_QK), pk.reshape(b, s, COL_QK), pv.reshape(b, s, COL_V),
                             bias_tiles, lam_q1[l], lam_k1[l], lam_q2[l], lam_k2[l], subln_g[l], lambda_init)
        yc = _gmlp_branch(pc, gmlp_ln_g[l], gmlp_ln_b[l], w_s[l], b_s[l])
        x = _merge(x.reshape(rows, d), ya.reshape(rows, BRANCH_D), yb.reshape(rows, BRANCH_D), yc, pg,
                   gate_b[l], w_br[l].astype(BF16), w_out[l].astype(BF16)).reshape(b, s, d)
        (kv,) = _norm_proj(mem.reshape(b * m, d), norm_mem_g[l], w_xkv[l].astype(BF16), (2 * XATTN_D,))
        x = _mem_cross_attention(x, kv.reshape(b, m, 2 * XATTN_D), norm_xattn_g[l], w_xq[l].astype(BF16),
                                 w_xo[l].astype(BF16))
        x = _conv_ffn(x, norm_ffn_g[l], w_up[l].astype(BF16), ffn_conv_w[l], ffn_conv_b[l],
                      w_down[l].astype(BF16), norm_final_g, final_norm=(l == DEPTH - 1))
    return x
```

```python
import functools
import math

import jax
import jax.numpy as jnp
from jax import lax
from jax.experimental import pallas as pl
from jax.experimental.pallas import tpu as pltpu

F32 = jnp.float32
BF16 = jnp.bfloat16

D_MODEL = 1024
DEPTH = 2
CHUNK = 64
BRANCH_D = D_MODEL // 2
N_BRANCH = 3
CONV_K = 31
DIFF_HEADS = 4
DIFF_DH = D_MODEL // 16
DIFF_VD = 2 * DIFF_DH
GMLP_GROUPS = 4
GMLP_GD = BRANCH_D // GMLP_GROUPS
GMLP_CHUNK = 128
N_REL_BUCKETS = 32
REL_MAX_DIST = 128
XATTN_HEADS = 4
XATTN_DH = D_MODEL // 8
XATTN_D = XATTN_HEADS * XATTN_DH
D_FF = 2816
FFN_K = 3
EPS = 1e-6
NEG_INF = -1e30

COL_A = 2 * BRANCH_D
COL_QK = DIFF_HEADS * 2 * DIFF_DH
COL_V = DIFF_HEADS * DIFF_VD
COL_C = 2 * BRANCH_D
COL_G = N_BRANCH * D_MODEL
IN_SPLIT_WIDTHS = (COL_A, COL_QK, COL_QK, COL_V, COL_C, COL_G)
LOG2_E = math.log2(math.e)
IN_SPLIT_SCALES = (1.0, DIFF_DH ** -0.5 * LOG2_E, 1.0, 1.0, 1.0, 1.0)

LANES = 128
MXU_WIDTH = 256
F32_SUBLANES = 8
BF16_SUBLANES = 16
VMEM_LIMIT_BYTES = 56 * 1024 * 1024

ROW_TILE = 512
FFN_ROW_TILE = 1024
PROJ_COL_CHUNK = 512
ATTN_TILE = 256
CONV_HALO = 32
CONV_ROW_CHUNK = 64
FFN_HALO = BF16_SUBLANES
FFN_COL_CHUNK = MXU_WIDTH

assert CONV_HALO >= CONV_K - 1 and FFN_HALO == F32_SUBLANES * (FFN_K - 1)
assert D_FF % FFN_COL_CHUNK == 0
assert ATTN_TILE % CHUNK == 0 and ATTN_TILE > REL_MAX_DIST


def _params(*semantics):
    return pltpu.CompilerParams(dimension_semantics=semantics, vmem_limit_bytes=VMEM_LIMIT_BYTES)


def _resident(shape):
    zeros = (0,) * len(shape)
    return pl.BlockSpec(shape, lambda *_: zeros, pipeline_mode=pl.Buffered(1))


def _rms_norm(x, g):
    return x * lax.rsqrt(jnp.mean(x * x, axis=-1, keepdims=True) + EPS) * g


def _layer_norm(x, g, b):
    mu = jnp.mean(x, axis=-1, keepdims=True)
    xc = x - mu
    var = jnp.mean(xc * xc, axis=-1, keepdims=True)
    return xc * lax.rsqrt(var + EPS) * g + b


def _dot(a, b):
    return jnp.dot(a, b, preferred_element_type=F32)


def _dot_nt(a, b):
    return lax.dot_general(a, b, (((1,), (1,)), ((), ())), preferred_element_type=F32)


def _norm_proj_kernel(x_ref, g_ref, w_ref, *out_refs, widths, scales):
    xn = _rms_norm(x_ref[...], g_ref[...]).astype(BF16)
    col = 0
    for o_ref, width, scale in zip(out_refs, widths, scales):
        for c in range(0, width, PROJ_COL_CHUNK):
            out = _dot(xn, w_ref[:, col + c:col + c + PROJ_COL_CHUNK])
            if scale != 1.0:
                out = out * scale
            o_ref[:, c:c + PROJ_COL_CHUNK] = out.astype(BF16)
        col += width


def _norm_proj(x, g, w, widths, scales=None, row_tile=ROW_TILE):
    rows, d = x.shape
    scales = scales or (1.0,) * len(widths)
    assert w.shape == (d, sum(widths)) and all(wd % PROJ_COL_CHUNK == 0 for wd in widths)
    return pl.pallas_call(
        functools.partial(_norm_proj_kernel, widths=widths, scales=scales),
        grid=(rows // row_tile,),
        in_specs=[pl.BlockSpec((row_tile, d), lambda i: (i, 0)), _resident((1, d)), _resident(w.shape)],
        out_specs=[pl.BlockSpec((row_tile, wd), lambda i: (i, 0)) for wd in widths],
        out_shape=[jax.ShapeDtypeStruct((rows, wd), BF16) for wd in widths],
        compiler_params=_params("parallel"),
        name="norm_proj",
    )(x, g.reshape(1, d), w)


def _conformer_kernel(pa_ref, halo_ref, cw_ref, cb_ref, lng_ref, lnb_ref, o_ref, xs_scr):
    rows = o_ref.shape[1]

    def glu(p):
        return p[:, :BRANCH_D].astype(F32) * jax.nn.sigmoid(p[:, BRANCH_D:].astype(F32))

    halo = glu(halo_ref[0])
    xs_scr[0, :CONV_HALO, :] = jnp.where(pl.program_id(1) > 0, halo, 0.0)
    xs_scr[0, CONV_HALO:, :] = glu(pa_ref[0])
    shifted_rows = CONV_HALO + rows - F32_SUBLANES
    for b in range(1, F32_SUBLANES):
        xs_scr[b, :shifted_rows, :] = xs_scr[0, b:b + shifted_rows, :]

    first_tap = CONV_HALO - (CONV_K - 1)
    for r in range(0, rows, CONV_ROW_CHUNK):
        acc = jnp.broadcast_to(cb_ref[...], (CONV_ROW_CHUNK, BRANCH_D))
        for k in range(CONV_K):
            b = (first_tap + k) % F32_SUBLANES
            start = r + first_tap + k - b
            acc = acc + cw_ref[k:k + 1, :] * xs_scr[b, start:start + CONV_ROW_CHUNK, :]
        o_ref[0, r:r + CONV_ROW_CHUNK, :] = jax.nn.silu(_layer_norm(acc, lng_ref[...], lnb_ref[...])).astype(BF16)


def _conformer_branch(pa, conv_w, conv_b, ln_g, ln_b, row_tile=ROW_TILE):
    b, s, _ = pa.shape
    halo_blocks = row_tile // CONV_HALO
    vec = lambda a: a.reshape(1, BRANCH_D)
    return pl.pallas_call(
        _conformer_kernel,
        grid=(b, s // row_tile),
        in_specs=[
            pl.BlockSpec((1, row_tile, COL_A), lambda bi, i: (bi, i, 0)),
            pl.BlockSpec((1, CONV_HALO, COL_A), lambda bi, i: (bi, jnp.maximum(i * halo_blocks - 1, 0), 0)),
            _resident((CONV_K, BRANCH_D)), _resident((1, BRANCH_D)), _resident((1, BRANCH_D)),
            _resident((1, BRANCH_D)),
        ],
        out_specs=pl.BlockSpec((1, row_tile, BRANCH_D), lambda bi, i: (bi, i, 0)),
        out_shape=jax.ShapeDtypeStruct((b, s, BRANCH_D), BF16),
        scratch_shapes=[pltpu.VMEM((F32_SUBLANES, CONV_HALO + row_tile, BRANCH_D), F32)],
        compiler_params=_params("parallel", "parallel"),
        name="conformer_branch",
    )(pa, pa, conv_w, vec(conv_b), vec(ln_g), vec(ln_b))


def _rel_bucket(rel):
    half = N_REL_BUCKETS // 2
    max_exact = half // 2
    n = jnp.abs(rel)
    log_ratio = jnp.log(jnp.maximum(n, 1).astype(F32) / max_exact) / math.log(REL_MAX_DIST / max_exact)
    large = jnp.minimum(max_exact + (log_ratio * (half - max_exact)).astype(jnp.int32), half - 1)
    return jnp.where(rel > 0, half, 0) + jnp.where(n < max_exact, n, large)


FAR_BUCKET = N_REL_BUCKETS // 2 - 1
MASKED_BUCKET = -1


def _near_buckets():
    r = jnp.arange(ATTN_TILE)[:, None]
    c = jnp.arange(ATTN_TILE)[None, :]
    visible = (c // CHUNK) <= (r // CHUNK)
    diag = jnp.where(visible, _rel_bucket(c - r), MASKED_BUCKET)
    sub = _rel_bucket(c - r - ATTN_TILE)
    return jnp.stack([diag, sub]).astype(jnp.int32)


def _bias_tiles_kernel(table_ref, bucket_ref, o_ref):
    h = pl.program_id(0)
    bucket = bucket_ref[...]
    far = table_ref[FAR_BUCKET, h]
    out = jnp.zeros(bucket.shape, F32)
    for b in range(N_REL_BUCKETS):
        out = jnp.where(bucket == b, (table_ref[b, h] - far) * LOG2_E, out)
    o_ref[0] = jnp.where(bucket == MASKED_BUCKET, NEG_INF, out)


def _bias_tiles(rel_bias):
    buckets = _near_buckets()
    return pl.pallas_call(
        _bias_tiles_kernel,
        grid=(DIFF_HEADS,),
        in_specs=[pl.BlockSpec(memory_space=pltpu.SMEM), _resident(buckets.shape)],
        out_specs=pl.BlockSpec((1,) + buckets.shape, lambda h: (h, 0, 0, 0)),
        out_shape=jax.ShapeDtypeStruct((DIFF_HEADS,) + buckets.shape, F32),
        compiler_params=_params("arbitrary"),
        name="rel_bias_tiles",
    )(rel_bias.astype(F32), buckets)


def _diff_attn_kernel(q_ref, k_ref, v_ref, bias_ref, lq1_ref, lk1_ref, lq2_ref, lk2_ref, g_ref, o_ref,
                      q2_scr, s_scr, m_scr, l_scr, acc_scr, *, lambda_init):
    t = ATTN_TILE
    i = pl.program_id(1)
    heads = range(DIFF_HEADS)

    def head_cols(h):
        return slice(h * DIFF_VD, (h + 1) * DIFF_VD)

    q = q_ref[0]
    lane = lax.broadcasted_iota(jnp.int32, (t, DIFF_VD), 1)
    for h in heads:
        qh = q[:, head_cols(h)]
        q2_scr[h, :t, :] = jnp.where(lane < DIFF_DH, qh, 0).astype(BF16)
        q2_scr[h, t:, :] = jnp.where(lane >= DIFF_DH, qh, 0).astype(BF16)
    m_scr[...] = jnp.full(m_scr.shape, NEG_INF, F32)
    l_scr[...] = jnp.zeros(l_scr.shape, F32)
    acc_scr[...] = jnp.zeros(acc_scr.shape, F32)

    def logits_tiles(j0, near):
        for n, bias_index in enumerate(near):
            start = pl.multiple_of((j0 + n) * t, t)
            for h in heads:
                s = _dot_nt(q2_scr[h], k_ref[0, pl.ds(start, t), head_cols(h)])
                if bias_index is not None:
                    bias = bias_ref[h, bias_index]
                    s = s + jnp.concatenate([bias, bias], axis=0)
                s_scr[j0 + n, h] = s
                m_scr[h] = jnp.maximum(m_scr[h], jnp.maximum(s[:, :LANES], s[:, LANES:]))

    def far_pair(pair, carry):
        logits_tiles(2 * pair, (None, None))
        return carry

    n_far = jnp.maximum(i - 1, 0)
    lax.fori_loop(0, n_far // 2, far_pair, 0)

    @pl.when(n_far % 2 == 1)
    def _():
        logits_tiles(n_far - 1, (None,))

    @pl.when(i > 0)
    def _():
        logits_tiles(i - 1, (1, 0))

    @pl.when(i == 0)
    def _():
        logits_tiles(0, (0,))

    for h in heads:
        m_scr[h] = jnp.broadcast_to(jnp.max(m_scr[h], axis=-1, keepdims=True), (2 * t, LANES))

    def pv_tiles(j0, n_tiles):
        start = pl.multiple_of(j0 * t, t)
        for h in heads:
            m = m_scr[h]
            parts = [jnp.exp2(s_scr[j0 + n, h, :, half * LANES:(half + 1) * LANES] - m)
                     for n in range(n_tiles) for half in range(t // LANES)]
            l_scr[h] += functools.reduce(lambda a, b: a + b, parts)
            p = jnp.concatenate(parts, axis=1).astype(BF16)
            acc_scr[h] += _dot(p, v_ref[0, pl.ds(start, n_tiles * t), head_cols(h)])

    def pv_pair(pair, carry):
        pv_tiles(2 * pair, 2)
        return carry

    n_visible = i + 1
    lax.fori_loop(0, n_visible // 2, pv_pair, 0)

    @pl.when(n_visible % 2 == 1)
    def _():
        pv_tiles(n_visible - 1, 1)

    lam = (jnp.exp(jnp.sum(lq1_ref[...] * lk1_ref[...])) - jnp.exp(jnp.sum(lq2_ref[...] * lk2_ref[...]))
           + lambda_init)
    for h in heads:
        l = jnp.sum(l_scr[h], axis=-1, keepdims=True)
        o = acc_scr[h, :t, :] / l[:t] - lam * (acc_scr[h, t:, :] / l[t:])
        o_ref[0, :, head_cols(h)] = (_rms_norm(o, g_ref[...]) * (1.0 - lambda_init)).astype(BF16)


def _diff_attention(pq, pk, pv, bias_tiles, lam_q1, lam_k1, lam_q2, lam_k2, subln_g, lambda_init):
    b, s, width = pq.shape
    t = ATTN_TILE
    vec = lambda a: a.reshape(1, -1).astype(F32)
    all_keys = pl.BlockSpec((1, s, width), lambda bi, i: (bi, 0, 0))
    lam_spec = _resident((1, DIFF_DH))
    return pl.pallas_call(
        functools.partial(_diff_attn_kernel, lambda_init=lambda_init),
        grid=(b, s // t),
        in_specs=[
            pl.BlockSpec((1, t, width), lambda bi, i: (bi, i, 0)),
            all_keys, all_keys, _resident(bias_tiles.shape),
            lam_spec, lam_spec, lam_spec, lam_spec, _resident((1, DIFF_VD)),
        ],
        out_specs=pl.BlockSpec((1, t, width), lambda bi, i: (bi, i, 0)),
        out_shape=jax.ShapeDtypeStruct((b, s, width), BF16),
        scratch_shapes=[
            pltpu.VMEM((DIFF_HEADS, 2 * t, DIFF_VD), BF16),
            pltpu.VMEM((s // t, DIFF_HEADS, 2 * t, t), F32),
            pltpu.VMEM((DIFF_HEADS, 2 * t, LANES), F32),
            pltpu.VMEM((DIFF_HEADS, 2 * t, LANES), F32),
            pltpu.VMEM((DIFF_HEADS, 2 * t, DIFF_VD), F32),
        ],
        compiler_params=_params("parallel", "parallel"),
        name="diff_attention",
    )(pq, pk, pv, bias_tiles, vec(lam_q1), vec(lam_k1), vec(lam_q2), vec(lam_k2), vec(subln_g))


def _gmlp_kernel(pc_ref, lng_ref, lnb_ref, ws_ref, bs_ref, o_ref):
    rows = o_ref.shape[0]
    z = jax.nn.gelu(pc_ref[...].astype(F32))
    u = z[:, :BRANCH_D]
    v = _layer_norm(z[:, BRANCH_D:], lng_ref[...], lnb_ref[...]).astype(BF16)
    t_idx = lax.broadcasted_iota(jnp.int32, (GMLP_CHUNK, GMLP_CHUNK), 0)
    s_idx = lax.broadcasted_iota(jnp.int32, (GMLP_CHUNK, GMLP_CHUNK), 1)
    for g in range(GMLP_GROUPS):
        w = jnp.where(s_idx <= t_idx, ws_ref[g], 0.0).astype(BF16)
        cols = slice(g * GMLP_GD, (g + 1) * GMLP_GD)
        for r in range(0, rows, GMLP_CHUNK):
            sv = _dot(w, v[r:r + GMLP_CHUNK, cols]) + bs_ref[g]
            o_ref[r:r + GMLP_CHUNK, cols] = (u[r:r + GMLP_CHUNK, cols] * sv).astype(BF16)


def _gmlp_branch(pc, ln_g, ln_b, w_s, b_s, row_tile=ROW_TILE):
    rows = pc.shape[0]
    vec = lambda a: a.reshape(1, BRANCH_D)
    return pl.pallas_call(
        _gmlp_kernel,
        grid=(rows // row_tile,),
        in_specs=[
            pl.BlockSpec((row_tile, COL_C), lambda i: (i, 0)),
            _resident((1, BRANCH_D)), _resident((1, BRANCH_D)),
            _resident(w_s.shape), _resident((GMLP_GROUPS, GMLP_CHUNK, 1)),
        ],
        out_specs=pl.BlockSpec((row_tile, BRANCH_D), lambda i: (i, 0)),
        out_shape=jax.ShapeDtypeStruct((rows, BRANCH_D), BF16),
        compiler_params=_params("parallel"),
        name="gmlp_branch",
    )(pc, vec(ln_g), vec(ln_b), w_s, b_s.reshape(GMLP_GROUPS, GMLP_CHUNK, 1))


def _merge_kernel(x_ref, ya_ref, yb_ref, yc_ref, pg_ref, gb_ref, wbr_ref, wout_ref, o_ref):
    merged = None
    for n, y_ref in enumerate((ya_ref, yb_ref, yc_ref)):
        logits = pg_ref[:, n * D_MODEL:(n + 1) * D_MODEL].astype(F32) + gb_ref[n:n + 1, :]
        term = jax.nn.sigmoid(logits) * _dot(y_ref[...], wbr_ref[n])
        merged = term if merged is None else merged + term
    o_ref[...] = x_ref[...] + _dot(merged.astype(BF16), wout_ref[...])


def _merge(x, ya, yb, yc, pg, gate_b, w_br, w_out, row_tile=ROW_TILE):
    rows = x.shape[0]
    row_spec = lambda width: pl.BlockSpec((row_tile, width), lambda i: (i, 0))
    return pl.pallas_call(
        _merge_kernel,
        grid=(rows // row_tile,),
        in_specs=[row_spec(D_MODEL), row_spec(BRANCH_D), row_spec(BRANCH_D), row_spec(BRANCH_D), row_spec(COL_G),
                  _resident(gate_b.shape), _resident(w_br.shape), _resident(w_out.shape)],
        out_specs=row_spec(D_MODEL),
        out_shape=jax.ShapeDtypeStruct(x.shape, F32),
        compiler_params=_params("parallel"),
        name="branch_merge",
    )(x, ya, yb, yc, pg, gate_b, w_br, w_out)


def _xattn_kernel(x_ref, g_ref, wq_ref, kv_ref, wo_ref, o_ref):
    x = x_ref[0]
    q = _dot(_rms_norm(x, g_ref[...]).astype(BF16), wq_ref[...]).astype(BF16)
    heads = []
    for h in range(XATTN_HEADS):
        cols = slice(h * XATTN_DH, (h + 1) * XATTN_DH)
        k = kv_ref[0, :, cols]
        v = kv_ref[0, :, XATTN_D + h * XATTN_DH:XATTN_D + (h + 1) * XATTN_DH]
        s = _dot_nt(q[:, cols], k) * (XATTN_DH ** -0.5)
        p = jnp.exp(s - jnp.max(s, axis=-1, keepdims=True))
        p = p / jnp.sum(p, axis=-1, keepdims=True)
        heads.append(_dot(p.astype(BF16), v).astype(BF16))
    o_ref[0] = x + _dot(jnp.concatenate(heads, axis=-1), wo_ref[...])


def _mem_cross_attention(x, kv, norm_g, w_xq, w_xo, row_tile=ROW_TILE):
    b, s, d = x.shape
    m = kv.shape[1]
    return pl.pallas_call(
        _xattn_kernel,
        grid=(b, s // row_tile),
        in_specs=[
            pl.BlockSpec((1, row_tile, d), lambda bi, i: (bi, i, 0)),
            _resident((1, d)), _resident(w_xq.shape),
            pl.BlockSpec((1, m, 2 * XATTN_D), lambda bi, i: (bi, 0, 0)),
            _resident(w_xo.shape),
        ],
        out_specs=pl.BlockSpec((1, row_tile, d), lambda bi, i: (bi, i, 0)),
        out_shape=jax.ShapeDtypeStruct(x.shape, F32),
        compiler_params=_params("parallel", "parallel"),
        name="mem_cross_attention",
    )(x, norm_g.reshape(1, d), w_xq, kv, w_xo)


def _ffn_kernel(x_ref, halo_ref, g_ref, wup_ref, cw_ref, cb_ref, wdn_ref, gf_ref, o_ref,
                slab_scr, h_scr, up_scr, act_scr, acc_scr, *, final_norm):
    tile_rows = o_ref.shape[1]
    rows = acc_scr.shape[0]
    groups = rows // F32_SUBLANES
    slabs = slab_scr.shape[0]

    def lanes(j):
        return slice(j * LANES, (j + 1) * LANES)

    for j in range(slabs):
        slab_scr[j, :tile_rows, :] = x_ref[0, :, lanes(j)]
    slab_scr[:, tile_rows:, :] = jnp.zeros((slabs, rows - tile_rows, LANES), F32)
    x = jnp.concatenate(
        [jnp.concatenate([slab_scr[j, pl.ds(gi, F32_SUBLANES, stride=groups), :] for j in range(slabs)], axis=1)
         for gi in range(groups)], axis=0)

    g = g_ref[...]
    halo = _rms_norm(halo_ref[0], g)
    h_scr[:FFN_HALO, :] = jnp.where(pl.program_id(1) > 0, halo, 0.0).astype(BF16)
    h_scr[FFN_HALO:, :] = _rms_norm(x, g).astype(BF16)
    acc_scr[...] = x

    n_chunks = D_FF // FFN_COL_CHUNK
    first_sublane = lax.broadcasted_iota(jnp.int32, (F32_SUBLANES, FFN_COL_CHUNK), 0) == 0

    def cols(c, half):
        start = half * D_FF + c * FFN_COL_CHUNK
        return slice(start, start + FFN_COL_CHUNK)

    def up_proj(c, slot):
        h = h_scr[...]
        for half in range(2):
            up = up_scr.at[slot, half]
            up[...] = _dot(h, wup_ref[:, cols(c, half)])
            prev_tile = up[FFN_HALO - F32_SUBLANES:FFN_HALO, :]
            for shift in range(1, FFN_K):
                wrapped = up[FFN_HALO + rows - shift * F32_SUBLANES:FFN_HALO + rows - (shift - 1) * F32_SUBLANES, :]
                block = jnp.where(first_sublane, pltpu.roll(prev_tile, shift, 0), pltpu.roll(wrapped, 1, 0))
                up[FFN_HALO - shift * F32_SUBLANES:FFN_HALO - (shift - 1) * F32_SUBLANES, :] = block

    def conv(c, slot, half):
        out = cb_ref[:, cols(c, half)]
        for k in range(FFN_K):
            window = up_scr[slot, half, k * F32_SUBLANES:k * F32_SUBLANES + rows, :]
            out = out + cw_ref[k:k + 1, cols(c, half)] * window
        return out

    def gate(c, slot):
        act_scr[slot] = (jax.nn.silu(conv(c, slot, 0)) * conv(c, slot, 1)).astype(BF16)

    def down_proj(c, slot):
        acc_scr[...] += _dot(act_scr[slot], wdn_ref[cols(c, 0), :])

    def stage(c, slot):
        up_proj(c + 1, 1 - slot)
        down_proj(c - 1, 1 - slot)
        gate(c, slot)

    up_proj(0, 0)
    up_proj(1, 1)
    gate(0, 0)
    for c in range(1, n_chunks - 1):
        stage(c, c % 2)
    down_proj(n_chunks - 2, (n_chunks - 2) % 2)
    gate(n_chunks - 1, (n_chunks - 1) % 2)
    down_proj(n_chunks - 1, (n_chunks - 1) % 2)

    for gi in range(groups):
        out = acc_scr[gi * F32_SUBLANES:(gi + 1) * F32_SUBLANES, :]
        if final_norm:
            out = _rms_norm(out, gf_ref[...])
        for j in range(slabs):
            slab_scr[j, pl.ds(gi, F32_SUBLANES, stride=groups), :] = out[:, lanes(j)]
    for j in range(slabs):
        o_ref[0, :, lanes(j)] = slab_scr[j, :tile_rows, :]


def _conv_ffn(x, norm_g, w_up, conv_w, conv_b, w_down, final_g, final_norm, row_tile=FFN_ROW_TILE):
    b, s, d = x.shape
    halo_blocks = row_tile // FFN_HALO
    conv_b = conv_b.reshape(1, 2 * D_FF)
    perm_rows = row_tile + 2 * F32_SUBLANES
    return pl.pallas_call(
        functools.partial(_ffn_kernel, final_norm=final_norm),
        grid=(b, s // row_tile),
        in_specs=[
            pl.BlockSpec((1, row_tile, d), lambda bi, i: (bi, i, 0)),
            pl.BlockSpec((1, FFN_HALO, d), lambda bi, i: (bi, jnp.maximum(i * halo_blocks - 1, 0), 0)),
            _resident((1, d)), _resident(w_up.shape), _resident(conv_w.shape), _resident(conv_b.shape),
            _resident(w_down.shape), _resident((1, d)),
        ],
        out_specs=pl.BlockSpec((1, row_tile, d), lambda bi, i: (bi, i, 0)),
        out_shape=jax.ShapeDtypeStruct(x.shape, F32),
        scratch_shapes=[
            pltpu.VMEM((d // LANES, perm_rows, LANES), F32),
            pltpu.VMEM((FFN_HALO + perm_rows, d), BF16),
            pltpu.VMEM((2, 2, FFN_HALO + perm_rows, FFN_COL_CHUNK), F32),
            pltpu.VMEM((2, perm_rows, FFN_COL_CHUNK), BF16),
            pltpu.VMEM((perm_rows, d), F32),
        ],
        compiler_params=_params("parallel", "parallel"),
        name="conv_ffn",
    )(x, x, norm_g.reshape(1, d), w_up, conv_w, conv_b, w_down, final_g.reshape(1, d))


def kernel(x, mem, rel_bias, norm_mix_g, w_in, gate_b, conv_w, conv_b, conv_ln_g, conv_ln_b, lam_q1, lam_k1, lam_q2, lam_k2, subln_g, gmlp_ln_g, gmlp_ln_b, w_s, b_s, w_br, w_out, norm_xattn_g, norm_mem_g, w_xq, w_xkv, w_xo, norm_ffn_g, w_up, ffn_conv_w, ffn_conv_b, w_down, norm_final_g):
    b, s, d = x.shape
    m = mem.shape[1]
    rows = b * s
    bias_tiles = _bias_tiles(rel_bias)
    for l in range(DEPTH):
        lambda_init = 0.8 - 0.6 * math.exp(-0.3 * l)
        pa, pq, pk, pv, pc, pg = _norm_proj(x.reshape(rows, d), norm_mix_g[l], w_in[l].astype(BF16),
                                            IN_SPLIT_WIDTHS, IN_SPLIT_SCALES)
        ya = _conformer_branch(pa.reshape(b, s, COL_A), conv_w[l], conv_b[l], conv_ln_g[l], conv_ln_b[l])
        yb = _diff_attention(pq.reshape(b, s, COL_QK), pk.reshape(b, s, COL_QK), pv.reshape(b, s, COL_V),
                             bias_tiles, lam_q1[l], lam_k1[l], lam_q2[l], lam_k2[l], subln_g[l], lambda_init)
        yc = _gmlp_branch(pc, gmlp_ln_g[l], gmlp_ln_b[l], w_s[l], b_s[l])
        x = _merge(x.reshape(rows, d), ya.reshape(rows, BRANCH_D), yb.reshape(rows, BRANCH_D), yc, pg,
                   gate_b[l], w_br[l].astype(BF16), w_out[l].astype(BF16)).reshape(b, s, d)
        (kv,) = _norm_proj(mem.reshape(b * m, d), norm_mem_g[l], w_xkv[l].astype(BF16), (2 * XATTN_D,))
        x = _mem_cross_attention(x, kv.reshape(b, m, 2 * XATTN_D), norm_xattn_g[l], w_xq[l].astype(BF16),
                                 w_xo[l].astype(BF16))
        x = _conv_ffn(x, norm_ffn_g[l], w_up[l].astype(BF16), ffn_conv_w[l], ffn_conv_b[l],
                      w_down[l].astype(BF16), norm_final_g, final_norm=(l == DEPTH - 1))
    return x
```

```python
import functools
import math

import jax
import jax.numpy as jnp
from jax import lax
from jax.experimental import pallas as pl
from jax.experimental.pallas import tpu as pltpu

F32 = jnp.float32
BF16 = jnp.bfloat16

D_MODEL = 1024
DEPTH = 2
CHUNK = 64
BRANCH_D = D_MODEL // 2
N_BRANCH = 3
CONV_K = 31
DIFF_HEADS = 4
DIFF_DH = D_MODEL // 16
DIFF_VD = 2 * DIFF_DH
GMLP_GROUPS = 4
GMLP_GD = BRANCH_D // GMLP_GROUPS
GMLP_CHUNK = 128
N_REL_BUCKETS = 32
REL_MAX_DIST = 128
XATTN_HEADS = 4
XATTN_DH = D_MODEL // 8
XATTN_D = XATTN_HEADS * XATTN_DH
D_FF = 2816
FFN_K = 3
EPS = 1e-6
NEG_INF = -1e30

COL_A = 2 * BRANCH_D
COL_QK = DIFF_HEADS * 2 * DIFF_DH
COL_V = DIFF_HEADS * DIFF_VD
COL_C = 2 * BRANCH_D
COL_G = N_BRANCH * D_MODEL
IN_SPLIT_WIDTHS = (COL_A, COL_QK, COL_QK, COL_V, COL_C, COL_G)
LOG2_E = math.log2(math.e)
IN_SPLIT_SCALES = (1.0, DIFF_DH ** -0.5 * LOG2_E, 1.0, 1.0, 1.0, 1.0)

LANES = 128
MXU_WIDTH = 256
F32_SUBLANES = 8
BF16_SUBLANES = 16
VMEM_LIMIT_BYTES = 56 * 1024 * 1024

ROW_TILE = 512
FFN_ROW_TILE = 1024
PROJ_COL_CHUNK = 512
ATTN_TILE = 256
CONV_HALO = 32
CONV_ROW_CHUNK = 64
FFN_HALO = BF16_SUBLANES
FFN_COL_CHUNK = MXU_WIDTH

assert CONV_HALO >= CONV_K - 1 and FFN_HALO == F32_SUBLANES * (FFN_K - 1)
assert D_FF % FFN_COL_CHUNK == 0
assert ATTN_TILE % CHUNK == 0 and ATTN_TILE > REL_MAX_DIST


def _params(*semantics):
    return pltpu.CompilerParams(dimension_semantics=semantics, vmem_limit_bytes=VMEM_LIMIT_BYTES)


def _resident(shape):
    zeros = (0,) * len(shape)
    return pl.BlockSpec(shape, lambda *_: zeros, pipeline_mode=pl.Buffered(1))


def _rms_norm(x, g):
    return x * lax.rsqrt(jnp.mean(x * x, axis=-1, keepdims=True) + EPS) * g


def _layer_norm(x, g, b):
    mu = jnp.mean(x, axis=-1, keepdims=True)
    xc = x - mu
    var = jnp.mean(xc * xc, axis=-1, keepdims=True)
    return xc * lax.rsqrt(var + EPS) * g + b


def _dot(a, b):
    return jnp.dot(a, b, preferred_element_type=F32)


def _dot_nt(a, b):
    return lax.dot_general(a, b, (((1,), (1,)), ((), ())), preferred_element_type=F32)


def _norm_proj_kernel(x_ref, g_ref, w_ref, *out_refs, widths, scales):
    xn = _rms_norm(x_ref[...], g_ref[...]).astype(BF16)
    col = 0
    for o_ref, width, scale in zip(out_refs, widths, scales):
        for c in range(0, width, PROJ_COL_CHUNK):
            out = _dot(xn, w_ref[:, col + c:col + c + PROJ_COL_CHUNK])
            if scale != 1.0:
                out = out * scale
            o_ref[:, c:c + PROJ_COL_CHUNK] = out.astype(BF16)
        col += width


def _norm_proj(x, g, w, widths, scales=None, row_tile=ROW_TILE):
    rows, d = x.shape
    scales = scales or (1.0,) * len(widths)
    assert w.shape == (d, sum(widths)) and all(wd % PROJ_COL_CHUNK == 0 for wd in widths)
    return pl.pallas_call(
        functools.partial(_norm_proj_kernel, widths=widths, scales=scales),
        grid=(rows // row_tile,),
        in_specs=[pl.BlockSpec((row_tile, d), lambda i: (i, 0)), _resident((1, d)), _resident(w.shape)],
        out_specs=[pl.BlockSpec((row_tile, wd), lambda i: (i, 0)) for wd in widths],
        out_shape=[jax.ShapeDtypeStruct((rows, wd), BF16) for wd in widths],
        compiler_params=_params("parallel"),
        name="norm_proj",
    )(x, g.reshape(1, d), w)


def _mixer_in_kernel(x_ref, g_ref, w_ref, cw_ref, cb_ref, lng_ref, lnb_ref,
                     ya_ref, pq_ref, pk_ref, pv_ref, pc_ref, pg_ref, xn_scr, xs_scr, halo_scr):
    rows = ya_ref.shape[1]
    xn_scr[...] = _rms_norm(x_ref[0], g_ref[...]).astype(BF16)

    def proj(col, width):
        return _dot(xn_scr[...], w_ref[:, col:col + width])

    glu = proj(0, BRANCH_D) * jax.nn.sigmoid(proj(BRANCH_D, BRANCH_D))

    @pl.when(pl.program_id(1) == 0)
    def _():
        halo_scr[...] = jnp.zeros(halo_scr.shape, F32)

    xs_scr[0, :CONV_HALO, :] = halo_scr[...]
    xs_scr[0, CONV_HALO:, :] = glu
    halo_scr[...] = glu[rows - CONV_HALO:, :]

    def shifted_copies():
        shifted_rows = CONV_HALO + rows - F32_SUBLANES
        for b in range(1, F32_SUBLANES):
            xs_scr[b, :shifted_rows, :] = xs_scr[0, b:b + shifted_rows, :]

    first_tap = CONV_HALO - (CONV_K - 1)

    def conv_rows(r):
        acc = jnp.broadcast_to(cb_ref[...], (CONV_ROW_CHUNK, BRANCH_D))
        for k in range(CONV_K):
            b = (first_tap + k) % F32_SUBLANES
            start = r + first_tap + k - b
            acc = acc + cw_ref[k:k + 1, :] * xs_scr[b, start:start + CONV_ROW_CHUNK, :]
        ya_ref[0, r:r + CONV_ROW_CHUNK, :] = jax.nn.silu(_layer_norm(acc, lng_ref[...], lnb_ref[...])).astype(BF16)

    def proj_chunk(o_ref, col, c, scale):
        out = proj(col + c, PROJ_COL_CHUNK)
        if scale != 1.0:
            out = out * scale
        o_ref[0, :, c:c + PROJ_COL_CHUNK] = out.astype(BF16)

    matmul_steps = []
    col = COL_A
    for o_ref, width, scale in zip((pq_ref, pk_ref, pv_ref, pc_ref, pg_ref), IN_SPLIT_WIDTHS[1:], IN_SPLIT_SCALES[1:]):
        matmul_steps += [functools.partial(proj_chunk, o_ref, col, c, scale) for c in range(0, width, PROJ_COL_CHUNK)]
        col += width
    vector_steps = [shifted_copies] + [functools.partial(conv_rows, r) for r in range(0, rows, CONV_ROW_CHUNK)]
    assert len(vector_steps) <= len(matmul_steps)
    for n in range(len(vector_steps)):
        last = n == len(vector_steps) - 1

        @pl.when(pl.program_id(1) > -1 - n)
        def _():
            for step in matmul_steps[n:] if last else matmul_steps[n:n + 1]:
                step()
            vector_steps[n]()


def _mixer_in_proj(x, norm_g, w_in, conv_w, conv_b, ln_g, ln_b, row_tile=ROW_TILE):
    b, s, d = x.shape
    vec = lambda a: a.reshape(1, -1)
    out_widths = (BRANCH_D,) + IN_SPLIT_WIDTHS[1:]
    return pl.pallas_call(
        _mixer_in_kernel,
        grid=(b, s // row_tile),
        in_specs=[
            pl.BlockSpec((1, row_tile, d), lambda bi, i: (bi, i, 0)),
            _resident((1, d)), _resident(w_in.shape),
            _resident((CONV_K, BRANCH_D)), _resident((1, BRANCH_D)), _resident((1, BRANCH_D)),
            _resident((1, BRANCH_D)),
        ],
        out_specs=[pl.BlockSpec((1, row_tile, wd), lambda bi, i: (bi, i, 0)) for wd in out_widths],
        out_shape=[jax.ShapeDtypeStruct((b, s, wd), BF16) for wd in out_widths],
        scratch_shapes=[
            pltpu.VMEM((row_tile, d), BF16),
            pltpu.VMEM((F32_SUBLANES, CONV_HALO + row_tile, BRANCH_D), F32),
            pltpu.VMEM((CONV_HALO, BRANCH_D), F32),
        ],
        compiler_params=_params("parallel", "arbitrary"),
        name="mixer_in_proj",
    )(x, vec(norm_g), w_in, conv_w, vec(conv_b), vec(ln_g), vec(ln_b))


def _rel_bucket(rel):
    half = N_REL_BUCKETS // 2
    max_exact = half // 2
    n = jnp.abs(rel)
    log_ratio = jnp.log(jnp.maximum(n, 1).astype(F32) / max_exact) / math.log(REL_MAX_DIST / max_exact)
    large = jnp.minimum(max_exact + (log_ratio * (half - max_exact)).astype(jnp.int32), half - 1)
    return jnp.where(rel > 0, half, 0) + jnp.where(n < max_exact, n, large)


FAR_BUCKET = N_REL_BUCKETS // 2 - 1
MASKED_BUCKET = -1


def _near_buckets():
    r = jnp.arange(ATTN_TILE)[:, None]
    c = jnp.arange(ATTN_TILE)[None, :]
    visible = (c // CHUNK) <= (r // CHUNK)
    diag = jnp.where(visible, _rel_bucket(c - r), MASKED_BUCKET)
    sub = _rel_bucket(c - r - ATTN_TILE)
    return jnp.stack([diag, sub]).astype(jnp.int32)


def _bias_tiles_kernel(table_ref, bucket_ref, o_ref):
    h = pl.program_id(0)
    bucket = bucket_ref[...]
    far = table_ref[FAR_BUCKET, h]
    out = jnp.zeros(bucket.shape, F32)
    for b in range(N_REL_BUCKETS):
        out = jnp.where(bucket == b, (table_ref[b, h] - far) * LOG2_E, out)
    o_ref[0] = jnp.where(bucket == MASKED_BUCKET, NEG_INF, out)


def _bias_tiles(rel_bias):
    buckets = _near_buckets()
    return pl.pallas_call(
        _bias_tiles_kernel,
        grid=(DIFF_HEADS,),
        in_specs=[pl.BlockSpec(memory_space=pltpu.SMEM), _resident(buckets.shape)],
        out_specs=pl.BlockSpec((1,) + buckets.shape, lambda h: (h, 0, 0, 0)),
        out_shape=jax.ShapeDtypeStruct((DIFF_HEADS,) + buckets.shape, F32),
        compiler_params=_params("arbitrary"),
        name="rel_bias_tiles",
    )(rel_bias.astype(F32), buckets)


def _diff_attn_kernel(q_ref, k_ref, v_ref, bias_ref, lq1_ref, lk1_ref, lq2_ref, lk2_ref, g_ref, o_ref,
                      q2_scr, s_scr, m_scr, l_scr, acc_scr, *, lambda_init):
    t = ATTN_TILE
    i = pl.program_id(1)
    heads = range(DIFF_HEADS)

    def head_cols(h):
        return slice(h * DIFF_VD, (h + 1) * DIFF_VD)

    q = q_ref[0]
    lane = lax.broadcasted_iota(jnp.int32, (t, DIFF_VD), 1)
    for h in heads:
        qh = q[:, head_cols(h)]
        q2_scr[h, :t, :] = jnp.where(lane < DIFF_DH, qh, 0).astype(BF16)
        q2_scr[h, t:, :] = jnp.where(lane >= DIFF_DH, qh, 0).astype(BF16)
    m_scr[...] = jnp.full(m_scr.shape, NEG_INF, F32)
    l_scr[...] = jnp.zeros(l_scr.shape, F32)
    acc_scr[...] = jnp.zeros(acc_scr.shape, F32)

    def logits_tiles(j0, near):
        for n, bias_index in enumerate(near):
            start = pl.multiple_of((j0 + n) * t, t)
            for h in heads:
                s = _dot_nt(q2_scr[h], k_ref[0, pl.ds(start, t), head_cols(h)])
                if bias_index is not None:
                    bias = bias_ref[h, bias_index]
                    s = s + jnp.concatenate([bias, bias], axis=0)
                s_scr[j0 + n, h] = s
                m_scr[h] = jnp.maximum(m_scr[h], jnp.maximum(s[:, :LANES], s[:, LANES:]))

    def far_pair(pair, carry):
        logits_tiles(2 * pair, (None, None))
        return carry

    n_far = jnp.maximum(i - 1, 0)
    lax.fori_loop(0, n_far // 2, far_pair, 0)

    @pl.when(n_far % 2 == 1)
    def _():
        logits_tiles(n_far - 1, (None,))

    @pl.when(i > 0)
    def _():
        logits_tiles(i - 1, (1, 0))

    @pl.when(i == 0)
    def _():
        logits_tiles(0, (0,))

    for h in heads:
        m_scr[h] = jnp.broadcast_to(jnp.max(m_scr[h], axis=-1, keepdims=True), (2 * t, LANES))

    def pv_tiles(j0, n_tiles):
        start = pl.multiple_of(j0 * t, t)
        for h in heads:
            m = m_scr[h]
            parts = [jnp.exp2(s_scr[j0 + n, h, :, half * LANES:(half + 1) * LANES] - m)
                     for n in range(n_tiles) for half in range(t // LANES)]
            l_scr[h] += functools.reduce(lambda a, b: a + b, parts)
            p = jnp.concatenate(parts, axis=1).astype(BF16)
            acc_scr[h] += _dot(p, v_ref[0, pl.ds(start, n_tiles * t), head_cols(h)])

    def pv_pair(pair, carry):
        pv_tiles(2 * pair, 2)
        return carry

    n_visible = i + 1
    lax.fori_loop(0, n_visible // 2, pv_pair, 0)

    @pl.when(n_visible % 2 == 1)
    def _():
        pv_tiles(n_visible - 1, 1)

    lam = (jnp.exp(jnp.sum(lq1_ref[...] * lk1_ref[...])) - jnp.exp(jnp.sum(lq2_ref[...] * lk2_ref[...]))
           + lambda_init)
    for h in heads:
        l = jnp.sum(l_scr[h], axis=-1, keepdims=True)
        o = acc_scr[h, :t, :] / l[:t] - lam * (acc_scr[h, t:, :] / l[t:])
        o_ref[0, :, head_cols(h)] = (_rms_norm(o, g_ref[...]) * (1.0 - lambda_init)).astype(BF16)


def _diff_attention(pq, pk, pv, bias_tiles, lam_q1, lam_k1, lam_q2, lam_k2, subln_g, lambda_init):
    b, s, width = pq.shape
    t = ATTN_TILE
    vec = lambda a: a.reshape(1, -1).astype(F32)
    all_keys = pl.BlockSpec((1, s, width), lambda bi, i: (bi, 0, 0))
    lam_spec = _resident((1, DIFF_DH))
    return pl.pallas_call(
        functools.partial(_diff_attn_kernel, lambda_init=lambda_init),
        grid=(b, s // t),
        in_specs=[
            pl.BlockSpec((1, t, width), lambda bi, i: (bi, i, 0)),
            all_keys, all_keys, _resident(bias_tiles.shape),
            lam_spec, lam_spec, lam_spec, lam_spec, _resident((1, DIFF_VD)),
        ],
        out_specs=pl.BlockSpec((1, t, width), lambda bi, i: (bi, i, 0)),
        out_shape=jax.ShapeDtypeStruct((b, s, width), BF16),
        scratch_shapes=[
            pltpu.VMEM((DIFF_HEADS, 2 * t, DIFF_VD), BF16),
            pltpu.VMEM((s // t, DIFF_HEADS, 2 * t, t), F32),
            pltpu.VMEM((DIFF_HEADS, 2 * t, LANES), F32),
            pltpu.VMEM((DIFF_HEADS, 2 * t, LANES), F32),
            pltpu.VMEM((DIFF_HEADS, 2 * t, DIFF_VD), F32),
        ],
        compiler_params=_params("parallel", "parallel"),
        name="diff_attention",
    )(pq, pk, pv, bias_tiles, vec(lam_q1), vec(lam_k1), vec(lam_q2), vec(lam_k2), vec(subln_g))


def _gmlp_kernel(pc_ref, lng_ref, lnb_ref, ws_ref, bs_ref, o_ref):
    rows = o_ref.shape[0]
    z = jax.nn.gelu(pc_ref[...].astype(F32))
    u = z[:, :BRANCH_D]
    v = _layer_norm(z[:, BRANCH_D:], lng_ref[...], lnb_ref[...]).astype(BF16)
    t_idx = lax.broadcasted_iota(jnp.int32, (GMLP_CHUNK, GMLP_CHUNK), 0)
    s_idx = lax.broadcasted_iota(jnp.int32, (GMLP_CHUNK, GMLP_CHUNK), 1)
    for g in range(GMLP_GROUPS):
        w = jnp.where(s_idx <= t_idx, ws_ref[g], 0.0).astype(BF16)
        cols = slice(g * GMLP_GD, (g + 1) * GMLP_GD)
        for r in range(0, rows, GMLP_CHUNK):
            sv = _dot(w, v[r:r + GMLP_CHUNK, cols]) + bs_ref[g]
            o_ref[r:r + GMLP_CHUNK, cols] = (u[r:r + GMLP_CHUNK, cols] * sv).astype(BF16)


def _gmlp_branch(pc, ln_g, ln_b, w_s, b_s, row_tile=ROW_TILE):
    rows = pc.shape[0]
    vec = lambda a: a.reshape(1, BRANCH_D)
    return pl.pallas_call(
        _gmlp_kernel,
        grid=(rows // row_tile,),
        in_specs=[
            pl.BlockSpec((row_tile, COL_C), lambda i: (i, 0)),
            _resident((1, BRANCH_D)), _resident((1, BRANCH_D)),
            _resident(w_s.shape), _resident((GMLP_GROUPS, GMLP_CHUNK, 1)),
        ],
        out_specs=pl.BlockSpec((row_tile, BRANCH_D), lambda i: (i, 0)),
        out_shape=jax.ShapeDtypeStruct((rows, BRANCH_D), BF16),
        compiler_params=_params("parallel"),
        name="gmlp_branch",
    )(pc, vec(ln_g), vec(ln_b), w_s, b_s.reshape(GMLP_GROUPS, GMLP_CHUNK, 1))


def _merge_kernel(x_ref, ya_ref, yb_ref, yc_ref, pg_ref, gb_ref, wbr_ref, wout_ref, o_ref):
    merged = None
    for n, y_ref in enumerate((ya_ref, yb_ref, yc_ref)):
        logits = pg_ref[:, n * D_MODEL:(n + 1) * D_MODEL].astype(F32) + gb_ref[n:n + 1, :]
        term = jax.nn.sigmoid(logits) * _dot(y_ref[...], wbr_ref[n])
        merged = term if merged is None else merged + term
    o_ref[...] = x_ref[...] + _dot(merged.astype(BF16), wout_ref[...])


def _merge(x, ya, yb, yc, pg, gate_b, w_br, w_out, row_tile=ROW_TILE):
    rows = x.shape[0]
    row_spec = lambda width: pl.BlockSpec((row_tile, width), lambda i: (i, 0))
    return pl.pallas_call(
        _merge_kernel,
        grid=(rows // row_tile,),
        in_specs=[row_spec(D_MODEL), row_spec(BRANCH_D), row_spec(BRANCH_D), row_spec(BRANCH_D), row_spec(COL_G),
                  _resident(gate_b.shape), _resident(w_br.shape), _resident(w_out.shape)],
        out_specs=row_spec(D_MODEL),
        out_shape=jax.ShapeDtypeStruct(x.shape, F32),
        compiler_params=_params("parallel"),
        name="branch_merge",
    )(x, ya, yb, yc, pg, gate_b, w_br, w_out)


def _xattn_kernel(x_ref, g_ref, wq_ref, kv_ref, wo_ref, o_ref):
    x = x_ref[0]
    q = _dot(_rms_norm(x, g_ref[...]).astype(BF16), wq_ref[...]).astype(BF16)
    heads = []
    for h in range(XATTN_HEADS):
        cols = slice(h * XATTN_DH, (h + 1) * XATTN_DH)
        k = kv_ref[0, :, cols]
        v = kv_ref[0, :, XATTN_D + h * XATTN_DH:XATTN_D + (h + 1) * XATTN_DH]
        s = _dot_nt(q[:, cols], k) * (XATTN_DH ** -0.5)
        p = jnp.exp(s - jnp.max(s, axis=-1, keepdims=True))
        p = p / jnp.sum(p, axis=-1, keepdims=True)
        heads.append(_dot(p.astype(BF16), v).astype(BF16))
    o_ref[0] = x + _dot(jnp.concatenate(heads, axis=-1), wo_ref[...])


def _mem_cross_attention(x, kv, norm_g, w_xq, w_xo, row_tile=ROW_TILE):
    b, s, d = x.shape
    m = kv.shape[1]
    return pl.pallas_call(
        _xattn_kernel,
        grid=(b, s // row_tile),
        in_specs=[
            pl.BlockSpec((1, row_tile, d), lambda bi, i: (bi, i, 0)),
            _resident((1, d)), _resident(w_xq.shape),
            pl.BlockSpec((1, m, 2 * XATTN_D), lambda bi, i: (bi, 0, 0)),
            _resident(w_xo.shape),
        ],
        out_specs=pl.BlockSpec((1, row_tile, d), lambda bi, i: (bi, i, 0)),
        out_shape=jax.ShapeDtypeStruct(x.shape, F32),
        compiler_params=_params("parallel", "parallel"),
        name="mem_cross_attention",
    )(x, norm_g.reshape(1, d), w_xq, kv, w_xo)


def _ffn_kernel(x_ref, halo_ref, g_ref, wup_ref, cw_ref, cb_ref, wdn_ref, gf_ref, o_ref,
                slab_scr, h_scr, up_scr, act_scr, acc_scr, *, final_norm):
    tile_rows = o_ref.shape[1]
    rows = acc_scr.shape[0]
    groups = rows // F32_SUBLANES
    slabs = slab_scr.shape[0]

    def lanes(j):
        return slice(j * LANES, (j + 1) * LANES)

    for j in range(slabs):
        slab_scr[j, :tile_rows, :] = x_ref[0, :, lanes(j)]
    slab_scr[:, tile_rows:, :] = jnp.zeros((slabs, rows - tile_rows, LANES), F32)
    x = jnp.concatenate(
        [jnp.concatenate([slab_scr[j, pl.ds(gi, F32_SUBLANES, stride=groups), :] for j in range(slabs)], axis=1)
         for gi in range(groups)], axis=0)

    g = g_ref[...]
    halo = _rms_norm(halo_ref[0], g)
    h_scr[:FFN_HALO, :] = jnp.where(pl.program_id(1) > 0, halo, 0.0).astype(BF16)
    h_scr[FFN_HALO:, :] = _rms_norm(x, g).astype(BF16)
    acc_scr[...] = x

    n_chunks = D_FF // FFN_COL_CHUNK
    first_sublane = lax.broadcasted_iota(jnp.int32, (F32_SUBLANES, FFN_COL_CHUNK), 0) == 0

    def cols(c, half):
        start = half * D_FF + c * FFN_COL_CHUNK
        return slice(start, start + FFN_COL_CHUNK)

    def up_proj(c, slot):
        h = h_scr[...]
        for half in range(2):
            up = up_scr.at[slot, half]
            up[...] = _dot(h, wup_ref[:, cols(c, half)])
            prev_tile = up[FFN_HALO - F32_SUBLANES:FFN_HALO, :]
            for shift in range(1, FFN_K):
                wrapped = up[FFN_HALO + rows - shift * F32_SUBLANES:FFN_HALO + rows - (shift - 1) * F32_SUBLANES, :]
                block = jnp.where(first_sublane, pltpu.roll(prev_tile, shift, 0), pltpu.roll(wrapped, 1, 0))
                up[FFN_HALO - shift * F32_SUBLANES:FFN_HALO - (shift - 1) * F32_SUBLANES, :] = block

    def conv(c, slot, half):
        out = cb_ref[:, cols(c, half)]
        for k in range(FFN_K):
            window = up_scr[slot, half, k * F32_SUBLANES:k * F32_SUBLANES + rows, :]
            out = out + cw_ref[k:k + 1, cols(c, half)] * window
        return out

    def gate(c, slot):
        act_scr[slot] = (jax.nn.silu(conv(c, slot, 0)) * conv(c, slot, 1)).astype(BF16)

    def down_proj(c, slot):
        acc_scr[...] += _dot(act_scr[slot], wdn_ref[cols(c, 0), :])

    def stage(c, slot):
        up_proj(c + 1, 1 - slot)
        down_proj(c - 1, 1 - slot)
        gate(c, slot)

    up_proj(0, 0)
    up_proj(1, 1)
    gate(0, 0)
    for c in range(1, n_chunks - 1):
        stage(c, c % 2)
    down_proj(n_chunks - 2, (n_chunks - 2) % 2)
    gate(n_chunks - 1, (n_chunks - 1) % 2)
    down_proj(n_chunks - 1, (n_chunks - 1) % 2)

    for gi in range(groups):
        out = acc_scr[gi * F32_SUBLANES:(gi + 1) * F32_SUBLANES, :]
        if final_norm:
            out = _rms_norm(out, gf_ref[...])
        for j in range(slabs):
            slab_scr[j, pl.ds(gi, F32_SUBLANES, stride=groups), :] = out[:, lanes(j)]
    for j in range(slabs):
        o_ref[0, :, lanes(j)] = slab_scr[j, :tile_rows, :]


def _conv_ffn(x, norm_g, w_up, conv_w, conv_b, w_down, final_g, final_norm, row_tile=FFN_ROW_TILE):
    b, s, d = x.shape
    halo_blocks = row_tile // FFN_HALO
    conv_b = conv_b.reshape(1, 2 * D_FF)
    perm_rows = row_tile + 2 * F32_SUBLANES
    return pl.pallas_call(
        functools.partial(_ffn_kernel, final_norm=final_norm),
        grid=(b, s // row_tile),
        in_specs=[
            pl.BlockSpec((1, row_tile, d), lambda bi, i: (bi, i, 0)),
            pl.BlockSpec((1, FFN_HALO, d), lambda bi, i: (bi, jnp.maximum(i * halo_blocks - 1, 0), 0)),
            _resident((1, d)), _resident(w_up.shape), _resident(conv_w.shape), _resident(conv_b.shape),
            _resident(w_down.shape), _resident((1, d)),
        ],
        out_specs=pl.BlockSpec((1, row_tile, d), lambda bi, i: (bi, i, 0)),
        out_shape=jax.ShapeDtypeStruct(x.shape, F32),
        scratch_shapes=[
            pltpu.VMEM((d // LANES, perm_rows, LANES), F32),
            pltpu.VMEM((FFN_HALO + perm_rows, d), BF16),
            pltpu.VMEM((2, 2, FFN_HALO + perm_rows, FFN_COL_CHUNK), F32),
            pltpu.VMEM((2, perm_rows, FFN_COL_CHUNK), BF16),
            pltpu.VMEM((perm_rows, d), F32),
        ],
        compiler_params=_params("parallel", "parallel"),
        name="conv_ffn",
    )(x, x, norm_g.reshape(1, d), w_up, conv_w, conv_b, w_down, final_g.reshape(1, d))


def kernel(x, mem, rel_bias, norm_mix_g, w_in, gate_b, conv_w, conv_b, conv_ln_g, conv_ln_b, lam_q1, lam_k1, lam_q2, lam_k2, subln_g, gmlp_ln_g, gmlp_ln_b, w_s, b_s, w_br, w_out, norm_xattn_g, norm_mem_g, w_xq, w_xkv, w_xo, norm_ffn_g, w_up, ffn_conv_w, ffn_conv_b, w_down, norm_final_g):
    b, s, d = x.shape
    m = mem.shape[1]
    rows = b * s
    bias_tiles = _bias_tiles(rel_bias)
    for l in range(DEPTH):
        lambda_init = 0.8 - 0.6 * math.exp(-0.3 * l)
        ya, pq, pk, pv, pc, pg = _mixer_in_proj(x, norm_mix_g[l], w_in[l].astype(BF16), conv_w[l], conv_b[l],
                                                conv_ln_g[l], conv_ln_b[l])
        yb = _diff_attention(pq, pk, pv, bias_tiles, lam_q1[l], lam_k1[l], lam_q2[l], lam_k2[l], subln_g[l],
                             lambda_init)
        yc = _gmlp_branch(pc.reshape(rows, COL_C), gmlp_ln_g[l], gmlp_ln_b[l], w_s[l], b_s[l])
        x = _merge(x.reshape(rows, d), ya.reshape(rows, BRANCH_D), yb.reshape(rows, BRANCH_D), yc,
                   pg.reshape(rows, COL_G), gate_b[l], w_br[l].astype(BF16), w_out[l].astype(BF16)).reshape(b, s, d)
        (kv,) = _norm_proj(mem.reshape(b * m, d), norm_mem_g[l], w_xkv[l].astype(BF16), (2 * XATTN_D,))
        x = _mem_cross_attention(x, kv.reshape(b, m, 2 * XATTN_D), norm_xattn_g[l], w_xq[l].astype(BF16),
                                 w_xo[l].astype(BF16))
        x = _conv_ffn(x, norm_ffn_g[l], w_up[l].astype(BF16), ffn_conv_w[l], ffn_conv_b[l],
                      w_down[l].astype(BF16), norm_final_g, final_norm=(l == DEPTH - 1))
    return x
```

```python
import functools
import math
from typing import NamedTuple

import jax
import jax.numpy as jnp
from jax import lax
from jax.experimental import pallas as pl
from jax.experimental.pallas import tpu as pltpu

F32 = jnp.float32
BF16 = jnp.bfloat16

D_MODEL = 1024
DEPTH = 2
CHUNK = 64
BRANCH_D = D_MODEL // 2
N_BRANCH = 3
CONV_K = 31
DIFF_HEADS = 4
DIFF_DH = D_MODEL // 16
DIFF_VD = 2 * DIFF_DH
GMLP_GROUPS = 4
GMLP_GD = BRANCH_D // GMLP_GROUPS
GMLP_CHUNK = 128
N_REL_BUCKETS = 32
REL_MAX_DIST = 128
XATTN_HEADS = 4
XATTN_DH = D_MODEL // 8
XATTN_D = XATTN_HEADS * XATTN_DH
D_FF = 2816
FFN_K = 3
EPS = 1e-6
NEG_INF = -1e30

COL_A = 2 * BRANCH_D
COL_QK = DIFF_HEADS * 2 * DIFF_DH
COL_V = DIFF_HEADS * DIFF_VD
COL_C = 2 * BRANCH_D
COL_G = N_BRANCH * D_MODEL
IN_SPLIT_WIDTHS = (COL_A, COL_QK, COL_QK, COL_V, COL_C, COL_G)
LOG2_E = math.log2(math.e)
IN_SPLIT_SCALES = (1.0, DIFF_DH ** -0.5 * LOG2_E, 1.0, 1.0, 1.0, 1.0)

LANES = 128
MXU_WIDTH = 256
F32_SUBLANES = 8
BF16_SUBLANES = 16
VMEM_LIMIT_BYTES = 56 * 1024 * 1024
CAST_BLOCK_BYTES = 4 * 1024 * 1024

ROW_TILE = 512
FFN_ROW_TILE = 1024
PROJ_COL_CHUNK = 512
ATTN_TILE = 256
CONV_HALO = 32
CONV_ROW_CHUNK = 64
FFN_HALO = BF16_SUBLANES
FFN_COL_CHUNK = MXU_WIDTH

assert CONV_HALO >= CONV_K - 1 and FFN_HALO == F32_SUBLANES * (FFN_K - 1)
assert D_FF % FFN_COL_CHUNK == 0
assert ATTN_TILE % CHUNK == 0 and ATTN_TILE > REL_MAX_DIST


def _params(*semantics):
    return pltpu.CompilerParams(dimension_semantics=semantics, vmem_limit_bytes=VMEM_LIMIT_BYTES)


def _resident(shape):
    zeros = (0,) * len(shape)
    return pl.BlockSpec(shape, lambda *_: zeros, pipeline_mode=pl.Buffered(1))


class _LayerWeight(NamedTuple):
    stacked: jax.Array
    layer: int

    @property
    def shape(self):
        return self.stacked.shape[1:]


def _resident_layer(w):
    index = (w.layer,) + (0,) * len(w.shape)
    return pl.BlockSpec((None,) + tuple(w.shape), lambda *_: index, pipeline_mode=pl.Buffered(1))


def _cast_kernel(w_ref, o_ref):
    o_ref[...] = w_ref[...].astype(BF16)


def _to_bf16(w):
    cols = w.shape[-1]
    flat = w.reshape(-1, cols)
    rows = flat.shape[0]
    block_rows = max(r for r in range(BF16_SUBLANES, rows + 1, BF16_SUBLANES)
                     if rows % r == 0 and r * cols * 4 <= CAST_BLOCK_BYTES)
    out = pl.pallas_call(
        _cast_kernel,
        grid=(rows // block_rows,),
        in_specs=[pl.BlockSpec((block_rows, cols), lambda i: (i, 0))],
        out_specs=pl.BlockSpec((block_rows, cols), lambda i: (i, 0)),
        out_shape=jax.ShapeDtypeStruct(flat.shape, BF16),
        compiler_params=_params("parallel"),
        name="weight_cast",
    )(flat)
    return out.reshape(w.shape)


def _rms_norm(x, g):
    return x * lax.rsqrt(jnp.mean(x * x, axis=-1, keepdims=True) + EPS) * g


def _layer_norm(x, g, b):
    mu = jnp.mean(x, axis=-1, keepdims=True)
    xc = x - mu
    var = jnp.mean(xc * xc, axis=-1, keepdims=True)
    return xc * lax.rsqrt(var + EPS) * g + b


def _dot(a, b):
    return jnp.dot(a, b, preferred_element_type=F32)


def _dot_nt(a, b):
    return lax.dot_general(a, b, (((1,), (1,)), ((), ())), preferred_element_type=F32)


def _norm_proj_kernel(x_ref, g_ref, w_ref, *out_refs, widths, scales):
    xn = _rms_norm(x_ref[...], g_ref[...]).astype(BF16)
    col = 0
    for o_ref, width, scale in zip(out_refs, widths, scales):
        for c in range(0, width, PROJ_COL_CHUNK):
            out = _dot(xn, w_ref[:, col + c:col + c + PROJ_COL_CHUNK])
            if scale != 1.0:
                out = out * scale
            o_ref[:, c:c + PROJ_COL_CHUNK] = out.astype(BF16)
        col += width


def _norm_proj(x, g, w, widths, scales=None, row_tile=ROW_TILE):
    rows, d = x.shape
    scales = scales or (1.0,) * len(widths)
    assert w.shape == (d, sum(widths)) and all(wd % PROJ_COL_CHUNK == 0 for wd in widths)
    return pl.pallas_call(
        functools.partial(_norm_proj_kernel, widths=widths, scales=scales),
        grid=(rows // row_tile,),
        in_specs=[pl.BlockSpec((row_tile, d), lambda i: (i, 0)), _resident((1, d)), _resident_layer(w)],
        out_specs=[pl.BlockSpec((row_tile, wd), lambda i: (i, 0)) for wd in widths],
        out_shape=[jax.ShapeDtypeStruct((rows, wd), BF16) for wd in widths],
        compiler_params=_params("parallel"),
        name="norm_proj",
    )(x, g.reshape(1, d), w.stacked)


def _mixer_in_kernel(x_ref, g_ref, w_ref, cw_ref, cb_ref, lng_ref, lnb_ref,
                     ya_ref, pq_ref, pk_ref, pv_ref, pc_ref, pg_ref, xn_scr, xs_scr, halo_scr):
    rows = ya_ref.shape[1]
    xn_scr[...] = _rms_norm(x_ref[0], g_ref[...]).astype(BF16)

    def proj(col, width):
        return _dot(xn_scr[...], w_ref[:, col:col + width])

    glu = proj(0, BRANCH_D) * jax.nn.sigmoid(proj(BRANCH_D, BRANCH_D))

    @pl.when(pl.program_id(1) == 0)
    def _():
        halo_scr[...] = jnp.zeros(halo_scr.shape, F32)

    xs_scr[0, :CONV_HALO, :] = halo_scr[...]
    xs_scr[0, CONV_HALO:, :] = glu
    halo_scr[...] = glu[rows - CONV_HALO:, :]

    def shifted_copies():
        shifted_rows = CONV_HALO + rows - F32_SUBLANES
        for b in range(1, F32_SUBLANES):
            xs_scr[b, :shifted_rows, :] = xs_scr[0, b:b + shifted_rows, :]

    first_tap = CONV_HALO - (CONV_K - 1)

    def conv_rows(r):
        acc = jnp.broadcast_to(cb_ref[...], (CONV_ROW_CHUNK, BRANCH_D))
        for k in range(CONV_K):
            b = (first_tap + k) % F32_SUBLANES
            start = r + first_tap + k - b
            acc = acc + cw_ref[k:k + 1, :] * xs_scr[b, start:start + CONV_ROW_CHUNK, :]
        ya_ref[0, r:r + CONV_ROW_CHUNK, :] = jax.nn.silu(_layer_norm(acc, lng_ref[...], lnb_ref[...])).astype(BF16)

    def proj_chunk(o_ref, col, c, scale):
        out = proj(col + c, PROJ_COL_CHUNK)
        if scale != 1.0:
            out = out * scale
        o_ref[0, :, c:c + PROJ_COL_CHUNK] = out.astype(BF16)

    col = COL_A
    for o_ref, width, scale in zip((pq_ref, pk_ref, pv_ref, pc_ref, pg_ref), IN_SPLIT_WIDTHS[1:], IN_SPLIT_SCALES[1:]):
        for c in range(0, width, PROJ_COL_CHUNK):
            proj_chunk(o_ref, col, c, scale)
        col += width
    shifted_copies()
    for r in range(0, rows, CONV_ROW_CHUNK):
        conv_rows(r)


def _mixer_in_proj(x, norm_g, w_in, conv_w, conv_b, ln_g, ln_b, row_tile=ROW_TILE):
    b, s, d = x.shape
    vec = lambda a: a.reshape(1, -1)
    out_widths = (BRANCH_D,) + IN_SPLIT_WIDTHS[1:]
    return pl.pallas_call(
        _mixer_in_kernel,
        grid=(b, s // row_tile),
        in_specs=[
            pl.BlockSpec((1, row_tile, d), lambda bi, i: (bi, i, 0)),
            _resident((1, d)), _resident_layer(w_in),
            _resident((CONV_K, BRANCH_D)), _resident((1, BRANCH_D)), _resident((1, BRANCH_D)),
            _resident((1, BRANCH_D)),
        ],
        out_specs=[pl.BlockSpec((1, row_tile, wd), lambda bi, i: (bi, i, 0)) for wd in out_widths],
        out_shape=[jax.ShapeDtypeStruct((b, s, wd), BF16) for wd in out_widths],
        scratch_shapes=[
            pltpu.VMEM((row_tile, d), BF16),
            pltpu.VMEM((F32_SUBLANES, CONV_HALO + row_tile, BRANCH_D), F32),
            pltpu.VMEM((CONV_HALO, BRANCH_D), F32),
        ],
        compiler_params=_params("parallel", "arbitrary"),
        name="mixer_in_proj",
    )(x, vec(norm_g), w_in.stacked, conv_w, vec(conv_b), vec(ln_g), vec(ln_b))


def _rel_bucket(rel):
    half = N_REL_BUCKETS // 2
    max_exact = half // 2
    n = jnp.abs(rel)
    log_ratio = jnp.log(jnp.maximum(n, 1).astype(F32) / max_exact) / math.log(REL_MAX_DIST / max_exact)
    large = jnp.minimum(max_exact + (log_ratio * (half - max_exact)).astype(jnp.int32), half - 1)
    return jnp.where(rel > 0, half, 0) + jnp.where(n < max_exact, n, large)


FAR_BUCKET = N_REL_BUCKETS // 2 - 1
MASKED_BUCKET = -1


def _near_buckets():
    r = jnp.arange(ATTN_TILE)[:, None]
    c = jnp.arange(ATTN_TILE)[None, :]
    visible = (c // CHUNK) <= (r // CHUNK)
    diag = jnp.where(visible, _rel_bucket(c - r), MASKED_BUCKET)
    sub = _rel_bucket(c - r - ATTN_TILE)
    return jnp.stack([diag, sub]).astype(jnp.int32)


def _bias_tiles_kernel(table_ref, bucket_ref, o_ref):
    h = pl.program_id(0)
    bucket = bucket_ref[...]
    far = table_ref[FAR_BUCKET, h]
    out = jnp.zeros(bucket.shape, F32)
    for b in range(N_REL_BUCKETS):
        out = jnp.where(bucket == b, (table_ref[b, h] - far) * LOG2_E, out)
    o_ref[0] = jnp.where(bucket == MASKED_BUCKET, NEG_INF, out)


def _bias_tiles(rel_bias):
    buckets = _near_buckets()
    return pl.pallas_call(
        _bias_tiles_kernel,
        grid=(DIFF_HEADS,),
        in_specs=[pl.BlockSpec(memory_space=pltpu.SMEM), _resident(buckets.shape)],
        out_specs=pl.BlockSpec((1,) + buckets.shape, lambda h: (h, 0, 0, 0)),
        out_shape=jax.ShapeDtypeStruct((DIFF_HEADS,) + buckets.shape, F32),
        compiler_params=_params("arbitrary"),
        name="rel_bias_tiles",
    )(rel_bias.astype(F32), buckets)


def _diff_attn_kernel(q_ref, k_ref, v_ref, bias_ref, lq1_ref, lk1_ref, lq2_ref, lk2_ref, g_ref, o_ref,
                      q2_scr, s_scr, m_scr, l_scr, acc_scr, *, lambda_init):
    t = ATTN_TILE
    i = pl.program_id(1)
    heads = range(DIFF_HEADS)

    def head_cols(h):
        return slice(h * DIFF_VD, (h + 1) * DIFF_VD)

    q = q_ref[0]
    lane = lax.broadcasted_iota(jnp.int32, (t, DIFF_VD), 1)
    for h in heads:
        qh = q[:, head_cols(h)]
        q2_scr[h, :t, :] = jnp.where(lane < DIFF_DH, qh, 0).astype(BF16)
        q2_scr[h, t:, :] = jnp.where(lane >= DIFF_DH, qh, 0).astype(BF16)
    m_scr[...] = jnp.full(m_scr.shape, NEG_INF, F32)
    l_scr[...] = jnp.zeros(l_scr.shape, F32)
    acc_scr[...] = jnp.zeros(acc_scr.shape, F32)

    def logits_tiles(j0, near):
        for n, bias_index in enumerate(near):
            start = pl.multiple_of((j0 + n) * t, t)
            for h in heads:
                s = _dot_nt(q2_scr[h], k_ref[0, pl.ds(start, t), head_cols(h)])
                if bias_index is not None:
                    bias = bias_ref[h, bias_index]
                    s = s + jnp.concatenate([bias, bias], axis=0)
                s_scr[j0 + n, h] = s
                m_scr[h] = jnp.maximum(m_scr[h], jnp.maximum(s[:, :LANES], s[:, LANES:]))

    def far_pair(pair, carry):
        logits_tiles(2 * pair, (None, None))
        return carry

    n_far = jnp.maximum(i - 1, 0)
    lax.fori_loop(0, n_far // 2, far_pair, 0)

    @pl.when(n_far % 2 == 1)
    def _():
        logits_tiles(n_far - 1, (None,))

    @pl.when(i > 0)
    def _():
        logits_tiles(i - 1, (1, 0))

    @pl.when(i == 0)
    def _():
        logits_tiles(0, (0,))

    for h in heads:
        m_scr[h] = jnp.broadcast_to(jnp.max(m_scr[h], axis=-1, keepdims=True), (2 * t, LANES))

    def pv_tiles(j0, n_tiles):
        start = pl.multiple_of(j0 * t, t)
        for h in heads:
            m = m_scr[h]
            parts = [jnp.exp2(s_scr[j0 + n, h, :, half * LANES:(half + 1) * LANES] - m)
                     for n in range(n_tiles) for half in range(t // LANES)]
            l_scr[h] += functools.reduce(lambda a, b: a + b, parts)
            p = jnp.concatenate(parts, axis=1).astype(BF16)
            acc_scr[h] += _dot(p, v_ref[0, pl.ds(start, n_tiles * t), head_cols(h)])

    def pv_pair(pair, carry):
        pv_tiles(2 * pair, 2)
        return carry

    n_visible = i + 1
    lax.fori_loop(0, n_visible // 2, pv_pair, 0)

    @pl.when(n_visible % 2 == 1)
    def _():
        pv_tiles(n_visible - 1, 1)

    lam = (jnp.exp(jnp.sum(lq1_ref[...] * lk1_ref[...])) - jnp.exp(jnp.sum(lq2_ref[...] * lk2_ref[...]))
           + lambda_init)
    for h in heads:
        l = jnp.sum(l_scr[h], axis=-1, keepdims=True)
        o = acc_scr[h, :t, :] / l[:t] - lam * (acc_scr[h, t:, :] / l[t:])
        o_ref[0, :, head_cols(h)] = (_rms_norm(o, g_ref[...]) * (1.0 - lambda_init)).astype(BF16)


def _diff_attention(pq, pk, pv, bias_tiles, lam_q1, lam_k1, lam_q2, lam_k2, subln_g, lambda_init):
    b, s, width = pq.shape
    t = ATTN_TILE
    vec = lambda a: a.reshape(1, -1).astype(F32)
    all_keys = pl.BlockSpec((1, s, width), lambda bi, i: (bi, 0, 0))
    lam_spec = _resident((1, DIFF_DH))
    return pl.pallas_call(
        functools.partial(_diff_attn_kernel, lambda_init=lambda_init),
        grid=(b, s // t),
        in_specs=[
            pl.BlockSpec((1, t, width), lambda bi, i: (bi, i, 0)),
            all_keys, all_keys, _resident(bias_tiles.shape),
            lam_spec, lam_spec, lam_spec, lam_spec, _resident((1, DIFF_VD)),
        ],
        out_specs=pl.BlockSpec((1, t, width), lambda bi, i: (bi, i, 0)),
        out_shape=jax.ShapeDtypeStruct((b, s, width), BF16),
        scratch_shapes=[
            pltpu.VMEM((DIFF_HEADS, 2 * t, DIFF_VD), BF16),
            pltpu.VMEM((s // t, DIFF_HEADS, 2 * t, t), F32),
            pltpu.VMEM((DIFF_HEADS, 2 * t, LANES), F32),
            pltpu.VMEM((DIFF_HEADS, 2 * t, LANES), F32),
            pltpu.VMEM((DIFF_HEADS, 2 * t, DIFF_VD), F32),
        ],
        compiler_params=_params("parallel", "parallel"),
        name="diff_attention",
    )(pq, pk, pv, bias_tiles, vec(lam_q1), vec(lam_k1), vec(lam_q2), vec(lam_k2), vec(subln_g))


def _gmlp_kernel(pc_ref, lng_ref, lnb_ref, ws_ref, bs_ref, o_ref):
    rows = o_ref.shape[0]
    z = jax.nn.gelu(pc_ref[...].astype(F32))
    u = z[:, :BRANCH_D]
    v = _layer_norm(z[:, BRANCH_D:], lng_ref[...], lnb_ref[...]).astype(BF16)
    t_idx = lax.broadcasted_iota(jnp.int32, (GMLP_CHUNK, GMLP_CHUNK), 0)
    s_idx = lax.broadcasted_iota(jnp.int32, (GMLP_CHUNK, GMLP_CHUNK), 1)
    for g in range(GMLP_GROUPS):
        w = jnp.where(s_idx <= t_idx, ws_ref[g], 0.0).astype(BF16)
        cols = slice(g * GMLP_GD, (g + 1) * GMLP_GD)
        for r in range(0, rows, GMLP_CHUNK):
            sv = _dot(w, v[r:r + GMLP_CHUNK, cols]) + bs_ref[g]
            o_ref[r:r + GMLP_CHUNK, cols] = (u[r:r + GMLP_CHUNK, cols] * sv).astype(BF16)


def _gmlp_branch(pc, ln_g, ln_b, w_s, b_s, row_tile=ROW_TILE):
    rows = pc.shape[0]
    vec = lambda a: a.reshape(1, BRANCH_D)
    return pl.pallas_call(
        _gmlp_kernel,
        grid=(rows // row_tile,),
        in_specs=[
            pl.BlockSpec((row_tile, COL_C), lambda i: (i, 0)),
            _resident((1, BRANCH_D)), _resident((1, BRANCH_D)),
            _resident(w_s.shape), _resident((GMLP_GROUPS, GMLP_CHUNK, 1)),
        ],
        out_specs=pl.BlockSpec((row_tile, BRANCH_D), lambda i: (i, 0)),
        out_shape=jax.ShapeDtypeStruct((rows, BRANCH_D), BF16),
        compiler_params=_params("parallel"),
        name="gmlp_branch",
    )(pc, vec(ln_g), vec(ln_b), w_s, b_s.reshape(GMLP_GROUPS, GMLP_CHUNK, 1))


def _merge_kernel(x_ref, ya_ref, yb_ref, yc_ref, pg_ref, gb_ref, wbr_ref, wout_ref, o_ref):
    merged = None
    for n, y_ref in enumerate((ya_ref, yb_ref, yc_ref)):
        logits = pg_ref[:, n * D_MODEL:(n + 1) * D_MODEL].astype(F32) + gb_ref[n:n + 1, :]
        term = jax.nn.sigmoid(logits) * _dot(y_ref[...], wbr_ref[n])
        merged = term if merged is None else merged + term
    o_ref[...] = x_ref[...] + _dot(merged.astype(BF16), wout_ref[...])


def _merge(x, ya, yb, yc, pg, gate_b, w_br, w_out, row_tile=ROW_TILE):
    rows = x.shape[0]
    row_spec = lambda width: pl.BlockSpec((row_tile, width), lambda i: (i, 0))
    return pl.pallas_call(
        _merge_kernel,
        grid=(rows // row_tile,),
        in_specs=[row_spec(D_MODEL), row_spec(BRANCH_D), row_spec(BRANCH_D), row_spec(BRANCH_D), row_spec(COL_G),
                  _resident(gate_b.shape), _resident_layer(w_br), _resident_layer(w_out)],
        out_specs=row_spec(D_MODEL),
        out_shape=jax.ShapeDtypeStruct(x.shape, F32),
        compiler_params=_params("parallel"),
        name="branch_merge",
    )(x, ya, yb, yc, pg, gate_b, w_br.stacked, w_out.stacked)


def _xattn_kernel(x_ref, g_ref, wq_ref, kv_ref, wo_ref, o_ref):
    x = x_ref[0]
    q = _dot(_rms_norm(x, g_ref[...]).astype(BF16), wq_ref[...]).astype(BF16)
    heads = []
    for h in range(XATTN_HEADS):
        cols = slice(h * XATTN_DH, (h + 1) * XATTN_DH)
        k = kv_ref[0, :, cols]
        v = kv_ref[0, :, XATTN_D + h * XATTN_DH:XATTN_D + (h + 1) * XATTN_DH]
        s = _dot_nt(q[:, cols], k) * (XATTN_DH ** -0.5)
        p = jnp.exp(s - jnp.max(s, axis=-1, keepdims=True))
        p = p / jnp.sum(p, axis=-1, keepdims=True)
        heads.append(_dot(p.astype(BF16), v).astype(BF16))
    o_ref[0] = x + _dot(jnp.concatenate(heads, axis=-1), wo_ref[...])


def _mem_cross_attention(x, kv, norm_g, w_xq, w_xo, row_tile=ROW_TILE):
    b, s, d = x.shape
    m = kv.shape[1]
    return pl.pallas_call(
        _xattn_kernel,
        grid=(b, s // row_tile),
        in_specs=[
            pl.BlockSpec((1, row_tile, d), lambda bi, i: (bi, i, 0)),
            _resident((1, d)), _resident_layer(w_xq),
            pl.BlockSpec((1, m, 2 * XATTN_D), lambda bi, i: (bi, 0, 0)),
            _resident_layer(w_xo),
        ],
        out_specs=pl.BlockSpec((1, row_tile, d), lambda bi, i: (bi, i, 0)),
        out_shape=jax.ShapeDtypeStruct(x.shape, F32),
        compiler_params=_params("parallel", "parallel"),
        name="mem_cross_attention",
    )(x, norm_g.reshape(1, d), w_xq.stacked, kv, w_xo.stacked)


def _ffn_kernel(x_ref, halo_ref, g_ref, wup_ref, cw_ref, cb_ref, wdn_ref, gf_ref, o_ref,
                slab_scr, h_scr, up_scr, act_scr, acc_scr, *, final_norm):
    tile_rows = o_ref.shape[1]
    rows = acc_scr.shape[0]
    groups = rows // F32_SUBLANES
    slabs = slab_scr.shape[0]

    def lanes(j):
        return slice(j * LANES, (j + 1) * LANES)

    for j in range(slabs):
        slab_scr[j, :tile_rows, :] = x_ref[0, :, lanes(j)]
    slab_scr[:, tile_rows:, :] = jnp.zeros((slabs, rows - tile_rows, LANES), F32)
    x = jnp.concatenate(
        [jnp.concatenate([slab_scr[j, pl.ds(gi, F32_SUBLANES, stride=groups), :] for j in range(slabs)], axis=1)
         for gi in range(groups)], axis=0)

    g = g_ref[...]
    halo = _rms_norm(halo_ref[0], g)
    h_scr[:FFN_HALO, :] = jnp.where(pl.program_id(1) > 0, halo, 0.0).astype(BF16)
    h_scr[FFN_HALO:, :] = _rms_norm(x, g).astype(BF16)
    acc_scr[...] = x

    n_chunks = D_FF // FFN_COL_CHUNK
    first_sublane = lax.broadcasted_iota(jnp.int32, (F32_SUBLANES, FFN_COL_CHUNK), 0) == 0

    def cols(c, half):
        start = half * D_FF + c * FFN_COL_CHUNK
        return slice(start, start + FFN_COL_CHUNK)

    def up_proj(c, slot):
        h = h_scr[...]
        for half in range(2):
            up = up_scr.at[slot, half]
            up[...] = _dot(h, wup_ref[:, cols(c, half)])
            prev_tile = up[FFN_HALO - F32_SUBLANES:FFN_HALO, :]
            for shift in range(1, FFN_K):
                wrapped = up[FFN_HALO + rows - shift * F32_SUBLANES:FFN_HALO + rows - (shift - 1) * F32_SUBLANES, :]
                block = jnp.where(first_sublane, pltpu.roll(prev_tile, shift, 0), pltpu.roll(wrapped, 1, 0))
                up[FFN_HALO - shift * F32_SUBLANES:FFN_HALO - (shift - 1) * F32_SUBLANES, :] = block

    def conv(c, slot, half):
        out = cb_ref[:, cols(c, half)]
        for k in range(FFN_K):
            window = up_scr[slot, half, k * F32_SUBLANES:k * F32_SUBLANES + rows, :]
            out = out + cw_ref[k:k + 1, cols(c, half)] * window
        return out

    def gate(c, slot):
        act_scr[slot] = (jax.nn.silu(conv(c, slot, 0)) * conv(c, slot, 1)).astype(BF16)

    def down_proj(c, slot):
        acc_scr[...] += _dot(act_scr[slot], wdn_ref[cols(c, 0), :])

    def stage(c, slot):
        up_proj(c + 1, 1 - slot)
        down_proj(c - 1, 1 - slot)
        gate(c, slot)

    up_proj(0, 0)
    up_proj(1, 1)
    gate(0, 0)
    for c in range(1, n_chunks - 1):
        stage(c, c % 2)
    down_proj(n_chunks - 2, (n_chunks - 2) % 2)
    gate(n_chunks - 1, (n_chunks - 1) % 2)
    down_proj(n_chunks - 1, (n_chunks - 1) % 2)

    for gi in range(groups):
        out = acc_scr[gi * F32_SUBLANES:(gi + 1) * F32_SUBLANES, :]
        if final_norm:
            out = _rms_norm(out, gf_ref[...])
        for j in range(slabs):
            slab_scr[j, pl.ds(gi, F32_SUBLANES, stride=groups), :] = out[:, lanes(j)]
    for j in range(slabs):
        o_ref[0, :, lanes(j)] = slab_scr[j, :tile_rows, :]


def _conv_ffn(x, norm_g, w_up, conv_w, conv_b, w_down, final_g, final_norm, row_tile=FFN_ROW_TILE):
    b, s, d = x.shape
    halo_blocks = row_tile // FFN_HALO
    conv_b = conv_b.reshape(1, 2 * D_FF)
    perm_rows = row_tile + 2 * F32_SUBLANES
    return pl.pallas_call(
        functools.partial(_ffn_kernel, final_norm=final_norm),
        grid=(b, s // row_tile),
        in_specs=[
            pl.BlockSpec((1, row_tile, d), lambda bi, i: (bi, i, 0)),
            pl.BlockSpec((1, FFN_HALO, d), lambda bi, i: (bi, jnp.maximum(i * halo_blocks - 1, 0), 0)),
            _resident((1, d)), _resident_layer(w_up), _resident(conv_w.shape), _resident(conv_b.shape),
            _resident_layer(w_down), _resident((1, d)),
        ],
        out_specs=pl.BlockSpec((1, row_tile, d), lambda bi, i: (bi, i, 0)),
        out_shape=jax.ShapeDtypeStruct(x.shape, F32),
        scratch_shapes=[
            pltpu.VMEM((d // LANES, perm_rows, LANES), F32),
            pltpu.VMEM((FFN_HALO + perm_rows, d), BF16),
            pltpu.VMEM((2, 2, FFN_HALO + perm_rows, FFN_COL_CHUNK), F32),
            pltpu.VMEM((2, perm_rows, FFN_COL_CHUNK), BF16),
            pltpu.VMEM((perm_rows, d), F32),
        ],
        compiler_params=_params("parallel", "parallel"),
        name="conv_ffn",
    )(x, x, norm_g.reshape(1, d), w_up.stacked, conv_w, conv_b, w_down.stacked, final_g.reshape(1, d))


def kernel(x, mem, rel_bias, norm_mix_g, w_in, gate_b, conv_w, conv_b, conv_ln_g, conv_ln_b, lam_q1, lam_k1, lam_q2, lam_k2, subln_g, gmlp_ln_g, gmlp_ln_b, w_s, b_s, w_br, w_out, norm_xattn_g, norm_mem_g, w_xq, w_xkv, w_xo, norm_ffn_g, w_up, ffn_conv_w, ffn_conv_b, w_down, norm_final_g):
    b, s, d = x.shape
    m = mem.shape[1]
    rows = b * s
    bias_tiles = _bias_tiles(rel_bias)
    w_in, w_br, w_out, w_xq, w_xkv, w_xo, w_up, w_down = (
        _to_bf16(w) for w in (w_in, w_br, w_out, w_xq, w_xkv, w_xo, w_up, w_down))
    for l in range(DEPTH):
        layer = lambda w: _LayerWeight(w, l)
        lambda_init = 0.8 - 0.6 * math.exp(-0.3 * l)
        ya, pq, pk, pv, pc, pg = _mixer_in_proj(x, norm_mix_g[l], layer(w_in), conv_w[l], conv_b[l],
                                                conv_ln_g[l], conv_ln_b[l])
        yb = _diff_attention(pq, pk, pv, bias_tiles, lam_q1[l], lam_k1[l], lam_q2[l], lam_k2[l], subln_g[l],
                             lambda_init)
        yc = _gmlp_branch(pc.reshape(rows, COL_C), gmlp_ln_g[l], gmlp_ln_b[l], w_s[l], b_s[l])
        x = _merge(x.reshape(rows, d), ya.reshape(rows, BRANCH_D), yb.reshape(rows, BRANCH_D), yc,
                   pg.reshape(rows, COL_G), gate_b[l], layer(w_br), layer(w_out)).reshape(b, s, d)
        (kv,) = _norm_proj(mem.reshape(b * m, d), norm_mem_g[l], layer(w_xkv), (2 * XATTN_D,))
        x = _mem_cross_attention(x, kv.reshape(b, m, 2 * XATTN_D), norm_xattn_g[l], layer(w_xq), layer(w_xo))
        x = _conv_ffn(x, norm_ffn_g[l], layer(w_up), ffn_conv_w[l], ffn_conv_b[l], layer(w_down), norm_final_g,
                      final_norm=(l == DEPTH - 1))
    return x
```

```python
import functools
import math
from typing import NamedTuple

import jax
import jax.numpy as jnp
from jax import lax
from jax.experimental import pallas as pl
from jax.experimental.pallas import tpu as pltpu

F32 = jnp.float32
BF16 = jnp.bfloat16

D_MODEL = 1024
DEPTH = 2
CHUNK = 64
BRANCH_D = D_MODEL // 2
N_BRANCH = 3
CONV_K = 31
DIFF_HEADS = 4
DIFF_DH = D_MODEL // 16
DIFF_VD = 2 * DIFF_DH
GMLP_GROUPS = 4
GMLP_GD = BRANCH_D // GMLP_GROUPS
GMLP_CHUNK = 128
N_REL_BUCKETS = 32
REL_MAX_DIST = 128
XATTN_HEADS = 4
XATTN_DH = D_MODEL // 8
XATTN_D = XATTN_HEADS * XATTN_DH
D_FF = 2816
FFN_K = 3
EPS = 1e-6
NEG_INF = -1e30

COL_A = 2 * BRANCH_D
COL_QK = DIFF_HEADS * 2 * DIFF_DH
COL_V = DIFF_HEADS * DIFF_VD
COL_C = 2 * BRANCH_D
COL_G = N_BRANCH * D_MODEL
IN_SPLIT_WIDTHS = (COL_A, COL_QK, COL_QK, COL_V, COL_C, COL_G)
LOG2_E = math.log2(math.e)
IN_SPLIT_SCALES = (1.0, DIFF_DH ** -0.5 * LOG2_E, 1.0, 1.0, 1.0, 1.0)

LANES = 128
MXU_WIDTH = 256
F32_SUBLANES = 8
BF16_SUBLANES = 16
VMEM_LIMIT_BYTES = 56 * 1024 * 1024
CAST_BLOCK_BYTES = 4 * 1024 * 1024

ROW_TILE = 512
FFN_ROW_TILE = 1024
PROJ_COL_CHUNK = 512
ATTN_TILE = 256
CONV_HALO = 32
CONV_ROW_CHUNK = 64
FFN_HALO = BF16_SUBLANES
FFN_COL_CHUNK = MXU_WIDTH

assert CONV_HALO >= CONV_K - 1 and FFN_HALO == F32_SUBLANES * (FFN_K - 1)
assert D_FF % FFN_COL_CHUNK == 0
assert ATTN_TILE % CHUNK == 0 and ATTN_TILE > REL_MAX_DIST


def _params(*semantics):
    return pltpu.CompilerParams(dimension_semantics=semantics, vmem_limit_bytes=VMEM_LIMIT_BYTES)


def _resident(shape):
    zeros = (0,) * len(shape)
    return pl.BlockSpec(shape, lambda *_: zeros, pipeline_mode=pl.Buffered(1))


class _LayerWeight(NamedTuple):
    stacked: jax.Array
    layer: int

    @property
    def shape(self):
        return self.stacked.shape[1:]


def _resident_layer(w):
    index = (w.layer,) + (0,) * len(w.shape)
    return pl.BlockSpec((None,) + tuple(w.shape), lambda *_: index, pipeline_mode=pl.Buffered(1))


def _cast_kernel(w_ref, o_ref):
    o_ref[...] = w_ref[...].astype(BF16)


def _to_bf16(w):
    cols = w.shape[-1]
    flat = w.reshape(-1, cols)
    rows = flat.shape[0]
    block_rows = max(r for r in range(BF16_SUBLANES, rows + 1, BF16_SUBLANES)
                     if rows % r == 0 and r * cols * 4 <= CAST_BLOCK_BYTES)
    out = pl.pallas_call(
        _cast_kernel,
        grid=(rows // block_rows,),
        in_specs=[pl.BlockSpec((block_rows, cols), lambda i: (i, 0))],
        out_specs=pl.BlockSpec((block_rows, cols), lambda i: (i, 0)),
        out_shape=jax.ShapeDtypeStruct(flat.shape, BF16),
        compiler_params=_params("parallel"),
        name="weight_cast",
    )(flat)
    return out.reshape(w.shape)


def _rms_norm(x, g):
    return x * lax.rsqrt(jnp.mean(x * x, axis=-1, keepdims=True) + EPS) * g


def _layer_norm(x, g, b):
    mu = jnp.mean(x, axis=-1, keepdims=True)
    xc = x - mu
    var = jnp.mean(xc * xc, axis=-1, keepdims=True)
    return xc * lax.rsqrt(var + EPS) * g + b


def _dot(a, b):
    return jnp.dot(a, b, preferred_element_type=F32)


def _dot_nt(a, b):
    return lax.dot_general(a, b, (((1,), (1,)), ((), ())), preferred_element_type=F32)


def _norm_proj_kernel(x_ref, g_ref, w_ref, *out_refs, widths, scales):
    xn = _rms_norm(x_ref[...], g_ref[...]).astype(BF16)
    col = 0
    for o_ref, width, scale in zip(out_refs, widths, scales):
        for c in range(0, width, PROJ_COL_CHUNK):
            out = _dot(xn, w_ref[:, col + c:col + c + PROJ_COL_CHUNK])
            if scale != 1.0:
                out = out * scale
            o_ref[:, c:c + PROJ_COL_CHUNK] = out.astype(BF16)
        col += width


def _norm_proj(x, g, w, widths, scales=None, row_tile=ROW_TILE):
    rows, d = x.shape
    scales = scales or (1.0,) * len(widths)
    assert w.shape == (d, sum(widths)) and all(wd % PROJ_COL_CHUNK == 0 for wd in widths)
    return pl.pallas_call(
        functools.partial(_norm_proj_kernel, widths=widths, scales=scales),
        grid=(rows // row_tile,),
        in_specs=[pl.BlockSpec((row_tile, d), lambda i: (i, 0)), _resident((1, d)), _resident_layer(w)],
        out_specs=[pl.BlockSpec((row_tile, wd), lambda i: (i, 0)) for wd in widths],
        out_shape=[jax.ShapeDtypeStruct((rows, wd), BF16) for wd in widths],
        compiler_params=_params("parallel"),
        name="norm_proj",
    )(x, g.reshape(1, d), w.stacked)


def _mixer_in_kernel(x_ref, g_ref, w_ref, cw_ref, cb_ref, lng_ref, lnb_ref,
                     ya_ref, pq_ref, pk_ref, pv_ref, pc_ref, pg_ref, xn_scr, xs_scr, halo_scr):
    rows = ya_ref.shape[1]
    xn_scr[...] = _rms_norm(x_ref[0], g_ref[...]).astype(BF16)

    def proj(col, width):
        return _dot(xn_scr[...], w_ref[:, col:col + width])

    glu = proj(0, BRANCH_D) * jax.nn.sigmoid(proj(BRANCH_D, BRANCH_D))

    @pl.when(pl.program_id(1) == 0)
    def _():
        halo_scr[...] = jnp.zeros(halo_scr.shape, F32)

    xs_scr[0, :CONV_HALO, :] = halo_scr[...]
    xs_scr[0, CONV_HALO:, :] = glu
    halo_scr[...] = glu[rows - CONV_HALO:, :]

    def shifted_copies():
        shifted_rows = CONV_HALO + rows - F32_SUBLANES
        for b in range(1, F32_SUBLANES):
            xs_scr[b, :shifted_rows, :] = xs_scr[0, b:b + shifted_rows, :]

    first_tap = CONV_HALO - (CONV_K - 1)

    def conv_rows(r):
        acc = jnp.broadcast_to(cb_ref[...], (CONV_ROW_CHUNK, BRANCH_D))
        for k in range(CONV_K):
            b = (first_tap + k) % F32_SUBLANES
            start = r + first_tap + k - b
            acc = acc + cw_ref[k:k + 1, :] * xs_scr[b, start:start + CONV_ROW_CHUNK, :]
        ya_ref[0, r:r + CONV_ROW_CHUNK, :] = jax.nn.silu(_layer_norm(acc, lng_ref[...], lnb_ref[...])).astype(BF16)

    def proj_chunk(o_ref, col, c, scale):
        out = proj(col + c, PROJ_COL_CHUNK)
        if scale != 1.0:
            out = out * scale
        o_ref[0, :, c:c + PROJ_COL_CHUNK] = out.astype(BF16)

    col = COL_A
    for o_ref, width, scale in zip((pq_ref, pk_ref, pv_ref, pc_ref, pg_ref), IN_SPLIT_WIDTHS[1:], IN_SPLIT_SCALES[1:]):
        for c in range(0, width, PROJ_COL_CHUNK):
            proj_chunk(o_ref, col, c, scale)
        col += width
    shifted_copies()
    for r in range(0, rows, CONV_ROW_CHUNK):
        conv_rows(r)


def _mixer_in_proj(x, norm_g, w_in, conv_w, conv_b, ln_g, ln_b, row_tile=ROW_TILE):
    b, s, d = x.shape
    vec = lambda a: a.reshape(1, -1)
    out_widths = (BRANCH_D,) + IN_SPLIT_WIDTHS[1:]
    return pl.pallas_call(
        _mixer_in_kernel,
        grid=(b, s // row_tile),
        in_specs=[
            pl.BlockSpec((1, row_tile, d), lambda bi, i: (bi, i, 0)),
            _resident((1, d)), _resident_layer(w_in),
            _resident((CONV_K, BRANCH_D)), _resident((1, BRANCH_D)), _resident((1, BRANCH_D)),
            _resident((1, BRANCH_D)),
        ],
        out_specs=[pl.BlockSpec((1, row_tile, wd), lambda bi, i: (bi, i, 0)) for wd in out_widths],
        out_shape=[jax.ShapeDtypeStruct((b, s, wd), BF16) for wd in out_widths],
        scratch_shapes=[
            pltpu.VMEM((row_tile, d), BF16),
            pltpu.VMEM((F32_SUBLANES, CONV_HALO + row_tile, BRANCH_D), F32),
            pltpu.VMEM((CONV_HALO, BRANCH_D), F32),
        ],
        compiler_params=_params("parallel", "arbitrary"),
        name="mixer_in_proj",
    )(x, vec(norm_g), w_in.stacked, conv_w, vec(conv_b), vec(ln_g), vec(ln_b))


def _rel_bucket(rel):
    half = N_REL_BUCKETS // 2
    max_exact = half // 2
    n = jnp.abs(rel)
    log_ratio = jnp.log(jnp.maximum(n, 1).astype(F32) / max_exact) / math.log(REL_MAX_DIST / max_exact)
    large = jnp.minimum(max_exact + (log_ratio * (half - max_exact)).astype(jnp.int32), half - 1)
    return jnp.where(rel > 0, half, 0) + jnp.where(n < max_exact, n, large)


FAR_BUCKET = N_REL_BUCKETS // 2 - 1
MASKED_BUCKET = -1


def _near_buckets():
    r = jnp.arange(ATTN_TILE)[:, None]
    c = jnp.arange(ATTN_TILE)[None, :]
    visible = (c // CHUNK) <= (r // CHUNK)
    diag = jnp.where(visible, _rel_bucket(c - r), MASKED_BUCKET)
    sub = _rel_bucket(c - r - ATTN_TILE)
    return jnp.stack([diag, sub]).astype(jnp.int32)


def _bias_tiles_kernel(table_ref, bucket_ref, o_ref):
    h = pl.program_id(0)
    bucket = bucket_ref[...]
    far = table_ref[FAR_BUCKET, h]
    out = jnp.zeros(bucket.shape, F32)
    for b in range(N_REL_BUCKETS):
        out = jnp.where(bucket == b, (table_ref[b, h] - far) * LOG2_E, out)
    o_ref[0] = jnp.where(bucket == MASKED_BUCKET, NEG_INF, out)


def _bias_tiles(rel_bias):
    buckets = _near_buckets()
    return pl.pallas_call(
        _bias_tiles_kernel,
        grid=(DIFF_HEADS,),
        in_specs=[pl.BlockSpec(memory_space=pltpu.SMEM), _resident(buckets.shape)],
        out_specs=pl.BlockSpec((1,) + buckets.shape, lambda h: (h, 0, 0, 0)),
        out_shape=jax.ShapeDtypeStruct((DIFF_HEADS,) + buckets.shape, F32),
        compiler_params=_params("arbitrary"),
        name="rel_bias_tiles",
    )(rel_bias.astype(F32), buckets)


def _diff_attn_kernel(q_ref, k_ref, v_ref, bias_ref, lq1_ref, lk1_ref, lq2_ref, lk2_ref, g_ref, o_ref,
                      q2_scr, s_scr, m_scr, acc_scr, vones_scr, *, lambda_init):
    t = ATTN_TILE
    i = pl.program_id(1)
    heads = range(DIFF_HEADS)

    def head_cols(h):
        return slice(h * DIFF_VD, (h + 1) * DIFF_VD)

    @pl.when(i == 0)
    def _():
        for h in heads:
            vones_scr[h, :, :DIFF_VD] = v_ref[0, :, head_cols(h)]
            vones_scr[h, :, DIFF_VD:] = jnp.ones((vones_scr.shape[1], LANES), BF16)

    q = q_ref[0]
    lane = lax.broadcasted_iota(jnp.int32, (t, DIFF_VD), 1)
    for h in heads:
        qh = q[:, head_cols(h)]
        q2_scr[h, :t, :] = jnp.where(lane < DIFF_DH, qh, 0).astype(BF16)
        q2_scr[h, t:, :] = jnp.where(lane >= DIFF_DH, qh, 0).astype(BF16)
    m_scr[...] = jnp.full(m_scr.shape, NEG_INF, F32)
    acc_scr[...] = jnp.zeros(acc_scr.shape, F32)

    def logits_tiles(j0, near):
        for n, bias_index in enumerate(near):
            start = pl.multiple_of((j0 + n) * t, t)
            for h in heads:
                s = _dot_nt(q2_scr[h], k_ref[0, pl.ds(start, t), head_cols(h)])
                if bias_index is not None:
                    bias = bias_ref[h, bias_index]
                    s = s + jnp.concatenate([bias, bias], axis=0)
                s_scr[j0 + n, h] = s
                m_scr[h] = jnp.maximum(m_scr[h], jnp.maximum(s[:, :LANES], s[:, LANES:]))

    def far_pair(pair, carry):
        logits_tiles(2 * pair, (None, None))
        return carry

    n_far = jnp.maximum(i - 1, 0)
    lax.fori_loop(0, n_far // 2, far_pair, 0)

    @pl.when(n_far % 2 == 1)
    def _():
        logits_tiles(n_far - 1, (None,))

    @pl.when(i > 0)
    def _():
        logits_tiles(i - 1, (1, 0))

    @pl.when(i == 0)
    def _():
        logits_tiles(0, (0,))

    for h in heads:
        m_scr[h] = jnp.broadcast_to(jnp.max(m_scr[h], axis=-1, keepdims=True), (2 * t, LANES))

    def pv_tiles(j0, n_tiles):
        start = pl.multiple_of(j0 * t, t)
        for h in heads:
            m = m_scr[h]
            parts = [jnp.exp2(s_scr[j0 + n, h, :, half * LANES:(half + 1) * LANES] - m)
                     for n in range(n_tiles) for half in range(t // LANES)]
            p = jnp.concatenate(parts, axis=1).astype(BF16)
            acc_scr[h] += _dot(p, vones_scr[h, pl.ds(start, n_tiles * t), :])

    def pv_pair(pair, carry):
        pv_tiles(2 * pair, 2)
        return carry

    n_visible = i + 1
    lax.fori_loop(0, n_visible // 2, pv_pair, 0)

    @pl.when(n_visible % 2 == 1)
    def _():
        pv_tiles(n_visible - 1, 1)

    lam = (jnp.exp(jnp.sum(lq1_ref[...] * lk1_ref[...])) - jnp.exp(jnp.sum(lq2_ref[...] * lk2_ref[...]))
           + lambda_init)
    for h in heads:
        o = (acc_scr[h, :t, :DIFF_VD] / acc_scr[h, :t, DIFF_VD:]
             - lam * (acc_scr[h, t:, :DIFF_VD] / acc_scr[h, t:, DIFF_VD:]))
        o_ref[0, :, head_cols(h)] = (_rms_norm(o, g_ref[...]) * (1.0 - lambda_init)).astype(BF16)


def _diff_attention(pq, pk, pv, bias_tiles, lam_q1, lam_k1, lam_q2, lam_k2, subln_g, lambda_init):
    b, s, width = pq.shape
    t = ATTN_TILE
    vec = lambda a: a.reshape(1, -1).astype(F32)
    all_keys = pl.BlockSpec((1, s, width), lambda bi, i: (bi, 0, 0))
    lam_spec = _resident((1, DIFF_DH))
    return pl.pallas_call(
        functools.partial(_diff_attn_kernel, lambda_init=lambda_init),
        grid=(b, s // t),
        in_specs=[
            pl.BlockSpec((1, t, width), lambda bi, i: (bi, i, 0)),
            all_keys, all_keys, _resident(bias_tiles.shape),
            lam_spec, lam_spec, lam_spec, lam_spec, _resident((1, DIFF_VD)),
        ],
        out_specs=pl.BlockSpec((1, t, width), lambda bi, i: (bi, i, 0)),
        out_shape=jax.ShapeDtypeStruct((b, s, width), BF16),
        scratch_shapes=[
            pltpu.VMEM((DIFF_HEADS, 2 * t, DIFF_VD), BF16),
            pltpu.VMEM((s // t, DIFF_HEADS, 2 * t, t), F32),
            pltpu.VMEM((DIFF_HEADS, 2 * t, LANES), F32),
            pltpu.VMEM((DIFF_HEADS, 2 * t, DIFF_VD + LANES), F32),
            pltpu.VMEM((DIFF_HEADS, s, DIFF_VD + LANES), BF16),
        ],
        compiler_params=_params("parallel", "arbitrary"),
        name="diff_attention",
    )(pq, pk, pv, bias_tiles, vec(lam_q1), vec(lam_k1), vec(lam_q2), vec(lam_k2), vec(subln_g))


def _gmlp_kernel(pc_ref, lng_ref, lnb_ref, ws_ref, bs_ref, o_ref):
    rows = o_ref.shape[0]
    z = jax.nn.gelu(pc_ref[...].astype(F32))
    u = z[:, :BRANCH_D]
    v = _layer_norm(z[:, BRANCH_D:], lng_ref[...], lnb_ref[...]).astype(BF16)
    t_idx = lax.broadcasted_iota(jnp.int32, (GMLP_CHUNK, GMLP_CHUNK), 0)
    s_idx = lax.broadcasted_iota(jnp.int32, (GMLP_CHUNK, GMLP_CHUNK), 1)
    for g in range(GMLP_GROUPS):
        w = jnp.where(s_idx <= t_idx, ws_ref[g], 0.0).astype(BF16)
        cols = slice(g * GMLP_GD, (g + 1) * GMLP_GD)
        for r in range(0, rows, GMLP_CHUNK):
            sv = _dot(w, v[r:r + GMLP_CHUNK, cols]) + bs_ref[g]
            o_ref[r:r + GMLP_CHUNK, cols] = (u[r:r + GMLP_CHUNK, cols] * sv).astype(BF16)


def _gmlp_branch(pc, ln_g, ln_b, w_s, b_s, row_tile=ROW_TILE):
    rows = pc.shape[0]
    vec = lambda a: a.reshape(1, BRANCH_D)
    return pl.pallas_call(
        _gmlp_kernel,
        grid=(rows // row_tile,),
        in_specs=[
            pl.BlockSpec((row_tile, COL_C), lambda i: (i, 0)),
            _resident((1, BRANCH_D)), _resident((1, BRANCH_D)),
            _resident(w_s.shape), _resident((GMLP_GROUPS, GMLP_CHUNK, 1)),
        ],
        out_specs=pl.BlockSpec((row_tile, BRANCH_D), lambda i: (i, 0)),
        out_shape=jax.ShapeDtypeStruct((rows, BRANCH_D), BF16),
        compiler_params=_params("parallel"),
        name="gmlp_branch",
    )(pc, vec(ln_g), vec(ln_b), w_s, b_s.reshape(GMLP_GROUPS, GMLP_CHUNK, 1))


def _merge_kernel(x_ref, ya_ref, yb_ref, yc_ref, pg_ref, gb_ref, wbr_ref, wout_ref, o_ref):
    merged = None
    for n, y_ref in enumerate((ya_ref, yb_ref, yc_ref)):
        logits = pg_ref[:, n * D_MODEL:(n + 1) * D_MODEL].astype(F32) + gb_ref[n:n + 1, :]
        term = jax.nn.sigmoid(logits) * _dot(y_ref[...], wbr_ref[n])
        merged = term if merged is None else merged + term
    o_ref[...] = x_ref[...] + _dot(merged.astype(BF16), wout_ref[...])


def _merge(x, ya, yb, yc, pg, gate_b, w_br, w_out, row_tile=ROW_TILE):
    rows = x.shape[0]
    row_spec = lambda width: pl.BlockSpec((row_tile, width), lambda i: (i, 0))
    return pl.pallas_call(
        _merge_kernel,
        grid=(rows // row_tile,),
        in_specs=[row_spec(D_MODEL), row_spec(BRANCH_D), row_spec(BRANCH_D), row_spec(BRANCH_D), row_spec(COL_G),
                  _resident(gate_b.shape), _resident_layer(w_br), _resident_layer(w_out)],
        out_specs=row_spec(D_MODEL),
        out_shape=jax.ShapeDtypeStruct(x.shape, F32),
        compiler_params=_params("parallel"),
        name="branch_merge",
    )(x, ya, yb, yc, pg, gate_b, w_br.stacked, w_out.stacked)


def _xattn_kernel(x_ref, g_ref, wq_ref, kv_ref, wo_ref, o_ref):
    x = x_ref[0]
    q = _dot(_rms_norm(x, g_ref[...]).astype(BF16), wq_ref[...]) * (XATTN_DH ** -0.5 * LOG2_E)
    q = q.astype(BF16)
    heads = []
    for h in range(XATTN_HEADS):
        cols = slice(h * XATTN_DH, (h + 1) * XATTN_DH)
        k = kv_ref[0, :, cols]
        v = kv_ref[0, :, XATTN_D + h * XATTN_DH:XATTN_D + (h + 1) * XATTN_DH]
        s = _dot_nt(q[:, cols], k)
        p = jnp.exp2(s - jnp.max(s, axis=-1, keepdims=True))
        o = _dot(p.astype(BF16), v) / jnp.sum(p, axis=-1, keepdims=True)
        heads.append(o.astype(BF16))
    o_ref[0] = x + _dot(jnp.concatenate(heads, axis=-1), wo_ref[...])


def _mem_cross_attention(x, kv, norm_g, w_xq, w_xo, row_tile=ROW_TILE):
    b, s, d = x.shape
    m = kv.shape[1]
    return pl.pallas_call(
        _xattn_kernel,
        grid=(b, s // row_tile),
        in_specs=[
            pl.BlockSpec((1, row_tile, d), lambda bi, i: (bi, i, 0)),
            _resident((1, d)), _resident_layer(w_xq),
            pl.BlockSpec((1, m, 2 * XATTN_D), lambda bi, i: (bi, 0, 0)),
            _resident_layer(w_xo),
        ],
        out_specs=pl.BlockSpec((1, row_tile, d), lambda bi, i: (bi, i, 0)),
        out_shape=jax.ShapeDtypeStruct(x.shape, F32),
        compiler_params=_params("parallel", "parallel"),
        name="mem_cross_attention",
    )(x, norm_g.reshape(1, d), w_xq.stacked, kv, w_xo.stacked)


def _ffn_kernel(x_ref, halo_ref, g_ref, wup_ref, cw_ref, cb_ref, wdn_ref, gf_ref, o_ref,
                slab_scr, h_scr, up_scr, act_scr, acc_scr, *, final_norm):
    tile_rows = o_ref.shape[1]
    rows = acc_scr.shape[0]
    groups = rows // F32_SUBLANES
    slabs = slab_scr.shape[0]

    def lanes(j):
        return slice(j * LANES, (j + 1) * LANES)

    for j in range(slabs):
        slab_scr[j, :tile_rows, :] = x_ref[0, :, lanes(j)]
    slab_scr[:, tile_rows:, :] = jnp.zeros((slabs, rows - tile_rows, LANES), F32)
    x = jnp.concatenate(
        [jnp.concatenate([slab_scr[j, pl.ds(gi, F32_SUBLANES, stride=groups), :] for j in range(slabs)], axis=1)
         for gi in range(groups)], axis=0)

    g = g_ref[...]
    halo = _rms_norm(halo_ref[0], g)
    h_scr[:FFN_HALO, :] = jnp.where(pl.program_id(1) > 0, halo, 0.0).astype(BF16)
    h_scr[FFN_HALO:, :] = _rms_norm(x, g).astype(BF16)
    acc_scr[...] = x

    n_chunks = D_FF // FFN_COL_CHUNK
    first_sublane = lax.broadcasted_iota(jnp.int32, (F32_SUBLANES, FFN_COL_CHUNK), 0) == 0

    def cols(c, half):
        start = half * D_FF + c * FFN_COL_CHUNK
        return slice(start, start + FFN_COL_CHUNK)

    def up_proj(c, slot):
        h = h_scr[...]
        for half in range(2):
            up = up_scr.at[slot, half]
            up[...] = _dot(h, wup_ref[:, cols(c, half)])
            prev_tile = up[FFN_HALO - F32_SUBLANES:FFN_HALO, :]
            for shift in range(1, FFN_K):
                wrapped = up[FFN_HALO + rows - shift * F32_SUBLANES:FFN_HALO + rows - (shift - 1) * F32_SUBLANES, :]
                block = jnp.where(first_sublane, pltpu.roll(prev_tile, shift, 0), pltpu.roll(wrapped, 1, 0))
                up[FFN_HALO - shift * F32_SUBLANES:FFN_HALO - (shift - 1) * F32_SUBLANES, :] = block

    def conv(c, slot, half):
        out = cb_ref[:, cols(c, half)]
        for k in range(FFN_K):
            window = up_scr[slot, half, k * F32_SUBLANES:k * F32_SUBLANES + rows, :]
            out = out + cw_ref[k:k + 1, cols(c, half)] * window
        return out

    def gate(c, slot):
        act_scr[slot] = (jax.nn.silu(conv(c, slot, 0)) * conv(c, slot, 1)).astype(BF16)

    def down_proj(c, slot):
        acc_scr[...] += _dot(act_scr[slot], wdn_ref[cols(c, 0), :])

    def stage(c, slot):
        up_proj(c + 1, 1 - slot)
        down_proj(c - 1, 1 - slot)
        gate(c, slot)

    up_proj(0, 0)
    up_proj(1, 1)
    gate(0, 0)
    for c in range(1, n_chunks - 1):
        stage(c, c % 2)
    down_proj(n_chunks - 2, (n_chunks - 2) % 2)
    gate(n_chunks - 1, (n_chunks - 1) % 2)
    down_proj(n_chunks - 1, (n_chunks - 1) % 2)

    for gi in range(groups):
        out = acc_scr[gi * F32_SUBLANES:(gi + 1) * F32_SUBLANES, :]
        if final_norm:
            out = _rms_norm(out, gf_ref[...])
        for j in range(slabs):
            slab_scr[j, pl.ds(gi, F32_SUBLANES, stride=groups), :] = out[:, lanes(j)]
    for j in range(slabs):
        o_ref[0, :, lanes(j)] = slab_scr[j, :tile_rows, :]


def _conv_ffn(x, norm_g, w_up, conv_w, conv_b, w_down, final_g, final_norm, row_tile=FFN_ROW_TILE):
    b, s, d = x.shape
    halo_blocks = row_tile // FFN_HALO
    conv_b = conv_b.reshape(1, 2 * D_FF)
    perm_rows = row_tile + 2 * F32_SUBLANES
    return pl.pallas_call(
        functools.partial(_ffn_kernel, final_norm=final_norm),
        grid=(b, s // row_tile),
        in_specs=[
            pl.BlockSpec((1, row_tile, d), lambda bi, i: (bi, i, 0)),
            pl.BlockSpec((1, FFN_HALO, d), lambda bi, i: (bi, jnp.maximum(i * halo_blocks - 1, 0), 0)),
            _resident((1, d)), _resident_layer(w_up), _resident(conv_w.shape), _resident(conv_b.shape),
            _resident_layer(w_down), _resident((1, d)),
        ],
        out_specs=pl.BlockSpec((1, row_tile, d), lambda bi, i: (bi, i, 0)),
        out_shape=jax.ShapeDtypeStruct(x.shape, F32),
        scratch_shapes=[
            pltpu.VMEM((d // LANES, perm_rows, LANES), F32),
            pltpu.VMEM((FFN_HALO + perm_rows, d), BF16),
            pltpu.VMEM((2, 2, FFN_HALO + perm_rows, FFN_COL_CHUNK), F32),
            pltpu.VMEM((2, perm_rows, FFN_COL_CHUNK), BF16),
            pltpu.VMEM((perm_rows, d), F32),
        ],
        compiler_params=_params("parallel", "parallel"),
        name="conv_ffn",
    )(x, x, norm_g.reshape(1, d), w_up.stacked, conv_w, conv_b, w_down.stacked, final_g.reshape(1, d))


def kernel(x, mem, rel_bias, norm_mix_g, w_in, gate_b, conv_w, conv_b, conv_ln_g, conv_ln_b, lam_q1, lam_k1, lam_q2, lam_k2, subln_g, gmlp_ln_g, gmlp_ln_b, w_s, b_s, w_br, w_out, norm_xattn_g, norm_mem_g, w_xq, w_xkv, w_xo, norm_ffn_g, w_up, ffn_conv_w, ffn_conv_b, w_down, norm_final_g):
    b, s, d = x.shape
    m = mem.shape[1]
    rows = b * s
    bias_tiles = _bias_tiles(rel_bias)
    w_in, w_br, w_out, w_xq, w_xkv, w_xo, w_up, w_down = (
        _to_bf16(w) for w in (w_in, w_br, w_out, w_xq, w_xkv, w_xo, w_up, w_down))
    for l in range(DEPTH):
        layer = lambda w: _LayerWeight(w, l)
        lambda_init = 0.8 - 0.6 * math.exp(-0.3 * l)
        ya, pq, pk, pv, pc, pg = _mixer_in_proj(x, norm_mix_g[l], layer(w_in), conv_w[l], conv_b[l],
                                                conv_ln_g[l], conv_ln_b[l])
        yb = _diff_attention(pq, pk, pv, bias_tiles, lam_q1[l], lam_k1[l], lam_q2[l], lam_k2[l], subln_g[l],
                             lambda_init)
        yc = _gmlp_branch(pc.reshape(rows, COL_C), gmlp_ln_g[l], gmlp_ln_b[l], w_s[l], b_s[l])
        x = _merge(x.reshape(rows, d), ya.reshape(rows, BRANCH_D), yb.reshape(rows, BRANCH_D), yc,
                   pg.reshape(rows, COL_G), gate_b[l], layer(w_br), layer(w_out)).reshape(b, s, d)
        (kv,) = _norm_proj(mem.reshape(b * m, d), norm_mem_g[l], layer(w_xkv), (2 * XATTN_D,))
        x = _mem_cross_attention(x, kv.reshape(b, m, 2 * XATTN_D), norm_xattn_g[l], layer(w_xq), layer(w_xo))
        x = _conv_ffn(x, norm_ffn_g[l], layer(w_up), ffn_conv_w[l], ffn_conv_b[l], layer(w_down), norm_final_g,
                      final_norm=(l == DEPTH - 1))
    return x
```

```python
import functools
import math
from typing import NamedTuple

import jax
import jax.numpy as jnp
from jax import lax
from jax.experimental import pallas as pl
from jax.experimental.pallas import tpu as pltpu

F32 = jnp.float32
BF16 = jnp.bfloat16

D_MODEL = 1024
DEPTH = 2
CHUNK = 64
BRANCH_D = D_MODEL // 2
N_BRANCH = 3
CONV_K = 31
DIFF_HEADS = 4
DIFF_DH = D_MODEL // 16
DIFF_VD = 2 * DIFF_DH
GMLP_GROUPS = 4
GMLP_GD = BRANCH_D // GMLP_GROUPS
GMLP_CHUNK = 128
N_REL_BUCKETS = 32
REL_MAX_DIST = 128
XATTN_HEADS = 4
XATTN_DH = D_MODEL // 8
XATTN_D = XATTN_HEADS * XATTN_DH
D_FF = 2816
FFN_K = 3
EPS = 1e-6
NEG_INF = -1e30

COL_A = 2 * BRANCH_D
COL_QK = DIFF_HEADS * 2 * DIFF_DH
COL_V = DIFF_HEADS * DIFF_VD
COL_C = 2 * BRANCH_D
COL_G = N_BRANCH * D_MODEL
IN_SPLIT_WIDTHS = (COL_A, COL_QK, COL_QK, COL_V, COL_C)
COL_BRANCHES = sum(IN_SPLIT_WIDTHS)
LOG2_E = math.log2(math.e)
IN_SPLIT_SCALES = (1.0, DIFF_DH ** -0.5 * LOG2_E, 1.0, 1.0, 1.0)

LANES = 128
MXU_WIDTH = 256
F32_SUBLANES = 8
BF16_SUBLANES = 16
VMEM_LIMIT_BYTES = 56 * 1024 * 1024
CAST_BLOCK_BYTES = 4 * 1024 * 1024

ROW_TILE = 512
FFN_ROW_TILE = 1024
PROJ_COL_CHUNK = 512
ATTN_TILE = 256
CONV_HALO = 32
CONV_ROW_CHUNK = 64
FFN_HALO = BF16_SUBLANES
FFN_COL_CHUNK = MXU_WIDTH

assert CONV_HALO >= CONV_K - 1 and FFN_HALO == F32_SUBLANES * (FFN_K - 1)
assert D_FF % FFN_COL_CHUNK == 0
assert ATTN_TILE % CHUNK == 0 and ATTN_TILE > REL_MAX_DIST


def _params(*semantics):
    return pltpu.CompilerParams(dimension_semantics=semantics, vmem_limit_bytes=VMEM_LIMIT_BYTES)


def _resident(shape):
    zeros = (0,) * len(shape)
    return pl.BlockSpec(shape, lambda *_: zeros, pipeline_mode=pl.Buffered(1))


class _LayerWeight(NamedTuple):
    stacked: jax.Array
    layer: int

    @property
    def shape(self):
        return self.stacked.shape[1:]


def _resident_layer(w):
    index = (w.layer,) + (0,) * len(w.shape)
    return pl.BlockSpec((None,) + tuple(w.shape), lambda *_: index, pipeline_mode=pl.Buffered(1))


def _cast_kernel(w_ref, *o_refs):
    col = 0
    for o_ref in o_refs:
        width = o_ref.shape[-1]
        o_ref[...] = w_ref[:, col:col + width].astype(BF16)
        col += width


def _to_bf16(w, widths=None):
    cols = w.shape[-1]
    out_widths = widths or (cols,)
    assert sum(out_widths) == cols and all(wd % LANES == 0 for wd in out_widths)
    flat = w.reshape(-1, cols)
    rows = flat.shape[0]
    block_rows = max(r for r in range(BF16_SUBLANES, rows + 1, BF16_SUBLANES)
                     if rows % r == 0 and r * cols * 4 <= CAST_BLOCK_BYTES)
    outs = pl.pallas_call(
        _cast_kernel,
        grid=(rows // block_rows,),
        in_specs=[pl.BlockSpec((block_rows, cols), lambda i: (i, 0))],
        out_specs=[pl.BlockSpec((block_rows, wd), lambda i: (i, 0)) for wd in out_widths],
        out_shape=[jax.ShapeDtypeStruct((rows, wd), BF16) for wd in out_widths],
        compiler_params=_params("parallel"),
        name="weight_cast",
    )(flat)
    outs = [o.reshape(w.shape[:-1] + (wd,)) for o, wd in zip(outs, out_widths)]
    return outs if widths else outs[0]


def _rms_norm(x, g):
    return x * lax.rsqrt(jnp.mean(x * x, axis=-1, keepdims=True) + EPS) * g


def _layer_norm(x, g, b):
    mu = jnp.mean(x, axis=-1, keepdims=True)
    xc = x - mu
    var = jnp.mean(xc * xc, axis=-1, keepdims=True)
    return xc * lax.rsqrt(var + EPS) * g + b


def _dot(a, b):
    return jnp.dot(a, b, preferred_element_type=F32)


def _dot_nt(a, b):
    return lax.dot_general(a, b, (((1,), (1,)), ((), ())), preferred_element_type=F32)


def _norm_proj_kernel(x_ref, g_ref, w_ref, *out_refs, widths, scales):
    xn = _rms_norm(x_ref[...], g_ref[...]).astype(BF16)
    col = 0
    for o_ref, width, scale in zip(out_refs, widths, scales):
        for c in range(0, width, PROJ_COL_CHUNK):
            out = _dot(xn, w_ref[:, col + c:col + c + PROJ_COL_CHUNK])
            if scale != 1.0:
                out = out * scale
            o_ref[:, c:c + PROJ_COL_CHUNK] = out.astype(BF16)
        col += width


def _norm_proj(x, g, w, widths, scales=None, row_tile=ROW_TILE):
    rows, d = x.shape
    scales = scales or (1.0,) * len(widths)
    assert w.shape == (d, sum(widths)) and all(wd % PROJ_COL_CHUNK == 0 for wd in widths)
    return pl.pallas_call(
        functools.partial(_norm_proj_kernel, widths=widths, scales=scales),
        grid=(rows // row_tile,),
        in_specs=[pl.BlockSpec((row_tile, d), lambda i: (i, 0)), _resident((1, d)), _resident_layer(w)],
        out_specs=[pl.BlockSpec((row_tile, wd), lambda i: (i, 0)) for wd in widths],
        out_shape=[jax.ShapeDtypeStruct((rows, wd), BF16) for wd in widths],
        compiler_params=_params("parallel"),
        name="norm_proj",
    )(x, g.reshape(1, d), w.stacked)


def _mixer_in_kernel(x_ref, g_ref, w_ref, cw_ref, cb_ref, lng_ref, lnb_ref,
                     ya_ref, pq_ref, pk_ref, pv_ref, pc_ref, xn_scr, xs_scr, halo_scr):
    rows = ya_ref.shape[1]
    xn_scr[...] = _rms_norm(x_ref[0], g_ref[...]).astype(BF16)

    def proj(col, width):
        return _dot(xn_scr[...], w_ref[:, col:col + width])

    glu = proj(0, BRANCH_D) * jax.nn.sigmoid(proj(BRANCH_D, BRANCH_D))

    @pl.when(pl.program_id(1) == 0)
    def _():
        halo_scr[...] = jnp.zeros(halo_scr.shape, F32)

    xs_scr[0, :CONV_HALO, :] = halo_scr[...]
    xs_scr[0, CONV_HALO:, :] = glu
    halo_scr[...] = glu[rows - CONV_HALO:, :]

    def shifted_copies():
        shifted_rows = CONV_HALO + rows - F32_SUBLANES
        for b in range(1, F32_SUBLANES):
            xs_scr[b, :shifted_rows, :] = xs_scr[0, b:b + shifted_rows, :]

    first_tap = CONV_HALO - (CONV_K - 1)

    def conv_rows(r):
        acc = jnp.broadcast_to(cb_ref[...], (CONV_ROW_CHUNK, BRANCH_D))
        for k in range(CONV_K):
            b = (first_tap + k) % F32_SUBLANES
            start = r + first_tap + k - b
            acc = acc + cw_ref[k:k + 1, :] * xs_scr[b, start:start + CONV_ROW_CHUNK, :]
        ya_ref[0, r:r + CONV_ROW_CHUNK, :] = jax.nn.silu(_layer_norm(acc, lng_ref[...], lnb_ref[...])).astype(BF16)

    def proj_chunk(o_ref, col, c, scale):
        out = proj(col + c, PROJ_COL_CHUNK)
        if scale != 1.0:
            out = out * scale
        o_ref[0, :, c:c + PROJ_COL_CHUNK] = out.astype(BF16)

    col = COL_A
    for o_ref, width, scale in zip((pq_ref, pk_ref, pv_ref, pc_ref), IN_SPLIT_WIDTHS[1:], IN_SPLIT_SCALES[1:]):
        for c in range(0, width, PROJ_COL_CHUNK):
            proj_chunk(o_ref, col, c, scale)
        col += width
    shifted_copies()
    for r in range(0, rows, CONV_ROW_CHUNK):
        conv_rows(r)


def _mixer_in_proj(x, norm_g, w_in, conv_w, conv_b, ln_g, ln_b, row_tile=ROW_TILE):
    b, s, d = x.shape
    vec = lambda a: a.reshape(1, -1)
    out_widths = (BRANCH_D,) + IN_SPLIT_WIDTHS[1:]
    return pl.pallas_call(
        _mixer_in_kernel,
        grid=(b, s // row_tile),
        in_specs=[
            pl.BlockSpec((1, row_tile, d), lambda bi, i: (bi, i, 0)),
            _resident((1, d)), _resident_layer(w_in),
            _resident((CONV_K, BRANCH_D)), _resident((1, BRANCH_D)), _resident((1, BRANCH_D)),
            _resident((1, BRANCH_D)),
        ],
        out_specs=[pl.BlockSpec((1, row_tile, wd), lambda bi, i: (bi, i, 0)) for wd in out_widths],
        out_shape=[jax.ShapeDtypeStruct((b, s, wd), BF16) for wd in out_widths],
        scratch_shapes=[
            pltpu.VMEM((row_tile, d), BF16),
            pltpu.VMEM((F32_SUBLANES, CONV_HALO + row_tile, BRANCH_D), F32),
            pltpu.VMEM((CONV_HALO, BRANCH_D), F32),
        ],
        compiler_params=_params("parallel", "arbitrary"),
        name="mixer_in_proj",
    )(x, vec(norm_g), w_in.stacked, conv_w, vec(conv_b), vec(ln_g), vec(ln_b))


def _rel_bucket(rel):
    half = N_REL_BUCKETS // 2
    max_exact = half // 2
    n = jnp.abs(rel)
    log_ratio = jnp.log(jnp.maximum(n, 1).astype(F32) / max_exact) / math.log(REL_MAX_DIST / max_exact)
    large = jnp.minimum(max_exact + (log_ratio * (half - max_exact)).astype(jnp.int32), half - 1)
    return jnp.where(rel > 0, half, 0) + jnp.where(n < max_exact, n, large)


FAR_BUCKET = N_REL_BUCKETS // 2 - 1
MASKED_BUCKET = -1


def _near_buckets():
    r = jnp.arange(ATTN_TILE)[:, None]
    c = jnp.arange(ATTN_TILE)[None, :]
    visible = (c // CHUNK) <= (r // CHUNK)
    diag = jnp.where(visible, _rel_bucket(c - r), MASKED_BUCKET)
    sub = _rel_bucket(c - r - ATTN_TILE)
    return jnp.stack([diag, sub]).astype(jnp.int32)


def _bias_tiles_kernel(table_ref, bucket_ref, o_ref):
    h = pl.program_id(0)
    bucket = bucket_ref[...]
    far = table_ref[FAR_BUCKET, h]
    out = jnp.zeros(bucket.shape, F32)
    for b in range(N_REL_BUCKETS):
        out = jnp.where(bucket == b, (table_ref[b, h] - far) * LOG2_E, out)
    o_ref[0] = jnp.where(bucket == MASKED_BUCKET, NEG_INF, out)


def _bias_tiles(rel_bias):
    buckets = _near_buckets()
    return pl.pallas_call(
        _bias_tiles_kernel,
        grid=(DIFF_HEADS,),
        in_specs=[pl.BlockSpec(memory_space=pltpu.SMEM), _resident(buckets.shape)],
        out_specs=pl.BlockSpec((1,) + buckets.shape, lambda h: (h, 0, 0, 0)),
        out_shape=jax.ShapeDtypeStruct((DIFF_HEADS,) + buckets.shape, F32),
        compiler_params=_params("arbitrary"),
        name="rel_bias_tiles",
    )(rel_bias.astype(F32), buckets)


def _diff_attn_kernel(q_ref, k_ref, v_ref, bias_ref, lq1_ref, lk1_ref, lq2_ref, lk2_ref, g_ref, o_ref,
                      q2_scr, s_scr, m_scr, acc_scr, vones_scr, *, lambda_init):
    t = ATTN_TILE
    i = pl.program_id(1)
    heads = range(DIFF_HEADS)

    def head_cols(h):
        return slice(h * DIFF_VD, (h + 1) * DIFF_VD)

    @pl.when(i == 0)
    def _():
        for h in heads:
            vones_scr[h, :, :DIFF_VD] = v_ref[0, :, head_cols(h)]
            vones_scr[h, :, DIFF_VD:] = jnp.ones((vones_scr.shape[1], LANES), BF16)

    q = q_ref[0]
    lane = lax.broadcasted_iota(jnp.int32, (t, DIFF_VD), 1)
    for h in heads:
        qh = q[:, head_cols(h)]
        q2_scr[h, :t, :] = jnp.where(lane < DIFF_DH, qh, 0).astype(BF16)
        q2_scr[h, t:, :] = jnp.where(lane >= DIFF_DH, qh, 0).astype(BF16)
    m_scr[...] = jnp.full(m_scr.shape, NEG_INF, F32)
    acc_scr[...] = jnp.zeros(acc_scr.shape, F32)

    def logits_tiles(j0, near):
        for n, bias_index in enumerate(near):
            start = pl.multiple_of((j0 + n) * t, t)
            for h in heads:
                s = _dot_nt(q2_scr[h], k_ref[0, pl.ds(start, t), head_cols(h)])
                if bias_index is not None:
                    bias = bias_ref[h, bias_index]
                    s = s + jnp.concatenate([bias, bias], axis=0)
                s_scr[j0 + n, h] = s
                m_scr[h] = jnp.maximum(m_scr[h], jnp.maximum(s[:, :LANES], s[:, LANES:]))

    def far_pair(pair, carry):
        logits_tiles(2 * pair, (None, None))
        return carry

    n_far = jnp.maximum(i - 1, 0)
    lax.fori_loop(0, n_far // 2, far_pair, 0)

    @pl.when(n_far % 2 == 1)
    def _():
        logits_tiles(n_far - 1, (None,))

    @pl.when(i > 0)
    def _():
        logits_tiles(i - 1, (1, 0))

    @pl.when(i == 0)
    def _():
        logits_tiles(0, (0,))

    for h in heads:
        m_scr[h] = jnp.broadcast_to(jnp.max(m_scr[h], axis=-1, keepdims=True), (2 * t, LANES))

    def pv_tiles(j0, n_tiles):
        start = pl.multiple_of(j0 * t, t)
        for h in heads:
            m = m_scr[h]
            parts = [jnp.exp2(s_scr[j0 + n, h, :, half * LANES:(half + 1) * LANES] - m)
                     for n in range(n_tiles) for half in range(t // LANES)]
            p = jnp.concatenate(parts, axis=1).astype(BF16)
            acc_scr[h] += _dot(p, vones_scr[h, pl.ds(start, n_tiles * t), :])

    def pv_pair(pair, carry):
        pv_tiles(2 * pair, 2)
        return carry

    n_visible = i + 1
    lax.fori_loop(0, n_visible // 2, pv_pair, 0)

    @pl.when(n_visible % 2 == 1)
    def _():
        pv_tiles(n_visible - 1, 1)

    lam = (jnp.exp(jnp.sum(lq1_ref[...] * lk1_ref[...])) - jnp.exp(jnp.sum(lq2_ref[...] * lk2_ref[...]))
           + lambda_init)
    for h in heads:
        o = (acc_scr[h, :t, :DIFF_VD] / acc_scr[h, :t, DIFF_VD:]
             - lam * (acc_scr[h, t:, :DIFF_VD] / acc_scr[h, t:, DIFF_VD:]))
        o_ref[0, :, head_cols(h)] = (_rms_norm(o, g_ref[...]) * (1.0 - lambda_init)).astype(BF16)


def _diff_attention(pq, pk, pv, bias_tiles, lam_q1, lam_k1, lam_q2, lam_k2, subln_g, lambda_init):
    b, s, width = pq.shape
    t = ATTN_TILE
    vec = lambda a: a.reshape(1, -1).astype(F32)
    all_keys = pl.BlockSpec((1, s, width), lambda bi, i: (bi, 0, 0))
    lam_spec = _resident((1, DIFF_DH))
    return pl.pallas_call(
        functools.partial(_diff_attn_kernel, lambda_init=lambda_init),
        grid=(b, s // t),
        in_specs=[
            pl.BlockSpec((1, t, width), lambda bi, i: (bi, i, 0)),
            all_keys, all_keys, _resident(bias_tiles.shape),
            lam_spec, lam_spec, lam_spec, lam_spec, _resident((1, DIFF_VD)),
        ],
        out_specs=pl.BlockSpec((1, t, width), lambda bi, i: (bi, i, 0)),
        out_shape=jax.ShapeDtypeStruct((b, s, width), BF16),
        scratch_shapes=[
            pltpu.VMEM((DIFF_HEADS, 2 * t, DIFF_VD), BF16),
            pltpu.VMEM((s // t, DIFF_HEADS, 2 * t, t), F32),
            pltpu.VMEM((DIFF_HEADS, 2 * t, LANES), F32),
            pltpu.VMEM((DIFF_HEADS, 2 * t, DIFF_VD + LANES), F32),
            pltpu.VMEM((DIFF_HEADS, s, DIFF_VD + LANES), BF16),
        ],
        compiler_params=_params("parallel", "arbitrary"),
        name="diff_attention",
    )(pq, pk, pv, bias_tiles, vec(lam_q1), vec(lam_k1), vec(lam_q2), vec(lam_k2), vec(subln_g))


def _gmlp_kernel(pc_ref, lng_ref, lnb_ref, ws_ref, bs_ref, o_ref):
    rows = o_ref.shape[0]
    z = jax.nn.gelu(pc_ref[...].astype(F32))
    u = z[:, :BRANCH_D]
    v = _layer_norm(z[:, BRANCH_D:], lng_ref[...], lnb_ref[...]).astype(BF16)
    t_idx = lax.broadcasted_iota(jnp.int32, (GMLP_CHUNK, GMLP_CHUNK), 0)
    s_idx = lax.broadcasted_iota(jnp.int32, (GMLP_CHUNK, GMLP_CHUNK), 1)
    for g in range(GMLP_GROUPS):
        w = jnp.where(s_idx <= t_idx, ws_ref[g], 0.0).astype(BF16)
        cols = slice(g * GMLP_GD, (g + 1) * GMLP_GD)
        for r in range(0, rows, GMLP_CHUNK):
            sv = _dot(w, v[r:r + GMLP_CHUNK, cols]) + bs_ref[g]
            o_ref[r:r + GMLP_CHUNK, cols] = (u[r:r + GMLP_CHUNK, cols] * sv).astype(BF16)


def _gmlp_branch(pc, ln_g, ln_b, w_s, b_s, row_tile=ROW_TILE):
    rows = pc.shape[0]
    vec = lambda a: a.reshape(1, BRANCH_D)
    return pl.pallas_call(
        _gmlp_kernel,
        grid=(rows // row_tile,),
        in_specs=[
            pl.BlockSpec((row_tile, COL_C), lambda i: (i, 0)),
            _resident((1, BRANCH_D)), _resident((1, BRANCH_D)),
            _resident(w_s.shape), _resident((GMLP_GROUPS, GMLP_CHUNK, 1)),
        ],
        out_specs=pl.BlockSpec((row_tile, BRANCH_D), lambda i: (i, 0)),
        out_shape=jax.ShapeDtypeStruct((rows, BRANCH_D), BF16),
        compiler_params=_params("parallel"),
        name="gmlp_branch",
    )(pc, vec(ln_g), vec(ln_b), w_s, b_s.reshape(GMLP_GROUPS, GMLP_CHUNK, 1))


def _merge_kernel(x_ref, ya_ref, yb_ref, yc_ref, g_ref, wg_ref, gb_ref, wbr_ref, wout_ref, o_ref):
    x = x_ref[...]
    xn = _rms_norm(x, g_ref[...]).astype(BF16)
    merged = None
    for n, y_ref in enumerate((ya_ref, yb_ref, yc_ref)):
        logits = _dot(xn, wg_ref[:, n * D_MODEL:(n + 1) * D_MODEL]) + gb_ref[n:n + 1, :]
        term = jax.nn.sigmoid(logits) * _dot(y_ref[...], wbr_ref[n])
        merged = term if merged is None else merged + term
    o_ref[...] = x + _dot(merged.astype(BF16), wout_ref[...])


def _merge(x, ya, yb, yc, norm_g, w_gate, gate_b, w_br, w_out, row_tile=ROW_TILE):
    rows, d = x.shape
    row_spec = lambda width: pl.BlockSpec((row_tile, width), lambda i: (i, 0))
    return pl.pallas_call(
        _merge_kernel,
        grid=(rows // row_tile,),
        in_specs=[row_spec(D_MODEL), row_spec(BRANCH_D), row_spec(BRANCH_D), row_spec(BRANCH_D),
                  _resident((1, d)), _resident_layer(w_gate), _resident(gate_b.shape), _resident_layer(w_br),
                  _resident_layer(w_out)],
        out_specs=row_spec(D_MODEL),
        out_shape=jax.ShapeDtypeStruct(x.shape, F32),
        compiler_params=_params("parallel"),
        name="branch_merge",
    )(x, ya, yb, yc, norm_g.reshape(1, d), w_gate.stacked, gate_b, w_br.stacked, w_out.stacked)


def _xattn_kernel(x_ref, g_ref, wq_ref, kv_ref, wo_ref, o_ref):
    x = x_ref[0]
    q = _dot(_rms_norm(x, g_ref[...]).astype(BF16), wq_ref[...]) * (XATTN_DH ** -0.5 * LOG2_E)
    q = q.astype(BF16)
    heads = []
    for h in range(XATTN_HEADS):
        cols = slice(h * XATTN_DH, (h + 1) * XATTN_DH)
        k = kv_ref[0, :, cols]
        v = kv_ref[0, :, XATTN_D + h * XATTN_DH:XATTN_D + (h + 1) * XATTN_DH]
        s = _dot_nt(q[:, cols], k)
        p = jnp.exp2(s - jnp.max(s, axis=-1, keepdims=True))
        o = _dot(p.astype(BF16), v) / jnp.sum(p, axis=-1, keepdims=True)
        heads.append(o.astype(BF16))
    o_ref[0] = x + _dot(jnp.concatenate(heads, axis=-1), wo_ref[...])


def _mem_cross_attention(x, kv, norm_g, w_xq, w_xo, row_tile=ROW_TILE):
    b, s, d = x.shape
    m = kv.shape[1]
    return pl.pallas_call(
        _xattn_kernel,
        grid=(b, s // row_tile),
        in_specs=[
            pl.BlockSpec((1, row_tile, d), lambda bi, i: (bi, i, 0)),
            _resident((1, d)), _resident_layer(w_xq),
            pl.BlockSpec((1, m, 2 * XATTN_D), lambda bi, i: (bi, 0, 0)),
            _resident_layer(w_xo),
        ],
        out_specs=pl.BlockSpec((1, row_tile, d), lambda bi, i: (bi, i, 0)),
        out_shape=jax.ShapeDtypeStruct(x.shape, F32),
        compiler_params=_params("parallel", "parallel"),
        name="mem_cross_attention",
    )(x, norm_g.reshape(1, d), w_xq.stacked, kv, w_xo.stacked)


def _ffn_kernel(x_ref, halo_ref, g_ref, wup_ref, cw_ref, cb_ref, wdn_ref, gf_ref, o_ref,
                slab_scr, h_scr, up_scr, act_scr, acc_scr, *, final_norm):
    tile_rows = o_ref.shape[1]
    rows = acc_scr.shape[0]
    groups = rows // F32_SUBLANES
    slabs = slab_scr.shape[0]

    def lanes(j):
        return slice(j * LANES, (j + 1) * LANES)

    for j in range(slabs):
        slab_scr[j, :tile_rows, :] = x_ref[0, :, lanes(j)]
    slab_scr[:, tile_rows:, :] = jnp.zeros((slabs, rows - tile_rows, LANES), F32)
    x = jnp.concatenate(
        [jnp.concatenate([slab_scr[j, pl.ds(gi, F32_SUBLANES, stride=groups), :] for j in range(slabs)], axis=1)
         for gi in range(groups)], axis=0)

    g = g_ref[...]
    halo = _rms_norm(halo_ref[0], g)
    h_scr[:FFN_HALO, :] = jnp.where(pl.program_id(1) > 0, halo, 0.0).astype(BF16)
    h_scr[FFN_HALO:, :] = _rms_norm(x, g).astype(BF16)
    acc_scr[...] = x

    n_chunks = D_FF // FFN_COL_CHUNK
    first_sublane = lax.broadcasted_iota(jnp.int32, (F32_SUBLANES, FFN_COL_CHUNK), 0) == 0

    def cols(c, half):
        start = half * D_FF + c * FFN_COL_CHUNK
        return slice(start, start + FFN_COL_CHUNK)

    def up_proj(c, slot):
        h = h_scr[...]
        for half in range(2):
            up = up_scr.at[slot, half]
            up[...] = _dot(h, wup_ref[:, cols(c, half)])
            prev_tile = up[FFN_HALO - F32_SUBLANES:FFN_HALO, :]
            for shift in range(1, FFN_K):
                wrapped = up[FFN_HALO + rows - shift * F32_SUBLANES:FFN_HALO + rows - (shift - 1) * F32_SUBLANES, :]
                block = jnp.where(first_sublane, pltpu.roll(prev_tile, shift, 0), pltpu.roll(wrapped, 1, 0))
                up[FFN_HALO - shift * F32_SUBLANES:FFN_HALO - (shift - 1) * F32_SUBLANES, :] = block

    def conv(c, slot, half):
        out = cb_ref[:, cols(c, half)]
        for k in range(FFN_K):
            window = up_scr[slot, half, k * F32_SUBLANES:k * F32_SUBLANES + rows, :]
            out = out + cw_ref[k:k + 1, cols(c, half)] * window
        return out

    def gate(c, slot):
        act_scr[slot] = (jax.nn.silu(conv(c, slot, 0)) * conv(c, slot, 1)).astype(BF16)

    def down_proj(c, slot):
        acc_scr[...] += _dot(act_scr[slot], wdn_ref[cols(c, 0), :])

    def stage(c, slot):
        up_proj(c + 1, 1 - slot)
        down_proj(c - 1, 1 - slot)
        gate(c, slot)

    up_proj(0, 0)
    up_proj(1, 1)
    gate(0, 0)
    for c in range(1, n_chunks - 1):
        stage(c, c % 2)
    down_proj(n_chunks - 2, (n_chunks - 2) % 2)
    gate(n_chunks - 1, (n_chunks - 1) % 2)
    down_proj(n_chunks - 1, (n_chunks - 1) % 2)

    for gi in range(groups):
        out = acc_scr[gi * F32_SUBLANES:(gi + 1) * F32_SUBLANES, :]
        if final_norm:
            out = _rms_norm(out, gf_ref[...])
        for j in range(slabs):
            slab_scr[j, pl.ds(gi, F32_SUBLANES, stride=groups), :] = out[:, lanes(j)]
    for j in range(slabs):
        o_ref[0, :, lanes(j)] = slab_scr[j, :tile_rows, :]


def _conv_ffn(x, norm_g, w_up, conv_w, conv_b, w_down, final_g, final_norm, row_tile=FFN_ROW_TILE):
    b, s, d = x.shape
    halo_blocks = row_tile // FFN_HALO
    conv_b = conv_b.reshape(1, 2 * D_FF)
    perm_rows = row_tile + 2 * F32_SUBLANES
    return pl.pallas_call(
        functools.partial(_ffn_kernel, final_norm=final_norm),
        grid=(b, s // row_tile),
        in_specs=[
            pl.BlockSpec((1, row_tile, d), lambda bi, i: (bi, i, 0)),
            pl.BlockSpec((1, FFN_HALO, d), lambda bi, i: (bi, jnp.maximum(i * halo_blocks - 1, 0), 0)),
            _resident((1, d)), _resident_layer(w_up), _resident(conv_w.shape), _resident(conv_b.shape),
            _resident_layer(w_down), _resident((1, d)),
        ],
        out_specs=pl.BlockSpec((1, row_tile, d), lambda bi, i: (bi, i, 0)),
        out_shape=jax.ShapeDtypeStruct(x.shape, F32),
        scratch_shapes=[
            pltpu.VMEM((d // LANES, perm_rows, LANES), F32),
            pltpu.VMEM((FFN_HALO + perm_rows, d), BF16),
            pltpu.VMEM((2, 2, FFN_HALO + perm_rows, FFN_COL_CHUNK), F32),
            pltpu.VMEM((2, perm_rows, FFN_COL_CHUNK), BF16),
            pltpu.VMEM((perm_rows, d), F32),
        ],
        compiler_params=_params("parallel", "parallel"),
        name="conv_ffn",
    )(x, x, norm_g.reshape(1, d), w_up.stacked, conv_w, conv_b, w_down.stacked, final_g.reshape(1, d))


def kernel(x, mem, rel_bias, norm_mix_g, w_in, gate_b, conv_w, conv_b, conv_ln_g, conv_ln_b, lam_q1, lam_k1, lam_q2, lam_k2, subln_g, gmlp_ln_g, gmlp_ln_b, w_s, b_s, w_br, w_out, norm_xattn_g, norm_mem_g, w_xq, w_xkv, w_xo, norm_ffn_g, w_up, ffn_conv_w, ffn_conv_b, w_down, norm_final_g):
    b, s, d = x.shape
    m = mem.shape[1]
    rows = b * s
    bias_tiles = _bias_tiles(rel_bias)
    w_in, w_gate = _to_bf16(w_in, (COL_BRANCHES, COL_G))
    w_br, w_out, w_xq, w_xkv, w_xo, w_up, w_down = (
        _to_bf16(w) for w in (w_br, w_out, w_xq, w_xkv, w_xo, w_up, w_down))
    for l in range(DEPTH):
        layer = lambda w: _LayerWeight(w, l)
        lambda_init = 0.8 - 0.6 * math.exp(-0.3 * l)
        ya, pq, pk, pv, pc = _mixer_in_proj(x, norm_mix_g[l], layer(w_in), conv_w[l], conv_b[l], conv_ln_g[l],
                                            conv_ln_b[l])
        yb = _diff_attention(pq, pk, pv, bias_tiles, lam_q1[l], lam_k1[l], lam_q2[l], lam_k2[l], subln_g[l],
                             lambda_init)
        yc = _gmlp_branch(pc.reshape(rows, COL_C), gmlp_ln_g[l], gmlp_ln_b[l], w_s[l], b_s[l])
        x = _merge(x.reshape(rows, d), ya.reshape(rows, BRANCH_D), yb.reshape(rows, BRANCH_D), yc, norm_mix_g[l],
                   layer(w_gate), gate_b[l], layer(w_br), layer(w_out)).reshape(b, s, d)
        (kv,) = _norm_proj(mem.reshape(b * m, d), norm_mem_g[l], layer(w_xkv), (2 * XATTN_D,))
        x = _mem_cross_attention(x, kv.reshape(b, m, 2 * XATTN_D), norm_xattn_g[l], layer(w_xq), layer(w_xo))
        x = _conv_ffn(x, norm_ffn_g[l], layer(w_up), ffn_conv_w[l], ffn_conv_b[l], layer(w_down), norm_final_g,
                      final_norm=(l == DEPTH - 1))
    return x
```

```python
import functools
import math
from typing import NamedTuple

import jax
import jax.numpy as jnp
from jax import lax
from jax.experimental import pallas as pl
from jax.experimental.pallas import tpu as pltpu

F32 = jnp.float32
BF16 = jnp.bfloat16

D_MODEL = 1024
DEPTH = 2
CHUNK = 64
BRANCH_D = D_MODEL // 2
N_BRANCH = 3
CONV_K = 31
DIFF_HEADS = 4
DIFF_DH = D_MODEL // 16
DIFF_VD = 2 * DIFF_DH
GMLP_GROUPS = 4
GMLP_GD = BRANCH_D // GMLP_GROUPS
GMLP_CHUNK = 128
N_REL_BUCKETS = 32
REL_MAX_DIST = 128
XATTN_HEADS = 4
XATTN_DH = D_MODEL // 8
XATTN_D = XATTN_HEADS * XATTN_DH
D_FF = 2816
FFN_K = 3
EPS = 1e-6
NEG_INF = -1e30

COL_A = 2 * BRANCH_D
COL_QK = DIFF_HEADS * 2 * DIFF_DH
COL_V = DIFF_HEADS * DIFF_VD
COL_C = 2 * BRANCH_D
COL_G = N_BRANCH * D_MODEL
IN_SPLIT_WIDTHS = (COL_A, COL_QK, COL_QK, COL_V, COL_C)
COL_BRANCHES = sum(IN_SPLIT_WIDTHS)
LOG2_E = math.log2(math.e)
IN_SPLIT_SCALES = (1.0, DIFF_DH ** -0.5 * LOG2_E, 1.0, 1.0, 1.0)

LANES = 128
MXU_WIDTH = 256
F32_SUBLANES = 8
BF16_SUBLANES = 16
VMEM_LIMIT_BYTES = 56 * 1024 * 1024
CAST_BLOCK_BYTES = 4 * 1024 * 1024

ROW_TILE = 512
FFN_ROW_TILE = 1024
PROJ_COL_CHUNK = 512
ATTN_TILE = 256
CONV_HALO = 32
CONV_ROW_CHUNK = 64
FFN_HALO = BF16_SUBLANES
FFN_COL_CHUNK = MXU_WIDTH

assert CONV_HALO >= CONV_K - 1 and FFN_HALO == F32_SUBLANES * (FFN_K - 1)
assert D_FF % FFN_COL_CHUNK == 0
assert ATTN_TILE % CHUNK == 0 and ATTN_TILE > REL_MAX_DIST


def _params(*semantics):
    return pltpu.CompilerParams(dimension_semantics=semantics, vmem_limit_bytes=VMEM_LIMIT_BYTES)


def _resident(shape):
    zeros = (0,) * len(shape)
    return pl.BlockSpec(shape, lambda *_: zeros, pipeline_mode=pl.Buffered(1))


class _LayerWeight(NamedTuple):
    stacked: jax.Array
    layer: int

    @property
    def shape(self):
        return self.stacked.shape[1:]


def _resident_layer(w):
    index = (w.layer,) + (0,) * len(w.shape)
    return pl.BlockSpec((None,) + tuple(w.shape), lambda *_: index, pipeline_mode=pl.Buffered(1))


def _cast_kernel(w_ref, o_ref):
    o_ref[...] = w_ref[...].astype(BF16)


def _to_bf16(w, layer):
    _, rows, cols = w.shape
    block_rows = max(r for r in range(BF16_SUBLANES, rows + 1, BF16_SUBLANES)
                     if rows % r == 0 and r * cols * 4 <= CAST_BLOCK_BYTES)
    out = pl.pallas_call(
        _cast_kernel,
        grid=(rows // block_rows,),
        in_specs=[pl.BlockSpec((None, block_rows, cols), lambda i: (layer, i, 0))],
        out_specs=pl.BlockSpec((None, block_rows, cols), lambda i: (0, i, 0)),
        out_shape=jax.ShapeDtypeStruct((1, rows, cols), BF16),
        compiler_params=_params("parallel"),
        name="weight_cast",
    )(w)
    return _LayerWeight(out, 0)


def _rms_norm(x, g):
    return x * lax.rsqrt(jnp.mean(x * x, axis=-1, keepdims=True) + EPS) * g


def _layer_norm(x, g, b):
    mu = jnp.mean(x, axis=-1, keepdims=True)
    xc = x - mu
    var = jnp.mean(xc * xc, axis=-1, keepdims=True)
    return xc * lax.rsqrt(var + EPS) * g + b


def _dot(a, b):
    return jnp.dot(a, b, preferred_element_type=F32)


def _dot_nt(a, b):
    return lax.dot_general(a, b, (((1,), (1,)), ((), ())), preferred_element_type=F32)


def _norm_proj_kernel(x_ref, g_ref, w_ref, *out_refs, widths, scales):
    xn = _rms_norm(x_ref[...], g_ref[...]).astype(BF16)
    col = 0
    for o_ref, width, scale in zip(out_refs, widths, scales):
        for c in range(0, width, PROJ_COL_CHUNK):
            out = _dot(xn, w_ref[:, col + c:col + c + PROJ_COL_CHUNK])
            if scale != 1.0:
                out = out * scale
            o_ref[:, c:c + PROJ_COL_CHUNK] = out.astype(BF16)
        col += width


def _norm_proj(x, g, w, widths, scales=None, row_tile=ROW_TILE):
    rows, d = x.shape
    scales = scales or (1.0,) * len(widths)
    assert w.shape == (d, sum(widths)) and all(wd % PROJ_COL_CHUNK == 0 for wd in widths)
    return pl.pallas_call(
        functools.partial(_norm_proj_kernel, widths=widths, scales=scales),
        grid=(rows // row_tile,),
        in_specs=[pl.BlockSpec((row_tile, d), lambda i: (i, 0)), _resident((1, d)), _resident_layer(w)],
        out_specs=[pl.BlockSpec((row_tile, wd), lambda i: (i, 0)) for wd in widths],
        out_shape=[jax.ShapeDtypeStruct((rows, wd), BF16) for wd in widths],
        compiler_params=_params("parallel"),
        name="norm_proj",
    )(x, g.reshape(1, d), w.stacked)


def _mixer_in_kernel(x_ref, g_ref, w_ref, cw_ref, cb_ref, lng_ref, lnb_ref,
                     ya_ref, pq_ref, pk_ref, pv_ref, pc_ref, xn_scr, xs_scr, halo_scr):
    rows = ya_ref.shape[1]
    xn_scr[...] = _rms_norm(x_ref[0], g_ref[...]).astype(BF16)

    def proj(col, width):
        return _dot(xn_scr[...], w_ref[:, col:col + width])

    glu = proj(0, BRANCH_D) * jax.nn.sigmoid(proj(BRANCH_D, BRANCH_D))

    @pl.when(pl.program_id(1) == 0)
    def _():
        halo_scr[...] = jnp.zeros(halo_scr.shape, F32)

    xs_scr[0, :CONV_HALO, :] = halo_scr[...]
    xs_scr[0, CONV_HALO:, :] = glu
    halo_scr[...] = glu[rows - CONV_HALO:, :]

    def shifted_copies():
        shifted_rows = CONV_HALO + rows - F32_SUBLANES
        for b in range(1, F32_SUBLANES):
            xs_scr[b, :shifted_rows, :] = xs_scr[0, b:b + shifted_rows, :]

    first_tap = CONV_HALO - (CONV_K - 1)

    def conv_rows(r):
        acc = jnp.broadcast_to(cb_ref[...], (CONV_ROW_CHUNK, BRANCH_D))
        for k in range(CONV_K):
            b = (first_tap + k) % F32_SUBLANES
            start = r + first_tap + k - b
            acc = acc + cw_ref[k:k + 1, :] * xs_scr[b, start:start + CONV_ROW_CHUNK, :]
        ya_ref[0, r:r + CONV_ROW_CHUNK, :] = jax.nn.silu(_layer_norm(acc, lng_ref[...], lnb_ref[...])).astype(BF16)

    def proj_chunk(o_ref, col, c, scale):
        out = proj(col + c, PROJ_COL_CHUNK)
        if scale != 1.0:
            out = out * scale
        o_ref[0, :, c:c + PROJ_COL_CHUNK] = out.astype(BF16)

    col = COL_A
    for o_ref, width, scale in zip((pq_ref, pk_ref, pv_ref, pc_ref), IN_SPLIT_WIDTHS[1:], IN_SPLIT_SCALES[1:]):
        for c in range(0, width, PROJ_COL_CHUNK):
            proj_chunk(o_ref, col, c, scale)
        col += width
    shifted_copies()
    for r in range(0, rows, CONV_ROW_CHUNK):
        conv_rows(r)


def _mixer_in_proj(x, norm_g, w_in, conv_w, conv_b, ln_g, ln_b, row_tile=ROW_TILE):
    b, s, d = x.shape
    vec = lambda a: a.reshape(1, -1)
    out_widths = (BRANCH_D,) + IN_SPLIT_WIDTHS[1:]
    return pl.pallas_call(
        _mixer_in_kernel,
        grid=(b, s // row_tile),
        in_specs=[
            pl.BlockSpec((1, row_tile, d), lambda bi, i: (bi, i, 0)),
            _resident((1, d)), _resident_layer(w_in),
            _resident((CONV_K, BRANCH_D)), _resident((1, BRANCH_D)), _resident((1, BRANCH_D)),
            _resident((1, BRANCH_D)),
        ],
        out_specs=[pl.BlockSpec((1, row_tile, wd), lambda bi, i: (bi, i, 0)) for wd in out_widths],
        out_shape=[jax.ShapeDtypeStruct((b, s, wd), BF16) for wd in out_widths],
        scratch_shapes=[
            pltpu.VMEM((row_tile, d), BF16),
            pltpu.VMEM((F32_SUBLANES, CONV_HALO + row_tile, BRANCH_D), F32),
            pltpu.VMEM((CONV_HALO, BRANCH_D), F32),
        ],
        compiler_params=_params("parallel", "arbitrary"),
        name="mixer_in_proj",
    )(x, vec(norm_g), w_in.stacked, conv_w, vec(conv_b), vec(ln_g), vec(ln_b))


def _rel_bucket(rel):
    half = N_REL_BUCKETS // 2
    max_exact = half // 2
    n = jnp.abs(rel)
    log_ratio = jnp.log(jnp.maximum(n, 1).astype(F32) / max_exact) / math.log(REL_MAX_DIST / max_exact)
    large = jnp.minimum(max_exact + (log_ratio * (half - max_exact)).astype(jnp.int32), half - 1)
    return jnp.where(rel > 0, half, 0) + jnp.where(n < max_exact, n, large)


FAR_BUCKET = N_REL_BUCKETS // 2 - 1
MASKED_BUCKET = -1


def _near_buckets():
    r = jnp.arange(ATTN_TILE)[:, None]
    c = jnp.arange(ATTN_TILE)[None, :]
    visible = (c // CHUNK) <= (r // CHUNK)
    diag = jnp.where(visible, _rel_bucket(c - r), MASKED_BUCKET)
    sub = _rel_bucket(c - r - ATTN_TILE)
    return jnp.stack([diag, sub]).astype(jnp.int32)


def _bias_tiles_kernel(table_ref, bucket_ref, o_ref):
    h = pl.program_id(0)
    bucket = bucket_ref[...]
    far = table_ref[FAR_BUCKET, h]
    out = jnp.zeros(bucket.shape, F32)
    for b in range(N_REL_BUCKETS):
        out = jnp.where(bucket == b, (table_ref[b, h] - far) * LOG2_E, out)
    o_ref[0] = jnp.where(bucket == MASKED_BUCKET, NEG_INF, out)


def _bias_tiles(rel_bias):
    buckets = _near_buckets()
    return pl.pallas_call(
        _bias_tiles_kernel,
        grid=(DIFF_HEADS,),
        in_specs=[pl.BlockSpec(memory_space=pltpu.SMEM), _resident(buckets.shape)],
        out_specs=pl.BlockSpec((1,) + buckets.shape, lambda h: (h, 0, 0, 0)),
        out_shape=jax.ShapeDtypeStruct((DIFF_HEADS,) + buckets.shape, F32),
        compiler_params=_params("arbitrary"),
        name="rel_bias_tiles",
    )(rel_bias.astype(F32), buckets)


def _diff_attn_kernel(*refs, lambda_init, n_casts):
    (q_ref, k_ref, v_ref, bias_ref, lq1_ref, lk1_ref, lq2_ref, lk2_ref, g_ref), refs = refs[:9], refs[9:]
    cast_src, o_ref, cast_dst = refs[:n_casts], refs[n_casts], refs[n_casts + 1:2 * n_casts + 1]
    q2_scr, s_scr, m_scr, acc_scr, vones_scr = refs[2 * n_casts + 1:]
    for src, dst in zip(cast_src, cast_dst):
        dst[...] = src[...].astype(BF16)
    _diff_attn_body(q_ref, k_ref, v_ref, bias_ref, lq1_ref, lk1_ref, lq2_ref, lk2_ref, g_ref, o_ref,
                    q2_scr, s_scr, m_scr, acc_scr, vones_scr, lambda_init=lambda_init)


def _diff_attn_body(q_ref, k_ref, v_ref, bias_ref, lq1_ref, lk1_ref, lq2_ref, lk2_ref, g_ref, o_ref,
                    q2_scr, s_scr, m_scr, acc_scr, vones_scr, *, lambda_init):
    t = ATTN_TILE
    i = pl.program_id(1)
    heads = range(DIFF_HEADS)

    def head_cols(h):
        return slice(h * DIFF_VD, (h + 1) * DIFF_VD)

    @pl.when(i == 0)
    def _():
        for h in heads:
            vones_scr[h, :, :DIFF_VD] = v_ref[0, :, head_cols(h)]
            vones_scr[h, :, DIFF_VD:] = jnp.ones((vones_scr.shape[1], LANES), BF16)

    q = q_ref[0]
    lane = lax.broadcasted_iota(jnp.int32, (t, DIFF_VD), 1)
    for h in heads:
        qh = q[:, head_cols(h)]
        q2_scr[h, :t, :] = jnp.where(lane < DIFF_DH, qh, 0).astype(BF16)
        q2_scr[h, t:, :] = jnp.where(lane >= DIFF_DH, qh, 0).astype(BF16)
    m_scr[...] = jnp.full(m_scr.shape, NEG_INF, F32)
    acc_scr[...] = jnp.zeros(acc_scr.shape, F32)

    def logits_tiles(j0, near):
        for n, bias_index in enumerate(near):
            start = pl.multiple_of((j0 + n) * t, t)
            for h in heads:
                s = _dot_nt(q2_scr[h], k_ref[0, pl.ds(start, t), head_cols(h)])
                if bias_index is not None:
                    bias = bias_ref[h, bias_index]
                    s = s + jnp.concatenate([bias, bias], axis=0)
                s_scr[j0 + n, h] = s
                m_scr[h] = jnp.maximum(m_scr[h], jnp.maximum(s[:, :LANES], s[:, LANES:]))

    def far_pair(pair, carry):
        logits_tiles(2 * pair, (None, None))
        return carry

    n_far = jnp.maximum(i - 1, 0)
    lax.fori_loop(0, n_far // 2, far_pair, 0)

    @pl.when(n_far % 2 == 1)
    def _():
        logits_tiles(n_far - 1, (None,))

    @pl.when(i > 0)
    def _():
        logits_tiles(i - 1, (1, 0))

    @pl.when(i == 0)
    def _():
        logits_tiles(0, (0,))

    for h in heads:
        m_scr[h] = jnp.broadcast_to(jnp.max(m_scr[h], axis=-1, keepdims=True), (2 * t, LANES))

    def pv_tiles(j0, n_tiles):
        start = pl.multiple_of(j0 * t, t)
        for h in heads:
            m = m_scr[h]
            parts = [jnp.exp2(s_scr[j0 + n, h, :, half * LANES:(half + 1) * LANES] - m)
                     for n in range(n_tiles) for half in range(t // LANES)]
            p = jnp.concatenate(parts, axis=1).astype(BF16)
            acc_scr[h] += _dot(p, vones_scr[h, pl.ds(start, n_tiles * t), :])

    def pv_pair(pair, carry):
        pv_tiles(2 * pair, 2)
        return carry

    n_visible = i + 1
    lax.fori_loop(0, n_visible // 2, pv_pair, 0)

    @pl.when(n_visible % 2 == 1)
    def _():
        pv_tiles(n_visible - 1, 1)

    lam = (jnp.exp(jnp.sum(lq1_ref[...] * lk1_ref[...])) - jnp.exp(jnp.sum(lq2_ref[...] * lk2_ref[...]))
           + lambda_init)
    for h in heads:
        o = (acc_scr[h, :t, :DIFF_VD] / acc_scr[h, :t, DIFF_VD:]
             - lam * (acc_scr[h, t:, :DIFF_VD] / acc_scr[h, t:, DIFF_VD:]))
        o_ref[0, :, head_cols(h)] = (_rms_norm(o, g_ref[...]) * (1.0 - lambda_init)).astype(BF16)


def _diff_attention(pq, pk, pv, bias_tiles, lam_q1, lam_k1, lam_q2, lam_k2, subln_g, lambda_init, side_casts):
    b, s, width = pq.shape
    t = ATTN_TILE
    tiles = s // t
    steps = b * tiles
    vec = lambda a: a.reshape(1, -1).astype(F32)
    all_keys = pl.BlockSpec((1, s, width), lambda bi, i: (bi, 0, 0))
    lam_spec = _resident((1, DIFF_DH))

    cast_views, cast_in_specs, cast_out_specs, cast_out_shapes = [], [], [], []
    for w, layer in side_casts:
        slice_rows = math.prod(w.shape[1:]) // (steps * LANES)
        assert slice_rows % BF16_SUBLANES == 0 and slice_rows * steps * LANES == math.prod(w.shape[1:])
        cast_views.append(w.reshape(w.shape[0], steps, slice_rows, LANES))
        cast_in_specs.append(pl.BlockSpec((None, None, slice_rows, LANES),
                                          lambda bi, i, layer=layer: (layer, bi * tiles + i, 0, 0)))
        cast_out_specs.append(pl.BlockSpec((None, slice_rows, LANES), lambda bi, i: (bi * tiles + i, 0, 0)))
        cast_out_shapes.append(jax.ShapeDtypeStruct((steps, slice_rows, LANES), BF16))

    out, *casts = pl.pallas_call(
        functools.partial(_diff_attn_kernel, lambda_init=lambda_init, n_casts=len(side_casts)),
        grid=(b, tiles),
        in_specs=[
            pl.BlockSpec((1, t, width), lambda bi, i: (bi, i, 0)),
            all_keys, all_keys, _resident(bias_tiles.shape),
            lam_spec, lam_spec, lam_spec, lam_spec, _resident((1, DIFF_VD)),
        ] + cast_in_specs,
        out_specs=[pl.BlockSpec((1, t, width), lambda bi, i: (bi, i, 0))] + cast_out_specs,
        out_shape=[jax.ShapeDtypeStruct((b, s, width), BF16)] + cast_out_shapes,
        scratch_shapes=[
            pltpu.VMEM((DIFF_HEADS, 2 * t, DIFF_VD), BF16),
            pltpu.VMEM((s // t, DIFF_HEADS, 2 * t, t), F32),
            pltpu.VMEM((DIFF_HEADS, 2 * t, LANES), F32),
            pltpu.VMEM((DIFF_HEADS, 2 * t, DIFF_VD + LANES), F32),
            pltpu.VMEM((DIFF_HEADS, s, DIFF_VD + LANES), BF16),
        ],
        compiler_params=_params("parallel", "arbitrary"),
        name="diff_attention",
    )(pq, pk, pv, bias_tiles, vec(lam_q1), vec(lam_k1), vec(lam_q2), vec(lam_k2), vec(subln_g), *cast_views)
    casts = [_LayerWeight(c.reshape((1,) + w.shape[1:]), 0) for c, (w, _) in zip(casts, side_casts)]
    return out, casts


def _gmlp_kernel(pc_ref, lng_ref, lnb_ref, ws_ref, bs_ref, o_ref):
    rows = o_ref.shape[0]
    z = jax.nn.gelu(pc_ref[...].astype(F32))
    u = z[:, :BRANCH_D]
    v = _layer_norm(z[:, BRANCH_D:], lng_ref[...], lnb_ref[...]).astype(BF16)
    t_idx = lax.broadcasted_iota(jnp.int32, (GMLP_CHUNK, GMLP_CHUNK), 0)
    s_idx = lax.broadcasted_iota(jnp.int32, (GMLP_CHUNK, GMLP_CHUNK), 1)
    for g in range(GMLP_GROUPS):
        w = jnp.where(s_idx <= t_idx, ws_ref[g], 0.0).astype(BF16)
        cols = slice(g * GMLP_GD, (g + 1) * GMLP_GD)
        for r in range(0, rows, GMLP_CHUNK):
            sv = _dot(w, v[r:r + GMLP_CHUNK, cols]) + bs_ref[g]
            o_ref[r:r + GMLP_CHUNK, cols] = (u[r:r + GMLP_CHUNK, cols] * sv).astype(BF16)


def _gmlp_branch(pc, ln_g, ln_b, w_s, b_s, row_tile=ROW_TILE):
    rows = pc.shape[0]
    vec = lambda a: a.reshape(1, BRANCH_D)
    return pl.pallas_call(
        _gmlp_kernel,
        grid=(rows // row_tile,),
        in_specs=[
            pl.BlockSpec((row_tile, COL_C), lambda i: (i, 0)),
            _resident((1, BRANCH_D)), _resident((1, BRANCH_D)),
            _resident(w_s.shape), _resident((GMLP_GROUPS, GMLP_CHUNK, 1)),
        ],
        out_specs=pl.BlockSpec((row_tile, BRANCH_D), lambda i: (i, 0)),
        out_shape=jax.ShapeDtypeStruct((rows, BRANCH_D), BF16),
        compiler_params=_params("parallel"),
        name="gmlp_branch",
    )(pc, vec(ln_g), vec(ln_b), w_s, b_s.reshape(GMLP_GROUPS, GMLP_CHUNK, 1))


def _merge_kernel(x_ref, ya_ref, yb_ref, yc_ref, g_ref, win_ref, gb_ref, wbr_ref, wout_ref, o_ref):
    x = x_ref[...]
    xn = _rms_norm(x, g_ref[...]).astype(BF16)
    merged = None
    for n, y_ref in enumerate((ya_ref, yb_ref, yc_ref)):
        col = COL_BRANCHES + n * D_MODEL
        logits = _dot(xn, win_ref[:, col:col + D_MODEL]) + gb_ref[n:n + 1, :]
        term = jax.nn.sigmoid(logits) * _dot(y_ref[...], wbr_ref[n])
        merged = term if merged is None else merged + term
    o_ref[...] = x + _dot(merged.astype(BF16), wout_ref[...])


def _merge(x, ya, yb, yc, norm_g, w_in, gate_b, w_br, w_out, row_tile=ROW_TILE):
    rows, d = x.shape
    row_spec = lambda width: pl.BlockSpec((row_tile, width), lambda i: (i, 0))
    return pl.pallas_call(
        _merge_kernel,
        grid=(rows // row_tile,),
        in_specs=[row_spec(D_MODEL), row_spec(BRANCH_D), row_spec(BRANCH_D), row_spec(BRANCH_D),
                  _resident((1, d)), _resident_layer(w_in), _resident(gate_b.shape), _resident_layer(w_br),
                  _resident_layer(w_out)],
        out_specs=row_spec(D_MODEL),
        out_shape=jax.ShapeDtypeStruct(x.shape, F32),
        compiler_params=_params("parallel"),
        name="branch_merge",
    )(x, ya, yb, yc, norm_g.reshape(1, d), w_in.stacked, gate_b, w_br.stacked, w_out.stacked)


def _xattn_kernel(x_ref, g_ref, wq_ref, kv_ref, wo_ref, o_ref):
    x = x_ref[0]
    q = _dot(_rms_norm(x, g_ref[...]).astype(BF16), wq_ref[...]) * (XATTN_DH ** -0.5 * LOG2_E)
    q = q.astype(BF16)
    heads = []
    for h in range(XATTN_HEADS):
        cols = slice(h * XATTN_DH, (h + 1) * XATTN_DH)
        k = kv_ref[0, :, cols]
        v = kv_ref[0, :, XATTN_D + h * XATTN_DH:XATTN_D + (h + 1) * XATTN_DH]
        s = _dot_nt(q[:, cols], k)
        p = jnp.exp2(s - jnp.max(s, axis=-1, keepdims=True))
        o = _dot(p.astype(BF16), v) / jnp.sum(p, axis=-1, keepdims=True)
        heads.append(o.astype(BF16))
    o_ref[0] = x + _dot(jnp.concatenate(heads, axis=-1), wo_ref[...])


def _mem_cross_attention(x, kv, norm_g, w_xq, w_xo, row_tile=ROW_TILE):
    b, s, d = x.shape
    m = kv.shape[1]
    return pl.pallas_call(
        _xattn_kernel,
        grid=(b, s // row_tile),
        in_specs=[
            pl.BlockSpec((1, row_tile, d), lambda bi, i: (bi, i, 0)),
            _resident((1, d)), _resident_layer(w_xq),
            pl.BlockSpec((1, m, 2 * XATTN_D), lambda bi, i: (bi, 0, 0)),
            _resident_layer(w_xo),
        ],
        out_specs=pl.BlockSpec((1, row_tile, d), lambda bi, i: (bi, i, 0)),
        out_shape=jax.ShapeDtypeStruct(x.shape, F32),
        compiler_params=_params("parallel", "parallel"),
        name="mem_cross_attention",
    )(x, norm_g.reshape(1, d), w_xq.stacked, kv, w_xo.stacked)


def _ffn_kernel(x_ref, halo_ref, g_ref, wup_ref, cw_ref, cb_ref, wdn_ref, gf_ref, o_ref,
                slab_scr, h_scr, up_scr, act_scr, acc_scr, *, final_norm):
    tile_rows = o_ref.shape[1]
    rows = acc_scr.shape[0]
    groups = rows // F32_SUBLANES
    slabs = slab_scr.shape[0]

    def lanes(j):
        return slice(j * LANES, (j + 1) * LANES)

    for j in range(slabs):
        slab_scr[j, :tile_rows, :] = x_ref[0, :, lanes(j)]
    slab_scr[:, tile_rows:, :] = jnp.zeros((slabs, rows - tile_rows, LANES), F32)
    x = jnp.concatenate(
        [jnp.concatenate([slab_scr[j, pl.ds(gi, F32_SUBLANES, stride=groups), :] for j in range(slabs)], axis=1)
         for gi in range(groups)], axis=0)

    g = g_ref[...]
    halo = _rms_norm(halo_ref[0], g)
    h_scr[:FFN_HALO, :] = jnp.where(pl.program_id(1) > 0, halo, 0.0).astype(BF16)
    h_scr[FFN_HALO:, :] = _rms_norm(x, g).astype(BF16)
    acc_scr[...] = x

    n_chunks = D_FF // FFN_COL_CHUNK
    first_sublane = lax.broadcasted_iota(jnp.int32, (F32_SUBLANES, FFN_COL_CHUNK), 0) == 0

    def cols(c, half):
        start = half * D_FF + c * FFN_COL_CHUNK
        return slice(start, start + FFN_COL_CHUNK)

    def up_proj(c, slot):
        h = h_scr[...]
        for half in range(2):
            up = up_scr.at[slot, half]
            up[...] = _dot(h, wup_ref[:, cols(c, half)])
            prev_tile = up[FFN_HALO - F32_SUBLANES:FFN_HALO, :]
            for shift in range(1, FFN_K):
                wrapped = up[FFN_HALO + rows - shift * F32_SUBLANES:FFN_HALO + rows - (shift - 1) * F32_SUBLANES, :]
                block = jnp.where(first_sublane, pltpu.roll(prev_tile, shift, 0), pltpu.roll(wrapped, 1, 0))
                up[FFN_HALO - shift * F32_SUBLANES:FFN_HALO - (shift - 1) * F32_SUBLANES, :] = block

    def conv(c, slot, half):
        out = cb_ref[:, cols(c, half)]
        for k in range(FFN_K):
            window = up_scr[slot, half, k * F32_SUBLANES:k * F32_SUBLANES + rows, :]
            out = out + cw_ref[k:k + 1, cols(c, half)] * window
        return out

    def gate(c, slot):
        act_scr[slot] = (jax.nn.silu(conv(c, slot, 0)) * conv(c, slot, 1)).astype(BF16)

    def down_proj(c, slot):
        acc_scr[...] += _dot(act_scr[slot], wdn_ref[cols(c, 0), :])

    def stage(c, slot):
        up_proj(c + 1, 1 - slot)
        down_proj(c - 1, 1 - slot)
        gate(c, slot)

    up_proj(0, 0)
    up_proj(1, 1)
    gate(0, 0)
    for c in range(1, n_chunks - 1):
        stage(c, c % 2)
    down_proj(n_chunks - 2, (n_chunks - 2) % 2)
    gate(n_chunks - 1, (n_chunks - 1) % 2)
    down_proj(n_chunks - 1, (n_chunks - 1) % 2)

    for gi in range(groups):
        out = acc_scr[gi * F32_SUBLANES:(gi + 1) * F32_SUBLANES, :]
        if final_norm:
            out = _rms_norm(out, gf_ref[...])
        for j in range(slabs):
            slab_scr[j, pl.ds(gi, F32_SUBLANES, stride=groups), :] = out[:, lanes(j)]
    for j in range(slabs):
        o_ref[0, :, lanes(j)] = slab_scr[j, :tile_rows, :]


def _conv_ffn(x, norm_g, w_up, conv_w, conv_b, w_down, final_g, final_norm, row_tile=FFN_ROW_TILE):
    b, s, d = x.shape
    halo_blocks = row_tile // FFN_HALO
    conv_b = conv_b.reshape(1, 2 * D_FF)
    perm_rows = row_tile + 2 * F32_SUBLANES
    return pl.pallas_call(
        functools.partial(_ffn_kernel, final_norm=final_norm),
        grid=(b, s // row_tile),
        in_specs=[
            pl.BlockSpec((1, row_tile, d), lambda bi, i: (bi, i, 0)),
            pl.BlockSpec((1, FFN_HALO, d), lambda bi, i: (bi, jnp.maximum(i * halo_blocks - 1, 0), 0)),
            _resident((1, d)), _resident_layer(w_up), _resident(conv_w.shape), _resident(conv_b.shape),
            _resident_layer(w_down), _resident((1, d)),
        ],
        out_specs=pl.BlockSpec((1, row_tile, d), lambda bi, i: (bi, i, 0)),
        out_shape=jax.ShapeDtypeStruct(x.shape, F32),
        scratch_shapes=[
            pltpu.VMEM((d // LANES, perm_rows, LANES), F32),
            pltpu.VMEM((FFN_HALO + perm_rows, d), BF16),
            pltpu.VMEM((2, 2, FFN_HALO + perm_rows, FFN_COL_CHUNK), F32),
            pltpu.VMEM((2, perm_rows, FFN_COL_CHUNK), BF16),
            pltpu.VMEM((perm_rows, d), F32),
        ],
        compiler_params=_params("parallel", "parallel"),
        name="conv_ffn",
    )(x, x, norm_g.reshape(1, d), w_up.stacked, conv_w, conv_b, w_down.stacked, final_g.reshape(1, d))


def kernel(x, mem, rel_bias, norm_mix_g, w_in, gate_b, conv_w, conv_b, conv_ln_g, conv_ln_b, lam_q1, lam_k1, lam_q2, lam_k2, subln_g, gmlp_ln_g, gmlp_ln_b, w_s, b_s, w_br, w_out, norm_xattn_g, norm_mem_g, w_xq, w_xkv, w_xo, norm_ffn_g, w_up, ffn_conv_w, ffn_conv_b, w_down, norm_final_g):
    b, s, d = x.shape
    m = mem.shape[1]
    rows = b * s
    bias_tiles = _bias_tiles(rel_bias)
    w_in_l = _to_bf16(w_in, 0)
    for l in range(DEPTH):
        lambda_init = 0.8 - 0.6 * math.exp(-0.3 * l)
        ya, pq, pk, pv, pc = _mixer_in_proj(x, norm_mix_g[l], w_in_l, conv_w[l], conv_b[l], conv_ln_g[l],
                                            conv_ln_b[l])
        side_casts = [(w, l) for w in (w_br, w_out, w_xq, w_xkv, w_xo, w_up, w_down)]
        if l + 1 < DEPTH:
            side_casts.append((w_in, l + 1))
        yb, casts = _diff_attention(pq, pk, pv, bias_tiles, lam_q1[l], lam_k1[l], lam_q2[l], lam_k2[l],
                                    subln_g[l], lambda_init, side_casts)
        w_br_l, w_out_l, w_xq_l, w_xkv_l, w_xo_l, w_up_l, w_down_l = casts[:7]
        yc = _gmlp_branch(pc.reshape(rows, COL_C), gmlp_ln_g[l], gmlp_ln_b[l], w_s[l], b_s[l])
        x = _merge(x.reshape(rows, d), ya.reshape(rows, BRANCH_D), yb.reshape(rows, BRANCH_D), yc, norm_mix_g[l],
                   w_in_l, gate_b[l], w_br_l, w_out_l).reshape(b, s, d)
        (kv,) = _norm_proj(mem.reshape(b * m, d), norm_mem_g[l], w_xkv_l, (2 * XATTN_D,))
        x = _mem_cross_attention(x, kv.reshape(b, m, 2 * XATTN_D), norm_xattn_g[l], w_xq_l, w_xo_l)
        x = _conv_ffn(x, norm_ffn_g[l], w_up_l, ffn_conv_w[l], ffn_conv_b[l], w_down_l, norm_final_g,
                      final_norm=(l == DEPTH - 1))
        if l + 1 < DEPTH:
            w_in_l = casts[7]
    return x
```

```python
import functools
import math
from typing import NamedTuple

import jax
import jax.numpy as jnp
from jax import lax
from jax.experimental import pallas as pl
from jax.experimental.pallas import tpu as pltpu

F32 = jnp.float32
BF16 = jnp.bfloat16

D_MODEL = 1024
DEPTH = 2
CHUNK = 64
BRANCH_D = D_MODEL // 2
N_BRANCH = 3
CONV_K = 31
DIFF_HEADS = 4
DIFF_DH = D_MODEL // 16
DIFF_VD = 2 * DIFF_DH
GMLP_GROUPS = 4
GMLP_GD = BRANCH_D // GMLP_GROUPS
GMLP_CHUNK = 128
N_REL_BUCKETS = 32
REL_MAX_DIST = 128
XATTN_HEADS = 4
XATTN_DH = D_MODEL // 8
XATTN_D = XATTN_HEADS * XATTN_DH
D_FF = 2816
FFN_K = 3
EPS = 1e-6
NEG_INF = -1e30

COL_A = 2 * BRANCH_D
COL_QK = DIFF_HEADS * 2 * DIFF_DH
COL_V = DIFF_HEADS * DIFF_VD
COL_C = 2 * BRANCH_D
COL_G = N_BRANCH * D_MODEL
IN_SPLIT_WIDTHS = (COL_A, COL_QK, COL_QK, COL_V, COL_C)
COL_BRANCHES = sum(IN_SPLIT_WIDTHS)
LOG2_E = math.log2(math.e)
IN_SPLIT_SCALES = (1.0, DIFF_DH ** -0.5 * LOG2_E, 1.0, 1.0, 1.0)

LANES = 128
MXU_WIDTH = 256
F32_SUBLANES = 8
BF16_SUBLANES = 16
VMEM_LIMIT_BYTES = 56 * 1024 * 1024
CAST_BLOCK_BYTES = 4 * 1024 * 1024

ROW_TILE = 512
FFN_ROW_TILE = 1024
PROJ_COL_CHUNK = 512
ATTN_TILE = 256
CONV_HALO = 32
CONV_ROW_CHUNK = 64
FFN_HALO = BF16_SUBLANES
FFN_COL_CHUNK = MXU_WIDTH

assert CONV_HALO >= CONV_K - 1 and FFN_HALO == F32_SUBLANES * (FFN_K - 1)
assert D_FF % FFN_COL_CHUNK == 0
assert ATTN_TILE % CHUNK == 0 and ATTN_TILE > REL_MAX_DIST


def _params(*semantics):
    return pltpu.CompilerParams(dimension_semantics=semantics, vmem_limit_bytes=VMEM_LIMIT_BYTES)


def _resident(shape):
    zeros = (0,) * len(shape)
    return pl.BlockSpec(shape, lambda *_: zeros, pipeline_mode=pl.Buffered(1))


class _LayerWeight(NamedTuple):
    stacked: jax.Array
    layer: int

    @property
    def shape(self):
        return self.stacked.shape[1:]


def _resident_layer(w):
    index = (w.layer,) + (0,) * len(w.shape)
    return pl.BlockSpec((None,) + tuple(w.shape), lambda *_: index, pipeline_mode=pl.Buffered(1))


def _cast_kernel(w_ref, o_ref):
    o_ref[...] = w_ref[...].astype(BF16)


def _to_bf16(w, layer):
    _, rows, cols = w.shape
    block_rows = max(r for r in range(BF16_SUBLANES, rows + 1, BF16_SUBLANES)
                     if rows % r == 0 and r * cols * 4 <= CAST_BLOCK_BYTES)
    out = pl.pallas_call(
        _cast_kernel,
        grid=(rows // block_rows,),
        in_specs=[pl.BlockSpec((None, block_rows, cols), lambda i: (layer, i, 0))],
        out_specs=pl.BlockSpec((None, block_rows, cols), lambda i: (0, i, 0)),
        out_shape=jax.ShapeDtypeStruct((1, rows, cols), BF16),
        compiler_params=_params("parallel"),
        name="weight_cast",
    )(w)
    return _LayerWeight(out, 0)


def _rms_norm(x, g):
    return x * lax.rsqrt(jnp.mean(x * x, axis=-1, keepdims=True) + EPS) * g


def _layer_norm(x, g, b):
    mu = jnp.mean(x, axis=-1, keepdims=True)
    xc = x - mu
    var = jnp.mean(xc * xc, axis=-1, keepdims=True)
    return xc * lax.rsqrt(var + EPS) * g + b


def _dot(a, b):
    return jnp.dot(a, b, preferred_element_type=F32)


def _dot_nt(a, b):
    return lax.dot_general(a, b, (((1,), (1,)), ((), ())), preferred_element_type=F32)


def _norm_proj_kernel(x_ref, g_ref, w_ref, *out_refs, widths, scales):
    xn = _rms_norm(x_ref[...], g_ref[...]).astype(BF16)
    col = 0
    for o_ref, width, scale in zip(out_refs, widths, scales):
        for c in range(0, width, PROJ_COL_CHUNK):
            out = _dot(xn, w_ref[:, col + c:col + c + PROJ_COL_CHUNK])
            if scale != 1.0:
                out = out * scale
            o_ref[:, c:c + PROJ_COL_CHUNK] = out.astype(BF16)
        col += width


def _norm_proj(x, g, w, widths, scales=None, row_tile=ROW_TILE):
    rows, d = x.shape
    scales = scales or (1.0,) * len(widths)
    assert w.shape == (d, sum(widths)) and all(wd % PROJ_COL_CHUNK == 0 for wd in widths)
    return pl.pallas_call(
        functools.partial(_norm_proj_kernel, widths=widths, scales=scales),
        grid=(rows // row_tile,),
        in_specs=[pl.BlockSpec((row_tile, d), lambda i: (i, 0)), _resident((1, d)), _resident_layer(w)],
        out_specs=[pl.BlockSpec((row_tile, wd), lambda i: (i, 0)) for wd in widths],
        out_shape=[jax.ShapeDtypeStruct((rows, wd), BF16) for wd in widths],
        compiler_params=_params("parallel"),
        name="norm_proj",
    )(x, g.reshape(1, d), w.stacked)


def _mixer_in_kernel(x_ref, g_ref, w_ref, cw_ref, cb_ref, lng_ref, lnb_ref,
                     ya_ref, pq_ref, pk_ref, pv_ref, pc_ref, xn_scr, xs_scr, halo_scr):
    rows = ya_ref.shape[1]
    xn_scr[...] = _rms_norm(x_ref[0], g_ref[...]).astype(BF16)

    def proj(col, width):
        return _dot(xn_scr[...], w_ref[:, col:col + width])

    glu = proj(0, BRANCH_D) * jax.nn.sigmoid(proj(BRANCH_D, BRANCH_D))

    @pl.when(pl.program_id(1) == 0)
    def _():
        halo_scr[...] = jnp.zeros(halo_scr.shape, F32)

    xs_scr[0, :CONV_HALO, :] = halo_scr[...]
    xs_scr[0, CONV_HALO:, :] = glu
    halo_scr[...] = glu[rows - CONV_HALO:, :]

    def shifted_copies():
        shifted_rows = CONV_HALO + rows - F32_SUBLANES
        for b in range(1, F32_SUBLANES):
            xs_scr[b, :shifted_rows, :] = xs_scr[0, b:b + shifted_rows, :]

    first_tap = CONV_HALO - (CONV_K - 1)

    def conv_rows(r):
        acc = jnp.broadcast_to(cb_ref[...], (CONV_ROW_CHUNK, BRANCH_D))
        for k in range(CONV_K):
            b = (first_tap + k) % F32_SUBLANES
            start = r + first_tap + k - b
            acc = acc + cw_ref[k:k + 1, :] * xs_scr[b, start:start + CONV_ROW_CHUNK, :]
        ya_ref[0, r:r + CONV_ROW_CHUNK, :] = jax.nn.silu(_layer_norm(acc, lng_ref[...], lnb_ref[...])).astype(BF16)

    def proj_chunk(o_ref, col, c, scale):
        out = proj(col + c, PROJ_COL_CHUNK)
        if scale != 1.0:
            out = out * scale
        o_ref[0, :, c:c + PROJ_COL_CHUNK] = out.astype(BF16)

    col = COL_A
    for o_ref, width, scale in zip((pq_ref, pk_ref, pv_ref, pc_ref), IN_SPLIT_WIDTHS[1:], IN_SPLIT_SCALES[1:]):
        for c in range(0, width, PROJ_COL_CHUNK):
            proj_chunk(o_ref, col, c, scale)
        col += width
    shifted_copies()
    for r in range(0, rows, CONV_ROW_CHUNK):
        conv_rows(r)


def _mixer_in_proj(x, norm_g, w_in, conv_w, conv_b, ln_g, ln_b, row_tile=ROW_TILE):
    b, s, d = x.shape
    vec = lambda a: a.reshape(1, -1)
    out_widths = (BRANCH_D,) + IN_SPLIT_WIDTHS[1:]
    return pl.pallas_call(
        _mixer_in_kernel,
        grid=(b, s // row_tile),
        in_specs=[
            pl.BlockSpec((1, row_tile, d), lambda bi, i: (bi, i, 0)),
            _resident((1, d)), _resident_layer(w_in),
            _resident((CONV_K, BRANCH_D)), _resident((1, BRANCH_D)), _resident((1, BRANCH_D)),
            _resident((1, BRANCH_D)),
        ],
        out_specs=[pl.BlockSpec((1, row_tile, wd), lambda bi, i: (bi, i, 0)) for wd in out_widths],
        out_shape=[jax.ShapeDtypeStruct((b, s, wd), BF16) for wd in out_widths],
        scratch_shapes=[
            pltpu.VMEM((row_tile, d), BF16),
            pltpu.VMEM((F32_SUBLANES, CONV_HALO + row_tile, BRANCH_D), F32),
            pltpu.VMEM((CONV_HALO, BRANCH_D), F32),
        ],
        compiler_params=_params("parallel", "arbitrary"),
        name="mixer_in_proj",
    )(x, vec(norm_g), w_in.stacked, conv_w, vec(conv_b), vec(ln_g), vec(ln_b))


def _rel_bucket(rel):
    half = N_REL_BUCKETS // 2
    max_exact = half // 2
    n = jnp.abs(rel)
    log_ratio = jnp.log(jnp.maximum(n, 1).astype(F32) / max_exact) / math.log(REL_MAX_DIST / max_exact)
    large = jnp.minimum(max_exact + (log_ratio * (half - max_exact)).astype(jnp.int32), half - 1)
    return jnp.where(rel > 0, half, 0) + jnp.where(n < max_exact, n, large)


FAR_BUCKET = N_REL_BUCKETS // 2 - 1
MASKED_BUCKET = -1


def _near_buckets():
    r = jnp.arange(ATTN_TILE)[:, None]
    c = jnp.arange(ATTN_TILE)[None, :]
    visible = (c // CHUNK) <= (r // CHUNK)
    diag = jnp.where(visible, _rel_bucket(c - r), MASKED_BUCKET)
    sub = _rel_bucket(c - r - ATTN_TILE)
    return jnp.stack([diag, sub]).astype(jnp.int32)


def _bias_tiles_kernel(table_ref, bucket_ref, o_ref):
    h = pl.program_id(0)
    bucket = bucket_ref[...]
    far = table_ref[FAR_BUCKET, h]
    out = jnp.zeros(bucket.shape, F32)
    for b in range(N_REL_BUCKETS):
        out = jnp.where(bucket == b, (table_ref[b, h] - far) * LOG2_E, out)
    o_ref[0] = jnp.where(bucket == MASKED_BUCKET, NEG_INF, out)


def _bias_tiles(rel_bias):
    buckets = _near_buckets()
    return pl.pallas_call(
        _bias_tiles_kernel,
        grid=(DIFF_HEADS,),
        in_specs=[pl.BlockSpec(memory_space=pltpu.SMEM), _resident(buckets.shape)],
        out_specs=pl.BlockSpec((1,) + buckets.shape, lambda h: (h, 0, 0, 0)),
        out_shape=jax.ShapeDtypeStruct((DIFF_HEADS,) + buckets.shape, F32),
        compiler_params=_params("arbitrary"),
        name="rel_bias_tiles",
    )(rel_bias.astype(F32), buckets)


def _diff_attn_kernel(*refs, lambda_init, cast_chunks):
    n_casts = len(cast_chunks)
    (q_ref, k_ref, v_ref, bias_ref, lq1_ref, lk1_ref, lq2_ref, lk2_ref, g_ref), refs = refs[:9], refs[9:]
    cast_src, o_ref, cast_dst = refs[:n_casts], refs[n_casts], refs[n_casts + 1:2 * n_casts + 1]
    q2_scr, s_scr, m_scr, acc_scr, vones_scr = refs[2 * n_casts + 1:]
    step = pl.program_id(0) * pl.num_programs(1) + pl.program_id(1)
    for src, dst, chunks in zip(cast_src, cast_dst, cast_chunks):
        @pl.when(step < chunks)
        def _():
            dst[...] = src[...].astype(BF16)

    _diff_attn_body(q_ref, k_ref, v_ref, bias_ref, lq1_ref, lk1_ref, lq2_ref, lk2_ref, g_ref, o_ref,
                    q2_scr, s_scr, m_scr, acc_scr, vones_scr, lambda_init=lambda_init)


def _diff_attn_body(q_ref, k_ref, v_ref, bias_ref, lq1_ref, lk1_ref, lq2_ref, lk2_ref, g_ref, o_ref,
                    q2_scr, s_scr, m_scr, acc_scr, vones_scr, *, lambda_init):
    t = ATTN_TILE
    i = pl.program_id(1)
    heads = range(DIFF_HEADS)

    def head_cols(h):
        return slice(h * DIFF_VD, (h + 1) * DIFF_VD)

    @pl.when(i == 0)
    def _():
        for h in heads:
            vones_scr[h, :, :DIFF_VD] = v_ref[0, :, head_cols(h)]
            vones_scr[h, :, DIFF_VD:] = jnp.ones((vones_scr.shape[1], LANES), BF16)

    q = q_ref[0]
    lane = lax.broadcasted_iota(jnp.int32, (t, DIFF_VD), 1)
    for h in heads:
        qh = q[:, head_cols(h)]
        q2_scr[h, :t, :] = jnp.where(lane < DIFF_DH, qh, 0).astype(BF16)
        q2_scr[h, t:, :] = jnp.where(lane >= DIFF_DH, qh, 0).astype(BF16)
    m_scr[...] = jnp.full(m_scr.shape, NEG_INF, F32)
    acc_scr[...] = jnp.zeros(acc_scr.shape, F32)

    def logits_tiles(j0, near):
        for n, bias_index in enumerate(near):
            start = pl.multiple_of((j0 + n) * t, t)
            for h in heads:
                s = _dot_nt(q2_scr[h], k_ref[0, pl.ds(start, t), head_cols(h)])
                if bias_index is not None:
                    bias = bias_ref[h, bias_index]
                    s = s + jnp.concatenate([bias, bias], axis=0)
                s_scr[j0 + n, h] = s
                m_scr[h] = jnp.maximum(m_scr[h], jnp.maximum(s[:, :LANES], s[:, LANES:]))

    def far_pair(pair, carry):
        logits_tiles(2 * pair, (None, None))
        return carry

    n_far = jnp.maximum(i - 1, 0)
    lax.fori_loop(0, n_far // 2, far_pair, 0)

    @pl.when(n_far % 2 == 1)
    def _():
        logits_tiles(n_far - 1, (None,))

    @pl.when(i > 0)
    def _():
        logits_tiles(i - 1, (1, 0))

    @pl.when(i == 0)
    def _():
        logits_tiles(0, (0,))

    for h in heads:
        m_scr[h] = jnp.broadcast_to(jnp.max(m_scr[h], axis=-1, keepdims=True), (2 * t, LANES))

    def pv_tiles(j0, n_tiles):
        start = pl.multiple_of(j0 * t, t)
        for h in heads:
            m = m_scr[h]
            parts = [jnp.exp2(s_scr[j0 + n, h, :, half * LANES:(half + 1) * LANES] - m)
                     for n in range(n_tiles) for half in range(t // LANES)]
            p = jnp.concatenate(parts, axis=1).astype(BF16)
            acc_scr[h] += _dot(p, vones_scr[h, pl.ds(start, n_tiles * t), :])

    def pv_pair(pair, carry):
        pv_tiles(2 * pair, 2)
        return carry

    n_visible = i + 1
    lax.fori_loop(0, n_visible // 2, pv_pair, 0)

    @pl.when(n_visible % 2 == 1)
    def _():
        pv_tiles(n_visible - 1, 1)

    lam = (jnp.exp(jnp.sum(lq1_ref[...] * lk1_ref[...])) - jnp.exp(jnp.sum(lq2_ref[...] * lk2_ref[...]))
           + lambda_init)
    for h in heads:
        o = (acc_scr[h, :t, :DIFF_VD] / acc_scr[h, :t, DIFF_VD:]
             - lam * (acc_scr[h, t:, :DIFF_VD] / acc_scr[h, t:, DIFF_VD:]))
        o_ref[0, :, head_cols(h)] = (_rms_norm(o, g_ref[...]) * (1.0 - lambda_init)).astype(BF16)


def _diff_attention(pq, pk, pv, bias_tiles, lam_q1, lam_k1, lam_q2, lam_k2, subln_g, lambda_init, side_casts):
    b, s, width = pq.shape
    t = ATTN_TILE
    tiles = s // t
    steps = b * tiles
    vec = lambda a: a.reshape(1, -1).astype(F32)
    all_keys = pl.BlockSpec((1, s, width), lambda bi, i: (bi, 0, 0))
    lam_spec = _resident((1, DIFF_DH))

    cast_views, cast_in_specs, cast_out_specs, cast_out_shapes, cast_chunks = [], [], [], [], []
    for w, layer in side_casts:
        view = w.reshape(w.shape[0], -1, w.shape[-1])
        _, rows, cols = view.shape
        chunks = max(c for c in range(1, steps + 1) if rows % (c * BF16_SUBLANES) == 0)

        def chunk(bi, i, chunks=chunks):
            return jnp.minimum(bi * tiles + i, chunks - 1)

        cast_views.append(view)
        cast_chunks.append(chunks)
        cast_in_specs.append(pl.BlockSpec((None, rows // chunks, cols),
                                          lambda bi, i, layer=layer, chunk=chunk: (layer, chunk(bi, i), 0)))
        cast_out_specs.append(pl.BlockSpec((rows // chunks, cols), lambda bi, i, chunk=chunk: (chunk(bi, i), 0)))
        cast_out_shapes.append(jax.ShapeDtypeStruct((rows, cols), BF16))

    out, *casts = pl.pallas_call(
        functools.partial(_diff_attn_kernel, lambda_init=lambda_init, cast_chunks=tuple(cast_chunks)),
        grid=(b, tiles),
        in_specs=[
            pl.BlockSpec((1, t, width), lambda bi, i: (bi, i, 0)),
            all_keys, all_keys, _resident(bias_tiles.shape),
            lam_spec, lam_spec, lam_spec, lam_spec, _resident((1, DIFF_VD)),
        ] + cast_in_specs,
        out_specs=[pl.BlockSpec((1, t, width), lambda bi, i: (bi, i, 0))] + cast_out_specs,
        out_shape=[jax.ShapeDtypeStruct((b, s, width), BF16)] + cast_out_shapes,
        scratch_shapes=[
            pltpu.VMEM((DIFF_HEADS, 2 * t, DIFF_VD), BF16),
            pltpu.VMEM((s // t, DIFF_HEADS, 2 * t, t), F32),
            pltpu.VMEM((DIFF_HEADS, 2 * t, LANES), F32),
            pltpu.VMEM((DIFF_HEADS, 2 * t, DIFF_VD + LANES), F32),
            pltpu.VMEM((DIFF_HEADS, s, DIFF_VD + LANES), BF16),
        ],
        compiler_params=_params("parallel", "arbitrary"),
        name="diff_attention",
    )(pq, pk, pv, bias_tiles, vec(lam_q1), vec(lam_k1), vec(lam_q2), vec(lam_k2), vec(subln_g), *cast_views)
    casts = [_LayerWeight(c.reshape((1,) + w.shape[1:]), 0) for c, (w, _) in zip(casts, side_casts)]
    return out, casts


def _gmlp_kernel(pc_ref, lng_ref, lnb_ref, ws_ref, bs_ref, o_ref):
    rows = o_ref.shape[0]
    z = jax.nn.gelu(pc_ref[...].astype(F32))
    u = z[:, :BRANCH_D]
    v = _layer_norm(z[:, BRANCH_D:], lng_ref[...], lnb_ref[...]).astype(BF16)
    t_idx = lax.broadcasted_iota(jnp.int32, (GMLP_CHUNK, GMLP_CHUNK), 0)
    s_idx = lax.broadcasted_iota(jnp.int32, (GMLP_CHUNK, GMLP_CHUNK), 1)
    for g in range(GMLP_GROUPS):
        w = jnp.where(s_idx <= t_idx, ws_ref[g], 0.0).astype(BF16)
        cols = slice(g * GMLP_GD, (g + 1) * GMLP_GD)
        for r in range(0, rows, GMLP_CHUNK):
            sv = _dot(w, v[r:r + GMLP_CHUNK, cols]) + bs_ref[g]
            o_ref[r:r + GMLP_CHUNK, cols] = (u[r:r + GMLP_CHUNK, cols] * sv).astype(BF16)


def _gmlp_branch(pc, ln_g, ln_b, w_s, b_s, row_tile=ROW_TILE):
    rows = pc.shape[0]
    vec = lambda a: a.reshape(1, BRANCH_D)
    return pl.pallas_call(
        _gmlp_kernel,
        grid=(rows // row_tile,),
        in_specs=[
            pl.BlockSpec((row_tile, COL_C), lambda i: (i, 0)),
            _resident((1, BRANCH_D)), _resident((1, BRANCH_D)),
            _resident(w_s.shape), _resident((GMLP_GROUPS, GMLP_CHUNK, 1)),
        ],
        out_specs=pl.BlockSpec((row_tile, BRANCH_D), lambda i: (i, 0)),
        out_shape=jax.ShapeDtypeStruct((rows, BRANCH_D), BF16),
        compiler_params=_params("parallel"),
        name="gmlp_branch",
    )(pc, vec(ln_g), vec(ln_b), w_s, b_s.reshape(GMLP_GROUPS, GMLP_CHUNK, 1))


def _merge_kernel(x_ref, ya_ref, yb_ref, yc_ref, g_ref, win_ref, gb_ref, wbr_ref, wout_ref, o_ref):
    x = x_ref[...]
    xn = _rms_norm(x, g_ref[...]).astype(BF16)
    merged = None
    for n, y_ref in enumerate((ya_ref, yb_ref, yc_ref)):
        col = COL_BRANCHES + n * D_MODEL
        logits = _dot(xn, win_ref[:, col:col + D_MODEL]) + gb_ref[n:n + 1, :]
        term = jax.nn.sigmoid(logits) * _dot(y_ref[...], wbr_ref[n])
        merged = term if merged is None else merged + term
    o_ref[...] = x + _dot(merged.astype(BF16), wout_ref[...])


def _merge(x, ya, yb, yc, norm_g, w_in, gate_b, w_br, w_out, row_tile=ROW_TILE):
    rows, d = x.shape
    row_spec = lambda width: pl.BlockSpec((row_tile, width), lambda i: (i, 0))
    return pl.pallas_call(
        _merge_kernel,
        grid=(rows // row_tile,),
        in_specs=[row_spec(D_MODEL), row_spec(BRANCH_D), row_spec(BRANCH_D), row_spec(BRANCH_D),
                  _resident((1, d)), _resident_layer(w_in), _resident(gate_b.shape), _resident_layer(w_br),
                  _resident_layer(w_out)],
        out_specs=row_spec(D_MODEL),
        out_shape=jax.ShapeDtypeStruct(x.shape, F32),
        compiler_params=_params("parallel"),
        name="branch_merge",
    )(x, ya, yb, yc, norm_g.reshape(1, d), w_in.stacked, gate_b, w_br.stacked, w_out.stacked)


def _xattn_kernel(x_ref, g_ref, wq_ref, kv_ref, wo_ref, o_ref):
    x = x_ref[0]
    q = _dot(_rms_norm(x, g_ref[...]).astype(BF16), wq_ref[...]) * (XATTN_DH ** -0.5 * LOG2_E)
    q = q.astype(BF16)
    heads = []
    for h in range(XATTN_HEADS):
        cols = slice(h * XATTN_DH, (h + 1) * XATTN_DH)
        k = kv_ref[0, :, cols]
        v = kv_ref[0, :, XATTN_D + h * XATTN_DH:XATTN_D + (h + 1) * XATTN_DH]
        s = _dot_nt(q[:, cols], k)
        p = jnp.exp2(s - jnp.max(s, axis=-1, keepdims=True))
        o = _dot(p.astype(BF16), v) / jnp.sum(p, axis=-1, keepdims=True)
        heads.append(o.astype(BF16))
    o_ref[0] = x + _dot(jnp.concatenate(heads, axis=-1), wo_ref[...])


def _mem_cross_attention(x, kv, norm_g, w_xq, w_xo, row_tile=ROW_TILE):
    b, s, d = x.shape
    m = kv.shape[1]
    return pl.pallas_call(
        _xattn_kernel,
        grid=(b, s // row_tile),
        in_specs=[
            pl.BlockSpec((1, row_tile, d), lambda bi, i: (bi, i, 0)),
            _resident((1, d)), _resident_layer(w_xq),
            pl.BlockSpec((1, m, 2 * XATTN_D), lambda bi, i: (bi, 0, 0)),
            _resident_layer(w_xo),
        ],
        out_specs=pl.BlockSpec((1, row_tile, d), lambda bi, i: (bi, i, 0)),
        out_shape=jax.ShapeDtypeStruct(x.shape, F32),
        compiler_params=_params("parallel", "parallel"),
        name="mem_cross_attention",
    )(x, norm_g.reshape(1, d), w_xq.stacked, kv, w_xo.stacked)


def _ffn_kernel(x_ref, halo_ref, g_ref, wup_ref, cw_ref, cb_ref, wdn_ref, gf_ref, o_ref,
                slab_scr, h_scr, up_scr, act_scr, acc_scr, *, final_norm):
    tile_rows = o_ref.shape[1]
    rows = acc_scr.shape[0]
    groups = rows // F32_SUBLANES
    slabs = slab_scr.shape[0]

    def lanes(j):
        return slice(j * LANES, (j + 1) * LANES)

    for j in range(slabs):
        slab_scr[j, :tile_rows, :] = x_ref[0, :, lanes(j)]
    slab_scr[:, tile_rows:, :] = jnp.zeros((slabs, rows - tile_rows, LANES), F32)
    x = jnp.concatenate(
        [jnp.concatenate([slab_scr[j, pl.ds(gi, F32_SUBLANES, stride=groups), :] for j in range(slabs)], axis=1)
         for gi in range(groups)], axis=0)

    g = g_ref[...]
    halo = _rms_norm(halo_ref[0], g)
    h_scr[:FFN_HALO, :] = jnp.where(pl.program_id(1) > 0, halo, 0.0).astype(BF16)
    h_scr[FFN_HALO:, :] = _rms_norm(x, g).astype(BF16)
    acc_scr[...] = x

    n_chunks = D_FF // FFN_COL_CHUNK
    first_sublane = lax.broadcasted_iota(jnp.int32, (F32_SUBLANES, FFN_COL_CHUNK), 0) == 0

    def cols(c, half):
        start = half * D_FF + c * FFN_COL_CHUNK
        return slice(start, start + FFN_COL_CHUNK)

    def up_proj(c, slot):
        h = h_scr[...]
        for half in range(2):
            up = up_scr.at[slot, half]
            up[...] = _dot(h, wup_ref[:, cols(c, half)])
            prev_tile = up[FFN_HALO - F32_SUBLANES:FFN_HALO, :]
            for shift in range(1, FFN_K):
                wrapped = up[FFN_HALO + rows - shift * F32_SUBLANES:FFN_HALO + rows - (shift - 1) * F32_SUBLANES, :]
                block = jnp.where(first_sublane, pltpu.roll(prev_tile, shift, 0), pltpu.roll(wrapped, 1, 0))
                up[FFN_HALO - shift * F32_SUBLANES:FFN_HALO - (shift - 1) * F32_SUBLANES, :] = block

    def conv(c, slot, half):
        out = cb_ref[:, cols(c, half)]
        for k in range(FFN_K):
            window = up_scr[slot, half, k * F32_SUBLANES:k * F32_SUBLANES + rows, :]
            out = out + cw_ref[k:k + 1, cols(c, half)] * window
        return out

    def gate(c, slot):
        act_scr[slot] = (jax.nn.silu(conv(c, slot, 0)) * conv(c, slot, 1)).astype(BF16)

    def down_proj(c, slot):
        acc_scr[...] += _dot(act_scr[slot], wdn_ref[cols(c, 0), :])

    def stage(c, slot):
        up_proj(c + 1, 1 - slot)
        down_proj(c - 1, 1 - slot)
        gate(c, slot)

    up_proj(0, 0)
    up_proj(1, 1)
    gate(0, 0)
    for c in range(1, n_chunks - 1):
        stage(c, c % 2)
    down_proj(n_chunks - 2, (n_chunks - 2) % 2)
    gate(n_chunks - 1, (n_chunks - 1) % 2)
    down_proj(n_chunks - 1, (n_chunks - 1) % 2)

    for gi in range(groups):
        out = acc_scr[gi * F32_SUBLANES:(gi + 1) * F32_SUBLANES, :]
        if final_norm:
            out = _rms_norm(out, gf_ref[...])
        for j in range(slabs):
            slab_scr[j, pl.ds(gi, F32_SUBLANES, stride=groups), :] = out[:, lanes(j)]
    for j in range(slabs):
        o_ref[0, :, lanes(j)] = slab_scr[j, :tile_rows, :]


def _conv_ffn(x, norm_g, w_up, conv_w, conv_b, w_down, final_g, final_norm, row_tile=FFN_ROW_TILE):
    b, s, d = x.shape
    halo_blocks = row_tile // FFN_HALO
    conv_b = conv_b.reshape(1, 2 * D_FF)
    perm_rows = row_tile + 2 * F32_SUBLANES
    return pl.pallas_call(
        functools.partial(_ffn_kernel, final_norm=final_norm),
        grid=(b, s // row_tile),
        in_specs=[
            pl.BlockSpec((1, row_tile, d), lambda bi, i: (bi, i, 0)),
            pl.BlockSpec((1, FFN_HALO, d), lambda bi, i: (bi, jnp.maximum(i * halo_blocks - 1, 0), 0)),
            _resident((1, d)), _resident_layer(w_up), _resident(conv_w.shape), _resident(conv_b.shape),
            _resident_layer(w_down), _resident((1, d)),
        ],
        out_specs=pl.BlockSpec((1, row_tile, d), lambda bi, i: (bi, i, 0)),
        out_shape=jax.ShapeDtypeStruct(x.shape, F32),
        scratch_shapes=[
            pltpu.VMEM((d // LANES, perm_rows, LANES), F32),
            pltpu.VMEM((FFN_HALO + perm_rows, d), BF16),
            pltpu.VMEM((2, 2, FFN_HALO + perm_rows, FFN_COL_CHUNK), F32),
            pltpu.VMEM((2, perm_rows, FFN_COL_CHUNK), BF16),
            pltpu.VMEM((perm_rows, d), F32),
        ],
        compiler_params=_params("parallel", "parallel"),
        name="conv_ffn",
    )(x, x, norm_g.reshape(1, d), w_up.stacked, conv_w, conv_b, w_down.stacked, final_g.reshape(1, d))


def kernel(x, mem, rel_bias, norm_mix_g, w_in, gate_b, conv_w, conv_b, conv_ln_g, conv_ln_b, lam_q1, lam_k1, lam_q2, lam_k2, subln_g, gmlp_ln_g, gmlp_ln_b, w_s, b_s, w_br, w_out, norm_xattn_g, norm_mem_g, w_xq, w_xkv, w_xo, norm_ffn_g, w_up, ffn_conv_w, ffn_conv_b, w_down, norm_final_g):
    b, s, d = x.shape
    m = mem.shape[1]
    rows = b * s
    bias_tiles = _bias_tiles(rel_bias)
    w_in_l = _to_bf16(w_in, 0)
    for l in range(DEPTH):
        lambda_init = 0.8 - 0.6 * math.exp(-0.3 * l)
        ya, pq, pk, pv, pc = _mixer_in_proj(x, norm_mix_g[l], w_in_l, conv_w[l], conv_b[l], conv_ln_g[l],
                                            conv_ln_b[l])
        side_casts = [(w, l) for w in (w_br, w_out, w_xq, w_xkv, w_xo, w_up, w_down)]
        if l + 1 < DEPTH:
            side_casts.append((w_in, l + 1))
        yb, casts = _diff_attention(pq, pk, pv, bias_tiles, lam_q1[l], lam_k1[l], lam_q2[l], lam_k2[l],
                                    subln_g[l], lambda_init, side_casts)
        w_br_l, w_out_l, w_xq_l, w_xkv_l, w_xo_l, w_up_l, w_down_l = casts[:7]
        yc = _gmlp_branch(pc.reshape(rows, COL_C), gmlp_ln_g[l], gmlp_ln_b[l], w_s[l], b_s[l])
        x = _merge(x.reshape(rows, d), ya.reshape(rows, BRANCH_D), yb.reshape(rows, BRANCH_D), yc, norm_mix_g[l],
                   w_in_l, gate_b[l], w_br_l, w_out_l).reshape(b, s, d)
        (kv,) = _norm_proj(mem.reshape(b * m, d), norm_mem_g[l], w_xkv_l, (2 * XATTN_D,))
        x = _mem_cross_attention(x, kv.reshape(b, m, 2 * XATTN_D), norm_xattn_g[l], w_xq_l, w_xo_l)
        x = _conv_ffn(x, norm_ffn_g[l], w_up_l, ffn_conv_w[l], ffn_conv_b[l], w_down_l, norm_final_g,
                      final_norm=(l == DEPTH - 1))
        if l + 1 < DEPTH:
            w_in_l = casts[7]
    return x
```

```python
import functools
import math
from typing import NamedTuple

import jax
import jax.numpy as jnp
from jax import lax
from jax.experimental import pallas as pl
from jax.experimental.pallas import tpu as pltpu

F32 = jnp.float32
BF16 = jnp.bfloat16

D_MODEL = 1024
DEPTH = 2
CHUNK = 64
BRANCH_D = D_MODEL // 2
N_BRANCH = 3
CONV_K = 31
DIFF_HEADS = 4
DIFF_DH = D_MODEL // 16
DIFF_VD = 2 * DIFF_DH
GMLP_GROUPS = 4
GMLP_GD = BRANCH_D // GMLP_GROUPS
GMLP_CHUNK = 128
N_REL_BUCKETS = 32
REL_MAX_DIST = 128
XATTN_HEADS = 4
XATTN_DH = D_MODEL // 8
XATTN_D = XATTN_HEADS * XATTN_DH
D_FF = 2816
FFN_K = 3
EPS = 1e-6
NEG_INF = -1e30

COL_A = 2 * BRANCH_D
COL_QK = DIFF_HEADS * 2 * DIFF_DH
COL_V = DIFF_HEADS * DIFF_VD
COL_C = 2 * BRANCH_D
COL_G = N_BRANCH * D_MODEL
IN_SPLIT_WIDTHS = (COL_A, COL_QK, COL_QK, COL_V, COL_C)
COL_BRANCHES = sum(IN_SPLIT_WIDTHS)
LOG2_E = math.log2(math.e)
IN_SPLIT_SCALES = (1.0, DIFF_DH ** -0.5 * LOG2_E, 1.0, 1.0, 1.0)

LANES = 128
MXU_WIDTH = 256
F32_SUBLANES = 8
BF16_SUBLANES = 16
VMEM_LIMIT_BYTES = 56 * 1024 * 1024
CAST_BLOCK_BYTES = 4 * 1024 * 1024

ROW_TILE = 512
WIDE_ROW_TILE = 1024
FFN_ROW_TILE = 1024
PROJ_COL_CHUNK = 512
ATTN_TILE = 256
CONV_HALO = 32
CONV_ROW_CHUNK = 64
FFN_HALO = BF16_SUBLANES
FFN_COL_CHUNK = MXU_WIDTH

assert CONV_HALO >= CONV_K - 1 and FFN_HALO == F32_SUBLANES * (FFN_K - 1)
assert D_FF % FFN_COL_CHUNK == 0
assert ATTN_TILE % CHUNK == 0 and ATTN_TILE > REL_MAX_DIST


def _params(*semantics):
    return pltpu.CompilerParams(dimension_semantics=semantics, vmem_limit_bytes=VMEM_LIMIT_BYTES)


def _resident(shape):
    zeros = (0,) * len(shape)
    return pl.BlockSpec(shape, lambda *_: zeros, pipeline_mode=pl.Buffered(1))


class _LayerWeight(NamedTuple):
    stacked: jax.Array
    layer: int

    @property
    def shape(self):
        return self.stacked.shape[1:]


def _resident_layer(w):
    index = (w.layer,) + (0,) * len(w.shape)
    return pl.BlockSpec((None,) + tuple(w.shape), lambda *_: index, pipeline_mode=pl.Buffered(1))


def _cast_kernel(w_ref, o_ref):
    o_ref[...] = w_ref[...].astype(BF16)


def _to_bf16(w, layer):
    _, rows, cols = w.shape
    block_rows = max(r for r in range(BF16_SUBLANES, rows + 1, BF16_SUBLANES)
                     if rows % r == 0 and r * cols * 4 <= CAST_BLOCK_BYTES)
    out = pl.pallas_call(
        _cast_kernel,
        grid=(rows // block_rows,),
        in_specs=[pl.BlockSpec((None, block_rows, cols), lambda i: (layer, i, 0))],
        out_specs=pl.BlockSpec((None, block_rows, cols), lambda i: (0, i, 0)),
        out_shape=jax.ShapeDtypeStruct((1, rows, cols), BF16),
        compiler_params=_params("parallel"),
        name="weight_cast",
    )(w)
    return _LayerWeight(out, 0)


def _rms_norm(x, g):
    return x * lax.rsqrt(jnp.mean(x * x, axis=-1, keepdims=True) + EPS) * g


def _layer_norm(x, g, b):
    mu = jnp.mean(x, axis=-1, keepdims=True)
    xc = x - mu
    var = jnp.mean(xc * xc, axis=-1, keepdims=True)
    return xc * lax.rsqrt(var + EPS) * g + b


def _dot(a, b):
    return jnp.dot(a, b, preferred_element_type=F32)


def _dot_nt(a, b):
    return lax.dot_general(a, b, (((1,), (1,)), ((), ())), preferred_element_type=F32)


def _norm_proj_kernel(x_ref, g_ref, w_ref, *out_refs, widths, scales):
    xn = _rms_norm(x_ref[...], g_ref[...]).astype(BF16)
    col = 0
    for o_ref, width, scale in zip(out_refs, widths, scales):
        for c in range(0, width, PROJ_COL_CHUNK):
            out = _dot(xn, w_ref[:, col + c:col + c + PROJ_COL_CHUNK])
            if scale != 1.0:
                out = out * scale
            o_ref[:, c:c + PROJ_COL_CHUNK] = out.astype(BF16)
        col += width


def _norm_proj(x, g, w, widths, scales=None, row_tile=ROW_TILE):
    rows, d = x.shape
    scales = scales or (1.0,) * len(widths)
    assert w.shape == (d, sum(widths)) and all(wd % PROJ_COL_CHUNK == 0 for wd in widths)
    return pl.pallas_call(
        functools.partial(_norm_proj_kernel, widths=widths, scales=scales),
        grid=(rows // row_tile,),
        in_specs=[pl.BlockSpec((row_tile, d), lambda i: (i, 0)), _resident((1, d)), _resident_layer(w)],
        out_specs=[pl.BlockSpec((row_tile, wd), lambda i: (i, 0)) for wd in widths],
        out_shape=[jax.ShapeDtypeStruct((rows, wd), BF16) for wd in widths],
        compiler_params=_params("parallel"),
        name="norm_proj",
    )(x, g.reshape(1, d), w.stacked)


def _mixer_in_kernel(x_ref, g_ref, w_ref, cw_ref, cb_ref, lng_ref, lnb_ref,
                     ya_ref, pq_ref, pk_ref, pv_ref, pc_ref, xn_scr, xs_scr, halo_scr):
    rows = ya_ref.shape[1]
    xn_scr[...] = _rms_norm(x_ref[0], g_ref[...]).astype(BF16)

    def proj(col, width):
        return _dot(xn_scr[...], w_ref[:, col:col + width])

    glu = proj(0, BRANCH_D) * jax.nn.sigmoid(proj(BRANCH_D, BRANCH_D))

    @pl.when(pl.program_id(1) == 0)
    def _():
        halo_scr[...] = jnp.zeros(halo_scr.shape, F32)

    xs_scr[0, :CONV_HALO, :] = halo_scr[...]
    xs_scr[0, CONV_HALO:, :] = glu
    halo_scr[...] = glu[rows - CONV_HALO:, :]

    def shifted_copies():
        shifted_rows = CONV_HALO + rows - F32_SUBLANES
        for b in range(1, F32_SUBLANES):
            xs_scr[b, :shifted_rows, :] = xs_scr[0, b:b + shifted_rows, :]

    first_tap = CONV_HALO - (CONV_K - 1)

    def conv_rows(r):
        acc = jnp.broadcast_to(cb_ref[...], (CONV_ROW_CHUNK, BRANCH_D))
        for k in range(CONV_K):
            b = (first_tap + k) % F32_SUBLANES
            start = r + first_tap + k - b
            acc = acc + cw_ref[k:k + 1, :] * xs_scr[b, start:start + CONV_ROW_CHUNK, :]
        ya_ref[0, r:r + CONV_ROW_CHUNK, :] = jax.nn.silu(_layer_norm(acc, lng_ref[...], lnb_ref[...])).astype(BF16)

    def proj_chunk(o_ref, col, c, scale):
        out = proj(col + c, PROJ_COL_CHUNK)
        if scale != 1.0:
            out = out * scale
        o_ref[0, :, c:c + PROJ_COL_CHUNK] = out.astype(BF16)

    col = COL_A
    for o_ref, width, scale in zip((pq_ref, pk_ref, pv_ref, pc_ref), IN_SPLIT_WIDTHS[1:], IN_SPLIT_SCALES[1:]):
        for c in range(0, width, PROJ_COL_CHUNK):
            proj_chunk(o_ref, col, c, scale)
        col += width
    shifted_copies()
    for r in range(0, rows, CONV_ROW_CHUNK):
        conv_rows(r)


def _mixer_in_proj(x, norm_g, w_in, conv_w, conv_b, ln_g, ln_b, row_tile=ROW_TILE):
    b, s, d = x.shape
    vec = lambda a: a.reshape(1, -1)
    out_widths = (BRANCH_D,) + IN_SPLIT_WIDTHS[1:]
    return pl.pallas_call(
        _mixer_in_kernel,
        grid=(b, s // row_tile),
        in_specs=[
            pl.BlockSpec((1, row_tile, d), lambda bi, i: (bi, i, 0)),
            _resident((1, d)), _resident_layer(w_in),
            _resident((CONV_K, BRANCH_D)), _resident((1, BRANCH_D)), _resident((1, BRANCH_D)),
            _resident((1, BRANCH_D)),
        ],
        out_specs=[pl.BlockSpec((1, row_tile, wd), lambda bi, i: (bi, i, 0)) for wd in out_widths],
        out_shape=[jax.ShapeDtypeStruct((b, s, wd), BF16) for wd in out_widths],
        scratch_shapes=[
            pltpu.VMEM((row_tile, d), BF16),
            pltpu.VMEM((F32_SUBLANES, CONV_HALO + row_tile, BRANCH_D), F32),
            pltpu.VMEM((CONV_HALO, BRANCH_D), F32),
        ],
        compiler_params=_params("parallel", "arbitrary"),
        name="mixer_in_proj",
    )(x, vec(norm_g), w_in.stacked, conv_w, vec(conv_b), vec(ln_g), vec(ln_b))


def _rel_bucket(rel):
    half = N_REL_BUCKETS // 2
    max_exact = half // 2
    n = jnp.abs(rel)
    log_ratio = jnp.log(jnp.maximum(n, 1).astype(F32) / max_exact) / math.log(REL_MAX_DIST / max_exact)
    large = jnp.minimum(max_exact + (log_ratio * (half - max_exact)).astype(jnp.int32), half - 1)
    return jnp.where(rel > 0, half, 0) + jnp.where(n < max_exact, n, large)


FAR_BUCKET = N_REL_BUCKETS // 2 - 1
MASKED_BUCKET = -1


def _near_buckets():
    r = jnp.arange(ATTN_TILE)[:, None]
    c = jnp.arange(ATTN_TILE)[None, :]
    visible = (c // CHUNK) <= (r // CHUNK)
    diag = jnp.where(visible, _rel_bucket(c - r), MASKED_BUCKET)
    sub = _rel_bucket(c - r - ATTN_TILE)
    return jnp.stack([diag, sub]).astype(jnp.int32)


def _bias_tiles_kernel(table_ref, bucket_ref, o_ref):
    h = pl.program_id(0)
    bucket = bucket_ref[...]
    far = table_ref[FAR_BUCKET, h]
    out = jnp.zeros(bucket.shape, F32)
    for b in range(N_REL_BUCKETS):
        out = jnp.where(bucket == b, (table_ref[b, h] - far) * LOG2_E, out)
    o_ref[0] = jnp.where(bucket == MASKED_BUCKET, NEG_INF, out)


def _bias_tiles(rel_bias):
    buckets = _near_buckets()
    return pl.pallas_call(
        _bias_tiles_kernel,
        grid=(DIFF_HEADS,),
        in_specs=[pl.BlockSpec(memory_space=pltpu.SMEM), _resident(buckets.shape)],
        out_specs=pl.BlockSpec((1,) + buckets.shape, lambda h: (h, 0, 0, 0)),
        out_shape=jax.ShapeDtypeStruct((DIFF_HEADS,) + buckets.shape, F32),
        compiler_params=_params("arbitrary"),
        name="rel_bias_tiles",
    )(rel_bias.astype(F32), buckets)


def _diff_attn_kernel(*refs, lambda_init, cast_chunks):
    n_casts = len(cast_chunks)
    (q_ref, k_ref, v_ref, bias_ref, lq1_ref, lk1_ref, lq2_ref, lk2_ref, g_ref), refs = refs[:9], refs[9:]
    cast_src, o_ref, cast_dst = refs[:n_casts], refs[n_casts], refs[n_casts + 1:2 * n_casts + 1]
    q2_scr, s_scr, m_scr, acc_scr, vones_scr = refs[2 * n_casts + 1:]
    step = pl.program_id(0) * pl.num_programs(1) + pl.program_id(1)
    for src, dst, chunks in zip(cast_src, cast_dst, cast_chunks):
        @pl.when(step < chunks)
        def _():
            dst[...] = src[...].astype(BF16)

    _diff_attn_body(q_ref, k_ref, v_ref, bias_ref, lq1_ref, lk1_ref, lq2_ref, lk2_ref, g_ref, o_ref,
                    q2_scr, s_scr, m_scr, acc_scr, vones_scr, lambda_init=lambda_init)


def _diff_attn_body(q_ref, k_ref, v_ref, bias_ref, lq1_ref, lk1_ref, lq2_ref, lk2_ref, g_ref, o_ref,
                    q2_scr, s_scr, m_scr, acc_scr, vones_scr, *, lambda_init):
    t = ATTN_TILE
    i = pl.program_id(1)
    heads = range(DIFF_HEADS)

    def head_cols(h):
        return slice(h * DIFF_VD, (h + 1) * DIFF_VD)

    @pl.when(i == 0)
    def _():
        for h in heads:
            vones_scr[h, :, :DIFF_VD] = v_ref[0, :, head_cols(h)]
            vones_scr[h, :, DIFF_VD:] = jnp.ones((vones_scr.shape[1], LANES), BF16)

    q = q_ref[0]
    lane = lax.broadcasted_iota(jnp.int32, (t, DIFF_VD), 1)
    for h in heads:
        qh = q[:, head_cols(h)]
        q2_scr[h, :t, :] = jnp.where(lane < DIFF_DH, qh, 0).astype(BF16)
        q2_scr[h, t:, :] = jnp.where(lane >= DIFF_DH, qh, 0).astype(BF16)
    m_scr[...] = jnp.full(m_scr.shape, NEG_INF, F32)
    acc_scr[...] = jnp.zeros(acc_scr.shape, F32)

    def logits_tiles(j0, near):
        for n, bias_index in enumerate(near):
            start = pl.multiple_of((j0 + n) * t, t)
            for h in heads:
                s = _dot_nt(q2_scr[h], k_ref[0, pl.ds(start, t), head_cols(h)])
                if bias_index is not None:
                    bias = bias_ref[h, bias_index]
                    s = s + jnp.concatenate([bias, bias], axis=0)
                s_scr[j0 + n, h] = s
                m_scr[h] = jnp.maximum(m_scr[h], jnp.maximum(s[:, :LANES], s[:, LANES:]))

    def far_pair(pair, carry):
        logits_tiles(2 * pair, (None, None))
        return carry

    n_far = jnp.maximum(i - 1, 0)
    lax.fori_loop(0, n_far // 2, far_pair, 0)

    @pl.when(n_far % 2 == 1)
    def _():
        logits_tiles(n_far - 1, (None,))

    @pl.when(i > 0)
    def _():
        logits_tiles(i - 1, (1, 0))

    @pl.when(i == 0)
    def _():
        logits_tiles(0, (0,))

    for h in heads:
        m_scr[h] = jnp.broadcast_to(jnp.max(m_scr[h], axis=-1, keepdims=True), (2 * t, LANES))

    def pv_tiles(j0, n_tiles):
        start = pl.multiple_of(j0 * t, t)
        for h in heads:
            m = m_scr[h]
            parts = [jnp.exp2(s_scr[j0 + n, h, :, half * LANES:(half + 1) * LANES] - m)
                     for n in range(n_tiles) for half in range(t // LANES)]
            p = jnp.concatenate(parts, axis=1).astype(BF16)
            acc_scr[h] += _dot(p, vones_scr[h, pl.ds(start, n_tiles * t), :])

    def pv_pair(pair, carry):
        pv_tiles(2 * pair, 2)
        return carry

    n_visible = i + 1
    lax.fori_loop(0, n_visible // 2, pv_pair, 0)

    @pl.when(n_visible % 2 == 1)
    def _():
        pv_tiles(n_visible - 1, 1)

    lam = (jnp.exp(jnp.sum(lq1_ref[...] * lk1_ref[...])) - jnp.exp(jnp.sum(lq2_ref[...] * lk2_ref[...]))
           + lambda_init)
    for h in heads:
        o = (acc_scr[h, :t, :DIFF_VD] / acc_scr[h, :t, DIFF_VD:]
             - lam * (acc_scr[h, t:, :DIFF_VD] / acc_scr[h, t:, DIFF_VD:]))
        o_ref[0, :, head_cols(h)] = (_rms_norm(o, g_ref[...]) * (1.0 - lambda_init)).astype(BF16)


def _diff_attention(pq, pk, pv, bias_tiles, lam_q1, lam_k1, lam_q2, lam_k2, subln_g, lambda_init, side_casts):
    b, s, width = pq.shape
    t = ATTN_TILE
    tiles = s // t
    steps = b * tiles
    vec = lambda a: a.reshape(1, -1).astype(F32)
    all_keys = pl.BlockSpec((1, s, width), lambda bi, i: (bi, 0, 0))
    lam_spec = _resident((1, DIFF_DH))

    cast_views, cast_in_specs, cast_out_specs, cast_out_shapes, cast_chunks = [], [], [], [], []
    for w, layer in side_casts:
        view = w.reshape(w.shape[0], -1, w.shape[-1])
        _, rows, cols = view.shape
        chunks = max(c for c in range(1, steps + 1) if rows % (c * BF16_SUBLANES) == 0)

        def chunk(bi, i, chunks=chunks):
            return jnp.minimum(bi * tiles + i, chunks - 1)

        cast_views.append(view)
        cast_chunks.append(chunks)
        cast_in_specs.append(pl.BlockSpec((None, rows // chunks, cols),
                                          lambda bi, i, layer=layer, chunk=chunk: (layer, chunk(bi, i), 0)))
        cast_out_specs.append(pl.BlockSpec((rows // chunks, cols), lambda bi, i, chunk=chunk: (chunk(bi, i), 0)))
        cast_out_shapes.append(jax.ShapeDtypeStruct((rows, cols), BF16))

    out, *casts = pl.pallas_call(
        functools.partial(_diff_attn_kernel, lambda_init=lambda_init, cast_chunks=tuple(cast_chunks)),
        grid=(b, tiles),
        in_specs=[
            pl.BlockSpec((1, t, width), lambda bi, i: (bi, i, 0)),
            all_keys, all_keys, _resident(bias_tiles.shape),
            lam_spec, lam_spec, lam_spec, lam_spec, _resident((1, DIFF_VD)),
        ] + cast_in_specs,
        out_specs=[pl.BlockSpec((1, t, width), lambda bi, i: (bi, i, 0))] + cast_out_specs,
        out_shape=[jax.ShapeDtypeStruct((b, s, width), BF16)] + cast_out_shapes,
        scratch_shapes=[
            pltpu.VMEM((DIFF_HEADS, 2 * t, DIFF_VD), BF16),
            pltpu.VMEM((s // t, DIFF_HEADS, 2 * t, t), F32),
            pltpu.VMEM((DIFF_HEADS, 2 * t, LANES), F32),
            pltpu.VMEM((DIFF_HEADS, 2 * t, DIFF_VD + LANES), F32),
            pltpu.VMEM((DIFF_HEADS, s, DIFF_VD + LANES), BF16),
        ],
        compiler_params=_params("parallel", "arbitrary"),
        name="diff_attention",
    )(pq, pk, pv, bias_tiles, vec(lam_q1), vec(lam_k1), vec(lam_q2), vec(lam_k2), vec(subln_g), *cast_views)
    casts = [_LayerWeight(c.reshape((1,) + w.shape[1:]), 0) for c, (w, _) in zip(casts, side_casts)]
    return out, casts


def _gmlp_kernel(pc_ref, lng_ref, lnb_ref, ws_ref, bs_ref, o_ref):
    rows = o_ref.shape[0]
    z = jax.nn.gelu(pc_ref[...].astype(F32))
    u = z[:, :BRANCH_D]
    v = _layer_norm(z[:, BRANCH_D:], lng_ref[...], lnb_ref[...]).astype(BF16)
    t_idx = lax.broadcasted_iota(jnp.int32, (GMLP_CHUNK, GMLP_CHUNK), 0)
    s_idx = lax.broadcasted_iota(jnp.int32, (GMLP_CHUNK, GMLP_CHUNK), 1)
    for g in range(GMLP_GROUPS):
        w = jnp.where(s_idx <= t_idx, ws_ref[g], 0.0).astype(BF16)
        cols = slice(g * GMLP_GD, (g + 1) * GMLP_GD)
        for r in range(0, rows, GMLP_CHUNK):
            sv = _dot(w, v[r:r + GMLP_CHUNK, cols]) + bs_ref[g]
            o_ref[r:r + GMLP_CHUNK, cols] = (u[r:r + GMLP_CHUNK, cols] * sv).astype(BF16)


def _gmlp_branch(pc, ln_g, ln_b, w_s, b_s, row_tile=WIDE_ROW_TILE):
    rows = pc.shape[0]
    vec = lambda a: a.reshape(1, BRANCH_D)
    return pl.pallas_call(
        _gmlp_kernel,
        grid=(rows // row_tile,),
        in_specs=[
            pl.BlockSpec((row_tile, COL_C), lambda i: (i, 0)),
            _resident((1, BRANCH_D)), _resident((1, BRANCH_D)),
            _resident(w_s.shape), _resident((GMLP_GROUPS, GMLP_CHUNK, 1)),
        ],
        out_specs=pl.BlockSpec((row_tile, BRANCH_D), lambda i: (i, 0)),
        out_shape=jax.ShapeDtypeStruct((rows, BRANCH_D), BF16),
        compiler_params=_params("parallel"),
        name="gmlp_branch",
    )(pc, vec(ln_g), vec(ln_b), w_s, b_s.reshape(GMLP_GROUPS, GMLP_CHUNK, 1))


def _merge_kernel(x_ref, ya_ref, yb_ref, yc_ref, g_ref, win_ref, gb_ref, wbr_ref, wout_ref, o_ref):
    x = x_ref[...]
    xn = _rms_norm(x, g_ref[...]).astype(BF16)
    merged = None
    for n, y_ref in enumerate((ya_ref, yb_ref, yc_ref)):
        col = COL_BRANCHES + n * D_MODEL
        logits = _dot(xn, win_ref[:, col:col + D_MODEL]) + gb_ref[n:n + 1, :]
        term = jax.nn.sigmoid(logits) * _dot(y_ref[...], wbr_ref[n])
        merged = term if merged is None else merged + term
    o_ref[...] = x + _dot(merged.astype(BF16), wout_ref[...])


def _merge(x, ya, yb, yc, norm_g, w_in, gate_b, w_br, w_out, row_tile=WIDE_ROW_TILE):
    rows, d = x.shape
    row_spec = lambda width: pl.BlockSpec((row_tile, width), lambda i: (i, 0))
    return pl.pallas_call(
        _merge_kernel,
        grid=(rows // row_tile,),
        in_specs=[row_spec(D_MODEL), row_spec(BRANCH_D), row_spec(BRANCH_D), row_spec(BRANCH_D),
                  _resident((1, d)), _resident_layer(w_in), _resident(gate_b.shape), _resident_layer(w_br),
                  _resident_layer(w_out)],
        out_specs=row_spec(D_MODEL),
        out_shape=jax.ShapeDtypeStruct(x.shape, F32),
        compiler_params=_params("parallel"),
        name="branch_merge",
    )(x, ya, yb, yc, norm_g.reshape(1, d), w_in.stacked, gate_b, w_br.stacked, w_out.stacked)


def _xattn_kernel(x_ref, g_ref, wq_ref, kv_ref, wo_ref, o_ref):
    x = x_ref[0]
    q = _dot(_rms_norm(x, g_ref[...]).astype(BF16), wq_ref[...]) * (XATTN_DH ** -0.5 * LOG2_E)
    q = q.astype(BF16)
    heads = []
    for h in range(XATTN_HEADS):
        cols = slice(h * XATTN_DH, (h + 1) * XATTN_DH)
        k = kv_ref[0, :, cols]
        v = kv_ref[0, :, XATTN_D + h * XATTN_DH:XATTN_D + (h + 1) * XATTN_DH]
        s = _dot_nt(q[:, cols], k)
        p = jnp.exp2(s - jnp.max(s, axis=-1, keepdims=True))
        o = _dot(p.astype(BF16), v) / jnp.sum(p, axis=-1, keepdims=True)
        heads.append(o.astype(BF16))
    o_ref[0] = x + _dot(jnp.concatenate(heads, axis=-1), wo_ref[...])


def _mem_cross_attention(x, kv, norm_g, w_xq, w_xo, row_tile=WIDE_ROW_TILE):
    b, s, d = x.shape
    m = kv.shape[1]
    return pl.pallas_call(
        _xattn_kernel,
        grid=(b, s // row_tile),
        in_specs=[
            pl.BlockSpec((1, row_tile, d), lambda bi, i: (bi, i, 0)),
            _resident((1, d)), _resident_layer(w_xq),
            pl.BlockSpec((1, m, 2 * XATTN_D), lambda bi, i: (bi, 0, 0)),
            _resident_layer(w_xo),
        ],
        out_specs=pl.BlockSpec((1, row_tile, d), lambda bi, i: (bi, i, 0)),
        out_shape=jax.ShapeDtypeStruct(x.shape, F32),
        compiler_params=_params("parallel", "parallel"),
        name="mem_cross_attention",
    )(x, norm_g.reshape(1, d), w_xq.stacked, kv, w_xo.stacked)


def _ffn_kernel(x_ref, halo_ref, g_ref, wup_ref, cw_ref, cb_ref, wdn_ref, gf_ref, o_ref,
                slab_scr, h_scr, up_scr, act_scr, acc_scr, *, final_norm):
    tile_rows = o_ref.shape[1]
    rows = acc_scr.shape[0]
    groups = rows // F32_SUBLANES
    slabs = slab_scr.shape[0]

    def lanes(j):
        return slice(j * LANES, (j + 1) * LANES)

    for j in range(slabs):
        slab_scr[j, :tile_rows, :] = x_ref[0, :, lanes(j)]
    slab_scr[:, tile_rows:, :] = jnp.zeros((slabs, rows - tile_rows, LANES), F32)
    x = jnp.concatenate(
        [jnp.concatenate([slab_scr[j, pl.ds(gi, F32_SUBLANES, stride=groups), :] for j in range(slabs)], axis=1)
         for gi in range(groups)], axis=0)

    g = g_ref[...]
    halo = _rms_norm(halo_ref[0], g)
    h_scr[:FFN_HALO, :] = jnp.where(pl.program_id(1) > 0, halo, 0.0).astype(BF16)
    h_scr[FFN_HALO:, :] = _rms_norm(x, g).astype(BF16)
    acc_scr[...] = x

    n_chunks = D_FF // FFN_COL_CHUNK
    first_sublane = lax.broadcasted_iota(jnp.int32, (F32_SUBLANES, FFN_COL_CHUNK), 0) == 0

    def cols(c, half):
        start = half * D_FF + c * FFN_COL_CHUNK
        return slice(start, start + FFN_COL_CHUNK)

    def up_proj(c, slot):
        h = h_scr[...]
        for half in range(2):
            up = up_scr.at[slot, half]
            up[...] = _dot(h, wup_ref[:, cols(c, half)])
            prev_tile = up[FFN_HALO - F32_SUBLANES:FFN_HALO, :]
            for shift in range(1, FFN_K):
                wrapped = up[FFN_HALO + rows - shift * F32_SUBLANES:FFN_HALO + rows - (shift - 1) * F32_SUBLANES, :]
                block = jnp.where(first_sublane, pltpu.roll(prev_tile, shift, 0), pltpu.roll(wrapped, 1, 0))
                up[FFN_HALO - shift * F32_SUBLANES:FFN_HALO - (shift - 1) * F32_SUBLANES, :] = block

    def conv(c, slot, half):
        out = cb_ref[:, cols(c, half)]
        for k in range(FFN_K):
            window = up_scr[slot, half, k * F32_SUBLANES:k * F32_SUBLANES + rows, :]
            out = out + cw_ref[k:k + 1, cols(c, half)] * window
        return out

    def gate(c, slot):
        act_scr[slot] = (jax.nn.silu(conv(c, slot, 0)) * conv(c, slot, 1)).astype(BF16)

    def down_proj(c, slot):
        acc_scr[...] += _dot(act_scr[slot], wdn_ref[cols(c, 0), :])

    def stage(c, slot):
        up_proj(c + 1, 1 - slot)
        down_proj(c - 1, 1 - slot)
        gate(c, slot)

    up_proj(0, 0)
    up_proj(1, 1)
    gate(0, 0)
    for c in range(1, n_chunks - 1):
        stage(c, c % 2)
    down_proj(n_chunks - 2, (n_chunks - 2) % 2)
    gate(n_chunks - 1, (n_chunks - 1) % 2)
    down_proj(n_chunks - 1, (n_chunks - 1) % 2)

    for gi in range(groups):
        out = acc_scr[gi * F32_SUBLANES:(gi + 1) * F32_SUBLANES, :]
        if final_norm:
            out = _rms_norm(out, gf_ref[...])
        for j in range(slabs):
            slab_scr[j, pl.ds(gi, F32_SUBLANES, stride=groups), :] = out[:, lanes(j)]
    for j in range(slabs):
        o_ref[0, :, lanes(j)] = slab_scr[j, :tile_rows, :]


def _conv_ffn(x, norm_g, w_up, conv_w, conv_b, w_down, final_g, final_norm, row_tile=FFN_ROW_TILE):
    b, s, d = x.shape
    halo_blocks = row_tile // FFN_HALO
    conv_b = conv_b.reshape(1, 2 * D_FF)
    perm_rows = row_tile + 2 * F32_SUBLANES
    return pl.pallas_call(
        functools.partial(_ffn_kernel, final_norm=final_norm),
        grid=(b, s // row_tile),
        in_specs=[
            pl.BlockSpec((1, row_tile, d), lambda bi, i: (bi, i, 0)),
            pl.BlockSpec((1, FFN_HALO, d), lambda bi, i: (bi, jnp.maximum(i * halo_blocks - 1, 0), 0)),
            _resident((1, d)), _resident_layer(w_up), _resident(conv_w.shape), _resident(conv_b.shape),
            _resident_layer(w_down), _resident((1, d)),
        ],
        out_specs=pl.BlockSpec((1, row_tile, d), lambda bi, i: (bi, i, 0)),
        out_shape=jax.ShapeDtypeStruct(x.shape, F32),
        scratch_shapes=[
            pltpu.VMEM((d // LANES, perm_rows, LANES), F32),
            pltpu.VMEM((FFN_HALO + perm_rows, d), BF16),
            pltpu.VMEM((2, 2, FFN_HALO + perm_rows, FFN_COL_CHUNK), F32),
            pltpu.VMEM((2, perm_rows, FFN_COL_CHUNK), BF16),
            pltpu.VMEM((perm_rows, d), F32),
        ],
        compiler_params=_params("parallel", "parallel"),
        name="conv_ffn",
    )(x, x, norm_g.reshape(1, d), w_up.stacked, conv_w, conv_b, w_down.stacked, final_g.reshape(1, d))


def kernel(x, mem, rel_bias, norm_mix_g, w_in, gate_b, conv_w, conv_b, conv_ln_g, conv_ln_b, lam_q1, lam_k1, lam_q2, lam_k2, subln_g, gmlp_ln_g, gmlp_ln_b, w_s, b_s, w_br, w_out, norm_xattn_g, norm_mem_g, w_xq, w_xkv, w_xo, norm_ffn_g, w_up, ffn_conv_w, ffn_conv_b, w_down, norm_final_g):
    b, s, d = x.shape
    m = mem.shape[1]
    rows = b * s
    bias_tiles = _bias_tiles(rel_bias)
    w_in_l = _to_bf16(w_in, 0)
    for l in range(DEPTH):
        lambda_init = 0.8 - 0.6 * math.exp(-0.3 * l)
        ya, pq, pk, pv, pc = _mixer_in_proj(x, norm_mix_g[l], w_in_l, conv_w[l], conv_b[l], conv_ln_g[l],
                                            conv_ln_b[l])
        side_casts = [(w, l) for w in (w_br, w_out, w_xq, w_xkv, w_xo, w_up, w_down)]
        if l + 1 < DEPTH:
            side_casts.append((w_in, l + 1))
        yb, casts = _diff_attention(pq, pk, pv, bias_tiles, lam_q1[l], lam_k1[l], lam_q2[l], lam_k2[l],
                                    subln_g[l], lambda_init, side_casts)
        w_br_l, w_out_l, w_xq_l, w_xkv_l, w_xo_l, w_up_l, w_down_l = casts[:7]
        yc = _gmlp_branch(pc.reshape(rows, COL_C), gmlp_ln_g[l], gmlp_ln_b[l], w_s[l], b_s[l])
        x = _merge(x.reshape(rows, d), ya.reshape(rows, BRANCH_D), yb.reshape(rows, BRANCH_D), yc, norm_mix_g[l],
                   w_in_l, gate_b[l], w_br_l, w_out_l).reshape(b, s, d)
        (kv,) = _norm_proj(mem.reshape(b * m, d), norm_mem_g[l], w_xkv_l, (2 * XATTN_D,))
        x = _mem_cross_attention(x, kv.reshape(b, m, 2 * XATTN_D), norm_xattn_g[l], w_xq_l, w_xo_l)
        x = _conv_ffn(x, norm_ffn_g[l], w_up_l, ffn_conv_w[l], ffn_conv_b[l], w_down_l, norm_final_g,
                      final_norm=(l == DEPTH - 1))
        if l + 1 < DEPTH:
            w_in_l = casts[7]
    return x
```

```python
import functools
import math
from typing import NamedTuple

import jax
import jax.numpy as jnp
from jax import lax
from jax.experimental import pallas as pl
from jax.experimental.pallas import tpu as pltpu

F32 = jnp.float32
BF16 = jnp.bfloat16

D_MODEL = 1024
DEPTH = 2
CHUNK = 64
BRANCH_D = D_MODEL // 2
N_BRANCH = 3
CONV_K = 31
DIFF_HEADS = 4
DIFF_DH = D_MODEL // 16
DIFF_VD = 2 * DIFF_DH
GMLP_GROUPS = 4
GMLP_GD = BRANCH_D // GMLP_GROUPS
GMLP_CHUNK = 128
N_REL_BUCKETS = 32
REL_MAX_DIST = 128
XATTN_HEADS = 4
XATTN_DH = D_MODEL // 8
XATTN_D = XATTN_HEADS * XATTN_DH
D_FF = 2816
FFN_K = 3
EPS = 1e-6
NEG_INF = -1e30

COL_A = 2 * BRANCH_D
COL_QK = DIFF_HEADS * 2 * DIFF_DH
COL_V = DIFF_HEADS * DIFF_VD
COL_C = 2 * BRANCH_D
COL_G = N_BRANCH * D_MODEL
IN_SPLIT_WIDTHS = (COL_A, COL_QK, COL_QK, COL_V, COL_C)
COL_BRANCHES = sum(IN_SPLIT_WIDTHS)
LOG2_E = math.log2(math.e)
IN_SPLIT_SCALES = (1.0, DIFF_DH ** -0.5 * LOG2_E, 1.0, 1.0, 1.0)

LANES = 128
MXU_WIDTH = 256
F32_SUBLANES = 8
BF16_SUBLANES = 16
VMEM_LIMIT_BYTES = 56 * 1024 * 1024
CAST_BLOCK_BYTES = 4 * 1024 * 1024

ROW_TILE = 512
WIDE_ROW_TILE = 1024
FFN_ROW_TILE = 1024
PROJ_COL_CHUNK = 512
ATTN_TILE = 256
CONV_HALO = 32
CONV_ROW_CHUNK = 64
FFN_HALO = BF16_SUBLANES
FFN_COL_CHUNK = MXU_WIDTH

assert CONV_HALO >= CONV_K - 1 and FFN_HALO == F32_SUBLANES * (FFN_K - 1)
assert D_FF % FFN_COL_CHUNK == 0
assert ATTN_TILE % CHUNK == 0 and ATTN_TILE > REL_MAX_DIST


def _params(*semantics):
    return pltpu.CompilerParams(dimension_semantics=semantics, vmem_limit_bytes=VMEM_LIMIT_BYTES)


def _resident(shape):
    zeros = (0,) * len(shape)
    return pl.BlockSpec(shape, lambda *_: zeros, pipeline_mode=pl.Buffered(1))


class _LayerWeight(NamedTuple):
    stacked: jax.Array
    layer: int

    @property
    def shape(self):
        return self.stacked.shape[1:]


def _resident_layer(w):
    index = (w.layer,) + (0,) * len(w.shape)
    return pl.BlockSpec((None,) + tuple(w.shape), lambda *_: index, pipeline_mode=pl.Buffered(1))


def _cast_kernel(w_ref, o_ref):
    o_ref[...] = w_ref[...].astype(BF16)


def _to_bf16(w, layer):
    _, rows, cols = w.shape
    block_rows = max(r for r in range(BF16_SUBLANES, rows + 1, BF16_SUBLANES)
                     if rows % r == 0 and r * cols * 4 <= CAST_BLOCK_BYTES)
    out = pl.pallas_call(
        _cast_kernel,
        grid=(rows // block_rows,),
        in_specs=[pl.BlockSpec((None, block_rows, cols), lambda i: (layer, i, 0))],
        out_specs=pl.BlockSpec((None, block_rows, cols), lambda i: (0, i, 0)),
        out_shape=jax.ShapeDtypeStruct((1, rows, cols), BF16),
        compiler_params=_params("parallel"),
        name="weight_cast",
    )(w)
    return _LayerWeight(out, 0)


def _rms_norm(x, g):
    return x * lax.rsqrt(jnp.mean(x * x, axis=-1, keepdims=True) + EPS) * g


def _layer_norm(x, g, b):
    mu = jnp.mean(x, axis=-1, keepdims=True)
    xc = x - mu
    var = jnp.mean(xc * xc, axis=-1, keepdims=True)
    return xc * lax.rsqrt(var + EPS) * g + b


def _dot(a, b):
    return jnp.dot(a, b, preferred_element_type=F32)


def _dot_nt(a, b):
    return lax.dot_general(a, b, (((1,), (1,)), ((), ())), preferred_element_type=F32)


def _norm_proj_kernel(x_ref, g_ref, w_ref, *out_refs, widths, scales):
    xn = _rms_norm(x_ref[...], g_ref[...]).astype(BF16)
    col = 0
    for o_ref, width, scale in zip(out_refs, widths, scales):
        for c in range(0, width, PROJ_COL_CHUNK):
            out = _dot(xn, w_ref[:, col + c:col + c + PROJ_COL_CHUNK])
            if scale != 1.0:
                out = out * scale
            o_ref[:, c:c + PROJ_COL_CHUNK] = out.astype(BF16)
        col += width


def _norm_proj(x, g, w, widths, scales=None, row_tile=ROW_TILE):
    rows, d = x.shape
    scales = scales or (1.0,) * len(widths)
    assert w.shape == (d, sum(widths)) and all(wd % PROJ_COL_CHUNK == 0 for wd in widths)
    return pl.pallas_call(
        functools.partial(_norm_proj_kernel, widths=widths, scales=scales),
        grid=(rows // row_tile,),
        in_specs=[pl.BlockSpec((row_tile, d), lambda i: (i, 0)), _resident((1, d)), _resident_layer(w)],
        out_specs=[pl.BlockSpec((row_tile, wd), lambda i: (i, 0)) for wd in widths],
        out_shape=[jax.ShapeDtypeStruct((rows, wd), BF16) for wd in widths],
        compiler_params=_params("parallel"),
        name="norm_proj",
    )(x, g.reshape(1, d), w.stacked)


def _mixer_in_kernel(x_ref, g_ref, w_ref, cw_ref, cb_ref, lng_ref, lnb_ref,
                     ya_ref, pq_ref, pk_ref, pv_ref, pc_ref, xn_scr, xs_scr, halo_scr):
    rows = ya_ref.shape[1]
    xn_scr[...] = _rms_norm(x_ref[0], g_ref[...]).astype(BF16)

    def proj(col, width):
        return _dot(xn_scr[...], w_ref[:, col:col + width])

    glu = proj(0, BRANCH_D) * jax.nn.sigmoid(proj(BRANCH_D, BRANCH_D))

    @pl.when(pl.program_id(1) == 0)
    def _():
        halo_scr[...] = jnp.zeros(halo_scr.shape, F32)

    xs_scr[0, :CONV_HALO, :] = halo_scr[...]
    xs_scr[0, CONV_HALO:, :] = glu
    halo_scr[...] = glu[rows - CONV_HALO:, :]

    def shifted_copies():
        total = CONV_HALO + rows
        base = xs_scr[0]
        for b in range(1, F32_SUBLANES):
            xs_scr[b] = pltpu.roll(base, total - b, 0)

    first_tap = CONV_HALO - (CONV_K - 1)

    def conv_rows(r):
        acc = jnp.broadcast_to(cb_ref[...], (CONV_ROW_CHUNK, BRANCH_D))
        for k in range(CONV_K):
            b = (first_tap + k) % F32_SUBLANES
            start = r + first_tap + k - b
            acc = acc + cw_ref[k:k + 1, :] * xs_scr[b, start:start + CONV_ROW_CHUNK, :]
        ya_ref[0, r:r + CONV_ROW_CHUNK, :] = jax.nn.silu(_layer_norm(acc, lng_ref[...], lnb_ref[...])).astype(BF16)

    def proj_chunk(o_ref, col, c, scale):
        out = proj(col + c, PROJ_COL_CHUNK)
        if scale != 1.0:
            out = out * scale
        o_ref[0, :, c:c + PROJ_COL_CHUNK] = out.astype(BF16)

    col = COL_A
    for o_ref, width, scale in zip((pq_ref, pk_ref, pv_ref, pc_ref), IN_SPLIT_WIDTHS[1:], IN_SPLIT_SCALES[1:]):
        for c in range(0, width, PROJ_COL_CHUNK):
            proj_chunk(o_ref, col, c, scale)
        col += width
    shifted_copies()
    for r in range(0, rows, CONV_ROW_CHUNK):
        conv_rows(r)


def _mixer_in_proj(x, norm_g, w_in, conv_w, conv_b, ln_g, ln_b, row_tile=ROW_TILE):
    b, s, d = x.shape
    vec = lambda a: a.reshape(1, -1)
    out_widths = (BRANCH_D,) + IN_SPLIT_WIDTHS[1:]
    return pl.pallas_call(
        _mixer_in_kernel,
        grid=(b, s // row_tile),
        in_specs=[
            pl.BlockSpec((1, row_tile, d), lambda bi, i: (bi, i, 0)),
            _resident((1, d)), _resident_layer(w_in),
            _resident((CONV_K, BRANCH_D)), _resident((1, BRANCH_D)), _resident((1, BRANCH_D)),
            _resident((1, BRANCH_D)),
        ],
        out_specs=[pl.BlockSpec((1, row_tile, wd), lambda bi, i: (bi, i, 0)) for wd in out_widths],
        out_shape=[jax.ShapeDtypeStruct((b, s, wd), BF16) for wd in out_widths],
        scratch_shapes=[
            pltpu.VMEM((row_tile, d), BF16),
            pltpu.VMEM((F32_SUBLANES, CONV_HALO + row_tile, BRANCH_D), F32),
            pltpu.VMEM((CONV_HALO, BRANCH_D), F32),
        ],
        compiler_params=_params("parallel", "arbitrary"),
        name="mixer_in_proj",
    )(x, vec(norm_g), w_in.stacked, conv_w, vec(conv_b), vec(ln_g), vec(ln_b))


def _rel_bucket(rel):
    half = N_REL_BUCKETS // 2
    max_exact = half // 2
    n = jnp.abs(rel)
    log_ratio = jnp.log(jnp.maximum(n, 1).astype(F32) / max_exact) / math.log(REL_MAX_DIST / max_exact)
    large = jnp.minimum(max_exact + (log_ratio * (half - max_exact)).astype(jnp.int32), half - 1)
    return jnp.where(rel > 0, half, 0) + jnp.where(n < max_exact, n, large)


FAR_BUCKET = N_REL_BUCKETS // 2 - 1
MASKED_BUCKET = -1


def _near_buckets():
    r = jnp.arange(ATTN_TILE)[:, None]
    c = jnp.arange(ATTN_TILE)[None, :]
    visible = (c // CHUNK) <= (r // CHUNK)
    diag = jnp.where(visible, _rel_bucket(c - r), MASKED_BUCKET)
    sub = _rel_bucket(c - r - ATTN_TILE)
    return jnp.stack([diag, sub]).astype(jnp.int32)


def _bias_tiles_kernel(table_ref, bucket_ref, o_ref):
    h = pl.program_id(0)
    bucket = bucket_ref[...]
    far = table_ref[FAR_BUCKET, h]
    out = jnp.zeros(bucket.shape, F32)
    for b in range(N_REL_BUCKETS):
        out = jnp.where(bucket == b, (table_ref[b, h] - far) * LOG2_E, out)
    o_ref[0] = jnp.where(bucket == MASKED_BUCKET, NEG_INF, out)


def _bias_tiles(rel_bias):
    buckets = _near_buckets()
    return pl.pallas_call(
        _bias_tiles_kernel,
        grid=(DIFF_HEADS,),
        in_specs=[pl.BlockSpec(memory_space=pltpu.SMEM), _resident(buckets.shape)],
        out_specs=pl.BlockSpec((1,) + buckets.shape, lambda h: (h, 0, 0, 0)),
        out_shape=jax.ShapeDtypeStruct((DIFF_HEADS,) + buckets.shape, F32),
        compiler_params=_params("arbitrary"),
        name="rel_bias_tiles",
    )(rel_bias.astype(F32), buckets)


def _diff_attn_kernel(*refs, lambda_init, cast_chunks):
    n_casts = len(cast_chunks)
    (q_ref, k_ref, v_ref, bias_ref, lq1_ref, lk1_ref, lq2_ref, lk2_ref, g_ref), refs = refs[:9], refs[9:]
    cast_src, o_ref, cast_dst = refs[:n_casts], refs[n_casts], refs[n_casts + 1:2 * n_casts + 1]
    q2_scr, s_scr, m_scr, acc_scr, vones_scr = refs[2 * n_casts + 1:]
    step = pl.program_id(0) * pl.num_programs(1) + pl.program_id(1)
    for src, dst, chunks in zip(cast_src, cast_dst, cast_chunks):
        @pl.when(step < chunks)
        def _():
            dst[...] = src[...].astype(BF16)

    _diff_attn_body(q_ref, k_ref, v_ref, bias_ref, lq1_ref, lk1_ref, lq2_ref, lk2_ref, g_ref, o_ref,
                    q2_scr, s_scr, m_scr, acc_scr, vones_scr, lambda_init=lambda_init)


def _diff_attn_body(q_ref, k_ref, v_ref, bias_ref, lq1_ref, lk1_ref, lq2_ref, lk2_ref, g_ref, o_ref,
                    q2_scr, s_scr, m_scr, acc_scr, vones_scr, *, lambda_init):
    t = ATTN_TILE
    i = pl.program_id(1)
    heads = range(DIFF_HEADS)

    def head_cols(h):
        return slice(h * DIFF_VD, (h + 1) * DIFF_VD)

    @pl.when(i == 0)
    def _():
        for h in heads:
            vones_scr[h, :, :DIFF_VD] = v_ref[0, :, head_cols(h)]
            vones_scr[h, :, DIFF_VD:] = jnp.ones((vones_scr.shape[1], LANES), BF16)

    q = q_ref[0]
    lane = lax.broadcasted_iota(jnp.int32, (t, DIFF_VD), 1)
    for h in heads:
        qh = q[:, head_cols(h)]
        q2_scr[h, :t, :] = jnp.where(lane < DIFF_DH, qh, 0).astype(BF16)
        q2_scr[h, t:, :] = jnp.where(lane >= DIFF_DH, qh, 0).astype(BF16)
    m_scr[...] = jnp.full(m_scr.shape, NEG_INF, F32)

    def logits_tiles(j0, near):
        for n, bias_index in enumerate(near):
            start = pl.multiple_of((j0 + n) * t, t)
            for h in heads:
                s = _dot_nt(q2_scr[h], k_ref[0, pl.ds(start, t), head_cols(h)])
                if bias_index is not None:
                    bias = bias_ref[h, bias_index]
                    s = s + jnp.concatenate([bias, bias], axis=0)
                s_scr[j0 + n, h] = s
                m_scr[h] = jnp.maximum(m_scr[h], jnp.maximum(s[:, :LANES], s[:, LANES:]))

    def far_pair(pair, carry):
        logits_tiles(2 * pair, (None, None))
        return carry

    n_far = jnp.maximum(i - 1, 0)
    lax.fori_loop(0, n_far // 2, far_pair, 0)

    @pl.when(n_far % 2 == 1)
    def _():
        logits_tiles(n_far - 1, (None,))

    @pl.when(i > 0)
    def _():
        logits_tiles(i - 1, (1, 0))

    @pl.when(i == 0)
    def _():
        logits_tiles(0, (0,))

    for h in heads:
        m_scr[h] = jnp.broadcast_to(jnp.max(m_scr[h], axis=-1, keepdims=True), (2 * t, LANES))

    def pv_tiles(j0, n_tiles, first):
        start = pl.multiple_of(j0 * t, t)
        for h in heads:
            m = m_scr[h]
            parts = [jnp.exp2(s_scr[j0 + n, h, :, half * LANES:(half + 1) * LANES] - m)
                     for n in range(n_tiles) for half in range(t // LANES)]
            p = jnp.concatenate(parts, axis=1).astype(BF16)
            pv = _dot(p, vones_scr[h, pl.ds(start, n_tiles * t), :])
            if first:
                acc_scr[h] = pv
            else:
                acc_scr[h] += pv

    n_visible = i + 1
    n_first = 2 - n_visible % 2

    @pl.when(n_first == 1)
    def _():
        pv_tiles(0, 1, True)

    @pl.when(n_first == 2)
    def _():
        pv_tiles(0, 2, True)

    def pv_pair(pair, carry):
        pv_tiles(n_first + 2 * pair, 2, False)
        return carry

    lax.fori_loop(0, (n_visible - n_first) // 2, pv_pair, 0)

    lam = (jnp.exp(jnp.sum(lq1_ref[...] * lk1_ref[...])) - jnp.exp(jnp.sum(lq2_ref[...] * lk2_ref[...]))
           + lambda_init)
    for h in heads:
        o = (acc_scr[h, :t, :DIFF_VD] / acc_scr[h, :t, DIFF_VD:]
             - lam * (acc_scr[h, t:, :DIFF_VD] / acc_scr[h, t:, DIFF_VD:]))
        o_ref[0, :, head_cols(h)] = (_rms_norm(o, g_ref[...]) * (1.0 - lambda_init)).astype(BF16)


def _diff_attention(pq, pk, pv, bias_tiles, lam_q1, lam_k1, lam_q2, lam_k2, subln_g, lambda_init, side_casts):
    b, s, width = pq.shape
    t = ATTN_TILE
    tiles = s // t
    steps = b * tiles
    vec = lambda a: a.reshape(1, -1).astype(F32)
    all_keys = pl.BlockSpec((1, s, width), lambda bi, i: (bi, 0, 0))
    lam_spec = _resident((1, DIFF_DH))

    cast_views, cast_in_specs, cast_out_specs, cast_out_shapes, cast_chunks = [], [], [], [], []
    for w, layer in side_casts:
        view = w.reshape(w.shape[0], -1, w.shape[-1])
        _, rows, cols = view.shape
        chunks = max(c for c in range(1, steps + 1) if rows % (c * BF16_SUBLANES) == 0)

        def chunk(bi, i, chunks=chunks):
            return jnp.minimum(bi * tiles + i, chunks - 1)

        cast_views.append(view)
        cast_chunks.append(chunks)
        cast_in_specs.append(pl.BlockSpec((None, rows // chunks, cols),
                                          lambda bi, i, layer=layer, chunk=chunk: (layer, chunk(bi, i), 0)))
        cast_out_specs.append(pl.BlockSpec((rows // chunks, cols), lambda bi, i, chunk=chunk: (chunk(bi, i), 0)))
        cast_out_shapes.append(jax.ShapeDtypeStruct((rows, cols), BF16))

    out, *casts = pl.pallas_call(
        functools.partial(_diff_attn_kernel, lambda_init=lambda_init, cast_chunks=tuple(cast_chunks)),
        grid=(b, tiles),
        in_specs=[
            pl.BlockSpec((1, t, width), lambda bi, i: (bi, i, 0)),
            all_keys, all_keys, _resident(bias_tiles.shape),
            lam_spec, lam_spec, lam_spec, lam_spec, _resident((1, DIFF_VD)),
        ] + cast_in_specs,
        out_specs=[pl.BlockSpec((1, t, width), lambda bi, i: (bi, i, 0))] + cast_out_specs,
        out_shape=[jax.ShapeDtypeStruct((b, s, width), BF16)] + cast_out_shapes,
        scratch_shapes=[
            pltpu.VMEM((DIFF_HEADS, 2 * t, DIFF_VD), BF16),
            pltpu.VMEM((s // t, DIFF_HEADS, 2 * t, t), F32),
            pltpu.VMEM((DIFF_HEADS, 2 * t, LANES), F32),
            pltpu.VMEM((DIFF_HEADS, 2 * t, DIFF_VD + LANES), F32),
            pltpu.VMEM((DIFF_HEADS, s, DIFF_VD + LANES), BF16),
        ],
        compiler_params=_params("parallel", "arbitrary"),
        name="diff_attention",
    )(pq, pk, pv, bias_tiles, vec(lam_q1), vec(lam_k1), vec(lam_q2), vec(lam_k2), vec(subln_g), *cast_views)
    casts = [_LayerWeight(c.reshape((1,) + w.shape[1:]), 0) for c, (w, _) in zip(casts, side_casts)]
    return out, casts


def _gmlp_kernel(pc_ref, lng_ref, lnb_ref, ws_ref, bs_ref, o_ref):
    rows = o_ref.shape[0]
    z = jax.nn.gelu(pc_ref[...].astype(F32))
    u = z[:, :BRANCH_D]
    v = _layer_norm(z[:, BRANCH_D:], lng_ref[...], lnb_ref[...]).astype(BF16)
    t_idx = lax.broadcasted_iota(jnp.int32, (GMLP_CHUNK, GMLP_CHUNK), 0)
    s_idx = lax.broadcasted_iota(jnp.int32, (GMLP_CHUNK, GMLP_CHUNK), 1)
    for g in range(GMLP_GROUPS):
        w = jnp.where(s_idx <= t_idx, ws_ref[g], 0.0).astype(BF16)
        cols = slice(g * GMLP_GD, (g + 1) * GMLP_GD)
        for r in range(0, rows, GMLP_CHUNK):
            sv = _dot(w, v[r:r + GMLP_CHUNK, cols]) + bs_ref[g]
            o_ref[r:r + GMLP_CHUNK, cols] = (u[r:r + GMLP_CHUNK, cols] * sv).astype(BF16)


def _gmlp_branch(pc, ln_g, ln_b, w_s, b_s, row_tile=WIDE_ROW_TILE):
    rows = pc.shape[0]
    vec = lambda a: a.reshape(1, BRANCH_D)
    return pl.pallas_call(
        _gmlp_kernel,
        grid=(rows // row_tile,),
        in_specs=[
            pl.BlockSpec((row_tile, COL_C), lambda i: (i, 0)),
            _resident((1, BRANCH_D)), _resident((1, BRANCH_D)),
            _resident(w_s.shape), _resident((GMLP_GROUPS, GMLP_CHUNK, 1)),
        ],
        out_specs=pl.BlockSpec((row_tile, BRANCH_D), lambda i: (i, 0)),
        out_shape=jax.ShapeDtypeStruct((rows, BRANCH_D), BF16),
        compiler_params=_params("parallel"),
        name="gmlp_branch",
    )(pc, vec(ln_g), vec(ln_b), w_s, b_s.reshape(GMLP_GROUPS, GMLP_CHUNK, 1))


def _merge_kernel(x_ref, ya_ref, yb_ref, yc_ref, g_ref, win_ref, gb_ref, wbr_ref, wout_ref, o_ref):
    x = x_ref[...]
    xn = _rms_norm(x, g_ref[...]).astype(BF16)
    merged = None
    for n, y_ref in enumerate((ya_ref, yb_ref, yc_ref)):
        col = COL_BRANCHES + n * D_MODEL
        logits = _dot(xn, win_ref[:, col:col + D_MODEL]) + gb_ref[n:n + 1, :]
        term = jax.nn.sigmoid(logits) * _dot(y_ref[...], wbr_ref[n])
        merged = term if merged is None else merged + term
    o_ref[...] = x + _dot(merged.astype(BF16), wout_ref[...])


def _merge(x, ya, yb, yc, norm_g, w_in, gate_b, w_br, w_out, row_tile=WIDE_ROW_TILE):
    rows, d = x.shape
    row_spec = lambda width: pl.BlockSpec((row_tile, width), lambda i: (i, 0))
    return pl.pallas_call(
        _merge_kernel,
        grid=(rows // row_tile,),
        in_specs=[row_spec(D_MODEL), row_spec(BRANCH_D), row_spec(BRANCH_D), row_spec(BRANCH_D),
                  _resident((1, d)), _resident_layer(w_in), _resident(gate_b.shape), _resident_layer(w_br),
                  _resident_layer(w_out)],
        out_specs=row_spec(D_MODEL),
        out_shape=jax.ShapeDtypeStruct(x.shape, F32),
        compiler_params=_params("parallel"),
        name="branch_merge",
    )(x, ya, yb, yc, norm_g.reshape(1, d), w_in.stacked, gate_b, w_br.stacked, w_out.stacked)


def _xattn_kernel(x_ref, g_ref, wq_ref, kv_ref, wo_ref, o_ref):
    x = x_ref[0]
    q = _dot(_rms_norm(x, g_ref[...]).astype(BF16), wq_ref[...]) * (XATTN_DH ** -0.5 * LOG2_E)
    q = q.astype(BF16)
    heads = []
    for h in range(XATTN_HEADS):
        cols = slice(h * XATTN_DH, (h + 1) * XATTN_DH)
        k = kv_ref[0, :, cols]
        v = kv_ref[0, :, XATTN_D + h * XATTN_DH:XATTN_D + (h + 1) * XATTN_DH]
        s = _dot_nt(q[:, cols], k)
        p = jnp.exp2(s - jnp.max(s, axis=-1, keepdims=True))
        o = _dot(p.astype(BF16), v) / jnp.sum(p, axis=-1, keepdims=True)
        heads.append(o.astype(BF16))
    o_ref[0] = x + _dot(jnp.concatenate(heads, axis=-1), wo_ref[...])


def _mem_cross_attention(x, kv, norm_g, w_xq, w_xo, row_tile=WIDE_ROW_TILE):
    b, s, d = x.shape
    m = kv.shape[1]
    return pl.pallas_call(
        _xattn_kernel,
        grid=(b, s // row_tile),
        in_specs=[
            pl.BlockSpec((1, row_tile, d), lambda bi, i: (bi, i, 0)),
            _resident((1, d)), _resident_layer(w_xq),
            pl.BlockSpec((1, m, 2 * XATTN_D), lambda bi, i: (bi, 0, 0)),
            _resident_layer(w_xo),
        ],
        out_specs=pl.BlockSpec((1, row_tile, d), lambda bi, i: (bi, i, 0)),
        out_shape=jax.ShapeDtypeStruct(x.shape, F32),
        compiler_params=_params("parallel", "parallel"),
        name="mem_cross_attention",
    )(x, norm_g.reshape(1, d), w_xq.stacked, kv, w_xo.stacked)


def _ffn_kernel(x_ref, halo_ref, g_ref, wup_ref, cw_ref, cb_ref, wdn_ref, gf_ref, o_ref,
                slab_scr, h_scr, up_scr, act_scr, acc_scr, *, final_norm):
    tile_rows = o_ref.shape[1]
    rows = acc_scr.shape[0]
    groups = rows // F32_SUBLANES
    slabs = slab_scr.shape[0]

    def lanes(j):
        return slice(j * LANES, (j + 1) * LANES)

    for j in range(slabs):
        slab_scr[j, :tile_rows, :] = x_ref[0, :, lanes(j)]
    slab_scr[:, tile_rows:, :] = jnp.zeros((slabs, rows - tile_rows, LANES), F32)
    x = jnp.concatenate(
        [jnp.concatenate([slab_scr[j, pl.ds(gi, F32_SUBLANES, stride=groups), :] for j in range(slabs)], axis=1)
         for gi in range(groups)], axis=0)

    g = g_ref[...]
    halo = _rms_norm(halo_ref[0], g)
    h_scr[:FFN_HALO, :] = jnp.where(pl.program_id(1) > 0, halo, 0.0).astype(BF16)
    h_scr[FFN_HALO:, :] = _rms_norm(x, g).astype(BF16)
    acc_scr[...] = x

    n_chunks = D_FF // FFN_COL_CHUNK
    first_sublane = lax.broadcasted_iota(jnp.int32, (F32_SUBLANES, FFN_COL_CHUNK), 0) == 0

    def cols(c, half):
        start = half * D_FF + c * FFN_COL_CHUNK
        return slice(start, start + FFN_COL_CHUNK)

    def up_proj(c, slot):
        h = h_scr[...]
        for half in range(2):
            up = up_scr.at[slot, half]
            up[...] = _dot(h, wup_ref[:, cols(c, half)])
            prev_tile = up[FFN_HALO - F32_SUBLANES:FFN_HALO, :]
            for shift in range(1, FFN_K):
                wrapped = up[FFN_HALO + rows - shift * F32_SUBLANES:FFN_HALO + rows - (shift - 1) * F32_SUBLANES, :]
                block = jnp.where(first_sublane, pltpu.roll(prev_tile, shift, 0), pltpu.roll(wrapped, 1, 0))
                up[FFN_HALO - shift * F32_SUBLANES:FFN_HALO - (shift - 1) * F32_SUBLANES, :] = block

    def conv(c, slot, half):
        out = cb_ref[:, cols(c, half)]
        for k in range(FFN_K):
            window = up_scr[slot, half, k * F32_SUBLANES:k * F32_SUBLANES + rows, :]
            out = out + cw_ref[k:k + 1, cols(c, half)] * window
        return out

    def gate(c, slot):
        act_scr[slot] = (jax.nn.silu(conv(c, slot, 0)) * conv(c, slot, 1)).astype(BF16)

    def down_proj(c, slot):
        acc_scr[...] += _dot(act_scr[slot], wdn_ref[cols(c, 0), :])

    def stage(c, slot):
        up_proj(c + 1, 1 - slot)
        down_proj(c - 1, 1 - slot)
        gate(c, slot)

    up_proj(0, 0)
    up_proj(1, 1)
    gate(0, 0)
    for c in range(1, n_chunks - 1):
        stage(c, c % 2)
    down_proj(n_chunks - 2, (n_chunks - 2) % 2)
    gate(n_chunks - 1, (n_chunks - 1) % 2)
    down_proj(n_chunks - 1, (n_chunks - 1) % 2)

    for gi in range(groups):
        out = acc_scr[gi * F32_SUBLANES:(gi + 1) * F32_SUBLANES, :]
        if final_norm:
            out = _rms_norm(out, gf_ref[...])
        for j in range(slabs):
            slab_scr[j, pl.ds(gi, F32_SUBLANES, stride=groups), :] = out[:, lanes(j)]
    for j in range(slabs):
        o_ref[0, :, lanes(j)] = slab_scr[j, :tile_rows, :]


def _conv_ffn(x, norm_g, w_up, conv_w, conv_b, w_down, final_g, final_norm, row_tile=FFN_ROW_TILE):
    b, s, d = x.shape
    halo_blocks = row_tile // FFN_HALO
    conv_b = conv_b.reshape(1, 2 * D_FF)
    perm_rows = row_tile + 2 * F32_SUBLANES
    return pl.pallas_call(
        functools.partial(_ffn_kernel, final_norm=final_norm),
        grid=(b, s // row_tile),
        in_specs=[
            pl.BlockSpec((1, row_tile, d), lambda bi, i: (bi, i, 0)),
            pl.BlockSpec((1, FFN_HALO, d), lambda bi, i: (bi, jnp.maximum(i * halo_blocks - 1, 0), 0)),
            _resident((1, d)), _resident_layer(w_up), _resident(conv_w.shape), _resident(conv_b.shape),
            _resident_layer(w_down), _resident((1, d)),
        ],
        out_specs=pl.BlockSpec((1, row_tile, d), lambda bi, i: (bi, i, 0)),
        out_shape=jax.ShapeDtypeStruct(x.shape, F32),
        scratch_shapes=[
            pltpu.VMEM((d // LANES, perm_rows, LANES), F32),
            pltpu.VMEM((FFN_HALO + perm_rows, d), BF16),
            pltpu.VMEM((2, 2, FFN_HALO + perm_rows, FFN_COL_CHUNK), F32),
            pltpu.VMEM((2, perm_rows, FFN_COL_CHUNK), BF16),
            pltpu.VMEM((perm_rows, d), F32),
        ],
        compiler_params=_params("parallel", "parallel"),
        name="conv_ffn",
    )(x, x, norm_g.reshape(1, d), w_up.stacked, conv_w, conv_b, w_down.stacked, final_g.reshape(1, d))


def kernel(x, mem, rel_bias, norm_mix_g, w_in, gate_b, conv_w, conv_b, conv_ln_g, conv_ln_b, lam_q1, lam_k1, lam_q2, lam_k2, subln_g, gmlp_ln_g, gmlp_ln_b, w_s, b_s, w_br, w_out, norm_xattn_g, norm_mem_g, w_xq, w_xkv, w_xo, norm_ffn_g, w_up, ffn_conv_w, ffn_conv_b, w_down, norm_final_g):
    b, s, d = x.shape
    m = mem.shape[1]
    rows = b * s
    bias_tiles = _bias_tiles(rel_bias)
    w_in_l = _to_bf16(w_in, 0)
    for l in range(DEPTH):
        lambda_init = 0.8 - 0.6 * math.exp(-0.3 * l)
        ya, pq, pk, pv, pc = _mixer_in_proj(x, norm_mix_g[l], w_in_l, conv_w[l], conv_b[l], conv_ln_g[l],
                                            conv_ln_b[l])
        side_casts = [(w, l) for w in (w_br, w_out, w_xq, w_xkv, w_xo, w_up, w_down)]
        if l + 1 < DEPTH:
            side_casts.append((w_in, l + 1))
        yb, casts = _diff_attention(pq, pk, pv, bias_tiles, lam_q1[l], lam_k1[l], lam_q2[l], lam_k2[l],
                                    subln_g[l], lambda_init, side_casts)
        w_br_l, w_out_l, w_xq_l, w_xkv_l, w_xo_l, w_up_l, w_down_l = casts[:7]
        yc = _gmlp_branch(pc.reshape(rows, COL_C), gmlp_ln_g[l], gmlp_ln_b[l], w_s[l], b_s[l])
        x = _merge(x.reshape(rows, d), ya.reshape(rows, BRANCH_D), yb.reshape(rows, BRANCH_D), yc, norm_mix_g[l],
                   w_in_l, gate_b[l], w_br_l, w_out_l).reshape(b, s, d)
        (kv,) = _norm_proj(mem.reshape(b * m, d), norm_mem_g[l], w_xkv_l, (2 * XATTN_D,))
        x = _mem_cross_attention(x, kv.reshape(b, m, 2 * XATTN_D), norm_xattn_g[l], w_xq_l, w_xo_l)
        x = _conv_ffn(x, norm_ffn_g[l], w_up_l, ffn_conv_w[l], ffn_conv_b[l], w_down_l, norm_final_g,
                      final_norm=(l == DEPTH - 1))
        if l + 1 < DEPTH:
            w_in_l = casts[7]
    return x
```

```python
import functools
import math
from typing import NamedTuple

import jax
import jax.numpy as jnp
from jax import lax
from jax.experimental import pallas as pl
from jax.experimental.pallas import tpu as pltpu

F32 = jnp.float32
BF16 = jnp.bfloat16

D_MODEL = 1024
DEPTH = 2
CHUNK = 64
BRANCH_D = D_MODEL // 2
N_BRANCH = 3
CONV_K = 31
DIFF_HEADS = 4
DIFF_DH = D_MODEL // 16
DIFF_VD = 2 * DIFF_DH
GMLP_GROUPS = 4
GMLP_GD = BRANCH_D // GMLP_GROUPS
GMLP_CHUNK = 128
N_REL_BUCKETS = 32
REL_MAX_DIST = 128
XATTN_HEADS = 4
XATTN_DH = D_MODEL // 8
XATTN_D = XATTN_HEADS * XATTN_DH
D_FF = 2816
FFN_K = 3
EPS = 1e-6
NEG_INF = -1e30

COL_A = 2 * BRANCH_D
COL_QK = DIFF_HEADS * 2 * DIFF_DH
COL_V = DIFF_HEADS * DIFF_VD
COL_C = 2 * BRANCH_D
COL_G = N_BRANCH * D_MODEL
IN_SPLIT_WIDTHS = (COL_A, COL_QK, COL_QK, COL_V, COL_C)
COL_BRANCHES = sum(IN_SPLIT_WIDTHS)
LOG2_E = math.log2(math.e)
IN_SPLIT_SCALES = (1.0, DIFF_DH ** -0.5 * LOG2_E, 1.0, 1.0, 1.0)

LANES = 128
MXU_WIDTH = 256
F32_SUBLANES = 8
BF16_SUBLANES = 16
VMEM_LIMIT_BYTES = 56 * 1024 * 1024
CAST_BLOCK_BYTES = 4 * 1024 * 1024

ROW_TILE = 512
WIDE_ROW_TILE = 1024
FFN_ROW_TILE = 1024
PROJ_COL_CHUNK = 512
ATTN_TILE = 256
CONV_HALO = 32
CONV_ROW_CHUNK = 64
FFN_HALO = BF16_SUBLANES
FFN_COL_CHUNK = MXU_WIDTH

assert CONV_HALO >= CONV_K - 1 and FFN_HALO == F32_SUBLANES * (FFN_K - 1)
assert D_FF % FFN_COL_CHUNK == 0
assert ATTN_TILE % CHUNK == 0 and ATTN_TILE > REL_MAX_DIST


def _params(*semantics):
    return pltpu.CompilerParams(dimension_semantics=semantics, vmem_limit_bytes=VMEM_LIMIT_BYTES)


def _resident(shape):
    zeros = (0,) * len(shape)
    return pl.BlockSpec(shape, lambda *_: zeros, pipeline_mode=pl.Buffered(1))


class _LayerWeight(NamedTuple):
    stacked: jax.Array
    layer: int

    @property
    def shape(self):
        return self.stacked.shape[1:]


def _resident_layer(w):
    index = (w.layer,) + (0,) * len(w.shape)
    return pl.BlockSpec((None,) + tuple(w.shape), lambda *_: index, pipeline_mode=pl.Buffered(1))


def _cast_kernel(w_ref, o_ref):
    o_ref[...] = w_ref[...].astype(BF16)


def _to_bf16(w, layer):
    _, rows, cols = w.shape
    block_rows = max(r for r in range(BF16_SUBLANES, rows + 1, BF16_SUBLANES)
                     if rows % r == 0 and r * cols * 4 <= CAST_BLOCK_BYTES)
    out = pl.pallas_call(
        _cast_kernel,
        grid=(rows // block_rows,),
        in_specs=[pl.BlockSpec((None, block_rows, cols), lambda i: (layer, i, 0))],
        out_specs=pl.BlockSpec((None, block_rows, cols), lambda i: (0, i, 0)),
        out_shape=jax.ShapeDtypeStruct((1, rows, cols), BF16),
        compiler_params=_params("parallel"),
        name="weight_cast",
    )(w)
    return _LayerWeight(out, 0)


def _rms_norm(x, g):
    return x * lax.rsqrt(jnp.mean(x * x, axis=-1, keepdims=True) + EPS) * g


def _layer_norm(x, g, b):
    mu = jnp.mean(x, axis=-1, keepdims=True)
    xc = x - mu
    var = jnp.mean(xc * xc, axis=-1, keepdims=True)
    return xc * lax.rsqrt(var + EPS) * g + b


def _dot(a, b):
    return jnp.dot(a, b, preferred_element_type=F32)


def _dot_nt(a, b):
    return lax.dot_general(a, b, (((1,), (1,)), ((), ())), preferred_element_type=F32)


def _norm_proj_kernel(x_ref, g_ref, w_ref, *out_refs, widths, scales):
    xn = _rms_norm(x_ref[...], g_ref[...]).astype(BF16)
    col = 0
    for o_ref, width, scale in zip(out_refs, widths, scales):
        for c in range(0, width, PROJ_COL_CHUNK):
            out = _dot(xn, w_ref[:, col + c:col + c + PROJ_COL_CHUNK])
            if scale != 1.0:
                out = out * scale
            o_ref[:, c:c + PROJ_COL_CHUNK] = out.astype(BF16)
        col += width


def _norm_proj(x, g, w, widths, scales=None, row_tile=ROW_TILE):
    rows, d = x.shape
    scales = scales or (1.0,) * len(widths)
    assert w.shape == (d, sum(widths)) and all(wd % PROJ_COL_CHUNK == 0 for wd in widths)
    return pl.pallas_call(
        functools.partial(_norm_proj_kernel, widths=widths, scales=scales),
        grid=(rows // row_tile,),
        in_specs=[pl.BlockSpec((row_tile, d), lambda i: (i, 0)), _resident((1, d)), _resident_layer(w)],
        out_specs=[pl.BlockSpec((row_tile, wd), lambda i: (i, 0)) for wd in widths],
        out_shape=[jax.ShapeDtypeStruct((rows, wd), BF16) for wd in widths],
        compiler_params=_params("parallel"),
        name="norm_proj",
    )(x, g.reshape(1, d), w.stacked)


def _mixer_in_kernel(x_ref, g_ref, w_ref, cw_ref, cb_ref, lng_ref, lnb_ref,
                     ya_ref, pq_ref, pk_ref, pv_ref, pc_ref, xn_scr, xs_scr, halo_scr):
    rows = ya_ref.shape[1]
    xn_scr[...] = _rms_norm(x_ref[0], g_ref[...]).astype(BF16)

    def proj(col, width):
        return _dot(xn_scr[...], w_ref[:, col:col + width])

    glu = proj(0, BRANCH_D) * jax.nn.sigmoid(proj(BRANCH_D, BRANCH_D))

    @pl.when(pl.program_id(1) == 0)
    def _():
        halo_scr[...] = jnp.zeros(halo_scr.shape, F32)

    xs_scr[0, :CONV_HALO, :] = halo_scr[...]
    xs_scr[0, CONV_HALO:, :] = glu
    halo_scr[...] = glu[rows - CONV_HALO:, :]

    def shifted_copies():
        total = CONV_HALO + rows
        base = xs_scr[0]
        for b in range(1, F32_SUBLANES):
            xs_scr[b] = pltpu.roll(base, total - b, 0)

    first_tap = CONV_HALO - (CONV_K - 1)

    def conv_rows(r):
        acc = jnp.broadcast_to(cb_ref[...], (CONV_ROW_CHUNK, BRANCH_D))
        for k in range(CONV_K):
            b = (first_tap + k) % F32_SUBLANES
            start = r + first_tap + k - b
            acc = acc + cw_ref[k:k + 1, :] * xs_scr[b, start:start + CONV_ROW_CHUNK, :]
        ya_ref[0, r:r + CONV_ROW_CHUNK, :] = jax.nn.silu(_layer_norm(acc, lng_ref[...], lnb_ref[...])).astype(BF16)

    def proj_chunk(o_ref, col, c, scale):
        out = proj(col + c, PROJ_COL_CHUNK)
        if scale != 1.0:
            out = out * scale
        o_ref[0, :, c:c + PROJ_COL_CHUNK] = out.astype(BF16)

    col = COL_A
    for o_ref, width, scale in zip((pq_ref, pk_ref, pv_ref, pc_ref), IN_SPLIT_WIDTHS[1:], IN_SPLIT_SCALES[1:]):
        for c in range(0, width, PROJ_COL_CHUNK):
            proj_chunk(o_ref, col, c, scale)
        col += width
    shifted_copies()
    for r in range(0, rows, CONV_ROW_CHUNK):
        conv_rows(r)


def _mixer_in_proj(x, norm_g, w_in, conv_w, conv_b, ln_g, ln_b, row_tile=ROW_TILE):
    b, s, d = x.shape
    vec = lambda a: a.reshape(1, -1)
    out_widths = (BRANCH_D,) + IN_SPLIT_WIDTHS[1:]
    return pl.pallas_call(
        _mixer_in_kernel,
        grid=(b, s // row_tile),
        in_specs=[
            pl.BlockSpec((1, row_tile, d), lambda bi, i: (bi, i, 0)),
            _resident((1, d)), _resident_layer(w_in),
            _resident((CONV_K, BRANCH_D)), _resident((1, BRANCH_D)), _resident((1, BRANCH_D)),
            _resident((1, BRANCH_D)),
        ],
        out_specs=[pl.BlockSpec((1, row_tile, wd), lambda bi, i: (bi, i, 0)) for wd in out_widths],
        out_shape=[jax.ShapeDtypeStruct((b, s, wd), BF16) for wd in out_widths],
        scratch_shapes=[
            pltpu.VMEM((row_tile, d), BF16),
            pltpu.VMEM((F32_SUBLANES, CONV_HALO + row_tile, BRANCH_D), F32),
            pltpu.VMEM((CONV_HALO, BRANCH_D), F32),
        ],
        compiler_params=_params("parallel", "arbitrary"),
        name="mixer_in_proj",
    )(x, vec(norm_g), w_in.stacked, conv_w, vec(conv_b), vec(ln_g), vec(ln_b))


def _rel_bucket(rel):
    half = N_REL_BUCKETS // 2
    max_exact = half // 2
    n = jnp.abs(rel)
    log_ratio = jnp.log(jnp.maximum(n, 1).astype(F32) / max_exact) / math.log(REL_MAX_DIST / max_exact)
    large = jnp.minimum(max_exact + (log_ratio * (half - max_exact)).astype(jnp.int32), half - 1)
    return jnp.where(rel > 0, half, 0) + jnp.where(n < max_exact, n, large)


FAR_BUCKET = N_REL_BUCKETS // 2 - 1
MASKED_BUCKET = -1


def _near_buckets():
    r = jnp.arange(ATTN_TILE)[:, None]
    c = jnp.arange(ATTN_TILE)[None, :]
    visible = (c // CHUNK) <= (r // CHUNK)
    diag = jnp.where(visible, _rel_bucket(c - r), MASKED_BUCKET)
    sub = _rel_bucket(c - r - ATTN_TILE)
    return jnp.stack([diag, sub]).astype(jnp.int32)


def _bias_tiles_kernel(table_ref, bucket_ref, o_ref):
    h = pl.program_id(0)
    bucket = bucket_ref[...]
    far = table_ref[FAR_BUCKET, h]
    out = jnp.zeros(bucket.shape, F32)
    for b in range(N_REL_BUCKETS):
        out = jnp.where(bucket == b, (table_ref[b, h] - far) * LOG2_E, out)
    o_ref[0] = jnp.where(bucket == MASKED_BUCKET, NEG_INF, out)


def _bias_tiles(rel_bias):
    buckets = _near_buckets()
    return pl.pallas_call(
        _bias_tiles_kernel,
        grid=(DIFF_HEADS,),
        in_specs=[pl.BlockSpec(memory_space=pltpu.SMEM), _resident(buckets.shape)],
        out_specs=pl.BlockSpec((1,) + buckets.shape, lambda h: (h, 0, 0, 0)),
        out_shape=jax.ShapeDtypeStruct((DIFF_HEADS,) + buckets.shape, F32),
        compiler_params=_params("arbitrary"),
        name="rel_bias_tiles",
    )(rel_bias.astype(F32), buckets)


def _diff_attn_kernel(*refs, lambda_init, cast_chunks):
    n_casts = len(cast_chunks)
    (q_ref, k_ref, v_ref, bias_ref, lq1_ref, lk1_ref, lq2_ref, lk2_ref, g_ref), refs = refs[:9], refs[9:]
    cast_src, o_ref, cast_dst = refs[:n_casts], refs[n_casts], refs[n_casts + 1:2 * n_casts + 1]
    q2_scr, s_scr, m_scr, acc_scr, vones_scr = refs[2 * n_casts + 1:]
    step = pl.program_id(0) * pl.num_programs(1) + pl.program_id(1)
    for src, dst, chunks in zip(cast_src, cast_dst, cast_chunks):
        @pl.when(step < chunks)
        def _():
            dst[...] = src[...].astype(BF16)

    _diff_attn_body(q_ref, k_ref, v_ref, bias_ref, lq1_ref, lk1_ref, lq2_ref, lk2_ref, g_ref, o_ref,
                    q2_scr, s_scr, m_scr, acc_scr, vones_scr, lambda_init=lambda_init)


def _diff_attn_body(q_ref, k_ref, v_ref, bias_ref, lq1_ref, lk1_ref, lq2_ref, lk2_ref, g_ref, o_ref,
                    q2_scr, s_scr, m_scr, acc_scr, vones_scr, *, lambda_init):
    t = ATTN_TILE
    i = pl.program_id(1)
    heads = range(DIFF_HEADS)

    def head_cols(h):
        return slice(h * DIFF_VD, (h + 1) * DIFF_VD)

    @pl.when(i == 0)
    def _():
        for h in heads:
            vones_scr[h, :, :DIFF_VD] = v_ref[0, :, head_cols(h)]
            vones_scr[h, :, DIFF_VD:] = jnp.ones((vones_scr.shape[1], LANES), BF16)

    q = q_ref[0]
    lane = lax.broadcasted_iota(jnp.int32, (t, DIFF_VD), 1)
    for h in heads:
        qh = q[:, head_cols(h)]
        q2_scr[h, :t, :] = jnp.where(lane < DIFF_DH, qh, 0).astype(BF16)
        q2_scr[h, t:, :] = jnp.where(lane >= DIFF_DH, qh, 0).astype(BF16)
    m_scr[...] = jnp.full(m_scr.shape, NEG_INF, F32)

    def logits_tiles(j0, near):
        for n, bias_index in enumerate(near):
            start = pl.multiple_of((j0 + n) * t, t)
            for h in heads:
                s = _dot_nt(q2_scr[h], k_ref[0, pl.ds(start, t), head_cols(h)])
                if bias_index is not None:
                    bias = bias_ref[h, bias_index]
                    s = s + jnp.concatenate([bias, bias], axis=0)
                s_scr[j0 + n, h] = s
                m_scr[h] = jnp.maximum(m_scr[h], jnp.maximum(s[:, :LANES], s[:, LANES:]))

    def far_quad(quad, carry):
        logits_tiles(4 * quad, (None,) * 4)
        return carry

    n_far = jnp.maximum(i - 1, 0)
    lax.fori_loop(0, n_far // 4, far_quad, 0)

    @pl.when(n_far % 4 >= 2)
    def _():
        logits_tiles(n_far - n_far % 4, (None, None))

    @pl.when(n_far % 2 == 1)
    def _():
        logits_tiles(n_far - 1, (None,))

    @pl.when(i > 0)
    def _():
        logits_tiles(i - 1, (1, 0))

    @pl.when(i == 0)
    def _():
        logits_tiles(0, (0,))

    for h in heads:
        m_scr[h] = jnp.broadcast_to(jnp.max(m_scr[h], axis=-1, keepdims=True), (2 * t, LANES))

    def pv_tiles(j0, n_tiles, first):
        start = pl.multiple_of(j0 * t, t)
        for h in heads:
            m = m_scr[h]
            parts = [jnp.exp2(s_scr[j0 + n, h, :, half * LANES:(half + 1) * LANES] - m)
                     for n in range(n_tiles) for half in range(t // LANES)]
            p = jnp.concatenate(parts, axis=1).astype(BF16)
            pv = _dot(p, vones_scr[h, pl.ds(start, n_tiles * t), :])
            if first:
                acc_scr[h] = pv
            else:
                acc_scr[h] += pv

    n_visible = i + 1
    n_first = 2 - n_visible % 2

    @pl.when(n_first == 1)
    def _():
        pv_tiles(0, 1, True)

    @pl.when(n_first == 2)
    def _():
        pv_tiles(0, 2, True)

    def pv_quad(quad, carry):
        pv_tiles(n_first + 4 * quad, 4, False)
        return carry

    n_rest = n_visible - n_first
    lax.fori_loop(0, n_rest // 4, pv_quad, 0)

    @pl.when(n_rest % 4 == 2)
    def _():
        pv_tiles(n_visible - 2, 2, False)

    lam = (jnp.exp(jnp.sum(lq1_ref[...] * lk1_ref[...])) - jnp.exp(jnp.sum(lq2_ref[...] * lk2_ref[...]))
           + lambda_init)
    for h in heads:
        o = (acc_scr[h, :t, :DIFF_VD] / acc_scr[h, :t, DIFF_VD:]
             - lam * (acc_scr[h, t:, :DIFF_VD] / acc_scr[h, t:, DIFF_VD:]))
        o_ref[0, :, head_cols(h)] = (_rms_norm(o, g_ref[...]) * (1.0 - lambda_init)).astype(BF16)


def _diff_attention(pq, pk, pv, bias_tiles, lam_q1, lam_k1, lam_q2, lam_k2, subln_g, lambda_init, side_casts):
    b, s, width = pq.shape
    t = ATTN_TILE
    tiles = s // t
    steps = b * tiles
    vec = lambda a: a.reshape(1, -1).astype(F32)
    all_keys = pl.BlockSpec((1, s, width), lambda bi, i: (bi, 0, 0))
    lam_spec = _resident((1, DIFF_DH))

    cast_views, cast_in_specs, cast_out_specs, cast_out_shapes, cast_chunks = [], [], [], [], []
    for w, layer in side_casts:
        view = w.reshape(w.shape[0], -1, w.shape[-1])
        _, rows, cols = view.shape
        chunks = max(c for c in range(1, steps + 1) if rows % (c * BF16_SUBLANES) == 0)

        def chunk(bi, i, chunks=chunks):
            return jnp.minimum(bi * tiles + i, chunks - 1)

        cast_views.append(view)
        cast_chunks.append(chunks)
        cast_in_specs.append(pl.BlockSpec((None, rows // chunks, cols),
                                          lambda bi, i, layer=layer, chunk=chunk: (layer, chunk(bi, i), 0)))
        cast_out_specs.append(pl.BlockSpec((rows // chunks, cols), lambda bi, i, chunk=chunk: (chunk(bi, i), 0)))
        cast_out_shapes.append(jax.ShapeDtypeStruct((rows, cols), BF16))

    out, *casts = pl.pallas_call(
        functools.partial(_diff_attn_kernel, lambda_init=lambda_init, cast_chunks=tuple(cast_chunks)),
        grid=(b, tiles),
        in_specs=[
            pl.BlockSpec((1, t, width), lambda bi, i: (bi, i, 0)),
            all_keys, all_keys, _resident(bias_tiles.shape),
            lam_spec, lam_spec, lam_spec, lam_spec, _resident((1, DIFF_VD)),
        ] + cast_in_specs,
        out_specs=[pl.BlockSpec((1, t, width), lambda bi, i: (bi, i, 0))] + cast_out_specs,
        out_shape=[jax.ShapeDtypeStruct((b, s, width), BF16)] + cast_out_shapes,
        scratch_shapes=[
            pltpu.VMEM((DIFF_HEADS, 2 * t, DIFF_VD), BF16),
            pltpu.VMEM((s // t, DIFF_HEADS, 2 * t, t), F32),
            pltpu.VMEM((DIFF_HEADS, 2 * t, LANES), F32),
            pltpu.VMEM((DIFF_HEADS, 2 * t, DIFF_VD + LANES), F32),
            pltpu.VMEM((DIFF_HEADS, s, DIFF_VD + LANES), BF16),
        ],
        compiler_params=_params("parallel", "arbitrary"),
        name="diff_attention",
    )(pq, pk, pv, bias_tiles, vec(lam_q1), vec(lam_k1), vec(lam_q2), vec(lam_k2), vec(subln_g), *cast_views)
    casts = [_LayerWeight(c.reshape((1,) + w.shape[1:]), 0) for c, (w, _) in zip(casts, side_casts)]
    return out, casts


def _gmlp_kernel(pc_ref, lng_ref, lnb_ref, ws_ref, bs_ref, o_ref):
    rows = o_ref.shape[0]
    z = jax.nn.gelu(pc_ref[...].astype(F32))
    u = z[:, :BRANCH_D]
    v = _layer_norm(z[:, BRANCH_D:], lng_ref[...], lnb_ref[...]).astype(BF16)
    t_idx = lax.broadcasted_iota(jnp.int32, (GMLP_CHUNK, GMLP_CHUNK), 0)
    s_idx = lax.broadcasted_iota(jnp.int32, (GMLP_CHUNK, GMLP_CHUNK), 1)
    for g in range(GMLP_GROUPS):
        w = jnp.where(s_idx <= t_idx, ws_ref[g], 0.0).astype(BF16)
        cols = slice(g * GMLP_GD, (g + 1) * GMLP_GD)
        for r in range(0, rows, GMLP_CHUNK):
            sv = _dot(w, v[r:r + GMLP_CHUNK, cols]) + bs_ref[g]
            o_ref[r:r + GMLP_CHUNK, cols] = (u[r:r + GMLP_CHUNK, cols] * sv).astype(BF16)


def _gmlp_branch(pc, ln_g, ln_b, w_s, b_s, row_tile=WIDE_ROW_TILE):
    rows = pc.shape[0]
    vec = lambda a: a.reshape(1, BRANCH_D)
    return pl.pallas_call(
        _gmlp_kernel,
        grid=(rows // row_tile,),
        in_specs=[
            pl.BlockSpec((row_tile, COL_C), lambda i: (i, 0)),
            _resident((1, BRANCH_D)), _resident((1, BRANCH_D)),
            _resident(w_s.shape), _resident((GMLP_GROUPS, GMLP_CHUNK, 1)),
        ],
        out_specs=pl.BlockSpec((row_tile, BRANCH_D), lambda i: (i, 0)),
        out_shape=jax.ShapeDtypeStruct((rows, BRANCH_D), BF16),
        compiler_params=_params("parallel"),
        name="gmlp_branch",
    )(pc, vec(ln_g), vec(ln_b), w_s, b_s.reshape(GMLP_GROUPS, GMLP_CHUNK, 1))


def _merge_kernel(x_ref, ya_ref, yb_ref, yc_ref, g_ref, win_ref, gb_ref, wbr_ref, wout_ref, o_ref):
    x = x_ref[...]
    xn = _rms_norm(x, g_ref[...]).astype(BF16)
    merged = None
    for n, y_ref in enumerate((ya_ref, yb_ref, yc_ref)):
        col = COL_BRANCHES + n * D_MODEL
        logits = _dot(xn, win_ref[:, col:col + D_MODEL]) + gb_ref[n:n + 1, :]
        term = jax.nn.sigmoid(logits) * _dot(y_ref[...], wbr_ref[n])
        merged = term if merged is None else merged + term
    o_ref[...] = x + _dot(merged.astype(BF16), wout_ref[...])


def _merge(x, ya, yb, yc, norm_g, w_in, gate_b, w_br, w_out, row_tile=WIDE_ROW_TILE):
    rows, d = x.shape
    row_spec = lambda width: pl.BlockSpec((row_tile, width), lambda i: (i, 0))
    return pl.pallas_call(
        _merge_kernel,
        grid=(rows // row_tile,),
        in_specs=[row_spec(D_MODEL), row_spec(BRANCH_D), row_spec(BRANCH_D), row_spec(BRANCH_D),
                  _resident((1, d)), _resident_layer(w_in), _resident(gate_b.shape), _resident_layer(w_br),
                  _resident_layer(w_out)],
        out_specs=row_spec(D_MODEL),
        out_shape=jax.ShapeDtypeStruct(x.shape, F32),
        compiler_params=_params("parallel"),
        name="branch_merge",
    )(x, ya, yb, yc, norm_g.reshape(1, d), w_in.stacked, gate_b, w_br.stacked, w_out.stacked)


def _xattn_kernel(x_ref, g_ref, wq_ref, kv_ref, wo_ref, o_ref):
    x = x_ref[0]
    q = _dot(_rms_norm(x, g_ref[...]).astype(BF16), wq_ref[...]) * (XATTN_DH ** -0.5 * LOG2_E)
    q = q.astype(BF16)
    heads = []
    for h in range(XATTN_HEADS):
        cols = slice(h * XATTN_DH, (h + 1) * XATTN_DH)
        k = kv_ref[0, :, cols]
        v = kv_ref[0, :, XATTN_D + h * XATTN_DH:XATTN_D + (h + 1) * XATTN_DH]
        s = _dot_nt(q[:, cols], k)
        p = jnp.exp2(s - jnp.max(s, axis=-1, keepdims=True))
        o = _dot(p.astype(BF16), v) / jnp.sum(p, axis=-1, keepdims=True)
        heads.append(o.astype(BF16))
    o_ref[0] = x + _dot(jnp.concatenate(heads, axis=-1), wo_ref[...])


def _mem_cross_attention(x, kv, norm_g, w_xq, w_xo, row_tile=WIDE_ROW_TILE):
    b, s, d = x.shape
    m = kv.shape[1]
    return pl.pallas_call(
        _xattn_kernel,
        grid=(b, s // row_tile),
        in_specs=[
            pl.BlockSpec((1, row_tile, d), lambda bi, i: (bi, i, 0)),
            _resident((1, d)), _resident_layer(w_xq),
            pl.BlockSpec((1, m, 2 * XATTN_D), lambda bi, i: (bi, 0, 0)),
            _resident_layer(w_xo),
        ],
        out_specs=pl.BlockSpec((1, row_tile, d), lambda bi, i: (bi, i, 0)),
        out_shape=jax.ShapeDtypeStruct(x.shape, F32),
        compiler_params=_params("parallel", "parallel"),
        name="mem_cross_attention",
    )(x, norm_g.reshape(1, d), w_xq.stacked, kv, w_xo.stacked)


def _ffn_kernel(x_ref, halo_ref, g_ref, wup_ref, cw_ref, cb_ref, wdn_ref, gf_ref, o_ref,
                slab_scr, h_scr, up_scr, act_scr, acc_scr, *, final_norm):
    tile_rows = o_ref.shape[1]
    rows = acc_scr.shape[0]
    groups = rows // F32_SUBLANES
    slabs = slab_scr.shape[0]

    def lanes(j):
        return slice(j * LANES, (j + 1) * LANES)

    for j in range(slabs):
        slab_scr[j, :tile_rows, :] = x_ref[0, :, lanes(j)]
    slab_scr[:, tile_rows:, :] = jnp.zeros((slabs, rows - tile_rows, LANES), F32)
    x = jnp.concatenate(
        [jnp.concatenate([slab_scr[j, pl.ds(gi, F32_SUBLANES, stride=groups), :] for j in range(slabs)], axis=1)
         for gi in range(groups)], axis=0)

    g = g_ref[...]
    halo = _rms_norm(halo_ref[0], g)
    h_scr[:FFN_HALO, :] = jnp.where(pl.program_id(1) > 0, halo, 0.0).astype(BF16)
    h_scr[FFN_HALO:, :] = _rms_norm(x, g).astype(BF16)
    acc_scr[...] = x

    n_chunks = D_FF // FFN_COL_CHUNK
    first_sublane = lax.broadcasted_iota(jnp.int32, (F32_SUBLANES, FFN_COL_CHUNK), 0) == 0

    def cols(c, half):
        start = half * D_FF + c * FFN_COL_CHUNK
        return slice(start, start + FFN_COL_CHUNK)

    def up_proj(c, slot):
        h = h_scr[...]
        for half in range(2):
            up = up_scr.at[slot, half]
            up[...] = _dot(h, wup_ref[:, cols(c, half)])
            prev_tile = up[FFN_HALO - F32_SUBLANES:FFN_HALO, :]
            for shift in range(1, FFN_K):
                wrapped = up[FFN_HALO + rows - shift * F32_SUBLANES:FFN_HALO + rows - (shift - 1) * F32_SUBLANES, :]
                block = jnp.where(first_sublane, pltpu.roll(prev_tile, shift, 0), pltpu.roll(wrapped, 1, 0))
                up[FFN_HALO - shift * F32_SUBLANES:FFN_HALO - (shift - 1) * F32_SUBLANES, :] = block

    def conv(c, slot, half):
        out = cb_ref[:, cols(c, half)]
        for k in range(FFN_K):
            window = up_scr[slot, half, k * F32_SUBLANES:k * F32_SUBLANES + rows, :]
            out = out + cw_ref[k:k + 1, cols(c, half)] * window
        return out

    def gate(c, slot):
        act_scr[slot] = (jax.nn.silu(conv(c, slot, 0)) * conv(c, slot, 1)).astype(BF16)

    def down_proj(c, slot):
        acc_scr[...] += _dot(act_scr[slot], wdn_ref[cols(c, 0), :])

    def stage(c, slot):
        up_proj(c + 1, 1 - slot)
        down_proj(c - 1, 1 - slot)
        gate(c, slot)

    up_proj(0, 0)
    up_proj(1, 1)
    gate(0, 0)
    for c in range(1, n_chunks - 1):
        stage(c, c % 2)
    down_proj(n_chunks - 2, (n_chunks - 2) % 2)
    gate(n_chunks - 1, (n_chunks - 1) % 2)
    down_proj(n_chunks - 1, (n_chunks - 1) % 2)

    for gi in range(groups):
        out = acc_scr[gi * F32_SUBLANES:(gi + 1) * F32_SUBLANES, :]
        if final_norm:
            out = _rms_norm(out, gf_ref[...])
        for j in range(slabs):
            slab_scr[j, pl.ds(gi, F32_SUBLANES, stride=groups), :] = out[:, lanes(j)]
    for j in range(slabs):
        o_ref[0, :, lanes(j)] = slab_scr[j, :tile_rows, :]


def _conv_ffn(x, norm_g, w_up, conv_w, conv_b, w_down, final_g, final_norm, row_tile=FFN_ROW_TILE):
    b, s, d = x.shape
    halo_blocks = row_tile // FFN_HALO
    conv_b = conv_b.reshape(1, 2 * D_FF)
    perm_rows = row_tile + 2 * F32_SUBLANES
    return pl.pallas_call(
        functools.partial(_ffn_kernel, final_norm=final_norm),
        grid=(b, s // row_tile),
        in_specs=[
            pl.BlockSpec((1, row_tile, d), lambda bi, i: (bi, i, 0)),
            pl.BlockSpec((1, FFN_HALO, d), lambda bi, i: (bi, jnp.maximum(i * halo_blocks - 1, 0), 0)),
            _resident((1, d)), _resident_layer(w_up), _resident(conv_w.shape), _resident(conv_b.shape),
            _resident_layer(w_down), _resident((1, d)),
        ],
        out_specs=pl.BlockSpec((1, row_tile, d), lambda bi, i: (bi, i, 0)),
        out_shape=jax.ShapeDtypeStruct(x.shape, F32),
        scratch_shapes=[
            pltpu.VMEM((d // LANES, perm_rows, LANES), F32),
            pltpu.VMEM((FFN_HALO + perm_rows, d), BF16),
            pltpu.VMEM((2, 2, FFN_HALO + perm_rows, FFN_COL_CHUNK), F32),
            pltpu.VMEM((2, perm_rows, FFN_COL_CHUNK), BF16),
            pltpu.VMEM((perm_rows, d), F32),
        ],
        compiler_params=_params("parallel", "parallel"),
        name="conv_ffn",
    )(x, x, norm_g.reshape(1, d), w_up.stacked, conv_w, conv_b, w_down.stacked, final_g.reshape(1, d))


def kernel(x, mem, rel_bias, norm_mix_g, w_in, gate_b, conv_w, conv_b, conv_ln_g, conv_ln_b, lam_q1, lam_k1, lam_q2, lam_k2, subln_g, gmlp_ln_g, gmlp_ln_b, w_s, b_s, w_br, w_out, norm_xattn_g, norm_mem_g, w_xq, w_xkv, w_xo, norm_ffn_g, w_up, ffn_conv_w, ffn_conv_b, w_down, norm_final_g):
    b, s, d = x.shape
    m = mem.shape[1]
    rows = b * s
    bias_tiles = _bias_tiles(rel_bias)
    w_in_l = _to_bf16(w_in, 0)
    for l in range(DEPTH):
        lambda_init = 0.8 - 0.6 * math.exp(-0.3 * l)
        ya, pq, pk, pv, pc = _mixer_in_proj(x, norm_mix_g[l], w_in_l, conv_w[l], conv_b[l], conv_ln_g[l],
                                            conv_ln_b[l])
        side_casts = [(w, l) for w in (w_br, w_out, w_xq, w_xkv, w_xo, w_up, w_down)]
        if l + 1 < DEPTH:
            side_casts.append((w_in, l + 1))
        yb, casts = _diff_attention(pq, pk, pv, bias_tiles, lam_q1[l], lam_k1[l], lam_q2[l], lam_k2[l],
                                    subln_g[l], lambda_init, side_casts)
        w_br_l, w_out_l, w_xq_l, w_xkv_l, w_xo_l, w_up_l, w_down_l = casts[:7]
        yc = _gmlp_branch(pc.reshape(rows, COL_C), gmlp_ln_g[l], gmlp_ln_b[l], w_s[l], b_s[l])
        x = _merge(x.reshape(rows, d), ya.reshape(rows, BRANCH_D), yb.reshape(rows, BRANCH_D), yc, norm_mix_g[l],
                   w_in_l, gate_b[l], w_br_l, w_out_l).reshape(b, s, d)
        (kv,) = _norm_proj(mem.reshape(b * m, d), norm_mem_g[l], w_xkv_l, (2 * XATTN_D,))
        x = _mem_cross_attention(x, kv.reshape(b, m, 2 * XATTN_D), norm_xattn_g[l], w_xq_l, w_xo_l)
        x = _conv_ffn(x, norm_ffn_g[l], w_up_l, ffn_conv_w[l], ffn_conv_b[l], w_down_l, norm_final_g,
                      final_norm=(l == DEPTH - 1))
        if l + 1 < DEPTH:
            w_in_l = casts[7]
    return x
```

```python
import functools
import math
from typing import NamedTuple

import jax
import jax.numpy as jnp
from jax import lax
from jax.experimental import pallas as pl
from jax.experimental.pallas import tpu as pltpu

F32 = jnp.float32
BF16 = jnp.bfloat16

D_MODEL = 1024
DEPTH = 2
CHUNK = 64
BRANCH_D = D_MODEL // 2
N_BRANCH = 3
CONV_K = 31
DIFF_HEADS = 4
DIFF_DH = D_MODEL // 16
DIFF_VD = 2 * DIFF_DH
GMLP_GROUPS = 4
GMLP_GD = BRANCH_D // GMLP_GROUPS
GMLP_CHUNK = 128
N_REL_BUCKETS = 32
REL_MAX_DIST = 128
XATTN_HEADS = 4
XATTN_DH = D_MODEL // 8
XATTN_D = XATTN_HEADS * XATTN_DH
D_FF = 2816
FFN_K = 3
EPS = 1e-6
NEG_INF = -1e30

COL_A = 2 * BRANCH_D
COL_QK = DIFF_HEADS * 2 * DIFF_DH
COL_V = DIFF_HEADS * DIFF_VD
COL_C = 2 * BRANCH_D
COL_G = N_BRANCH * D_MODEL
IN_SPLIT_WIDTHS = (COL_A, COL_QK, COL_QK, COL_V, COL_C)
COL_BRANCHES = sum(IN_SPLIT_WIDTHS)
LOG2_E = math.log2(math.e)
IN_SPLIT_SCALES = (1.0, DIFF_DH ** -0.5 * LOG2_E, 1.0, 1.0, 1.0)

LANES = 128
MXU_WIDTH = 256
F32_SUBLANES = 8
BF16_SUBLANES = 16
VMEM_LIMIT_BYTES = 56 * 1024 * 1024
CAST_BLOCK_BYTES = 4 * 1024 * 1024

ROW_TILE = 512
WIDE_ROW_TILE = 1024
FFN_ROW_TILE = 1024
PROJ_COL_CHUNK = 512
ATTN_TILE = 256
CONV_HALO = 32
CONV_ROW_CHUNK = 64
FFN_HALO = BF16_SUBLANES
FFN_COL_CHUNK = MXU_WIDTH

assert CONV_HALO >= CONV_K - 1 and FFN_HALO == F32_SUBLANES * (FFN_K - 1)
assert D_FF % FFN_COL_CHUNK == 0
assert ATTN_TILE % CHUNK == 0 and ATTN_TILE > REL_MAX_DIST


def _params(*semantics):
    return pltpu.CompilerParams(dimension_semantics=semantics, vmem_limit_bytes=VMEM_LIMIT_BYTES)


def _resident(shape):
    zeros = (0,) * len(shape)
    return pl.BlockSpec(shape, lambda *_: zeros, pipeline_mode=pl.Buffered(1))


class _LayerWeight(NamedTuple):
    stacked: jax.Array
    layer: int

    @property
    def shape(self):
        return self.stacked.shape[1:]


def _resident_layer(w):
    index = (w.layer,) + (0,) * len(w.shape)
    return pl.BlockSpec((None,) + tuple(w.shape), lambda *_: index, pipeline_mode=pl.Buffered(1))


def _cast_kernel(w_ref, o_ref):
    o_ref[...] = w_ref[...].astype(BF16)


def _to_bf16(w, layer):
    _, rows, cols = w.shape
    block_rows = max(r for r in range(BF16_SUBLANES, rows + 1, BF16_SUBLANES)
                     if rows % r == 0 and r * cols * 4 <= CAST_BLOCK_BYTES)
    out = pl.pallas_call(
        _cast_kernel,
        grid=(rows // block_rows,),
        in_specs=[pl.BlockSpec((None, block_rows, cols), lambda i: (layer, i, 0))],
        out_specs=pl.BlockSpec((None, block_rows, cols), lambda i: (0, i, 0)),
        out_shape=jax.ShapeDtypeStruct((1, rows, cols), BF16),
        compiler_params=_params("parallel"),
        name="weight_cast",
    )(w)
    return _LayerWeight(out, 0)


def _rms_norm(x, g):
    return x * lax.rsqrt(jnp.mean(x * x, axis=-1, keepdims=True) + EPS) * g


def _layer_norm(x, g, b):
    mu = jnp.mean(x, axis=-1, keepdims=True)
    xc = x - mu
    var = jnp.mean(xc * xc, axis=-1, keepdims=True)
    return xc * lax.rsqrt(var + EPS) * g + b


def _dot(a, b):
    return jnp.dot(a, b, preferred_element_type=F32)


def _dot_nt(a, b):
    return lax.dot_general(a, b, (((1,), (1,)), ((), ())), preferred_element_type=F32)


def _norm_proj_kernel(x_ref, g_ref, w_ref, *out_refs, widths, scales):
    xn = _rms_norm(x_ref[...], g_ref[...]).astype(BF16)
    col = 0
    for o_ref, width, scale in zip(out_refs, widths, scales):
        for c in range(0, width, PROJ_COL_CHUNK):
            out = _dot(xn, w_ref[:, col + c:col + c + PROJ_COL_CHUNK])
            if scale != 1.0:
                out = out * scale
            o_ref[:, c:c + PROJ_COL_CHUNK] = out.astype(BF16)
        col += width


def _norm_proj(x, g, w, widths, scales=None, row_tile=ROW_TILE):
    rows, d = x.shape
    scales = scales or (1.0,) * len(widths)
    assert w.shape == (d, sum(widths)) and all(wd % PROJ_COL_CHUNK == 0 for wd in widths)
    return pl.pallas_call(
        functools.partial(_norm_proj_kernel, widths=widths, scales=scales),
        grid=(rows // row_tile,),
        in_specs=[pl.BlockSpec((row_tile, d), lambda i: (i, 0)), _resident((1, d)), _resident_layer(w)],
        out_specs=[pl.BlockSpec((row_tile, wd), lambda i: (i, 0)) for wd in widths],
        out_shape=[jax.ShapeDtypeStruct((rows, wd), BF16) for wd in widths],
        compiler_params=_params("parallel"),
        name="norm_proj",
    )(x, g.reshape(1, d), w.stacked)


def _mixer_in_kernel(x_ref, g_ref, w_ref, cw_ref, cb_ref, lng_ref, lnb_ref,
                     ya_ref, pq_ref, pk_ref, pv_ref, pc_ref, xn_scr, xs_scr, halo_scr):
    rows = ya_ref.shape[1]
    xn_scr[...] = _rms_norm(x_ref[0], g_ref[...]).astype(BF16)

    def proj(col, width):
        return _dot(xn_scr[...], w_ref[:, col:col + width])

    glu = proj(0, BRANCH_D) * jax.nn.sigmoid(proj(BRANCH_D, BRANCH_D))

    @pl.when(pl.program_id(1) == 0)
    def _():
        halo_scr[...] = jnp.zeros(halo_scr.shape, F32)

    xs_scr[0, :CONV_HALO, :] = halo_scr[...]
    xs_scr[0, CONV_HALO:, :] = glu
    halo_scr[...] = glu[rows - CONV_HALO:, :]

    def shifted_copies():
        total = CONV_HALO + rows
        base = xs_scr[0]
        for b in range(1, F32_SUBLANES):
            xs_scr[b] = pltpu.roll(base, total - b, 0)

    first_tap = CONV_HALO - (CONV_K - 1)

    def conv_rows(r):
        acc = jnp.broadcast_to(cb_ref[...], (CONV_ROW_CHUNK, BRANCH_D))
        for k in range(CONV_K):
            b = (first_tap + k) % F32_SUBLANES
            start = r + first_tap + k - b
            acc = acc + cw_ref[k:k + 1, :] * xs_scr[b, start:start + CONV_ROW_CHUNK, :]
        ya_ref[0, r:r + CONV_ROW_CHUNK, :] = jax.nn.silu(_layer_norm(acc, lng_ref[...], lnb_ref[...])).astype(BF16)

    def proj_chunk(o_ref, col, c, scale):
        out = proj(col + c, PROJ_COL_CHUNK)
        if scale != 1.0:
            out = out * scale
        o_ref[0, :, c:c + PROJ_COL_CHUNK] = out.astype(BF16)

    col = COL_A
    for o_ref, width, scale in zip((pq_ref, pk_ref, pv_ref, pc_ref), IN_SPLIT_WIDTHS[1:], IN_SPLIT_SCALES[1:]):
        for c in range(0, width, PROJ_COL_CHUNK):
            proj_chunk(o_ref, col, c, scale)
        col += width
    shifted_copies()
    for r in range(0, rows, CONV_ROW_CHUNK):
        conv_rows(r)


def _mixer_in_proj(x, norm_g, w_in, conv_w, conv_b, ln_g, ln_b, row_tile=ROW_TILE):
    b, s, d = x.shape
    vec = lambda a: a.reshape(1, -1)
    out_widths = (BRANCH_D,) + IN_SPLIT_WIDTHS[1:]
    return pl.pallas_call(
        _mixer_in_kernel,
        grid=(b, s // row_tile),
        in_specs=[
            pl.BlockSpec((1, row_tile, d), lambda bi, i: (bi, i, 0)),
            _resident((1, d)), _resident_layer(w_in),
            _resident((CONV_K, BRANCH_D)), _resident((1, BRANCH_D)), _resident((1, BRANCH_D)),
            _resident((1, BRANCH_D)),
        ],
        out_specs=[pl.BlockSpec((1, row_tile, wd), lambda bi, i: (bi, i, 0)) for wd in out_widths],
        out_shape=[jax.ShapeDtypeStruct((b, s, wd), BF16) for wd in out_widths],
        scratch_shapes=[
            pltpu.VMEM((row_tile, d), BF16),
            pltpu.VMEM((F32_SUBLANES, CONV_HALO + row_tile, BRANCH_D), F32),
            pltpu.VMEM((CONV_HALO, BRANCH_D), F32),
        ],
        compiler_params=_params("parallel", "arbitrary"),
        name="mixer_in_proj",
    )(x, vec(norm_g), w_in.stacked, conv_w, vec(conv_b), vec(ln_g), vec(ln_b))


def _rel_bucket(rel):
    half = N_REL_BUCKETS // 2
    max_exact = half // 2
    n = jnp.abs(rel)
    log_ratio = jnp.log(jnp.maximum(n, 1).astype(F32) / max_exact) / math.log(REL_MAX_DIST / max_exact)
    large = jnp.minimum(max_exact + (log_ratio * (half - max_exact)).astype(jnp.int32), half - 1)
    return jnp.where(rel > 0, half, 0) + jnp.where(n < max_exact, n, large)


FAR_BUCKET = N_REL_BUCKETS // 2 - 1
MASKED_BUCKET = -1


def _near_buckets():
    r = jnp.arange(ATTN_TILE)[:, None]
    c = jnp.arange(ATTN_TILE)[None, :]
    visible = (c // CHUNK) <= (r // CHUNK)
    diag = jnp.where(visible, _rel_bucket(c - r), MASKED_BUCKET)
    sub = _rel_bucket(c - r - ATTN_TILE)
    return jnp.stack([diag, sub]).astype(jnp.int32)


def _bias_tiles_kernel(table_ref, bucket_ref, o_ref):
    h = pl.program_id(0)
    bucket = bucket_ref[...]
    far = table_ref[FAR_BUCKET, h]
    out = jnp.zeros(bucket.shape, F32)
    for b in range(N_REL_BUCKETS):
        out = jnp.where(bucket == b, (table_ref[b, h] - far) * LOG2_E, out)
    o_ref[0] = jnp.where(bucket == MASKED_BUCKET, NEG_INF, out)


def _bias_tiles(rel_bias):
    buckets = _near_buckets()
    return pl.pallas_call(
        _bias_tiles_kernel,
        grid=(DIFF_HEADS,),
        in_specs=[pl.BlockSpec(memory_space=pltpu.SMEM), _resident(buckets.shape)],
        out_specs=pl.BlockSpec((1,) + buckets.shape, lambda h: (h, 0, 0, 0)),
        out_shape=jax.ShapeDtypeStruct((DIFF_HEADS,) + buckets.shape, F32),
        compiler_params=_params("arbitrary"),
        name="rel_bias_tiles",
    )(rel_bias.astype(F32), buckets)


def _diff_attn_kernel(*refs, lambda_init, cast_chunks):
    n_casts = len(cast_chunks)
    (q_ref, k_ref, v_ref, bias_ref, lq1_ref, lk1_ref, lq2_ref, lk2_ref, g_ref), refs = refs[:9], refs[9:]
    cast_src, o_ref, cast_dst = refs[:n_casts], refs[n_casts], refs[n_casts + 1:2 * n_casts + 1]
    q2_scr, s_scr, m_scr, acc_scr, vones_scr = refs[2 * n_casts + 1:]
    step = pl.program_id(0) * pl.num_programs(1) + pl.program_id(1)
    for src, dst, chunks in zip(cast_src, cast_dst, cast_chunks):
        @pl.when(step < chunks)
        def _():
            dst[...] = src[...].astype(BF16)

    _diff_attn_body(q_ref, k_ref, v_ref, bias_ref, lq1_ref, lk1_ref, lq2_ref, lk2_ref, g_ref, o_ref,
                    q2_scr, s_scr, m_scr, acc_scr, vones_scr, lambda_init=lambda_init)


def _diff_attn_body(q_ref, k_ref, v_ref, bias_ref, lq1_ref, lk1_ref, lq2_ref, lk2_ref, g_ref, o_ref,
                    q2_scr, s_scr, m_scr, acc_scr, vones_scr, *, lambda_init):
    t = ATTN_TILE
    i = pl.program_id(1)
    heads = range(DIFF_HEADS)

    def head_cols(h):
        return slice(h * DIFF_VD, (h + 1) * DIFF_VD)

    @pl.when(i == 0)
    def _():
        for h in heads:
            vones_scr[h, :, :DIFF_VD] = v_ref[0, :, head_cols(h)]
            vones_scr[h, :, DIFF_VD:] = jnp.ones((vones_scr.shape[1], LANES), BF16)

    q = q_ref[0]
    lane = lax.broadcasted_iota(jnp.int32, (t, DIFF_VD), 1)
    for h in heads:
        qh = q[:, head_cols(h)]
        q2_scr[h, :t, :] = jnp.where(lane < DIFF_DH, qh, 0).astype(BF16)
        q2_scr[h, t:, :] = jnp.where(lane >= DIFF_DH, qh, 0).astype(BF16)
    def logits_tiles(near):
        for n, bias_index in enumerate(near):
            for h in heads:
                s = _dot_nt(q2_scr[h], k_ref[0, n * t:(n + 1) * t, head_cols(h)])
                if bias_index is not None:
                    bias = bias_ref[h, bias_index]
                    s = s + jnp.concatenate([bias, bias], axis=0)
                s_scr[n, h] = s
                lane_max = jnp.maximum(s[:, :LANES], s[:, LANES:])
                m_scr[h] = lane_max if n == 0 else jnp.maximum(m_scr[h], lane_max)

    def pv_tiles(n_tiles):
        for h in heads:
            m = m_scr[h]
            parts = [jnp.exp2(s_scr[n, h, :, half * LANES:(half + 1) * LANES] - m)
                     for n in range(n_tiles) for half in range(t // LANES)]
            p = jnp.concatenate(parts, axis=1).astype(BF16)
            acc_scr[h] = _dot(p, vones_scr[h, :n_tiles * t, :])

    def for_query_tile(fn):
        for tile in range(s_scr.shape[0]):
            pl.when(i == tile)(functools.partial(fn, tile))

    for_query_tile(lambda tile: logits_tiles((None,) * (tile - 1) + ((1, 0) if tile > 0 else (0,))))

    for h in heads:
        m_scr[h] = jnp.broadcast_to(jnp.max(m_scr[h], axis=-1, keepdims=True), (2 * t, LANES))

    for_query_tile(lambda tile: pv_tiles(tile + 1))

    lam = (jnp.exp(jnp.sum(lq1_ref[...] * lk1_ref[...])) - jnp.exp(jnp.sum(lq2_ref[...] * lk2_ref[...]))
           + lambda_init)
    for h in heads:
        o = (acc_scr[h, :t, :DIFF_VD] / acc_scr[h, :t, DIFF_VD:]
             - lam * (acc_scr[h, t:, :DIFF_VD] / acc_scr[h, t:, DIFF_VD:]))
        o_ref[0, :, head_cols(h)] = (_rms_norm(o, g_ref[...]) * (1.0 - lambda_init)).astype(BF16)


def _diff_attention(pq, pk, pv, bias_tiles, lam_q1, lam_k1, lam_q2, lam_k2, subln_g, lambda_init, side_casts):
    b, s, width = pq.shape
    t = ATTN_TILE
    tiles = s // t
    steps = b * tiles
    vec = lambda a: a.reshape(1, -1).astype(F32)
    all_keys = pl.BlockSpec((1, s, width), lambda bi, i: (bi, 0, 0))
    lam_spec = _resident((1, DIFF_DH))

    cast_views, cast_in_specs, cast_out_specs, cast_out_shapes, cast_chunks = [], [], [], [], []
    for w, layer in side_casts:
        view = w.reshape(w.shape[0], -1, w.shape[-1])
        _, rows, cols = view.shape
        chunks = max(c for c in range(1, steps + 1) if rows % (c * BF16_SUBLANES) == 0)

        def chunk(bi, i, chunks=chunks):
            return jnp.minimum(bi * tiles + i, chunks - 1)

        cast_views.append(view)
        cast_chunks.append(chunks)
        cast_in_specs.append(pl.BlockSpec((None, rows // chunks, cols),
                                          lambda bi, i, layer=layer, chunk=chunk: (layer, chunk(bi, i), 0)))
        cast_out_specs.append(pl.BlockSpec((rows // chunks, cols), lambda bi, i, chunk=chunk: (chunk(bi, i), 0)))
        cast_out_shapes.append(jax.ShapeDtypeStruct((rows, cols), BF16))

    out, *casts = pl.pallas_call(
        functools.partial(_diff_attn_kernel, lambda_init=lambda_init, cast_chunks=tuple(cast_chunks)),
        grid=(b, tiles),
        in_specs=[
            pl.BlockSpec((1, t, width), lambda bi, i: (bi, i, 0)),
            all_keys, all_keys, _resident(bias_tiles.shape),
            lam_spec, lam_spec, lam_spec, lam_spec, _resident((1, DIFF_VD)),
        ] + cast_in_specs,
        out_specs=[pl.BlockSpec((1, t, width), lambda bi, i: (bi, i, 0))] + cast_out_specs,
        out_shape=[jax.ShapeDtypeStruct((b, s, width), BF16)] + cast_out_shapes,
        scratch_shapes=[
            pltpu.VMEM((DIFF_HEADS, 2 * t, DIFF_VD), BF16),
            pltpu.VMEM((s // t, DIFF_HEADS, 2 * t, t), F32),
            pltpu.VMEM((DIFF_HEADS, 2 * t, LANES), F32),
            pltpu.VMEM((DIFF_HEADS, 2 * t, DIFF_VD + LANES), F32),
            pltpu.VMEM((DIFF_HEADS, s, DIFF_VD + LANES), BF16),
        ],
        compiler_params=_params("parallel", "arbitrary"),
        name="diff_attention",
    )(pq, pk, pv, bias_tiles, vec(lam_q1), vec(lam_k1), vec(lam_q2), vec(lam_k2), vec(subln_g), *cast_views)
    casts = [_LayerWeight(c.reshape((1,) + w.shape[1:]), 0) for c, (w, _) in zip(casts, side_casts)]
    return out, casts


def _gmlp_kernel(pc_ref, lng_ref, lnb_ref, ws_ref, bs_ref, o_ref):
    rows = o_ref.shape[0]
    z = jax.nn.gelu(pc_ref[...].astype(F32))
    u = z[:, :BRANCH_D]
    v = _layer_norm(z[:, BRANCH_D:], lng_ref[...], lnb_ref[...]).astype(BF16)
    t_idx = lax.broadcasted_iota(jnp.int32, (GMLP_CHUNK, GMLP_CHUNK), 0)
    s_idx = lax.broadcasted_iota(jnp.int32, (GMLP_CHUNK, GMLP_CHUNK), 1)
    for g in range(GMLP_GROUPS):
        w = jnp.where(s_idx <= t_idx, ws_ref[g], 0.0).astype(BF16)
        cols = slice(g * GMLP_GD, (g + 1) * GMLP_GD)
        for r in range(0, rows, GMLP_CHUNK):
            sv = _dot(w, v[r:r + GMLP_CHUNK, cols]) + bs_ref[g]
            o_ref[r:r + GMLP_CHUNK, cols] = (u[r:r + GMLP_CHUNK, cols] * sv).astype(BF16)


def _gmlp_branch(pc, ln_g, ln_b, w_s, b_s, row_tile=WIDE_ROW_TILE):
    rows = pc.shape[0]
    vec = lambda a: a.reshape(1, BRANCH_D)
    return pl.pallas_call(
        _gmlp_kernel,
        grid=(rows // row_tile,),
        in_specs=[
            pl.BlockSpec((row_tile, COL_C), lambda i: (i, 0)),
            _resident((1, BRANCH_D)), _resident((1, BRANCH_D)),
            _resident(w_s.shape), _resident((GMLP_GROUPS, GMLP_CHUNK, 1)),
        ],
        out_specs=pl.BlockSpec((row_tile, BRANCH_D), lambda i: (i, 0)),
        out_shape=jax.ShapeDtypeStruct((rows, BRANCH_D), BF16),
        compiler_params=_params("parallel"),
        name="gmlp_branch",
    )(pc, vec(ln_g), vec(ln_b), w_s, b_s.reshape(GMLP_GROUPS, GMLP_CHUNK, 1))


def _merge_kernel(x_ref, ya_ref, yb_ref, yc_ref, g_ref, win_ref, gb_ref, wbr_ref, wout_ref, o_ref):
    x = x_ref[...]
    xn = _rms_norm(x, g_ref[...]).astype(BF16)
    merged = None
    for n, y_ref in enumerate((ya_ref, yb_ref, yc_ref)):
        col = COL_BRANCHES + n * D_MODEL
        logits = _dot(xn, win_ref[:, col:col + D_MODEL]) + gb_ref[n:n + 1, :]
        term = jax.nn.sigmoid(logits) * _dot(y_ref[...], wbr_ref[n])
        merged = term if merged is None else merged + term
    o_ref[...] = x + _dot(merged.astype(BF16), wout_ref[...])


def _merge(x, ya, yb, yc, norm_g, w_in, gate_b, w_br, w_out, row_tile=WIDE_ROW_TILE):
    rows, d = x.shape
    row_spec = lambda width: pl.BlockSpec((row_tile, width), lambda i: (i, 0))
    return pl.pallas_call(
        _merge_kernel,
        grid=(rows // row_tile,),
        in_specs=[row_spec(D_MODEL), row_spec(BRANCH_D), row_spec(BRANCH_D), row_spec(BRANCH_D),
                  _resident((1, d)), _resident_layer(w_in), _resident(gate_b.shape), _resident_layer(w_br),
                  _resident_layer(w_out)],
        out_specs=row_spec(D_MODEL),
        out_shape=jax.ShapeDtypeStruct(x.shape, F32),
        compiler_params=_params("parallel"),
        name="branch_merge",
    )(x, ya, yb, yc, norm_g.reshape(1, d), w_in.stacked, gate_b, w_br.stacked, w_out.stacked)


def _xattn_kernel(x_ref, g_ref, wq_ref, kv_ref, wo_ref, o_ref):
    x = x_ref[0]
    q = _dot(_rms_norm(x, g_ref[...]).astype(BF16), wq_ref[...]) * (XATTN_DH ** -0.5 * LOG2_E)
    q = q.astype(BF16)
    heads = []
    for h in range(XATTN_HEADS):
        cols = slice(h * XATTN_DH, (h + 1) * XATTN_DH)
        k = kv_ref[0, :, cols]
        v = kv_ref[0, :, XATTN_D + h * XATTN_DH:XATTN_D + (h + 1) * XATTN_DH]
        s = _dot_nt(q[:, cols], k)
        p = jnp.exp2(s - jnp.max(s, axis=-1, keepdims=True))
        o = _dot(p.astype(BF16), v) / jnp.sum(p, axis=-1, keepdims=True)
        heads.append(o.astype(BF16))
    o_ref[0] = x + _dot(jnp.concatenate(heads, axis=-1), wo_ref[...])


def _mem_cross_attention(x, kv, norm_g, w_xq, w_xo, row_tile=WIDE_ROW_TILE):
    b, s, d = x.shape
    m = kv.shape[1]
    return pl.pallas_call(
        _xattn_kernel,
        grid=(b, s // row_tile),
        in_specs=[
            pl.BlockSpec((1, row_tile, d), lambda bi, i: (bi, i, 0)),
            _resident((1, d)), _resident_layer(w_xq),
            pl.BlockSpec((1, m, 2 * XATTN_D), lambda bi, i: (bi, 0, 0)),
            _resident_layer(w_xo),
        ],
        out_specs=pl.BlockSpec((1, row_tile, d), lambda bi, i: (bi, i, 0)),
        out_shape=jax.ShapeDtypeStruct(x.shape, F32),
        compiler_params=_params("parallel", "parallel"),
        name="mem_cross_attention",
    )(x, norm_g.reshape(1, d), w_xq.stacked, kv, w_xo.stacked)


def _ffn_kernel(x_ref, halo_ref, g_ref, wup_ref, cw_ref, cb_ref, wdn_ref, gf_ref, o_ref,
                slab_scr, h_scr, up_scr, act_scr, acc_scr, *, final_norm):
    tile_rows = o_ref.shape[1]
    rows = acc_scr.shape[0]
    groups = rows // F32_SUBLANES
    slabs = slab_scr.shape[0]

    def lanes(j):
        return slice(j * LANES, (j + 1) * LANES)

    for j in range(slabs):
        slab_scr[j, :tile_rows, :] = x_ref[0, :, lanes(j)]
    slab_scr[:, tile_rows:, :] = jnp.zeros((slabs, rows - tile_rows, LANES), F32)
    x = jnp.concatenate(
        [jnp.concatenate([slab_scr[j, pl.ds(gi, F32_SUBLANES, stride=groups), :] for j in range(slabs)], axis=1)
         for gi in range(groups)], axis=0)

    g = g_ref[...]
    halo = _rms_norm(halo_ref[0], g)
    h_scr[:FFN_HALO, :] = jnp.where(pl.program_id(1) > 0, halo, 0.0).astype(BF16)
    h_scr[FFN_HALO:, :] = _rms_norm(x, g).astype(BF16)
    acc_scr[...] = x

    n_chunks = D_FF // FFN_COL_CHUNK
    first_sublane = lax.broadcasted_iota(jnp.int32, (F32_SUBLANES, FFN_COL_CHUNK), 0) == 0

    def cols(c, half):
        start = half * D_FF + c * FFN_COL_CHUNK
        return slice(start, start + FFN_COL_CHUNK)

    def up_proj(c, slot):
        h = h_scr[...]
        for half in range(2):
            up = up_scr.at[slot, half]
            up[...] = _dot(h, wup_ref[:, cols(c, half)])
            prev_tile = up[FFN_HALO - F32_SUBLANES:FFN_HALO, :]
            for shift in range(1, FFN_K):
                wrapped = up[FFN_HALO + rows - shift * F32_SUBLANES:FFN_HALO + rows - (shift - 1) * F32_SUBLANES, :]
                block = jnp.where(first_sublane, pltpu.roll(prev_tile, shift, 0), pltpu.roll(wrapped, 1, 0))
                up[FFN_HALO - shift * F32_SUBLANES:FFN_HALO - (shift - 1) * F32_SUBLANES, :] = block

    def conv(c, slot, half):
        out = cb_ref[:, cols(c, half)]
        for k in range(FFN_K):
            window = up_scr[slot, half, k * F32_SUBLANES:k * F32_SUBLANES + rows, :]
            out = out + cw_ref[k:k + 1, cols(c, half)] * window
        return out

    def gate(c, slot):
        act_scr[slot] = (jax.nn.silu(conv(c, slot, 0)) * conv(c, slot, 1)).astype(BF16)

    def down_proj(c, slot):
        acc_scr[...] += _dot(act_scr[slot], wdn_ref[cols(c, 0), :])

    def stage(c, slot):
        up_proj(c + 1, 1 - slot)
        down_proj(c - 1, 1 - slot)
        gate(c, slot)

    up_proj(0, 0)
    up_proj(1, 1)
    gate(0, 0)
    for c in range(1, n_chunks - 1):
        stage(c, c % 2)
    down_proj(n_chunks - 2, (n_chunks - 2) % 2)
    gate(n_chunks - 1, (n_chunks - 1) % 2)
    down_proj(n_chunks - 1, (n_chunks - 1) % 2)

    for gi in range(groups):
        out = acc_scr[gi * F32_SUBLANES:(gi + 1) * F32_SUBLANES, :]
        if final_norm:
            out = _rms_norm(out, gf_ref[...])
        for j in range(slabs):
            slab_scr[j, pl.ds(gi, F32_SUBLANES, stride=groups), :] = out[:, lanes(j)]
    for j in range(slabs):
        o_ref[0, :, lanes(j)] = slab_scr[j, :tile_rows, :]


def _conv_ffn(x, norm_g, w_up, conv_w, conv_b, w_down, final_g, final_norm, row_tile=FFN_ROW_TILE):
    b, s, d = x.shape
    halo_blocks = row_tile // FFN_HALO
    conv_b = conv_b.reshape(1, 2 * D_FF)
    perm_rows = row_tile + 2 * F32_SUBLANES
    return pl.pallas_call(
        functools.partial(_ffn_kernel, final_norm=final_norm),
        grid=(b, s // row_tile),
        in_specs=[
            pl.BlockSpec((1, row_tile, d), lambda bi, i: (bi, i, 0)),
            pl.BlockSpec((1, FFN_HALO, d), lambda bi, i: (bi, jnp.maximum(i * halo_blocks - 1, 0), 0)),
            _resident((1, d)), _resident_layer(w_up), _resident(conv_w.shape), _resident(conv_b.shape),
            _resident_layer(w_down), _resident((1, d)),
        ],
        out_specs=pl.BlockSpec((1, row_tile, d), lambda bi, i: (bi, i, 0)),
        out_shape=jax.ShapeDtypeStruct(x.shape, F32),
        scratch_shapes=[
            pltpu.VMEM((d // LANES, perm_rows, LANES), F32),
            pltpu.VMEM((FFN_HALO + perm_rows, d), BF16),
            pltpu.VMEM((2, 2, FFN_HALO + perm_rows, FFN_COL_CHUNK), F32),
            pltpu.VMEM((2, perm_rows, FFN_COL_CHUNK), BF16),
            pltpu.VMEM((perm_rows, d), F32),
        ],
        compiler_params=_params("parallel", "parallel"),
        name="conv_ffn",
    )(x, x, norm_g.reshape(1, d), w_up.stacked, conv_w, conv_b, w_down.stacked, final_g.reshape(1, d))


def kernel(x, mem, rel_bias, norm_mix_g, w_in, gate_b, conv_w, conv_b, conv_ln_g, conv_ln_b, lam_q1, lam_k1, lam_q2, lam_k2, subln_g, gmlp_ln_g, gmlp_ln_b, w_s, b_s, w_br, w_out, norm_xattn_g, norm_mem_g, w_xq, w_xkv, w_xo, norm_ffn_g, w_up, ffn_conv_w, ffn_conv_b, w_down, norm_final_g):
    b, s, d = x.shape
    m = mem.shape[1]
    rows = b * s
    bias_tiles = _bias_tiles(rel_bias)
    w_in_l = _to_bf16(w_in, 0)
    for l in range(DEPTH):
        lambda_init = 0.8 - 0.6 * math.exp(-0.3 * l)
        ya, pq, pk, pv, pc = _mixer_in_proj(x, norm_mix_g[l], w_in_l, conv_w[l], conv_b[l], conv_ln_g[l],
                                            conv_ln_b[l])
        side_casts = [(w, l) for w in (w_br, w_out, w_xq, w_xkv, w_xo, w_up, w_down)]
        if l + 1 < DEPTH:
            side_casts.append((w_in, l + 1))
        yb, casts = _diff_attention(pq, pk, pv, bias_tiles, lam_q1[l], lam_k1[l], lam_q2[l], lam_k2[l],
                                    subln_g[l], lambda_init, side_casts)
        w_br_l, w_out_l, w_xq_l, w_xkv_l, w_xo_l, w_up_l, w_down_l = casts[:7]
        yc = _gmlp_branch(pc.reshape(rows, COL_C), gmlp_ln_g[l], gmlp_ln_b[l], w_s[l], b_s[l])
        x = _merge(x.reshape(rows, d), ya.reshape(rows, BRANCH_D), yb.reshape(rows, BRANCH_D), yc, norm_mix_g[l],
                   w_in_l, gate_b[l], w_br_l, w_out_l).reshape(b, s, d)
        (kv,) = _norm_proj(mem.reshape(b * m, d), norm_mem_g[l], w_xkv_l, (2 * XATTN_D,))
        x = _mem_cross_attention(x, kv.reshape(b, m, 2 * XATTN_D), norm_xattn_g[l], w_xq_l, w_xo_l)
        x = _conv_ffn(x, norm_ffn_g[l], w_up_l, ffn_conv_w[l], ffn_conv_b[l], w_down_l, norm_final_g,
                      final_norm=(l == DEPTH - 1))
        if l + 1 < DEPTH:
            w_in_l = casts[7]
    return x
```

```python
import functools
import math
from typing import NamedTuple

import jax
import jax.numpy as jnp
from jax import lax
from jax.experimental import pallas as pl
from jax.experimental.pallas import tpu as pltpu

F32 = jnp.float32
BF16 = jnp.bfloat16

D_MODEL = 1024
DEPTH = 2
CHUNK = 64
BRANCH_D = D_MODEL // 2
N_BRANCH = 3
CONV_K = 31
DIFF_HEADS = 4
DIFF_DH = D_MODEL // 16
DIFF_VD = 2 * DIFF_DH
GMLP_GROUPS = 4
GMLP_GD = BRANCH_D // GMLP_GROUPS
GMLP_CHUNK = 128
N_REL_BUCKETS = 32
REL_MAX_DIST = 128
XATTN_HEADS = 4
XATTN_DH = D_MODEL // 8
XATTN_D = XATTN_HEADS * XATTN_DH
D_FF = 2816
FFN_K = 3
EPS = 1e-6
NEG_INF = -1e30

COL_A = 2 * BRANCH_D
COL_QK = DIFF_HEADS * 2 * DIFF_DH
COL_V = DIFF_HEADS * DIFF_VD
COL_C = 2 * BRANCH_D
COL_G = N_BRANCH * D_MODEL
IN_SPLIT_WIDTHS = (COL_A, COL_QK, COL_QK, COL_V, COL_C)
COL_BRANCHES = sum(IN_SPLIT_WIDTHS)
LOG2_E = math.log2(math.e)
IN_SPLIT_SCALES = (1.0, DIFF_DH ** -0.5 * LOG2_E, 1.0, 1.0, 1.0)

LANES = 128
MXU_WIDTH = 256
F32_SUBLANES = 8
BF16_SUBLANES = 16
VMEM_LIMIT_BYTES = 56 * 1024 * 1024
CAST_BLOCK_BYTES = 4 * 1024 * 1024

ROW_TILE = 512
WIDE_ROW_TILE = 1024
FFN_ROW_TILE = 1024
PROJ_COL_CHUNK = 512
ATTN_TILE = 256
CONV_HALO = 32
CONV_ROW_CHUNK = 64
FFN_HALO = BF16_SUBLANES
FFN_COL_CHUNK = MXU_WIDTH

assert CONV_HALO >= CONV_K - 1 and FFN_HALO == F32_SUBLANES * (FFN_K - 1)
assert D_FF % FFN_COL_CHUNK == 0
assert ATTN_TILE % CHUNK == 0 and ATTN_TILE > REL_MAX_DIST


def _params(*semantics):
    return pltpu.CompilerParams(dimension_semantics=semantics, vmem_limit_bytes=VMEM_LIMIT_BYTES)


def _resident(shape):
    zeros = (0,) * len(shape)
    return pl.BlockSpec(shape, lambda *_: zeros, pipeline_mode=pl.Buffered(1))


class _LayerWeight(NamedTuple):
    stacked: jax.Array
    layer: int

    @property
    def shape(self):
        return self.stacked.shape[1:]


def _resident_layer(w):
    index = (w.layer,) + (0,) * len(w.shape)
    return pl.BlockSpec((None,) + tuple(w.shape), lambda *_: index, pipeline_mode=pl.Buffered(1))


def _cast_kernel(w_ref, o_ref):
    o_ref[...] = w_ref[...].astype(BF16)


def _to_bf16(w, layer):
    _, rows, cols = w.shape
    block_rows = max(r for r in range(BF16_SUBLANES, rows + 1, BF16_SUBLANES)
                     if rows % r == 0 and r * cols * 4 <= CAST_BLOCK_BYTES)
    out = pl.pallas_call(
        _cast_kernel,
        grid=(rows // block_rows,),
        in_specs=[pl.BlockSpec((None, block_rows, cols), lambda i: (layer, i, 0))],
        out_specs=pl.BlockSpec((None, block_rows, cols), lambda i: (0, i, 0)),
        out_shape=jax.ShapeDtypeStruct((1, rows, cols), BF16),
        compiler_params=_params("parallel"),
        name="weight_cast",
    )(w)
    return _LayerWeight(out, 0)


def _rms_norm(x, g):
    return x * lax.rsqrt(jnp.mean(x * x, axis=-1, keepdims=True) + EPS) * g


def _layer_norm(x, g, b):
    mu = jnp.mean(x, axis=-1, keepdims=True)
    xc = x - mu
    var = jnp.mean(xc * xc, axis=-1, keepdims=True)
    return xc * lax.rsqrt(var + EPS) * g + b


def _dot(a, b):
    return jnp.dot(a, b, preferred_element_type=F32)


def _dot_nt(a, b):
    return lax.dot_general(a, b, (((1,), (1,)), ((), ())), preferred_element_type=F32)


def _norm_proj_kernel(x_ref, g_ref, w_ref, *out_refs, widths, scales):
    xn = _rms_norm(x_ref[...], g_ref[...]).astype(BF16)
    col = 0
    for o_ref, width, scale in zip(out_refs, widths, scales):
        for c in range(0, width, PROJ_COL_CHUNK):
            out = _dot(xn, w_ref[:, col + c:col + c + PROJ_COL_CHUNK])
            if scale != 1.0:
                out = out * scale
            o_ref[:, c:c + PROJ_COL_CHUNK] = out.astype(BF16)
        col += width


def _norm_proj(x, g, w, widths, scales=None, row_tile=ROW_TILE):
    rows, d = x.shape
    scales = scales or (1.0,) * len(widths)
    assert w.shape == (d, sum(widths)) and all(wd % PROJ_COL_CHUNK == 0 for wd in widths)
    return pl.pallas_call(
        functools.partial(_norm_proj_kernel, widths=widths, scales=scales),
        grid=(rows // row_tile,),
        in_specs=[pl.BlockSpec((row_tile, d), lambda i: (i, 0)), _resident((1, d)), _resident_layer(w)],
        out_specs=[pl.BlockSpec((row_tile, wd), lambda i: (i, 0)) for wd in widths],
        out_shape=[jax.ShapeDtypeStruct((rows, wd), BF16) for wd in widths],
        compiler_params=_params("parallel"),
        name="norm_proj",
    )(x, g.reshape(1, d), w.stacked)


def _mixer_in_kernel(x_ref, g_ref, w_ref, cw_ref, cb_ref, lng_ref, lnb_ref,
                     ya_ref, pq_ref, pk_ref, pv_ref, pc_ref, xn_scr, xs_scr, halo_scr):
    rows = ya_ref.shape[1]
    xn_scr[...] = _rms_norm(x_ref[0], g_ref[...]).astype(BF16)

    def proj(col, width):
        return _dot(xn_scr[...], w_ref[:, col:col + width])

    glu = proj(0, BRANCH_D) * jax.nn.sigmoid(proj(BRANCH_D, BRANCH_D))

    @pl.when(pl.program_id(1) == 0)
    def _():
        halo_scr[...] = jnp.zeros(halo_scr.shape, F32)

    xs_scr[0, :CONV_HALO, :] = halo_scr[...]
    xs_scr[0, CONV_HALO:, :] = glu
    halo_scr[...] = glu[rows - CONV_HALO:, :]

    def shifted_copies():
        total = CONV_HALO + rows
        base = xs_scr[0]
        for b in range(1, F32_SUBLANES):
            xs_scr[b] = pltpu.roll(base, total - b, 0)

    first_tap = CONV_HALO - (CONV_K - 1)

    def conv_rows(r):
        acc = jnp.broadcast_to(cb_ref[...], (CONV_ROW_CHUNK, BRANCH_D))
        for k in range(CONV_K):
            b = (first_tap + k) % F32_SUBLANES
            start = r + first_tap + k - b
            acc = acc + cw_ref[k:k + 1, :] * xs_scr[b, start:start + CONV_ROW_CHUNK, :]
        ya_ref[0, r:r + CONV_ROW_CHUNK, :] = jax.nn.silu(_layer_norm(acc, lng_ref[...], lnb_ref[...])).astype(BF16)

    def proj_chunk(o_ref, col, c, scale):
        out = proj(col + c, PROJ_COL_CHUNK)
        if scale != 1.0:
            out = out * scale
        o_ref[0, :, c:c + PROJ_COL_CHUNK] = out.astype(BF16)

    col = COL_A
    for o_ref, width, scale in zip((pq_ref, pk_ref, pv_ref, pc_ref), IN_SPLIT_WIDTHS[1:], IN_SPLIT_SCALES[1:]):
        for c in range(0, width, PROJ_COL_CHUNK):
            proj_chunk(o_ref, col, c, scale)
        col += width
    shifted_copies()
    for r in range(0, rows, CONV_ROW_CHUNK):
        conv_rows(r)


def _mixer_in_proj(x, norm_g, w_in, conv_w, conv_b, ln_g, ln_b, row_tile=ROW_TILE):
    b, s, d = x.shape
    vec = lambda a: a.reshape(1, -1)
    out_widths = (BRANCH_D,) + IN_SPLIT_WIDTHS[1:]
    return pl.pallas_call(
        _mixer_in_kernel,
        grid=(b, s // row_tile),
        in_specs=[
            pl.BlockSpec((1, row_tile, d), lambda bi, i: (bi, i, 0)),
            _resident((1, d)), _resident_layer(w_in),
            _resident((CONV_K, BRANCH_D)), _resident((1, BRANCH_D)), _resident((1, BRANCH_D)),
            _resident((1, BRANCH_D)),
        ],
        out_specs=[pl.BlockSpec((1, row_tile, wd), lambda bi, i: (bi, i, 0)) for wd in out_widths],
        out_shape=[jax.ShapeDtypeStruct((b, s, wd), BF16) for wd in out_widths],
        scratch_shapes=[
            pltpu.VMEM((row_tile, d), BF16),
            pltpu.VMEM((F32_SUBLANES, CONV_HALO + row_tile, BRANCH_D), F32),
            pltpu.VMEM((CONV_HALO, BRANCH_D), F32),
        ],
        compiler_params=_params("parallel", "arbitrary"),
        name="mixer_in_proj",
    )(x, vec(norm_g), w_in.stacked, conv_w, vec(conv_b), vec(ln_g), vec(ln_b))


def _rel_bucket(rel):
    half = N_REL_BUCKETS // 2
    max_exact = half // 2
    n = jnp.abs(rel)
    log_ratio = jnp.log(jnp.maximum(n, 1).astype(F32) / max_exact) / math.log(REL_MAX_DIST / max_exact)
    large = jnp.minimum(max_exact + (log_ratio * (half - max_exact)).astype(jnp.int32), half - 1)
    return jnp.where(rel > 0, half, 0) + jnp.where(n < max_exact, n, large)


FAR_BUCKET = N_REL_BUCKETS // 2 - 1
MASKED_BUCKET = -1


def _near_buckets():
    r = jnp.arange(ATTN_TILE)[:, None]
    c = jnp.arange(ATTN_TILE)[None, :]
    visible = (c // CHUNK) <= (r // CHUNK)
    diag = jnp.where(visible, _rel_bucket(c - r), MASKED_BUCKET)
    sub = _rel_bucket(c - r - ATTN_TILE)
    return jnp.stack([diag, sub]).astype(jnp.int32)


def _bias_tiles_kernel(table_ref, bucket_ref, o_ref):
    h = pl.program_id(0)
    bucket = bucket_ref[...]
    far = table_ref[FAR_BUCKET, h]
    out = jnp.zeros(bucket.shape, F32)
    for b in range(N_REL_BUCKETS):
        out = jnp.where(bucket == b, (table_ref[b, h] - far) * LOG2_E, out)
    o_ref[0] = jnp.where(bucket == MASKED_BUCKET, NEG_INF, out)


def _bias_tiles(rel_bias):
    buckets = _near_buckets()
    return pl.pallas_call(
        _bias_tiles_kernel,
        grid=(DIFF_HEADS,),
        in_specs=[pl.BlockSpec(memory_space=pltpu.SMEM), _resident(buckets.shape)],
        out_specs=pl.BlockSpec((1,) + buckets.shape, lambda h: (h, 0, 0, 0)),
        out_shape=jax.ShapeDtypeStruct((DIFF_HEADS,) + buckets.shape, F32),
        compiler_params=_params("arbitrary"),
        name="rel_bias_tiles",
    )(rel_bias.astype(F32), buckets)


def _diff_attn_kernel(*refs, lambda_init, cast_chunks):
    n_casts = len(cast_chunks)
    (q_ref, k_ref, v_ref, bias_ref, lq1_ref, lk1_ref, lq2_ref, lk2_ref, g_ref), refs = refs[:9], refs[9:]
    cast_src, o_ref, cast_dst = refs[:n_casts], refs[n_casts], refs[n_casts + 1:2 * n_casts + 1]
    q2_scr, s_scr, m_scr, acc_scr, vones_scr = refs[2 * n_casts + 1:]
    step = pl.program_id(0) * pl.num_programs(1) + pl.program_id(1)
    for src, dst, chunks in zip(cast_src, cast_dst, cast_chunks):
        @pl.when(step < chunks)
        def _():
            dst[...] = src[...].astype(BF16)

    _diff_attn_body(q_ref, k_ref, v_ref, bias_ref, lq1_ref, lk1_ref, lq2_ref, lk2_ref, g_ref, o_ref,
                    q2_scr, s_scr, m_scr, acc_scr, vones_scr, lambda_init=lambda_init)


def _diff_attn_body(q_ref, k_ref, v_ref, bias_ref, lq1_ref, lk1_ref, lq2_ref, lk2_ref, g_ref, o_ref,
                    q2_scr, s_scr, m_scr, acc_scr, vones_scr, *, lambda_init):
    t = ATTN_TILE
    i = pl.program_id(1)
    heads = range(DIFF_HEADS)

    def head_cols(h):
        return slice(h * DIFF_VD, (h + 1) * DIFF_VD)

    def build_values_and_ones():
        for h in heads:
            vones_scr[h, :, :DIFF_VD] = v_ref[0, :, head_cols(h)]
            vones_scr[h, :, DIFF_VD:] = jnp.ones((vones_scr.shape[1], LANES), BF16)

    def stack_queries():
        q = q_ref[0]
        lane = lax.broadcasted_iota(jnp.int32, (t, DIFF_VD), 1)
        for h in heads:
            qh = q[:, head_cols(h)]
            q2_scr[h, :t, :] = jnp.where(lane < DIFF_DH, qh, 0).astype(BF16)
            q2_scr[h, t:, :] = jnp.where(lane >= DIFF_DH, qh, 0).astype(BF16)

    def logits_tiles(near):
        for n, bias_index in enumerate(near):
            for h in heads:
                s = _dot_nt(q2_scr[h], k_ref[0, n * t:(n + 1) * t, head_cols(h)])
                if bias_index is not None:
                    bias = bias_ref[h, bias_index]
                    s = s + jnp.concatenate([bias, bias], axis=0)
                s_scr[n, h] = s
                lane_max = jnp.maximum(s[:, :LANES], s[:, LANES:])
                m_scr[h] = lane_max if n == 0 else jnp.maximum(m_scr[h], lane_max)

    def pv_tiles(n_tiles):
        for h in heads:
            m = m_scr[h]
            parts = [jnp.exp2(s_scr[n, h, :, half * LANES:(half + 1) * LANES] - m)
                     for n in range(n_tiles) for half in range(t // LANES)]
            p = jnp.concatenate(parts, axis=1).astype(BF16)
            acc_scr[h] = _dot(p, vones_scr[h, :n_tiles * t, :])

    def combine_and_norm():
        lam = (jnp.exp(jnp.sum(lq1_ref[...] * lk1_ref[...])) - jnp.exp(jnp.sum(lq2_ref[...] * lk2_ref[...]))
               + lambda_init)
        for h in heads:
            o = (acc_scr[h, :t, :DIFF_VD] / acc_scr[h, :t, DIFF_VD:]
                 - lam * (acc_scr[h, t:, :DIFF_VD] / acc_scr[h, t:, DIFF_VD:]))
            o_ref[0, :, head_cols(h)] = (_rms_norm(o, g_ref[...]) * (1.0 - lambda_init)).astype(BF16)

    def query_tile(tile):
        if tile == 0:
            build_values_and_ones()
        stack_queries()
        logits_tiles((None,) * (tile - 1) + ((1, 0) if tile > 0 else (0,)))
        for h in heads:
            m_scr[h] = jnp.broadcast_to(jnp.max(m_scr[h], axis=-1, keepdims=True), (2 * t, LANES))
        pv_tiles(tile + 1)
        combine_and_norm()

    for tile in range(s_scr.shape[0]):
        pl.when(i == tile)(functools.partial(query_tile, tile))


def _diff_attention(pq, pk, pv, bias_tiles, lam_q1, lam_k1, lam_q2, lam_k2, subln_g, lambda_init, side_casts):
    b, s, width = pq.shape
    t = ATTN_TILE
    tiles = s // t
    steps = b * tiles
    vec = lambda a: a.reshape(1, -1).astype(F32)
    all_keys = pl.BlockSpec((1, s, width), lambda bi, i: (bi, 0, 0))
    lam_spec = _resident((1, DIFF_DH))

    cast_views, cast_in_specs, cast_out_specs, cast_out_shapes, cast_chunks = [], [], [], [], []
    for w, layer in side_casts:
        view = w.reshape(w.shape[0], -1, w.shape[-1])
        _, rows, cols = view.shape
        chunks = max(c for c in range(1, steps + 1) if rows % (c * BF16_SUBLANES) == 0)

        def chunk(bi, i, chunks=chunks):
            return jnp.minimum(bi * tiles + i, chunks - 1)

        cast_views.append(view)
        cast_chunks.append(chunks)
        cast_in_specs.append(pl.BlockSpec((None, rows // chunks, cols),
                                          lambda bi, i, layer=layer, chunk=chunk: (layer, chunk(bi, i), 0)))
        cast_out_specs.append(pl.BlockSpec((rows // chunks, cols), lambda bi, i, chunk=chunk: (chunk(bi, i), 0)))
        cast_out_shapes.append(jax.ShapeDtypeStruct((rows, cols), BF16))

    out, *casts = pl.pallas_call(
        functools.partial(_diff_attn_kernel, lambda_init=lambda_init, cast_chunks=tuple(cast_chunks)),
        grid=(b, tiles),
        in_specs=[
            pl.BlockSpec((1, t, width), lambda bi, i: (bi, i, 0)),
            all_keys, all_keys, _resident(bias_tiles.shape),
            lam_spec, lam_spec, lam_spec, lam_spec, _resident((1, DIFF_VD)),
        ] + cast_in_specs,
        out_specs=[pl.BlockSpec((1, t, width), lambda bi, i: (bi, i, 0))] + cast_out_specs,
        out_shape=[jax.ShapeDtypeStruct((b, s, width), BF16)] + cast_out_shapes,
        scratch_shapes=[
            pltpu.VMEM((DIFF_HEADS, 2 * t, DIFF_VD), BF16),
            pltpu.VMEM((s // t, DIFF_HEADS, 2 * t, t), F32),
            pltpu.VMEM((DIFF_HEADS, 2 * t, LANES), F32),
            pltpu.VMEM((DIFF_HEADS, 2 * t, DIFF_VD + LANES), F32),
            pltpu.VMEM((DIFF_HEADS, s, DIFF_VD + LANES), BF16),
        ],
        compiler_params=_params("parallel", "arbitrary"),
        name="diff_attention",
    )(pq, pk, pv, bias_tiles, vec(lam_q1), vec(lam_k1), vec(lam_q2), vec(lam_k2), vec(subln_g), *cast_views)
    casts = [_LayerWeight(c.reshape((1,) + w.shape[1:]), 0) for c, (w, _) in zip(casts, side_casts)]
    return out, casts


def _gmlp_kernel(pc_ref, lng_ref, lnb_ref, ws_ref, bs_ref, o_ref):
    rows = o_ref.shape[0]
    z = jax.nn.gelu(pc_ref[...].astype(F32))
    u = z[:, :BRANCH_D]
    v = _layer_norm(z[:, BRANCH_D:], lng_ref[...], lnb_ref[...]).astype(BF16)
    t_idx = lax.broadcasted_iota(jnp.int32, (GMLP_CHUNK, GMLP_CHUNK), 0)
    s_idx = lax.broadcasted_iota(jnp.int32, (GMLP_CHUNK, GMLP_CHUNK), 1)
    for g in range(GMLP_GROUPS):
        w = jnp.where(s_idx <= t_idx, ws_ref[g], 0.0).astype(BF16)
        cols = slice(g * GMLP_GD, (g + 1) * GMLP_GD)
        for r in range(0, rows, GMLP_CHUNK):
            sv = _dot(w, v[r:r + GMLP_CHUNK, cols]) + bs_ref[g]
            o_ref[r:r + GMLP_CHUNK, cols] = (u[r:r + GMLP_CHUNK, cols] * sv).astype(BF16)


def _gmlp_branch(pc, ln_g, ln_b, w_s, b_s, row_tile=WIDE_ROW_TILE):
    rows = pc.shape[0]
    vec = lambda a: a.reshape(1, BRANCH_D)
    return pl.pallas_call(
        _gmlp_kernel,
        grid=(rows // row_tile,),
        in_specs=[
            pl.BlockSpec((row_tile, COL_C), lambda i: (i, 0)),
            _resident((1, BRANCH_D)), _resident((1, BRANCH_D)),
            _resident(w_s.shape), _resident((GMLP_GROUPS, GMLP_CHUNK, 1)),
        ],
        out_specs=pl.BlockSpec((row_tile, BRANCH_D), lambda i: (i, 0)),
        out_shape=jax.ShapeDtypeStruct((rows, BRANCH_D), BF16),
        compiler_params=_params("parallel"),
        name="gmlp_branch",
    )(pc, vec(ln_g), vec(ln_b), w_s, b_s.reshape(GMLP_GROUPS, GMLP_CHUNK, 1))


def _merge_kernel(x_ref, ya_ref, yb_ref, yc_ref, g_ref, win_ref, gb_ref, wbr_ref, wout_ref, o_ref):
    x = x_ref[...]
    xn = _rms_norm(x, g_ref[...]).astype(BF16)
    merged = None
    for n, y_ref in enumerate((ya_ref, yb_ref, yc_ref)):
        col = COL_BRANCHES + n * D_MODEL
        logits = _dot(xn, win_ref[:, col:col + D_MODEL]) + gb_ref[n:n + 1, :]
        term = jax.nn.sigmoid(logits) * _dot(y_ref[...], wbr_ref[n])
        merged = term if merged is None else merged + term
    o_ref[...] = x + _dot(merged.astype(BF16), wout_ref[...])


def _merge(x, ya, yb, yc, norm_g, w_in, gate_b, w_br, w_out, row_tile=WIDE_ROW_TILE):
    rows, d = x.shape
    row_spec = lambda width: pl.BlockSpec((row_tile, width), lambda i: (i, 0))
    return pl.pallas_call(
        _merge_kernel,
        grid=(rows // row_tile,),
        in_specs=[row_spec(D_MODEL), row_spec(BRANCH_D), row_spec(BRANCH_D), row_spec(BRANCH_D),
                  _resident((1, d)), _resident_layer(w_in), _resident(gate_b.shape), _resident_layer(w_br),
                  _resident_layer(w_out)],
        out_specs=row_spec(D_MODEL),
        out_shape=jax.ShapeDtypeStruct(x.shape, F32),
        compiler_params=_params("parallel"),
        name="branch_merge",
    )(x, ya, yb, yc, norm_g.reshape(1, d), w_in.stacked, gate_b, w_br.stacked, w_out.stacked)


def _xattn_kernel(x_ref, g_ref, wq_ref, kv_ref, wo_ref, o_ref):
    x = x_ref[0]
    q = _dot(_rms_norm(x, g_ref[...]).astype(BF16), wq_ref[...]) * (XATTN_DH ** -0.5 * LOG2_E)
    q = q.astype(BF16)
    heads = []
    for h in range(XATTN_HEADS):
        cols = slice(h * XATTN_DH, (h + 1) * XATTN_DH)
        k = kv_ref[0, :, cols]
        v = kv_ref[0, :, XATTN_D + h * XATTN_DH:XATTN_D + (h + 1) * XATTN_DH]
        s = _dot_nt(q[:, cols], k)
        p = jnp.exp2(s - jnp.max(s, axis=-1, keepdims=True))
        o = _dot(p.astype(BF16), v) / jnp.sum(p, axis=-1, keepdims=True)
        heads.append(o.astype(BF16))
    o_ref[0] = x + _dot(jnp.concatenate(heads, axis=-1), wo_ref[...])


def _mem_cross_attention(x, kv, norm_g, w_xq, w_xo, row_tile=WIDE_ROW_TILE):
    b, s, d = x.shape
    m = kv.shape[1]
    return pl.pallas_call(
        _xattn_kernel,
        grid=(b, s // row_tile),
        in_specs=[
            pl.BlockSpec((1, row_tile, d), lambda bi, i: (bi, i, 0)),
            _resident((1, d)), _resident_layer(w_xq),
            pl.BlockSpec((1, m, 2 * XATTN_D), lambda bi, i: (bi, 0, 0)),
            _resident_layer(w_xo),
        ],
        out_specs=pl.BlockSpec((1, row_tile, d), lambda bi, i: (bi, i, 0)),
        out_shape=jax.ShapeDtypeStruct(x.shape, F32),
        compiler_params=_params("parallel", "parallel"),
        name="mem_cross_attention",
    )(x, norm_g.reshape(1, d), w_xq.stacked, kv, w_xo.stacked)


def _ffn_kernel(x_ref, halo_ref, g_ref, wup_ref, cw_ref, cb_ref, wdn_ref, gf_ref, o_ref,
                slab_scr, h_scr, up_scr, act_scr, acc_scr, *, final_norm):
    tile_rows = o_ref.shape[1]
    rows = acc_scr.shape[0]
    groups = rows // F32_SUBLANES
    slabs = slab_scr.shape[0]

    def lanes(j):
        return slice(j * LANES, (j + 1) * LANES)

    for j in range(slabs):
        slab_scr[j, :tile_rows, :] = x_ref[0, :, lanes(j)]
    slab_scr[:, tile_rows:, :] = jnp.zeros((slabs, rows - tile_rows, LANES), F32)
    x = jnp.concatenate(
        [jnp.concatenate([slab_scr[j, pl.ds(gi, F32_SUBLANES, stride=groups), :] for j in range(slabs)], axis=1)
         for gi in range(groups)], axis=0)

    g = g_ref[...]
    halo = _rms_norm(halo_ref[0], g)
    h_scr[:FFN_HALO, :] = jnp.where(pl.program_id(1) > 0, halo, 0.0).astype(BF16)
    h_scr[FFN_HALO:, :] = _rms_norm(x, g).astype(BF16)
    acc_scr[...] = x

    n_chunks = D_FF // FFN_COL_CHUNK
    first_sublane = lax.broadcasted_iota(jnp.int32, (F32_SUBLANES, FFN_COL_CHUNK), 0) == 0

    def cols(c, half):
        start = half * D_FF + c * FFN_COL_CHUNK
        return slice(start, start + FFN_COL_CHUNK)

    def up_proj(c, slot):
        h = h_scr[...]
        for half in range(2):
            up = up_scr.at[slot, half]
            up[...] = _dot(h, wup_ref[:, cols(c, half)])
            prev_tile = up[FFN_HALO - F32_SUBLANES:FFN_HALO, :]
            for shift in range(1, FFN_K):
                wrapped = up[FFN_HALO + rows - shift * F32_SUBLANES:FFN_HALO + rows - (shift - 1) * F32_SUBLANES, :]
                block = jnp.where(first_sublane, pltpu.roll(prev_tile, shift, 0), pltpu.roll(wrapped, 1, 0))
                up[FFN_HALO - shift * F32_SUBLANES:FFN_HALO - (shift - 1) * F32_SUBLANES, :] = block

    def conv(c, slot, half):
        out = cb_ref[:, cols(c, half)]
        for k in range(FFN_K):
            window = up_scr[slot, half, k * F32_SUBLANES:k * F32_SUBLANES + rows, :]
            out = out + cw_ref[k:k + 1, cols(c, half)] * window
        return out

    def gate(c, slot):
        act_scr[slot] = (jax.nn.silu(conv(c, slot, 0)) * conv(c, slot, 1)).astype(BF16)

    def down_proj(c, slot):
        acc_scr[...] += _dot(act_scr[slot], wdn_ref[cols(c, 0), :])

    def stage(c, slot):
        up_proj(c + 1, 1 - slot)
        down_proj(c - 1, 1 - slot)
        gate(c, slot)

    up_proj(0, 0)
    up_proj(1, 1)
    gate(0, 0)
    for c in range(1, n_chunks - 1):
        stage(c, c % 2)
    down_proj(n_chunks - 2, (n_chunks - 2) % 2)
    gate(n_chunks - 1, (n_chunks - 1) % 2)
    down_proj(n_chunks - 1, (n_chunks - 1) % 2)

    for gi in range(groups):
        out = acc_scr[gi * F32_SUBLANES:(gi + 1) * F32_SUBLANES, :]
        if final_norm:
            out = _rms_norm(out, gf_ref[...])
        for j in range(slabs):
            slab_scr[j, pl.ds(gi, F32_SUBLANES, stride=groups), :] = out[:, lanes(j)]
    for j in range(slabs):
        o_ref[0, :, lanes(j)] = slab_scr[j, :tile_rows, :]


def _conv_ffn(x, norm_g, w_up, conv_w, conv_b, w_down, final_g, final_norm, row_tile=FFN_ROW_TILE):
    b, s, d = x.shape
    halo_blocks = row_tile // FFN_HALO
    conv_b = conv_b.reshape(1, 2 * D_FF)
    perm_rows = row_tile + 2 * F32_SUBLANES
    return pl.pallas_call(
        functools.partial(_ffn_kernel, final_norm=final_norm),
        grid=(b, s // row_tile),
        in_specs=[
            pl.BlockSpec((1, row_tile, d), lambda bi, i: (bi, i, 0)),
            pl.BlockSpec((1, FFN_HALO, d), lambda bi, i: (bi, jnp.maximum(i * halo_blocks - 1, 0), 0)),
            _resident((1, d)), _resident_layer(w_up), _resident(conv_w.shape), _resident(conv_b.shape),
            _resident_layer(w_down), _resident((1, d)),
        ],
        out_specs=pl.BlockSpec((1, row_tile, d), lambda bi, i: (bi, i, 0)),
        out_shape=jax.ShapeDtypeStruct(x.shape, F32),
        scratch_shapes=[
            pltpu.VMEM((d // LANES, perm_rows, LANES), F32),
            pltpu.VMEM((FFN_HALO + perm_rows, d), BF16),
            pltpu.VMEM((2, 2, FFN_HALO + perm_rows, FFN_COL_CHUNK), F32),
            pltpu.VMEM((2, perm_rows, FFN_COL_CHUNK), BF16),
            pltpu.VMEM((perm_rows, d), F32),
        ],
        compiler_params=_params("parallel", "parallel"),
        name="conv_ffn",
    )(x, x, norm_g.reshape(1, d), w_up.stacked, conv_w, conv_b, w_down.stacked, final_g.reshape(1, d))


def kernel(x, mem, rel_bias, norm_mix_g, w_in, gate_b, conv_w, conv_b, conv_ln_g, conv_ln_b, lam_q1, lam_k1, lam_q2, lam_k2, subln_g, gmlp_ln_g, gmlp_ln_b, w_s, b_s, w_br, w_out, norm_xattn_g, norm_mem_g, w_xq, w_xkv, w_xo, norm_ffn_g, w_up, ffn_conv_w, ffn_conv_b, w_down, norm_final_g):
    b, s, d = x.shape
    m = mem.shape[1]
    rows = b * s
    bias_tiles = _bias_tiles(rel_bias)
    w_in_l = _to_bf16(w_in, 0)
    for l in range(DEPTH):
        lambda_init = 0.8 - 0.6 * math.exp(-0.3 * l)
        ya, pq, pk, pv, pc = _mixer_in_proj(x, norm_mix_g[l], w_in_l, conv_w[l], conv_b[l], conv_ln_g[l],
                                            conv_ln_b[l])
        side_casts = [(w, l) for w in (w_br, w_out, w_xq, w_xkv, w_xo, w_up, w_down)]
        if l + 1 < DEPTH:
            side_casts.append((w_in, l + 1))
        yb, casts = _diff_attention(pq, pk, pv, bias_tiles, lam_q1[l], lam_k1[l], lam_q2[l], lam_k2[l],
                                    subln_g[l], lambda_init, side_casts)
        w_br_l, w_out_l, w_xq_l, w_xkv_l, w_xo_l, w_up_l, w_down_l = casts[:7]
        yc = _gmlp_branch(pc.reshape(rows, COL_C), gmlp_ln_g[l], gmlp_ln_b[l], w_s[l], b_s[l])
        x = _merge(x.reshape(rows, d), ya.reshape(rows, BRANCH_D), yb.reshape(rows, BRANCH_D), yc, norm_mix_g[l],
                   w_in_l, gate_b[l], w_br_l, w_out_l).reshape(b, s, d)
        (kv,) = _norm_proj(mem.reshape(b * m, d), norm_mem_g[l], w_xkv_l, (2 * XATTN_D,))
        x = _mem_cross_attention(x, kv.reshape(b, m, 2 * XATTN_D), norm_xattn_g[l], w_xq_l, w_xo_l)
        x = _conv_ffn(x, norm_ffn_g[l], w_up_l, ffn_conv_w[l], ffn_conv_b[l], w_down_l, norm_final_g,
                      final_norm=(l == DEPTH - 1))
        if l + 1 < DEPTH:
            w_in_l = casts[7]
    return x
```

```python
import functools
import math
from typing import NamedTuple

import jax
import jax.numpy as jnp
from jax import lax
from jax.experimental import pallas as pl
from jax.experimental.pallas import tpu as pltpu

F32 = jnp.float32
BF16 = jnp.bfloat16

D_MODEL = 1024
DEPTH = 2
CHUNK = 64
BRANCH_D = D_MODEL // 2
N_BRANCH = 3
CONV_K = 31
DIFF_HEADS = 4
DIFF_DH = D_MODEL // 16
DIFF_VD = 2 * DIFF_DH
GMLP_GROUPS = 4
GMLP_GD = BRANCH_D // GMLP_GROUPS
GMLP_CHUNK = 128
N_REL_BUCKETS = 32
REL_MAX_DIST = 128
XATTN_HEADS = 4
XATTN_DH = D_MODEL // 8
XATTN_D = XATTN_HEADS * XATTN_DH
D_FF = 2816
FFN_K = 3
EPS = 1e-6
NEG_INF = -1e30

COL_A = 2 * BRANCH_D
COL_QK = DIFF_HEADS * 2 * DIFF_DH
COL_V = DIFF_HEADS * DIFF_VD
COL_C = 2 * BRANCH_D
COL_G = N_BRANCH * D_MODEL
IN_SPLIT_WIDTHS = (COL_A, COL_QK, COL_QK, COL_V, COL_C)
COL_BRANCHES = sum(IN_SPLIT_WIDTHS)
LOG2_E = math.log2(math.e)
IN_SPLIT_SCALES = (1.0, DIFF_DH ** -0.5 * LOG2_E, 1.0, 1.0, 1.0)

LANES = 128
MXU_WIDTH = 256
F32_SUBLANES = 8
BF16_SUBLANES = 16
VMEM_LIMIT_BYTES = 56 * 1024 * 1024
CAST_BLOCK_BYTES = 4 * 1024 * 1024

ROW_TILE = 512
WIDE_ROW_TILE = 1024
FFN_ROW_TILE = 1024
PROJ_COL_CHUNK = 512
ATTN_TILE = 256
CONV_HALO = 32
CONV_ROW_CHUNK = 64
FFN_HALO = BF16_SUBLANES
FFN_COL_CHUNK = MXU_WIDTH

assert CONV_HALO >= CONV_K - 1 and FFN_HALO == F32_SUBLANES * (FFN_K - 1)
assert D_FF % FFN_COL_CHUNK == 0
assert ATTN_TILE % CHUNK == 0 and ATTN_TILE > REL_MAX_DIST


def _params(*semantics):
    return pltpu.CompilerParams(dimension_semantics=semantics, vmem_limit_bytes=VMEM_LIMIT_BYTES)


def _resident(shape):
    zeros = (0,) * len(shape)
    return pl.BlockSpec(shape, lambda *_: zeros, pipeline_mode=pl.Buffered(1))


class _LayerWeight(NamedTuple):
    stacked: jax.Array
    layer: int

    @property
    def shape(self):
        return self.stacked.shape[1:]


def _resident_layer(w):
    index = (w.layer,) + (0,) * len(w.shape)
    return pl.BlockSpec((None,) + tuple(w.shape), lambda *_: index, pipeline_mode=pl.Buffered(1))


def _cast_kernel(w_ref, o_ref):
    o_ref[...] = w_ref[...].astype(BF16)


def _to_bf16(w, layer):
    _, rows, cols = w.shape
    block_rows = max(r for r in range(BF16_SUBLANES, rows + 1, BF16_SUBLANES)
                     if rows % r == 0 and r * cols * 4 <= CAST_BLOCK_BYTES)
    out = pl.pallas_call(
        _cast_kernel,
        grid=(rows // block_rows,),
        in_specs=[pl.BlockSpec((None, block_rows, cols), lambda i: (layer, i, 0))],
        out_specs=pl.BlockSpec((None, block_rows, cols), lambda i: (0, i, 0)),
        out_shape=jax.ShapeDtypeStruct((1, rows, cols), BF16),
        compiler_params=_params("parallel"),
        name="weight_cast",
    )(w)
    return _LayerWeight(out, 0)


def _rms_norm(x, g):
    return x * lax.rsqrt(jnp.mean(x * x, axis=-1, keepdims=True) + EPS) * g


def _layer_norm(x, g, b):
    mu = jnp.mean(x, axis=-1, keepdims=True)
    xc = x - mu
    var = jnp.mean(xc * xc, axis=-1, keepdims=True)
    return xc * lax.rsqrt(var + EPS) * g + b


def _dot(a, b):
    return jnp.dot(a, b, preferred_element_type=F32)


def _dot_nt(a, b):
    return lax.dot_general(a, b, (((1,), (1,)), ((), ())), preferred_element_type=F32)


def _norm_proj_kernel(x_ref, g_ref, w_ref, *out_refs, widths, scales):
    xn = _rms_norm(x_ref[...], g_ref[...]).astype(BF16)
    col = 0
    for o_ref, width, scale in zip(out_refs, widths, scales):
        for c in range(0, width, PROJ_COL_CHUNK):
            out = _dot(xn, w_ref[:, col + c:col + c + PROJ_COL_CHUNK])
            if scale != 1.0:
                out = out * scale
            o_ref[:, c:c + PROJ_COL_CHUNK] = out.astype(BF16)
        col += width


def _norm_proj(x, g, w, widths, scales=None, row_tile=ROW_TILE):
    rows, d = x.shape
    scales = scales or (1.0,) * len(widths)
    assert w.shape == (d, sum(widths)) and all(wd % PROJ_COL_CHUNK == 0 for wd in widths)
    return pl.pallas_call(
        functools.partial(_norm_proj_kernel, widths=widths, scales=scales),
        grid=(rows // row_tile,),
        in_specs=[pl.BlockSpec((row_tile, d), lambda i: (i, 0)), _resident((1, d)), _resident_layer(w)],
        out_specs=[pl.BlockSpec((row_tile, wd), lambda i: (i, 0)) for wd in widths],
        out_shape=[jax.ShapeDtypeStruct((rows, wd), BF16) for wd in widths],
        compiler_params=_params("parallel"),
        name="norm_proj",
    )(x, g.reshape(1, d), w.stacked)


def _mixer_in_kernel(x_ref, g_ref, w_ref, cw_ref, cb_ref, lng_ref, lnb_ref,
                     ya_ref, pq_ref, pk_ref, pv_ref, pc_ref, xn_scr, xs_scr, halo_scr):
    rows = ya_ref.shape[1]
    xn_scr[...] = _rms_norm(x_ref[0], g_ref[...]).astype(BF16)

    def proj(col, width):
        return _dot(xn_scr[...], w_ref[:, col:col + width])

    glu = proj(0, BRANCH_D) * jax.nn.sigmoid(proj(BRANCH_D, BRANCH_D))

    @pl.when(pl.program_id(1) == 0)
    def _():
        halo_scr[...] = jnp.zeros(halo_scr.shape, F32)

    xs_scr[0, :CONV_HALO, :] = halo_scr[...]
    xs_scr[0, CONV_HALO:, :] = glu
    halo_scr[...] = glu[rows - CONV_HALO:, :]

    def shifted_copies():
        total = CONV_HALO + rows
        base = xs_scr[0]
        for b in range(1, F32_SUBLANES):
            xs_scr[b] = pltpu.roll(base, total - b, 0)

    first_tap = CONV_HALO - (CONV_K - 1)

    def conv_rows(r):
        acc = jnp.broadcast_to(cb_ref[...], (CONV_ROW_CHUNK, BRANCH_D))
        for k in range(CONV_K):
            b = (first_tap + k) % F32_SUBLANES
            start = r + first_tap + k - b
            acc = acc + cw_ref[k:k + 1, :] * xs_scr[b, start:start + CONV_ROW_CHUNK, :]
        ya_ref[0, r:r + CONV_ROW_CHUNK, :] = jax.nn.silu(_layer_norm(acc, lng_ref[...], lnb_ref[...])).astype(BF16)

    def proj_chunk(o_ref, col, c, scale):
        out = proj(col + c, PROJ_COL_CHUNK)
        if scale != 1.0:
            out = out * scale
        o_ref[0, :, c:c + PROJ_COL_CHUNK] = out.astype(BF16)

    col = COL_A
    for o_ref, width, scale in zip((pq_ref, pk_ref, pv_ref, pc_ref), IN_SPLIT_WIDTHS[1:], IN_SPLIT_SCALES[1:]):
        for c in range(0, width, PROJ_COL_CHUNK):
            proj_chunk(o_ref, col, c, scale)
        col += width
    shifted_copies()
    for r in range(0, rows, CONV_ROW_CHUNK):
        conv_rows(r)


def _mixer_in_proj(x, norm_g, w_in, conv_w, conv_b, ln_g, ln_b, row_tile=ROW_TILE):
    b, s, d = x.shape
    vec = lambda a: a.reshape(1, -1)
    out_widths = (BRANCH_D,) + IN_SPLIT_WIDTHS[1:]
    return pl.pallas_call(
        _mixer_in_kernel,
        grid=(b, s // row_tile),
        in_specs=[
            pl.BlockSpec((1, row_tile, d), lambda bi, i: (bi, i, 0)),
            _resident((1, d)), _resident_layer(w_in),
            _resident((CONV_K, BRANCH_D)), _resident((1, BRANCH_D)), _resident((1, BRANCH_D)),
            _resident((1, BRANCH_D)),
        ],
        out_specs=[pl.BlockSpec((1, row_tile, wd), lambda bi, i: (bi, i, 0)) for wd in out_widths],
        out_shape=[jax.ShapeDtypeStruct((b, s, wd), BF16) for wd in out_widths],
        scratch_shapes=[
            pltpu.VMEM((row_tile, d), BF16),
            pltpu.VMEM((F32_SUBLANES, CONV_HALO + row_tile, BRANCH_D), F32),
            pltpu.VMEM((CONV_HALO, BRANCH_D), F32),
        ],
        compiler_params=_params("parallel", "arbitrary"),
        name="mixer_in_proj",
    )(x, vec(norm_g), w_in.stacked, conv_w, vec(conv_b), vec(ln_g), vec(ln_b))


def _rel_bucket(rel):
    half = N_REL_BUCKETS // 2
    max_exact = half // 2
    n = jnp.abs(rel)
    log_ratio = jnp.log(jnp.maximum(n, 1).astype(F32) / max_exact) / math.log(REL_MAX_DIST / max_exact)
    large = jnp.minimum(max_exact + (log_ratio * (half - max_exact)).astype(jnp.int32), half - 1)
    return jnp.where(rel > 0, half, 0) + jnp.where(n < max_exact, n, large)


FAR_BUCKET = N_REL_BUCKETS // 2 - 1
MASKED_BUCKET = -1


def _near_buckets():
    r = jnp.arange(ATTN_TILE)[:, None]
    c = jnp.arange(ATTN_TILE)[None, :]
    visible = (c // CHUNK) <= (r // CHUNK)
    diag = jnp.where(visible, _rel_bucket(c - r), MASKED_BUCKET)
    sub = _rel_bucket(c - r - ATTN_TILE)
    return jnp.stack([diag, sub]).astype(jnp.int32)


def _bias_tiles_kernel(table_ref, bucket_ref, o_ref):
    h = pl.program_id(0)
    bucket = bucket_ref[...]
    far = table_ref[FAR_BUCKET, h]
    out = jnp.zeros(bucket.shape, F32)
    for b in range(N_REL_BUCKETS):
        out = jnp.where(bucket == b, (table_ref[b, h] - far) * LOG2_E, out)
    o_ref[0] = jnp.where(bucket == MASKED_BUCKET, NEG_INF, out)


def _bias_tiles(rel_bias):
    buckets = _near_buckets()
    return pl.pallas_call(
        _bias_tiles_kernel,
        grid=(DIFF_HEADS,),
        in_specs=[pl.BlockSpec(memory_space=pltpu.SMEM), _resident(buckets.shape)],
        out_specs=pl.BlockSpec((1,) + buckets.shape, lambda h: (h, 0, 0, 0)),
        out_shape=jax.ShapeDtypeStruct((DIFF_HEADS,) + buckets.shape, F32),
        compiler_params=_params("arbitrary"),
        name="rel_bias_tiles",
    )(rel_bias.astype(F32), buckets)


def _diff_attn_kernel(*refs, lambda_init, n_casts):
    (q_ref, k_ref, v_ref, bias_ref, lq1_ref, lk1_ref, lq2_ref, lk2_ref, g_ref), refs = refs[:9], refs[9:]
    cast_src, o_ref, cast_dst = refs[:n_casts], refs[n_casts], refs[n_casts + 1:2 * n_casts + 1]
    q2_scr, s_scr, m_scr, acc_scr, vones_scr = refs[2 * n_casts + 1:]

    def side_casts():
        for src, dst in zip(cast_src, cast_dst):
            dst[...] = src[...].astype(BF16)

    _diff_attn_body(q_ref, k_ref, v_ref, bias_ref, lq1_ref, lk1_ref, lq2_ref, lk2_ref, g_ref, o_ref,
                    q2_scr, s_scr, m_scr, acc_scr, vones_scr, side_casts, lambda_init=lambda_init)


def _diff_attn_body(q_ref, k_ref, v_ref, bias_ref, lq1_ref, lk1_ref, lq2_ref, lk2_ref, g_ref, o_ref,
                    q2_scr, s_scr, m_scr, acc_scr, vones_scr, side_casts, *, lambda_init):
    t = ATTN_TILE
    i = pl.program_id(1)
    heads = range(DIFF_HEADS)

    def head_cols(h):
        return slice(h * DIFF_VD, (h + 1) * DIFF_VD)

    def build_values_and_ones():
        for h in heads:
            vones_scr[h, :, :DIFF_VD] = v_ref[0, :, head_cols(h)]
            vones_scr[h, :, DIFF_VD:] = jnp.ones((vones_scr.shape[1], LANES), BF16)

    def stack_queries():
        q = q_ref[0]
        lane = lax.broadcasted_iota(jnp.int32, (t, DIFF_VD), 1)
        for h in heads:
            qh = q[:, head_cols(h)]
            q2_scr[h, :t, :] = jnp.where(lane < DIFF_DH, qh, 0).astype(BF16)
            q2_scr[h, t:, :] = jnp.where(lane >= DIFF_DH, qh, 0).astype(BF16)

    def logits_tiles(near):
        for n, bias_index in enumerate(near):
            for h in heads:
                s = _dot_nt(q2_scr[h], k_ref[0, n * t:(n + 1) * t, head_cols(h)])
                if bias_index is not None:
                    bias = bias_ref[h, bias_index]
                    s = s + jnp.concatenate([bias, bias], axis=0)
                s_scr[n, h] = s
                lane_max = jnp.maximum(s[:, :LANES], s[:, LANES:])
                m_scr[h] = lane_max if n == 0 else jnp.maximum(m_scr[h], lane_max)

    def pv_tiles(n_tiles):
        for h in heads:
            m = m_scr[h]
            parts = [jnp.exp2(s_scr[n, h, :, half * LANES:(half + 1) * LANES] - m)
                     for n in range(n_tiles) for half in range(t // LANES)]
            p = jnp.concatenate(parts, axis=1).astype(BF16)
            acc_scr[h] = _dot(p, vones_scr[h, :n_tiles * t, :])

    def combine_and_norm():
        lam = (jnp.exp(jnp.sum(lq1_ref[...] * lk1_ref[...])) - jnp.exp(jnp.sum(lq2_ref[...] * lk2_ref[...]))
               + lambda_init)
        for h in heads:
            o = (acc_scr[h, :t, :DIFF_VD] / acc_scr[h, :t, DIFF_VD:]
                 - lam * (acc_scr[h, t:, :DIFF_VD] / acc_scr[h, t:, DIFF_VD:]))
            o_ref[0, :, head_cols(h)] = (_rms_norm(o, g_ref[...]) * (1.0 - lambda_init)).astype(BF16)

    def query_tile(tile):
        side_casts()
        if tile == 0:
            build_values_and_ones()
        stack_queries()
        logits_tiles((None,) * (tile - 1) + ((1, 0) if tile > 0 else (0,)))
        for h in heads:
            m_scr[h] = jnp.broadcast_to(jnp.max(m_scr[h], axis=-1, keepdims=True), (2 * t, LANES))
        pv_tiles(tile + 1)
        combine_and_norm()

    for tile in range(s_scr.shape[0]):
        pl.when(i == tile)(functools.partial(query_tile, tile))


def _diff_attention(pq, pk, pv, bias_tiles, lam_q1, lam_k1, lam_q2, lam_k2, subln_g, lambda_init, side_casts):
    b, s, width = pq.shape
    t = ATTN_TILE
    tiles = s // t
    steps = b * tiles
    vec = lambda a: a.reshape(1, -1).astype(F32)
    all_keys = pl.BlockSpec((1, s, width), lambda bi, i: (bi, 0, 0))
    lam_spec = _resident((1, DIFF_DH))

    cast_views, cast_in_specs, cast_out_specs, cast_out_shapes = [], [], [], []
    for w, layer in side_casts:
        view = w.reshape(w.shape[0], -1, w.shape[-1])
        _, rows, cols = view.shape
        chunks = max(c for c in range(1, steps + 1) if rows % (c * BF16_SUBLANES) == 0)

        def chunk(bi, i, chunks=chunks):
            return jnp.minimum(bi * tiles + i, chunks - 1)

        cast_views.append(view)
        cast_in_specs.append(pl.BlockSpec((None, rows // chunks, cols),
                                          lambda bi, i, layer=layer, chunk=chunk: (layer, chunk(bi, i), 0)))
        cast_out_specs.append(pl.BlockSpec((rows // chunks, cols), lambda bi, i, chunk=chunk: (chunk(bi, i), 0)))
        cast_out_shapes.append(jax.ShapeDtypeStruct((rows, cols), BF16))

    out, *casts = pl.pallas_call(
        functools.partial(_diff_attn_kernel, lambda_init=lambda_init, n_casts=len(side_casts)),
        grid=(b, tiles),
        in_specs=[
            pl.BlockSpec((1, t, width), lambda bi, i: (bi, i, 0)),
            all_keys, all_keys, _resident(bias_tiles.shape),
            lam_spec, lam_spec, lam_spec, lam_spec, _resident((1, DIFF_VD)),
        ] + cast_in_specs,
        out_specs=[pl.BlockSpec((1, t, width), lambda bi, i: (bi, i, 0))] + cast_out_specs,
        out_shape=[jax.ShapeDtypeStruct((b, s, width), BF16)] + cast_out_shapes,
        scratch_shapes=[
            pltpu.VMEM((DIFF_HEADS, 2 * t, DIFF_VD), BF16),
            pltpu.VMEM((s // t, DIFF_HEADS, 2 * t, t), F32),
            pltpu.VMEM((DIFF_HEADS, 2 * t, LANES), F32),
            pltpu.VMEM((DIFF_HEADS, 2 * t, DIFF_VD + LANES), F32),
            pltpu.VMEM((DIFF_HEADS, s, DIFF_VD + LANES), BF16),
        ],
        compiler_params=_params("parallel", "arbitrary"),
        name="diff_attention",
    )(pq, pk, pv, bias_tiles, vec(lam_q1), vec(lam_k1), vec(lam_q2), vec(lam_k2), vec(subln_g), *cast_views)
    casts = [_LayerWeight(c.reshape((1,) + w.shape[1:]), 0) for c, (w, _) in zip(casts, side_casts)]
    return out, casts


def _gmlp_kernel(pc_ref, lng_ref, lnb_ref, ws_ref, bs_ref, o_ref):
    rows = o_ref.shape[0]
    z = jax.nn.gelu(pc_ref[...].astype(F32))
    u = z[:, :BRANCH_D]
    v = _layer_norm(z[:, BRANCH_D:], lng_ref[...], lnb_ref[...]).astype(BF16)
    t_idx = lax.broadcasted_iota(jnp.int32, (GMLP_CHUNK, GMLP_CHUNK), 0)
    s_idx = lax.broadcasted_iota(jnp.int32, (GMLP_CHUNK, GMLP_CHUNK), 1)
    for g in range(GMLP_GROUPS):
        w = jnp.where(s_idx <= t_idx, ws_ref[g], 0.0).astype(BF16)
        cols = slice(g * GMLP_GD, (g + 1) * GMLP_GD)
        for r in range(0, rows, GMLP_CHUNK):
            sv = _dot(w, v[r:r + GMLP_CHUNK, cols]) + bs_ref[g]
            o_ref[r:r + GMLP_CHUNK, cols] = (u[r:r + GMLP_CHUNK, cols] * sv).astype(BF16)


def _gmlp_branch(pc, ln_g, ln_b, w_s, b_s, row_tile=WIDE_ROW_TILE):
    rows = pc.shape[0]
    vec = lambda a: a.reshape(1, BRANCH_D)
    return pl.pallas_call(
        _gmlp_kernel,
        grid=(rows // row_tile,),
        in_specs=[
            pl.BlockSpec((row_tile, COL_C), lambda i: (i, 0)),
            _resident((1, BRANCH_D)), _resident((1, BRANCH_D)),
            _resident(w_s.shape), _resident((GMLP_GROUPS, GMLP_CHUNK, 1)),
        ],
        out_specs=pl.BlockSpec((row_tile, BRANCH_D), lambda i: (i, 0)),
        out_shape=jax.ShapeDtypeStruct((rows, BRANCH_D), BF16),
        compiler_params=_params("parallel"),
        name="gmlp_branch",
    )(pc, vec(ln_g), vec(ln_b), w_s, b_s.reshape(GMLP_GROUPS, GMLP_CHUNK, 1))


def _merge_kernel(x_ref, ya_ref, yb_ref, yc_ref, g_ref, win_ref, gb_ref, wbr_ref, wout_ref, o_ref):
    x = x_ref[...]
    xn = _rms_norm(x, g_ref[...]).astype(BF16)
    merged = None
    for n, y_ref in enumerate((ya_ref, yb_ref, yc_ref)):
        col = COL_BRANCHES + n * D_MODEL
        logits = _dot(xn, win_ref[:, col:col + D_MODEL]) + gb_ref[n:n + 1, :]
        term = jax.nn.sigmoid(logits) * _dot(y_ref[...], wbr_ref[n])
        merged = term if merged is None else merged + term
    o_ref[...] = x + _dot(merged.astype(BF16), wout_ref[...])


def _merge(x, ya, yb, yc, norm_g, w_in, gate_b, w_br, w_out, row_tile=WIDE_ROW_TILE):
    rows, d = x.shape
    row_spec = lambda width: pl.BlockSpec((row_tile, width), lambda i: (i, 0))
    return pl.pallas_call(
        _merge_kernel,
        grid=(rows // row_tile,),
        in_specs=[row_spec(D_MODEL), row_spec(BRANCH_D), row_spec(BRANCH_D), row_spec(BRANCH_D),
                  _resident((1, d)), _resident_layer(w_in), _resident(gate_b.shape), _resident_layer(w_br),
                  _resident_layer(w_out)],
        out_specs=row_spec(D_MODEL),
        out_shape=jax.ShapeDtypeStruct(x.shape, F32),
        compiler_params=_params("parallel"),
        name="branch_merge",
    )(x, ya, yb, yc, norm_g.reshape(1, d), w_in.stacked, gate_b, w_br.stacked, w_out.stacked)


def _xattn_kernel(x_ref, g_ref, wq_ref, kv_ref, wo_ref, o_ref):
    x = x_ref[0]
    q = _dot(_rms_norm(x, g_ref[...]).astype(BF16), wq_ref[...]) * (XATTN_DH ** -0.5 * LOG2_E)
    q = q.astype(BF16)
    heads = []
    for h in range(XATTN_HEADS):
        cols = slice(h * XATTN_DH, (h + 1) * XATTN_DH)
        k = kv_ref[0, :, cols]
        v = kv_ref[0, :, XATTN_D + h * XATTN_DH:XATTN_D + (h + 1) * XATTN_DH]
        s = _dot_nt(q[:, cols], k)
        p = jnp.exp2(s - jnp.max(s, axis=-1, keepdims=True))
        o = _dot(p.astype(BF16), v) / jnp.sum(p, axis=-1, keepdims=True)
        heads.append(o.astype(BF16))
    o_ref[0] = x + _dot(jnp.concatenate(heads, axis=-1), wo_ref[...])


def _mem_cross_attention(x, kv, norm_g, w_xq, w_xo, row_tile=WIDE_ROW_TILE):
    b, s, d = x.shape
    m = kv.shape[1]
    return pl.pallas_call(
        _xattn_kernel,
        grid=(b, s // row_tile),
        in_specs=[
            pl.BlockSpec((1, row_tile, d), lambda bi, i: (bi, i, 0)),
            _resident((1, d)), _resident_layer(w_xq),
            pl.BlockSpec((1, m, 2 * XATTN_D), lambda bi, i: (bi, 0, 0)),
            _resident_layer(w_xo),
        ],
        out_specs=pl.BlockSpec((1, row_tile, d), lambda bi, i: (bi, i, 0)),
        out_shape=jax.ShapeDtypeStruct(x.shape, F32),
        compiler_params=_params("parallel", "parallel"),
        name="mem_cross_attention",
    )(x, norm_g.reshape(1, d), w_xq.stacked, kv, w_xo.stacked)


def _ffn_kernel(x_ref, halo_ref, g_ref, wup_ref, cw_ref, cb_ref, wdn_ref, gf_ref, o_ref,
                slab_scr, h_scr, up_scr, act_scr, acc_scr, *, final_norm):
    tile_rows = o_ref.shape[1]
    rows = acc_scr.shape[0]
    groups = rows // F32_SUBLANES
    slabs = slab_scr.shape[0]

    def lanes(j):
        return slice(j * LANES, (j + 1) * LANES)

    for j in range(slabs):
        slab_scr[j, :tile_rows, :] = x_ref[0, :, lanes(j)]
    slab_scr[:, tile_rows:, :] = jnp.zeros((slabs, rows - tile_rows, LANES), F32)
    x = jnp.concatenate(
        [jnp.concatenate([slab_scr[j, pl.ds(gi, F32_SUBLANES, stride=groups), :] for j in range(slabs)], axis=1)
         for gi in range(groups)], axis=0)

    g = g_ref[...]
    halo = _rms_norm(halo_ref[0], g)
    h_scr[:FFN_HALO, :] = jnp.where(pl.program_id(1) > 0, halo, 0.0).astype(BF16)
    h_scr[FFN_HALO:, :] = _rms_norm(x, g).astype(BF16)
    acc_scr[...] = x

    n_chunks = D_FF // FFN_COL_CHUNK
    first_sublane = lax.broadcasted_iota(jnp.int32, (F32_SUBLANES, FFN_COL_CHUNK), 0) == 0

    def cols(c, half):
        start = half * D_FF + c * FFN_COL_CHUNK
        return slice(start, start + FFN_COL_CHUNK)

    def up_proj(c, slot):
        h = h_scr[...]
        for half in range(2):
            up = up_scr.at[slot, half]
            up[...] = _dot(h, wup_ref[:, cols(c, half)])
            prev_tile = up[FFN_HALO - F32_SUBLANES:FFN_HALO, :]
            for shift in range(1, FFN_K):
                wrapped = up[FFN_HALO + rows - shift * F32_SUBLANES:FFN_HALO + rows - (shift - 1) * F32_SUBLANES, :]
                block = jnp.where(first_sublane, pltpu.roll(prev_tile, shift, 0), pltpu.roll(wrapped, 1, 0))
                up[FFN_HALO - shift * F32_SUBLANES:FFN_HALO - (shift - 1) * F32_SUBLANES, :] = block

    def conv(c, slot, half):
        out = cb_ref[:, cols(c, half)]
        for k in range(FFN_K):
            window = up_scr[slot, half, k * F32_SUBLANES:k * F32_SUBLANES + rows, :]
            out = out + cw_ref[k:k + 1, cols(c, half)] * window
        return out

    def gate(c, slot):
        act_scr[slot] = (jax.nn.silu(conv(c, slot, 0)) * conv(c, slot, 1)).astype(BF16)

    def down_proj(c, slot):
        acc_scr[...] += _dot(act_scr[slot], wdn_ref[cols(c, 0), :])

    def stage(c, slot):
        up_proj(c + 1, 1 - slot)
        down_proj(c - 1, 1 - slot)
        gate(c, slot)

    up_proj(0, 0)
    up_proj(1, 1)
    gate(0, 0)
    for c in range(1, n_chunks - 1):
        stage(c, c % 2)
    down_proj(n_chunks - 2, (n_chunks - 2) % 2)
    gate(n_chunks - 1, (n_chunks - 1) % 2)
    down_proj(n_chunks - 1, (n_chunks - 1) % 2)

    for gi in range(groups):
        out = acc_scr[gi * F32_SUBLANES:(gi + 1) * F32_SUBLANES, :]
        if final_norm:
            out = _rms_norm(out, gf_ref[...])
        for j in range(slabs):
            slab_scr[j, pl.ds(gi, F32_SUBLANES, stride=groups), :] = out[:, lanes(j)]
    for j in range(slabs):
        o_ref[0, :, lanes(j)] = slab_scr[j, :tile_rows, :]


def _conv_ffn(x, norm_g, w_up, conv_w, conv_b, w_down, final_g, final_norm, row_tile=FFN_ROW_TILE):
    b, s, d = x.shape
    halo_blocks = row_tile // FFN_HALO
    conv_b = conv_b.reshape(1, 2 * D_FF)
    perm_rows = row_tile + 2 * F32_SUBLANES
    return pl.pallas_call(
        functools.partial(_ffn_kernel, final_norm=final_norm),
        grid=(b, s // row_tile),
        in_specs=[
            pl.BlockSpec((1, row_tile, d), lambda bi, i: (bi, i, 0)),
            pl.BlockSpec((1, FFN_HALO, d), lambda bi, i: (bi, jnp.maximum(i * halo_blocks - 1, 0), 0)),
            _resident((1, d)), _resident_layer(w_up), _resident(conv_w.shape), _resident(conv_b.shape),
            _resident_layer(w_down), _resident((1, d)),
        ],
        out_specs=pl.BlockSpec((1, row_tile, d), lambda bi, i: (bi, i, 0)),
        out_shape=jax.ShapeDtypeStruct(x.shape, F32),
        scratch_shapes=[
            pltpu.VMEM((d // LANES, perm_rows, LANES), F32),
            pltpu.VMEM((FFN_HALO + perm_rows, d), BF16),
            pltpu.VMEM((2, 2, FFN_HALO + perm_rows, FFN_COL_CHUNK), F32),
            pltpu.VMEM((2, perm_rows, FFN_COL_CHUNK), BF16),
            pltpu.VMEM((perm_rows, d), F32),
        ],
        compiler_params=_params("parallel", "parallel"),
        name="conv_ffn",
    )(x, x, norm_g.reshape(1, d), w_up.stacked, conv_w, conv_b, w_down.stacked, final_g.reshape(1, d))


def kernel(x, mem, rel_bias, norm_mix_g, w_in, gate_b, conv_w, conv_b, conv_ln_g, conv_ln_b, lam_q1, lam_k1, lam_q2, lam_k2, subln_g, gmlp_ln_g, gmlp_ln_b, w_s, b_s, w_br, w_out, norm_xattn_g, norm_mem_g, w_xq, w_xkv, w_xo, norm_ffn_g, w_up, ffn_conv_w, ffn_conv_b, w_down, norm_final_g):
    b, s, d = x.shape
    m = mem.shape[1]
    rows = b * s
    bias_tiles = _bias_tiles(rel_bias)
    w_in_l = _to_bf16(w_in, 0)
    for l in range(DEPTH):
        lambda_init = 0.8 - 0.6 * math.exp(-0.3 * l)
        ya, pq, pk, pv, pc = _mixer_in_proj(x, norm_mix_g[l], w_in_l, conv_w[l], conv_b[l], conv_ln_g[l],
                                            conv_ln_b[l])
        side_casts = [(w, l) for w in (w_br, w_out, w_xq, w_xkv, w_xo, w_up, w_down)]
        if l + 1 < DEPTH:
            side_casts.append((w_in, l + 1))
        yb, casts = _diff_attention(pq, pk, pv, bias_tiles, lam_q1[l], lam_k1[l], lam_q2[l], lam_k2[l],
                                    subln_g[l], lambda_init, side_casts)
        w_br_l, w_out_l, w_xq_l, w_xkv_l, w_xo_l, w_up_l, w_down_l = casts[:7]
        yc = _gmlp_branch(pc.reshape(rows, COL_C), gmlp_ln_g[l], gmlp_ln_b[l], w_s[l], b_s[l])
        x = _merge(x.reshape(rows, d), ya.reshape(rows, BRANCH_D), yb.reshape(rows, BRANCH_D), yc, norm_mix_g[l],
                   w_in_l, gate_b[l], w_br_l, w_out_l).reshape(b, s, d)
        (kv,) = _norm_proj(mem.reshape(b * m, d), norm_mem_g[l], w_xkv_l, (2 * XATTN_D,))
        x = _mem_cross_attention(x, kv.reshape(b, m, 2 * XATTN_D), norm_xattn_g[l], w_xq_l, w_xo_l)
        x = _conv_ffn(x, norm_ffn_g[l], w_up_l, ffn_conv_w[l], ffn_conv_b[l], w_down_l, norm_final_g,
                      final_norm=(l == DEPTH - 1))
        if l + 1 < DEPTH:
            w_in_l = casts[7]
    return x
```

```python
import functools
import math
from typing import NamedTuple

import jax
import jax.numpy as jnp
from jax import lax
from jax.experimental import pallas as pl
from jax.experimental.pallas import tpu as pltpu

F32 = jnp.float32
BF16 = jnp.bfloat16

D_MODEL = 1024
DEPTH = 2
CHUNK = 64
BRANCH_D = D_MODEL // 2
N_BRANCH = 3
CONV_K = 31
DIFF_HEADS = 4
DIFF_DH = D_MODEL // 16
DIFF_VD = 2 * DIFF_DH
GMLP_GROUPS = 4
GMLP_GD = BRANCH_D // GMLP_GROUPS
GMLP_CHUNK = 128
N_REL_BUCKETS = 32
REL_MAX_DIST = 128
XATTN_HEADS = 4
XATTN_DH = D_MODEL // 8
XATTN_D = XATTN_HEADS * XATTN_DH
D_FF = 2816
FFN_K = 3
EPS = 1e-6
NEG_INF = -1e30

COL_A = 2 * BRANCH_D
COL_QK = DIFF_HEADS * 2 * DIFF_DH
COL_V = DIFF_HEADS * DIFF_VD
COL_C = 2 * BRANCH_D
COL_G = N_BRANCH * D_MODEL
IN_SPLIT_WIDTHS = (COL_A, COL_QK, COL_QK, COL_V, COL_C)
COL_BRANCHES = sum(IN_SPLIT_WIDTHS)
LOG2_E = math.log2(math.e)
IN_SPLIT_SCALES = (1.0, DIFF_DH ** -0.5 * LOG2_E, 1.0, 1.0, 1.0)

LANES = 128
MXU_WIDTH = 256
F32_SUBLANES = 8
BF16_SUBLANES = 16
VMEM_LIMIT_BYTES = 56 * 1024 * 1024
CAST_BLOCK_BYTES = 4 * 1024 * 1024

ROW_TILE = 512
WIDE_ROW_TILE = 1024
FFN_ROW_TILE = 512
PROJ_COL_CHUNK = 512
ATTN_TILE = 256
CONV_HALO = 32
CONV_ROW_CHUNK = 64
FFN_HALO = BF16_SUBLANES
FFN_COL_CHUNK = MXU_WIDTH

assert CONV_HALO >= CONV_K - 1 and FFN_HALO == F32_SUBLANES * (FFN_K - 1)
assert D_FF % FFN_COL_CHUNK == 0
assert ATTN_TILE % CHUNK == 0 and ATTN_TILE > REL_MAX_DIST


def _params(*semantics):
    return pltpu.CompilerParams(dimension_semantics=semantics, vmem_limit_bytes=VMEM_LIMIT_BYTES)


def _resident(shape):
    zeros = (0,) * len(shape)
    return pl.BlockSpec(shape, lambda *_: zeros, pipeline_mode=pl.Buffered(1))


class _LayerWeight(NamedTuple):
    stacked: jax.Array
    layer: int

    @property
    def shape(self):
        return self.stacked.shape[1:]


def _resident_layer(w):
    index = (w.layer,) + (0,) * len(w.shape)
    return pl.BlockSpec((None,) + tuple(w.shape), lambda *_: index, pipeline_mode=pl.Buffered(1))


def _cast_kernel(w_ref, o_ref):
    o_ref[...] = w_ref[...].astype(BF16)


def _to_bf16(w, layer):
    _, rows, cols = w.shape
    block_rows = max(r for r in range(BF16_SUBLANES, rows + 1, BF16_SUBLANES)
                     if rows % r == 0 and r * cols * 4 <= CAST_BLOCK_BYTES)
    out = pl.pallas_call(
        _cast_kernel,
        grid=(rows // block_rows,),
        in_specs=[pl.BlockSpec((None, block_rows, cols), lambda i: (layer, i, 0))],
        out_specs=pl.BlockSpec((None, block_rows, cols), lambda i: (0, i, 0)),
        out_shape=jax.ShapeDtypeStruct((1, rows, cols), BF16),
        compiler_params=_params("parallel"),
        name="weight_cast",
    )(w)
    return _LayerWeight(out, 0)


def _rms_norm(x, g):
    return x * lax.rsqrt(jnp.mean(x * x, axis=-1, keepdims=True) + EPS) * g


def _layer_norm(x, g, b):
    mu = jnp.mean(x, axis=-1, keepdims=True)
    xc = x - mu
    var = jnp.mean(xc * xc, axis=-1, keepdims=True)
    return xc * lax.rsqrt(var + EPS) * g + b


def _dot(a, b):
    return jnp.dot(a, b, preferred_element_type=F32)


def _dot_nt(a, b):
    return lax.dot_general(a, b, (((1,), (1,)), ((), ())), preferred_element_type=F32)


def _norm_proj_kernel(x_ref, g_ref, w_ref, *out_refs, widths, scales):
    xn = _rms_norm(x_ref[...], g_ref[...]).astype(BF16)
    col = 0
    for o_ref, width, scale in zip(out_refs, widths, scales):
        for c in range(0, width, PROJ_COL_CHUNK):
            out = _dot(xn, w_ref[:, col + c:col + c + PROJ_COL_CHUNK])
            if scale != 1.0:
                out = out * scale
            o_ref[:, c:c + PROJ_COL_CHUNK] = out.astype(BF16)
        col += width


def _norm_proj(x, g, w, widths, scales=None, row_tile=ROW_TILE):
    rows, d = x.shape
    scales = scales or (1.0,) * len(widths)
    assert w.shape == (d, sum(widths)) and all(wd % PROJ_COL_CHUNK == 0 for wd in widths)
    return pl.pallas_call(
        functools.partial(_norm_proj_kernel, widths=widths, scales=scales),
        grid=(rows // row_tile,),
        in_specs=[pl.BlockSpec((row_tile, d), lambda i: (i, 0)), _resident((1, d)), _resident_layer(w)],
        out_specs=[pl.BlockSpec((row_tile, wd), lambda i: (i, 0)) for wd in widths],
        out_shape=[jax.ShapeDtypeStruct((rows, wd), BF16) for wd in widths],
        compiler_params=_params("parallel"),
        name="norm_proj",
    )(x, g.reshape(1, d), w.stacked)


def _mixer_in_kernel(x_ref, g_ref, w_ref, cw_ref, cb_ref, lng_ref, lnb_ref,
                     ya_ref, pq_ref, pk_ref, pv_ref, pc_ref, xn_scr, xs_scr, halo_scr):
    rows = ya_ref.shape[1]
    xn_scr[...] = _rms_norm(x_ref[0], g_ref[...]).astype(BF16)

    def proj(col, width):
        return _dot(xn_scr[...], w_ref[:, col:col + width])

    glu = proj(0, BRANCH_D) * jax.nn.sigmoid(proj(BRANCH_D, BRANCH_D))

    @pl.when(pl.program_id(1) == 0)
    def _():
        halo_scr[...] = jnp.zeros(halo_scr.shape, F32)

    xs_scr[0, :CONV_HALO, :] = halo_scr[...]
    xs_scr[0, CONV_HALO:, :] = glu
    halo_scr[...] = glu[rows - CONV_HALO:, :]

    def shifted_copies():
        total = CONV_HALO + rows
        base = xs_scr[0]
        for b in range(1, F32_SUBLANES):
            xs_scr[b] = pltpu.roll(base, total - b, 0)

    first_tap = CONV_HALO - (CONV_K - 1)

    def conv_rows(r):
        acc = jnp.broadcast_to(cb_ref[...], (CONV_ROW_CHUNK, BRANCH_D))
        for k in range(CONV_K):
            b = (first_tap + k) % F32_SUBLANES
            start = r + first_tap + k - b
            acc = acc + cw_ref[k:k + 1, :] * xs_scr[b, start:start + CONV_ROW_CHUNK, :]
        ya_ref[0, r:r + CONV_ROW_CHUNK, :] = jax.nn.silu(_layer_norm(acc, lng_ref[...], lnb_ref[...])).astype(BF16)

    def proj_chunk(o_ref, col, c, scale):
        out = proj(col + c, PROJ_COL_CHUNK)
        if scale != 1.0:
            out = out * scale
        o_ref[0, :, c:c + PROJ_COL_CHUNK] = out.astype(BF16)

    col = COL_A
    for o_ref, width, scale in zip((pq_ref, pk_ref, pv_ref, pc_ref), IN_SPLIT_WIDTHS[1:], IN_SPLIT_SCALES[1:]):
        for c in range(0, width, PROJ_COL_CHUNK):
            proj_chunk(o_ref, col, c, scale)
        col += width
    shifted_copies()
    for r in range(0, rows, CONV_ROW_CHUNK):
        conv_rows(r)


def _mixer_in_proj(x, norm_g, w_in, conv_w, conv_b, ln_g, ln_b, row_tile=ROW_TILE):
    b, s, d = x.shape
    vec = lambda a: a.reshape(1, -1)
    out_widths = (BRANCH_D,) + IN_SPLIT_WIDTHS[1:]
    return pl.pallas_call(
        _mixer_in_kernel,
        grid=(b, s // row_tile),
        in_specs=[
            pl.BlockSpec((1, row_tile, d), lambda bi, i: (bi, i, 0)),
            _resident((1, d)), _resident_layer(w_in),
            _resident((CONV_K, BRANCH_D)), _resident((1, BRANCH_D)), _resident((1, BRANCH_D)),
            _resident((1, BRANCH_D)),
        ],
        out_specs=[pl.BlockSpec((1, row_tile, wd), lambda bi, i: (bi, i, 0)) for wd in out_widths],
        out_shape=[jax.ShapeDtypeStruct((b, s, wd), BF16) for wd in out_widths],
        scratch_shapes=[
            pltpu.VMEM((row_tile, d), BF16),
            pltpu.VMEM((F32_SUBLANES, CONV_HALO + row_tile, BRANCH_D), F32),
            pltpu.VMEM((CONV_HALO, BRANCH_D), F32),
        ],
        compiler_params=_params("parallel", "arbitrary"),
        name="mixer_in_proj",
    )(x, vec(norm_g), w_in.stacked, conv_w, vec(conv_b), vec(ln_g), vec(ln_b))


def _rel_bucket(rel):
    half = N_REL_BUCKETS // 2
    max_exact = half // 2
    n = jnp.abs(rel)
    log_ratio = jnp.log(jnp.maximum(n, 1).astype(F32) / max_exact) / math.log(REL_MAX_DIST / max_exact)
    large = jnp.minimum(max_exact + (log_ratio * (half - max_exact)).astype(jnp.int32), half - 1)
    return jnp.where(rel > 0, half, 0) + jnp.where(n < max_exact, n, large)


FAR_BUCKET = N_REL_BUCKETS // 2 - 1
MASKED_BUCKET = -1


def _near_buckets():
    r = jnp.arange(ATTN_TILE)[:, None]
    c = jnp.arange(ATTN_TILE)[None, :]
    visible = (c // CHUNK) <= (r // CHUNK)
    diag = jnp.where(visible, _rel_bucket(c - r), MASKED_BUCKET)
    sub = _rel_bucket(c - r - ATTN_TILE)
    return jnp.stack([diag, sub]).astype(jnp.int32)


def _bias_tiles_kernel(table_ref, bucket_ref, o_ref):
    h = pl.program_id(0)
    bucket = bucket_ref[...]
    far = table_ref[FAR_BUCKET, h]
    out = jnp.zeros(bucket.shape, F32)
    for b in range(N_REL_BUCKETS):
        out = jnp.where(bucket == b, (table_ref[b, h] - far) * LOG2_E, out)
    o_ref[0] = jnp.where(bucket == MASKED_BUCKET, NEG_INF, out)


def _bias_tiles(rel_bias):
    buckets = _near_buckets()
    return pl.pallas_call(
        _bias_tiles_kernel,
        grid=(DIFF_HEADS,),
        in_specs=[pl.BlockSpec(memory_space=pltpu.SMEM), _resident(buckets.shape)],
        out_specs=pl.BlockSpec((1,) + buckets.shape, lambda h: (h, 0, 0, 0)),
        out_shape=jax.ShapeDtypeStruct((DIFF_HEADS,) + buckets.shape, F32),
        compiler_params=_params("arbitrary"),
        name="rel_bias_tiles",
    )(rel_bias.astype(F32), buckets)


def _diff_attn_kernel(*refs, lambda_init, cast_chunks):
    n_casts = len(cast_chunks)
    (q_ref, k_ref, v_ref, bias_ref, lq1_ref, lk1_ref, lq2_ref, lk2_ref, g_ref), refs = refs[:9], refs[9:]
    cast_src, o_ref, cast_dst = refs[:n_casts], refs[n_casts], refs[n_casts + 1:2 * n_casts + 1]
    q2_scr, s_scr, m_scr, acc_scr, vones_scr = refs[2 * n_casts + 1:]
    step = pl.program_id(0) * pl.num_programs(1) + pl.program_id(1)
    for src, dst, chunks in zip(cast_src, cast_dst, cast_chunks):
        @pl.when(step < chunks)
        def _():
            dst[...] = src[...].astype(BF16)

    _diff_attn_body(q_ref, k_ref, v_ref, bias_ref, lq1_ref, lk1_ref, lq2_ref, lk2_ref, g_ref, o_ref,
                    q2_scr, s_scr, m_scr, acc_scr, vones_scr, lambda_init=lambda_init)


def _diff_attn_body(q_ref, k_ref, v_ref, bias_ref, lq1_ref, lk1_ref, lq2_ref, lk2_ref, g_ref, o_ref,
                    q2_scr, s_scr, m_scr, acc_scr, vones_scr, *, lambda_init):
    t = ATTN_TILE
    i = pl.program_id(1)
    heads = range(DIFF_HEADS)

    def head_cols(h):
        return slice(h * DIFF_VD, (h + 1) * DIFF_VD)

    def build_values_and_ones():
        for h in heads:
            vones_scr[h, :, :DIFF_VD] = v_ref[0, :, head_cols(h)]
            vones_scr[h, :, DIFF_VD:] = jnp.ones((vones_scr.shape[1], LANES), BF16)

    def stack_queries():
        q = q_ref[0]
        lane = lax.broadcasted_iota(jnp.int32, (t, DIFF_VD), 1)
        for h in heads:
            qh = q[:, head_cols(h)]
            q2_scr[h, :t, :] = jnp.where(lane < DIFF_DH, qh, 0).astype(BF16)
            q2_scr[h, t:, :] = jnp.where(lane >= DIFF_DH, qh, 0).astype(BF16)

    def logits_tiles(near):
        for n, bias_index in enumerate(near):
            for h in heads:
                s = _dot_nt(q2_scr[h], k_ref[0, n * t:(n + 1) * t, head_cols(h)])
                if bias_index is not None:
                    bias = bias_ref[h, bias_index]
                    s = s + jnp.concatenate([bias, bias], axis=0)
                s_scr[n, h] = s
                lane_max = jnp.maximum(s[:, :LANES], s[:, LANES:])
                m_scr[h] = lane_max if n == 0 else jnp.maximum(m_scr[h], lane_max)

    def pv_tiles(n_tiles):
        for h in heads:
            m = m_scr[h]
            parts = [jnp.exp2(s_scr[n, h, :, half * LANES:(half + 1) * LANES] - m)
                     for n in range(n_tiles) for half in range(t // LANES)]
            p = jnp.concatenate(parts, axis=1).astype(BF16)
            acc_scr[h] = _dot(p, vones_scr[h, :n_tiles * t, :])

    def combine_and_norm():
        lam = (jnp.exp(jnp.sum(lq1_ref[...] * lk1_ref[...])) - jnp.exp(jnp.sum(lq2_ref[...] * lk2_ref[...]))
               + lambda_init)
        for h in heads:
            o = (acc_scr[h, :t, :DIFF_VD] / acc_scr[h, :t, DIFF_VD:]
                 - lam * (acc_scr[h, t:, :DIFF_VD] / acc_scr[h, t:, DIFF_VD:]))
            o_ref[0, :, head_cols(h)] = (_rms_norm(o, g_ref[...]) * (1.0 - lambda_init)).astype(BF16)

    def query_tile(tile):
        if tile == 0:
            build_values_and_ones()
        stack_queries()
        logits_tiles((None,) * (tile - 1) + ((1, 0) if tile > 0 else (0,)))
        for h in heads:
            m_scr[h] = jnp.broadcast_to(jnp.max(m_scr[h], axis=-1, keepdims=True), (2 * t, LANES))
        pv_tiles(tile + 1)
        combine_and_norm()

    for tile in range(s_scr.shape[0]):
        pl.when(i == tile)(functools.partial(query_tile, tile))


def _diff_attention(pq, pk, pv, bias_tiles, lam_q1, lam_k1, lam_q2, lam_k2, subln_g, lambda_init, side_casts):
    b, s, width = pq.shape
    t = ATTN_TILE
    tiles = s // t
    steps = b * tiles
    vec = lambda a: a.reshape(1, -1).astype(F32)
    all_keys = pl.BlockSpec((1, s, width), lambda bi, i: (bi, 0, 0))
    lam_spec = _resident((1, DIFF_DH))

    cast_views, cast_in_specs, cast_out_specs, cast_out_shapes, cast_chunks = [], [], [], [], []
    for w, layer in side_casts:
        view = w.reshape(w.shape[0], -1, w.shape[-1])
        _, rows, cols = view.shape
        chunks = max(c for c in range(1, steps + 1) if rows % (c * BF16_SUBLANES) == 0)

        def chunk(bi, i, chunks=chunks):
            return jnp.minimum(bi * tiles + i, chunks - 1)

        cast_views.append(view)
        cast_chunks.append(chunks)
        cast_in_specs.append(pl.BlockSpec((None, rows // chunks, cols),
                                          lambda bi, i, layer=layer, chunk=chunk: (layer, chunk(bi, i), 0)))
        cast_out_specs.append(pl.BlockSpec((rows // chunks, cols), lambda bi, i, chunk=chunk: (chunk(bi, i), 0)))
        cast_out_shapes.append(jax.ShapeDtypeStruct((rows, cols), BF16))

    out, *casts = pl.pallas_call(
        functools.partial(_diff_attn_kernel, lambda_init=lambda_init, cast_chunks=tuple(cast_chunks)),
        grid=(b, tiles),
        in_specs=[
            pl.BlockSpec((1, t, width), lambda bi, i: (bi, i, 0)),
            all_keys, all_keys, _resident(bias_tiles.shape),
            lam_spec, lam_spec, lam_spec, lam_spec, _resident((1, DIFF_VD)),
        ] + cast_in_specs,
        out_specs=[pl.BlockSpec((1, t, width), lambda bi, i: (bi, i, 0))] + cast_out_specs,
        out_shape=[jax.ShapeDtypeStruct((b, s, width), BF16)] + cast_out_shapes,
        scratch_shapes=[
            pltpu.VMEM((DIFF_HEADS, 2 * t, DIFF_VD), BF16),
            pltpu.VMEM((s // t, DIFF_HEADS, 2 * t, t), F32),
            pltpu.VMEM((DIFF_HEADS, 2 * t, LANES), F32),
            pltpu.VMEM((DIFF_HEADS, 2 * t, DIFF_VD + LANES), F32),
            pltpu.VMEM((DIFF_HEADS, s, DIFF_VD + LANES), BF16),
        ],
        compiler_params=_params("parallel", "arbitrary"),
        name="diff_attention",
    )(pq, pk, pv, bias_tiles, vec(lam_q1), vec(lam_k1), vec(lam_q2), vec(lam_k2), vec(subln_g), *cast_views)
    casts = [_LayerWeight(c.reshape((1,) + w.shape[1:]), 0) for c, (w, _) in zip(casts, side_casts)]
    return out, casts


def _gmlp_kernel(pc_ref, lng_ref, lnb_ref, ws_ref, bs_ref, o_ref):
    rows = o_ref.shape[0]
    z = jax.nn.gelu(pc_ref[...].astype(F32))
    u = z[:, :BRANCH_D]
    v = _layer_norm(z[:, BRANCH_D:], lng_ref[...], lnb_ref[...]).astype(BF16)
    t_idx = lax.broadcasted_iota(jnp.int32, (GMLP_CHUNK, GMLP_CHUNK), 0)
    s_idx = lax.broadcasted_iota(jnp.int32, (GMLP_CHUNK, GMLP_CHUNK), 1)
    for g in range(GMLP_GROUPS):
        w = jnp.where(s_idx <= t_idx, ws_ref[g], 0.0).astype(BF16)
        cols = slice(g * GMLP_GD, (g + 1) * GMLP_GD)
        for r in range(0, rows, GMLP_CHUNK):
            sv = _dot(w, v[r:r + GMLP_CHUNK, cols]) + bs_ref[g]
            o_ref[r:r + GMLP_CHUNK, cols] = (u[r:r + GMLP_CHUNK, cols] * sv).astype(BF16)


def _gmlp_branch(pc, ln_g, ln_b, w_s, b_s, row_tile=WIDE_ROW_TILE):
    rows = pc.shape[0]
    vec = lambda a: a.reshape(1, BRANCH_D)
    return pl.pallas_call(
        _gmlp_kernel,
        grid=(rows // row_tile,),
        in_specs=[
            pl.BlockSpec((row_tile, COL_C), lambda i: (i, 0)),
            _resident((1, BRANCH_D)), _resident((1, BRANCH_D)),
            _resident(w_s.shape), _resident((GMLP_GROUPS, GMLP_CHUNK, 1)),
        ],
        out_specs=pl.BlockSpec((row_tile, BRANCH_D), lambda i: (i, 0)),
        out_shape=jax.ShapeDtypeStruct((rows, BRANCH_D), BF16),
        compiler_params=_params("parallel"),
        name="gmlp_branch",
    )(pc, vec(ln_g), vec(ln_b), w_s, b_s.reshape(GMLP_GROUPS, GMLP_CHUNK, 1))


def _merge_kernel(x_ref, ya_ref, yb_ref, yc_ref, g_ref, win_ref, gb_ref, wbr_ref, wout_ref, o_ref):
    x = x_ref[...]
    xn = _rms_norm(x, g_ref[...]).astype(BF16)
    merged = None
    for n, y_ref in enumerate((ya_ref, yb_ref, yc_ref)):
        col = COL_BRANCHES + n * D_MODEL
        logits = _dot(xn, win_ref[:, col:col + D_MODEL]) + gb_ref[n:n + 1, :]
        term = jax.nn.sigmoid(logits) * _dot(y_ref[...], wbr_ref[n])
        merged = term if merged is None else merged + term
    o_ref[...] = x + _dot(merged.astype(BF16), wout_ref[...])


def _merge(x, ya, yb, yc, norm_g, w_in, gate_b, w_br, w_out, row_tile=WIDE_ROW_TILE):
    rows, d = x.shape
    row_spec = lambda width: pl.BlockSpec((row_tile, width), lambda i: (i, 0))
    return pl.pallas_call(
        _merge_kernel,
        grid=(rows // row_tile,),
        in_specs=[row_spec(D_MODEL), row_spec(BRANCH_D), row_spec(BRANCH_D), row_spec(BRANCH_D),
                  _resident((1, d)), _resident_layer(w_in), _resident(gate_b.shape), _resident_layer(w_br),
                  _resident_layer(w_out)],
        out_specs=row_spec(D_MODEL),
        out_shape=jax.ShapeDtypeStruct(x.shape, F32),
        compiler_params=_params("parallel"),
        name="branch_merge",
    )(x, ya, yb, yc, norm_g.reshape(1, d), w_in.stacked, gate_b, w_br.stacked, w_out.stacked)


def _xattn_kernel(x_ref, g_ref, wq_ref, kv_ref, wo_ref, o_ref):
    x = x_ref[0]
    q = _dot(_rms_norm(x, g_ref[...]).astype(BF16), wq_ref[...]) * (XATTN_DH ** -0.5 * LOG2_E)
    q = q.astype(BF16)
    heads = []
    for h in range(XATTN_HEADS):
        cols = slice(h * XATTN_DH, (h + 1) * XATTN_DH)
        k = kv_ref[0, :, cols]
        v = kv_ref[0, :, XATTN_D + h * XATTN_DH:XATTN_D + (h + 1) * XATTN_DH]
        s = _dot_nt(q[:, cols], k)
        p = jnp.exp2(s - jnp.max(s, axis=-1, keepdims=True))
        o = _dot(p.astype(BF16), v) / jnp.sum(p, axis=-1, keepdims=True)
        heads.append(o.astype(BF16))
    o_ref[0] = x + _dot(jnp.concatenate(heads, axis=-1), wo_ref[...])


def _mem_cross_attention(x, kv, norm_g, w_xq, w_xo, row_tile=WIDE_ROW_TILE):
    b, s, d = x.shape
    m = kv.shape[1]
    return pl.pallas_call(
        _xattn_kernel,
        grid=(b, s // row_tile),
        in_specs=[
            pl.BlockSpec((1, row_tile, d), lambda bi, i: (bi, i, 0)),
            _resident((1, d)), _resident_layer(w_xq),
            pl.BlockSpec((1, m, 2 * XATTN_D), lambda bi, i: (bi, 0, 0)),
            _resident_layer(w_xo),
        ],
        out_specs=pl.BlockSpec((1, row_tile, d), lambda bi, i: (bi, i, 0)),
        out_shape=jax.ShapeDtypeStruct(x.shape, F32),
        compiler_params=_params("parallel", "parallel"),
        name="mem_cross_attention",
    )(x, norm_g.reshape(1, d), w_xq.stacked, kv, w_xo.stacked)


def _ffn_kernel(x_ref, halo_ref, g_ref, wup_ref, cw_ref, cb_ref, wdn_ref, gf_ref, o_ref,
                slab_scr, h_scr, up_scr, act_scr, acc_scr, *, final_norm):
    tile_rows = o_ref.shape[1]
    rows = acc_scr.shape[0]
    groups = rows // F32_SUBLANES
    slabs = slab_scr.shape[0]

    def lanes(j):
        return slice(j * LANES, (j + 1) * LANES)

    for j in range(slabs):
        slab_scr[j, :tile_rows, :] = x_ref[0, :, lanes(j)]
    slab_scr[:, tile_rows:, :] = jnp.zeros((slabs, rows - tile_rows, LANES), F32)
    x = jnp.concatenate(
        [jnp.concatenate([slab_scr[j, pl.ds(gi, F32_SUBLANES, stride=groups), :] for j in range(slabs)], axis=1)
         for gi in range(groups)], axis=0)

    g = g_ref[...]
    halo = _rms_norm(halo_ref[0], g)
    h_scr[:FFN_HALO, :] = jnp.where(pl.program_id(1) > 0, halo, 0.0).astype(BF16)
    h_scr[FFN_HALO:, :] = _rms_norm(x, g).astype(BF16)
    acc_scr[...] = x

    n_chunks = D_FF // FFN_COL_CHUNK
    first_sublane = lax.broadcasted_iota(jnp.int32, (F32_SUBLANES, FFN_COL_CHUNK), 0) == 0

    def cols(c, half):
        start = half * D_FF + c * FFN_COL_CHUNK
        return slice(start, start + FFN_COL_CHUNK)

    def up_proj(c, slot):
        h = h_scr[...]
        for half in range(2):
            up = up_scr.at[slot, half]
            up[...] = _dot(h, wup_ref[:, cols(c, half)])
            prev_tile = up[FFN_HALO - F32_SUBLANES:FFN_HALO, :]
            for shift in range(1, FFN_K):
                wrapped = up[FFN_HALO + rows - shift * F32_SUBLANES:FFN_HALO + rows - (shift - 1) * F32_SUBLANES, :]
                block = jnp.where(first_sublane, pltpu.roll(prev_tile, shift, 0), pltpu.roll(wrapped, 1, 0))
                up[FFN_HALO - shift * F32_SUBLANES:FFN_HALO - (shift - 1) * F32_SUBLANES, :] = block

    def conv(c, slot, half):
        out = cb_ref[:, cols(c, half)]
        for k in range(FFN_K):
            window = up_scr[slot, half, k * F32_SUBLANES:k * F32_SUBLANES + rows, :]
            out = out + cw_ref[k:k + 1, cols(c, half)] * window
        return out

    def gate(c, slot):
        act_scr[slot] = (jax.nn.silu(conv(c, slot, 0)) * conv(c, slot, 1)).astype(BF16)

    def down_proj(c, slot):
        acc_scr[...] += _dot(act_scr[slot], wdn_ref[cols(c, 0), :])

    def stage(c, slot):
        up_proj(c + 1, 1 - slot)
        down_proj(c - 1, 1 - slot)
        gate(c, slot)

    up_proj(0, 0)
    up_proj(1, 1)
    gate(0, 0)
    for c in range(1, n_chunks - 1):
        stage(c, c % 2)
    down_proj(n_chunks - 2, (n_chunks - 2) % 2)
    gate(n_chunks - 1, (n_chunks - 1) % 2)
    down_proj(n_chunks - 1, (n_chunks - 1) % 2)

    for gi in range(groups):
        out = acc_scr[gi * F32_SUBLANES:(gi + 1) * F32_SUBLANES, :]
        if final_norm:
            out = _rms_norm(out, gf_ref[...])
        for j in range(slabs):
            slab_scr[j, pl.ds(gi, F32_SUBLANES, stride=groups), :] = out[:, lanes(j)]
    for j in range(slabs):
        o_ref[0, :, lanes(j)] = slab_scr[j, :tile_rows, :]


def _conv_ffn(x, norm_g, w_up, conv_w, conv_b, w_down, final_g, final_norm, row_tile=FFN_ROW_TILE):
    b, s, d = x.shape
    halo_blocks = row_tile // FFN_HALO
    conv_b = conv_b.reshape(1, 2 * D_FF)
    perm_rows = row_tile + 2 * F32_SUBLANES
    return pl.pallas_call(
        functools.partial(_ffn_kernel, final_norm=final_norm),
        grid=(b, s // row_tile),
        in_specs=[
            pl.BlockSpec((1, row_tile, d), lambda bi, i: (bi, i, 0)),
            pl.BlockSpec((1, FFN_HALO, d), lambda bi, i: (bi, jnp.maximum(i * halo_blocks - 1, 0), 0)),
            _resident((1, d)), _resident_layer(w_up), _resident(conv_w.shape), _resident(conv_b.shape),
            _resident_layer(w_down), _resident((1, d)),
        ],
        out_specs=pl.BlockSpec((1, row_tile, d), lambda bi, i: (bi, i, 0)),
        out_shape=jax.ShapeDtypeStruct(x.shape, F32),
        scratch_shapes=[
            pltpu.VMEM((d // LANES, perm_rows, LANES), F32),
            pltpu.VMEM((FFN_HALO + perm_rows, d), BF16),
            pltpu.VMEM((2, 2, FFN_HALO + perm_rows, FFN_COL_CHUNK), F32),
            pltpu.VMEM((2, perm_rows, FFN_COL_CHUNK), BF16),
            pltpu.VMEM((perm_rows, d), F32),
        ],
        compiler_params=_params("parallel", "parallel"),
        name="conv_ffn",
    )(x, x, norm_g.reshape(1, d), w_up.stacked, conv_w, conv_b, w_down.stacked, final_g.reshape(1, d))


def kernel(x, mem, rel_bias, norm_mix_g, w_in, gate_b, conv_w, conv_b, conv_ln_g, conv_ln_b, lam_q1, lam_k1, lam_q2, lam_k2, subln_g, gmlp_ln_g, gmlp_ln_b, w_s, b_s, w_br, w_out, norm_xattn_g, norm_mem_g, w_xq, w_xkv, w_xo, norm_ffn_g, w_up, ffn_conv_w, ffn_conv_b, w_down, norm_final_g):
    b, s, d = x.shape
    m = mem.shape[1]
    rows = b * s
    bias_tiles = _bias_tiles(rel_bias)
    w_in_l = _to_bf16(w_in, 0)
    for l in range(DEPTH):
        lambda_init = 0.8 - 0.6 * math.exp(-0.3 * l)
        ya, pq, pk, pv, pc = _mixer_in_proj(x, norm_mix_g[l], w_in_l, conv_w[l], conv_b[l], conv_ln_g[l],
                                            conv_ln_b[l])
        side_casts = [(w, l) for w in (w_br, w_out, w_xq, w_xkv, w_xo, w_up, w_down)]
        if l + 1 < DEPTH:
            side_casts.append((w_in, l + 1))
        yb, casts = _diff_attention(pq, pk, pv, bias_tiles, lam_q1[l], lam_k1[l], lam_q2[l], lam_k2[l],
                                    subln_g[l], lambda_init, side_casts)
        w_br_l, w_out_l, w_xq_l, w_xkv_l, w_xo_l, w_up_l, w_down_l = casts[:7]
        yc = _gmlp_branch(pc.reshape(rows, COL_C), gmlp_ln_g[l], gmlp_ln_b[l], w_s[l], b_s[l])
        x = _merge(x.reshape(rows, d), ya.reshape(rows, BRANCH_D), yb.reshape(rows, BRANCH_D), yc, norm_mix_g[l],
                   w_in_l, gate_b[l], w_br_l, w_out_l).reshape(b, s, d)
        (kv,) = _norm_proj(mem.reshape(b * m, d), norm_mem_g[l], w_xkv_l, (2 * XATTN_D,))
        x = _mem_cross_attention(x, kv.reshape(b, m, 2 * XATTN_D), norm_xattn_g[l], w_xq_l, w_xo_l)
        x = _conv_ffn(x, norm_ffn_g[l], w_up_l, ffn_conv_w[l], ffn_conv_b[l], w_down_l, norm_final_g,
                      final_norm=(l == DEPTH - 1))
        if l + 1 < DEPTH:
            w_in_l = casts[7]
    return x
```

```python
import functools
import math
from typing import NamedTuple

import jax
import jax.numpy as jnp
from jax import lax
from jax.experimental import pallas as pl
from jax.experimental.pallas import tpu as pltpu

F32 = jnp.float32
BF16 = jnp.bfloat16

D_MODEL = 1024
DEPTH = 2
CHUNK = 64
BRANCH_D = D_MODEL // 2
N_BRANCH = 3
CONV_K = 31
DIFF_HEADS = 4
DIFF_DH = D_MODEL // 16
DIFF_VD = 2 * DIFF_DH
GMLP_GROUPS = 4
GMLP_GD = BRANCH_D // GMLP_GROUPS
GMLP_CHUNK = 128
N_REL_BUCKETS = 32
REL_MAX_DIST = 128
XATTN_HEADS = 4
XATTN_DH = D_MODEL // 8
XATTN_D = XATTN_HEADS * XATTN_DH
D_FF = 2816
FFN_K = 3
EPS = 1e-6
NEG_INF = -1e30

COL_A = 2 * BRANCH_D
COL_QK = DIFF_HEADS * 2 * DIFF_DH
COL_V = DIFF_HEADS * DIFF_VD
COL_C = 2 * BRANCH_D
COL_G = N_BRANCH * D_MODEL
IN_SPLIT_WIDTHS = (COL_A, COL_QK, COL_QK, COL_V, COL_C)
COL_BRANCHES = sum(IN_SPLIT_WIDTHS)
LOG2_E = math.log2(math.e)
IN_SPLIT_SCALES = (1.0, DIFF_DH ** -0.5 * LOG2_E, 1.0, 1.0, 1.0)

LANES = 128
MXU_WIDTH = 256
F32_SUBLANES = 8
BF16_SUBLANES = 16
VMEM_LIMIT_BYTES = 56 * 1024 * 1024
CAST_BLOCK_BYTES = 4 * 1024 * 1024

ROW_TILE = 512
WIDE_ROW_TILE = 1024
FFN_ROW_TILE = 1024
PROJ_COL_CHUNK = 512
ATTN_TILE = 256
CONV_HALO = 32
CONV_ROW_CHUNK = 64
FFN_HALO = BF16_SUBLANES
FFN_COL_CHUNK = MXU_WIDTH

assert CONV_HALO >= CONV_K - 1 and FFN_HALO == F32_SUBLANES * (FFN_K - 1)
assert D_FF % FFN_COL_CHUNK == 0
assert ATTN_TILE % CHUNK == 0 and ATTN_TILE > REL_MAX_DIST


def _params(*semantics):
    return pltpu.CompilerParams(dimension_semantics=semantics, vmem_limit_bytes=VMEM_LIMIT_BYTES)


def _resident(shape):
    zeros = (0,) * len(shape)
    return pl.BlockSpec(shape, lambda *_: zeros, pipeline_mode=pl.Buffered(1))


class _LayerWeight(NamedTuple):
    stacked: jax.Array
    layer: int

    @property
    def shape(self):
        return self.stacked.shape[1:]


def _resident_layer(w):
    index = (w.layer,) + (0,) * len(w.shape)
    return pl.BlockSpec((None,) + tuple(w.shape), lambda *_: index, pipeline_mode=pl.Buffered(1))


def _cast_kernel(w_ref, o_ref):
    o_ref[...] = w_ref[...].astype(BF16)


def _to_bf16(w, layer):
    _, rows, cols = w.shape
    block_rows = max(r for r in range(BF16_SUBLANES, rows + 1, BF16_SUBLANES)
                     if rows % r == 0 and r * cols * 4 <= CAST_BLOCK_BYTES)
    out = pl.pallas_call(
        _cast_kernel,
        grid=(rows // block_rows,),
        in_specs=[pl.BlockSpec((None, block_rows, cols), lambda i: (layer, i, 0))],
        out_specs=pl.BlockSpec((None, block_rows, cols), lambda i: (0, i, 0)),
        out_shape=jax.ShapeDtypeStruct((1, rows, cols), BF16),
        compiler_params=_params("parallel"),
        name="weight_cast",
    )(w)
    return _LayerWeight(out, 0)


def _rms_norm(x, g):
    return x * lax.rsqrt(jnp.mean(x * x, axis=-1, keepdims=True) + EPS) * g


def _layer_norm(x, g, b):
    mu = jnp.mean(x, axis=-1, keepdims=True)
    xc = x - mu
    var = jnp.mean(xc * xc, axis=-1, keepdims=True)
    return xc * lax.rsqrt(var + EPS) * g + b


def _dot(a, b):
    return jnp.dot(a, b, preferred_element_type=F32)


def _dot_nt(a, b):
    return lax.dot_general(a, b, (((1,), (1,)), ((), ())), preferred_element_type=F32)


def _norm_proj_kernel(x_ref, g_ref, w_ref, *out_refs, widths, scales):
    xn = _rms_norm(x_ref[...], g_ref[...]).astype(BF16)
    col = 0
    for o_ref, width, scale in zip(out_refs, widths, scales):
        for c in range(0, width, PROJ_COL_CHUNK):
            out = _dot(xn, w_ref[:, col + c:col + c + PROJ_COL_CHUNK])
            if scale != 1.0:
                out = out * scale
            o_ref[:, c:c + PROJ_COL_CHUNK] = out.astype(BF16)
        col += width


def _norm_proj(x, g, w, widths, scales=None, row_tile=ROW_TILE):
    rows, d = x.shape
    scales = scales or (1.0,) * len(widths)
    assert w.shape == (d, sum(widths)) and all(wd % PROJ_COL_CHUNK == 0 for wd in widths)
    return pl.pallas_call(
        functools.partial(_norm_proj_kernel, widths=widths, scales=scales),
        grid=(rows // row_tile,),
        in_specs=[pl.BlockSpec((row_tile, d), lambda i: (i, 0)), _resident((1, d)), _resident_layer(w)],
        out_specs=[pl.BlockSpec((row_tile, wd), lambda i: (i, 0)) for wd in widths],
        out_shape=[jax.ShapeDtypeStruct((rows, wd), BF16) for wd in widths],
        compiler_params=_params("parallel"),
        name="norm_proj",
    )(x, g.reshape(1, d), w.stacked)


def _mixer_in_kernel(x_ref, g_ref, w_ref, cw_ref, cb_ref, lng_ref, lnb_ref, glng_ref, glnb_ref, ws_ref, bs_ref,
                     ya_ref, pq_ref, pk_ref, pv_ref, yc_ref, xn_scr, xs_scr, halo_scr):
    rows = ya_ref.shape[1]
    xn_scr[...] = _rms_norm(x_ref[0], g_ref[...]).astype(BF16)

    def proj(col, width):
        return _dot(xn_scr[...], w_ref[:, col:col + width])

    glu = proj(0, BRANCH_D) * jax.nn.sigmoid(proj(BRANCH_D, BRANCH_D))

    @pl.when(pl.program_id(1) == 0)
    def _():
        halo_scr[...] = jnp.zeros(halo_scr.shape, F32)

    xs_scr[0, :CONV_HALO, :] = halo_scr[...]
    xs_scr[0, CONV_HALO:, :] = glu
    halo_scr[...] = glu[rows - CONV_HALO:, :]

    def shifted_copies():
        total = CONV_HALO + rows
        base = xs_scr[0]
        for b in range(1, F32_SUBLANES):
            xs_scr[b] = pltpu.roll(base, total - b, 0)

    first_tap = CONV_HALO - (CONV_K - 1)

    def conv_rows(r):
        acc = jnp.broadcast_to(cb_ref[...], (CONV_ROW_CHUNK, BRANCH_D))
        for k in range(CONV_K):
            b = (first_tap + k) % F32_SUBLANES
            start = r + first_tap + k - b
            acc = acc + cw_ref[k:k + 1, :] * xs_scr[b, start:start + CONV_ROW_CHUNK, :]
        ya_ref[0, r:r + CONV_ROW_CHUNK, :] = jax.nn.silu(_layer_norm(acc, lng_ref[...], lnb_ref[...])).astype(BF16)

    def proj_chunk(o_ref, col, c, scale):
        out = proj(col + c, PROJ_COL_CHUNK)
        if scale != 1.0:
            out = out * scale
        o_ref[0, :, c:c + PROJ_COL_CHUNK] = out.astype(BF16)

    def gmlp_branch(col):
        u = jax.nn.gelu(proj(col, BRANCH_D))
        v = _layer_norm(jax.nn.gelu(proj(col + BRANCH_D, BRANCH_D)), glng_ref[...], glnb_ref[...]).astype(BF16)
        t_idx = lax.broadcasted_iota(jnp.int32, (GMLP_CHUNK, GMLP_CHUNK), 0)
        s_idx = lax.broadcasted_iota(jnp.int32, (GMLP_CHUNK, GMLP_CHUNK), 1)
        for g in range(GMLP_GROUPS):
            w = jnp.where(s_idx <= t_idx, ws_ref[g], 0.0).astype(BF16)
            cols = slice(g * GMLP_GD, (g + 1) * GMLP_GD)
            for r in range(0, rows, GMLP_CHUNK):
                sv = _dot(w, v[r:r + GMLP_CHUNK, cols]) + bs_ref[g]
                yc_ref[0, r:r + GMLP_CHUNK, cols] = (u[r:r + GMLP_CHUNK, cols] * sv).astype(BF16)

    col = COL_A
    for o_ref, width, scale in zip((pq_ref, pk_ref, pv_ref), IN_SPLIT_WIDTHS[1:4], IN_SPLIT_SCALES[1:4]):
        for c in range(0, width, PROJ_COL_CHUNK):
            proj_chunk(o_ref, col, c, scale)
        col += width
    gmlp_branch(col)
    shifted_copies()
    for r in range(0, rows, CONV_ROW_CHUNK):
        conv_rows(r)


def _mixer_in_proj(x, norm_g, w_in, conv_w, conv_b, ln_g, ln_b, gmlp_ln_g, gmlp_ln_b, w_s, b_s,
                   row_tile=ROW_TILE):
    b, s, d = x.shape
    vec = lambda a: a.reshape(1, -1)
    out_widths = (BRANCH_D,) + IN_SPLIT_WIDTHS[1:4] + (BRANCH_D,)
    return pl.pallas_call(
        _mixer_in_kernel,
        grid=(b, s // row_tile),
        in_specs=[
            pl.BlockSpec((1, row_tile, d), lambda bi, i: (bi, i, 0)),
            _resident((1, d)), _resident_layer(w_in),
            _resident((CONV_K, BRANCH_D)), _resident((1, BRANCH_D)), _resident((1, BRANCH_D)),
            _resident((1, BRANCH_D)),
            _resident((1, BRANCH_D)), _resident((1, BRANCH_D)),
            _resident(w_s.shape), _resident((GMLP_GROUPS, GMLP_CHUNK, 1)),
        ],
        out_specs=[pl.BlockSpec((1, row_tile, wd), lambda bi, i: (bi, i, 0)) for wd in out_widths],
        out_shape=[jax.ShapeDtypeStruct((b, s, wd), BF16) for wd in out_widths],
        scratch_shapes=[
            pltpu.VMEM((row_tile, d), BF16),
            pltpu.VMEM((F32_SUBLANES, CONV_HALO + row_tile, BRANCH_D), F32),
            pltpu.VMEM((CONV_HALO, BRANCH_D), F32),
        ],
        compiler_params=_params("parallel", "arbitrary"),
        name="mixer_in_proj",
    )(x, vec(norm_g), w_in.stacked, conv_w, vec(conv_b), vec(ln_g), vec(ln_b), vec(gmlp_ln_g), vec(gmlp_ln_b),
      w_s, b_s.reshape(GMLP_GROUPS, GMLP_CHUNK, 1))


def _rel_bucket(rel):
    half = N_REL_BUCKETS // 2
    max_exact = half // 2
    n = jnp.abs(rel)
    log_ratio = jnp.log(jnp.maximum(n, 1).astype(F32) / max_exact) / math.log(REL_MAX_DIST / max_exact)
    large = jnp.minimum(max_exact + (log_ratio * (half - max_exact)).astype(jnp.int32), half - 1)
    return jnp.where(rel > 0, half, 0) + jnp.where(n < max_exact, n, large)


FAR_BUCKET = N_REL_BUCKETS // 2 - 1
MASKED_BUCKET = -1


def _near_buckets():
    r = jnp.arange(ATTN_TILE)[:, None]
    c = jnp.arange(ATTN_TILE)[None, :]
    visible = (c // CHUNK) <= (r // CHUNK)
    diag = jnp.where(visible, _rel_bucket(c - r), MASKED_BUCKET)
    sub = _rel_bucket(c - r - ATTN_TILE)
    return jnp.stack([diag, sub]).astype(jnp.int32)


def _bias_tiles_kernel(table_ref, bucket_ref, o_ref):
    h = pl.program_id(0)
    bucket = bucket_ref[...]
    far = table_ref[FAR_BUCKET, h]
    out = jnp.zeros(bucket.shape, F32)
    for b in range(N_REL_BUCKETS):
        out = jnp.where(bucket == b, (table_ref[b, h] - far) * LOG2_E, out)
    o_ref[0] = jnp.where(bucket == MASKED_BUCKET, NEG_INF, out)


def _bias_tiles(rel_bias):
    buckets = _near_buckets()
    return pl.pallas_call(
        _bias_tiles_kernel,
        grid=(DIFF_HEADS,),
        in_specs=[pl.BlockSpec(memory_space=pltpu.SMEM), _resident(buckets.shape)],
        out_specs=pl.BlockSpec((1,) + buckets.shape, lambda h: (h, 0, 0, 0)),
        out_shape=jax.ShapeDtypeStruct((DIFF_HEADS,) + buckets.shape, F32),
        compiler_params=_params("arbitrary"),
        name="rel_bias_tiles",
    )(rel_bias.astype(F32), buckets)


def _diff_attn_kernel(*refs, lambda_init, cast_chunks):
    n_casts = len(cast_chunks)
    (q_ref, k_ref, v_ref, bias_ref, lq1_ref, lk1_ref, lq2_ref, lk2_ref, g_ref), refs = refs[:9], refs[9:]
    cast_src, o_ref, cast_dst = refs[:n_casts], refs[n_casts], refs[n_casts + 1:2 * n_casts + 1]
    q2_scr, s_scr, m_scr, acc_scr, vones_scr = refs[2 * n_casts + 1:]
    step = pl.program_id(0) * pl.num_programs(1) + pl.program_id(1)
    for src, dst, chunks in zip(cast_src, cast_dst, cast_chunks):
        @pl.when(step < chunks)
        def _():
            dst[...] = src[...].astype(BF16)

    _diff_attn_body(q_ref, k_ref, v_ref, bias_ref, lq1_ref, lk1_ref, lq2_ref, lk2_ref, g_ref, o_ref,
                    q2_scr, s_scr, m_scr, acc_scr, vones_scr, lambda_init=lambda_init)


def _diff_attn_body(q_ref, k_ref, v_ref, bias_ref, lq1_ref, lk1_ref, lq2_ref, lk2_ref, g_ref, o_ref,
                    q2_scr, s_scr, m_scr, acc_scr, vones_scr, *, lambda_init):
    t = ATTN_TILE
    i = pl.program_id(1)
    heads = range(DIFF_HEADS)

    def head_cols(h):
        return slice(h * DIFF_VD, (h + 1) * DIFF_VD)

    def build_values_and_ones():
        for h in heads:
            vones_scr[h, :, :DIFF_VD] = v_ref[0, :, head_cols(h)]
            vones_scr[h, :, DIFF_VD:] = jnp.ones((vones_scr.shape[1], LANES), BF16)

    def stack_queries():
        q = q_ref[0]
        lane = lax.broadcasted_iota(jnp.int32, (t, DIFF_VD), 1)
        for h in heads:
            qh = q[:, head_cols(h)]
            q2_scr[h, :t, :] = jnp.where(lane < DIFF_DH, qh, 0).astype(BF16)
            q2_scr[h, t:, :] = jnp.where(lane >= DIFF_DH, qh, 0).astype(BF16)

    def logits_tiles(near):
        for n, bias_index in enumerate(near):
            for h in heads:
                s = _dot_nt(q2_scr[h], k_ref[0, n * t:(n + 1) * t, head_cols(h)])
                if bias_index is not None:
                    bias = bias_ref[h, bias_index]
                    s = s + jnp.concatenate([bias, bias], axis=0)
                s_scr[n, h] = s
                lane_max = jnp.maximum(s[:, :LANES], s[:, LANES:])
                m_scr[h] = lane_max if n == 0 else jnp.maximum(m_scr[h], lane_max)

    def pv_tiles(n_tiles):
        for h in heads:
            m = m_scr[h]
            parts = [jnp.exp2(s_scr[n, h, :, half * LANES:(half + 1) * LANES] - m)
                     for n in range(n_tiles) for half in range(t // LANES)]
            p = jnp.concatenate(parts, axis=1).astype(BF16)
            acc_scr[h] = _dot(p, vones_scr[h, :n_tiles * t, :])

    def combine_and_norm():
        lam = (jnp.exp(jnp.sum(lq1_ref[...] * lk1_ref[...])) - jnp.exp(jnp.sum(lq2_ref[...] * lk2_ref[...]))
               + lambda_init)
        for h in heads:
            o = (acc_scr[h, :t, :DIFF_VD] / acc_scr[h, :t, DIFF_VD:]
                 - lam * (acc_scr[h, t:, :DIFF_VD] / acc_scr[h, t:, DIFF_VD:]))
            o_ref[0, :, head_cols(h)] = (_rms_norm(o, g_ref[...]) * (1.0 - lambda_init)).astype(BF16)

    def query_tile(tile):
        if tile == 0:
            build_values_and_ones()
        stack_queries()
        logits_tiles((None,) * (tile - 1) + ((1, 0) if tile > 0 else (0,)))
        for h in heads:
            m_scr[h] = jnp.broadcast_to(jnp.max(m_scr[h], axis=-1, keepdims=True), (2 * t, LANES))
        pv_tiles(tile + 1)
        combine_and_norm()

    for tile in range(s_scr.shape[0]):
        pl.when(i == tile)(functools.partial(query_tile, tile))


def _diff_attention(pq, pk, pv, bias_tiles, lam_q1, lam_k1, lam_q2, lam_k2, subln_g, lambda_init, side_casts):
    b, s, width = pq.shape
    t = ATTN_TILE
    tiles = s // t
    steps = b * tiles
    vec = lambda a: a.reshape(1, -1).astype(F32)
    all_keys = pl.BlockSpec((1, s, width), lambda bi, i: (bi, 0, 0))
    lam_spec = _resident((1, DIFF_DH))

    cast_views, cast_in_specs, cast_out_specs, cast_out_shapes, cast_chunks = [], [], [], [], []
    for w, layer in side_casts:
        view = w.reshape(w.shape[0], -1, w.shape[-1])
        _, rows, cols = view.shape
        chunks = max(c for c in range(1, steps + 1) if rows % (c * BF16_SUBLANES) == 0)

        def chunk(bi, i, chunks=chunks):
            return jnp.minimum(bi * tiles + i, chunks - 1)

        cast_views.append(view)
        cast_chunks.append(chunks)
        cast_in_specs.append(pl.BlockSpec((None, rows // chunks, cols),
                                          lambda bi, i, layer=layer, chunk=chunk: (layer, chunk(bi, i), 0)))
        cast_out_specs.append(pl.BlockSpec((rows // chunks, cols), lambda bi, i, chunk=chunk: (chunk(bi, i), 0)))
        cast_out_shapes.append(jax.ShapeDtypeStruct((rows, cols), BF16))

    out, *casts = pl.pallas_call(
        functools.partial(_diff_attn_kernel, lambda_init=lambda_init, cast_chunks=tuple(cast_chunks)),
        grid=(b, tiles),
        in_specs=[
            pl.BlockSpec((1, t, width), lambda bi, i: (bi, i, 0)),
            all_keys, all_keys, _resident(bias_tiles.shape),
            lam_spec, lam_spec, lam_spec, lam_spec, _resident((1, DIFF_VD)),
        ] + cast_in_specs,
        out_specs=[pl.BlockSpec((1, t, width), lambda bi, i: (bi, i, 0))] + cast_out_specs,
        out_shape=[jax.ShapeDtypeStruct((b, s, width), BF16)] + cast_out_shapes,
        scratch_shapes=[
            pltpu.VMEM((DIFF_HEADS, 2 * t, DIFF_VD), BF16),
            pltpu.VMEM((s // t, DIFF_HEADS, 2 * t, t), F32),
            pltpu.VMEM((DIFF_HEADS, 2 * t, LANES), F32),
            pltpu.VMEM((DIFF_HEADS, 2 * t, DIFF_VD + LANES), F32),
            pltpu.VMEM((DIFF_HEADS, s, DIFF_VD + LANES), BF16),
        ],
        compiler_params=_params("parallel", "arbitrary"),
        name="diff_attention",
    )(pq, pk, pv, bias_tiles, vec(lam_q1), vec(lam_k1), vec(lam_q2), vec(lam_k2), vec(subln_g), *cast_views)
    casts = [_LayerWeight(c.reshape((1,) + w.shape[1:]), 0) for c, (w, _) in zip(casts, side_casts)]
    return out, casts


def _merge_kernel(x_ref, ya_ref, yb_ref, yc_ref, g_ref, win_ref, gb_ref, wbr_ref, wout_ref, o_ref):
    x = x_ref[...]
    xn = _rms_norm(x, g_ref[...]).astype(BF16)
    merged = None
    for n, y_ref in enumerate((ya_ref, yb_ref, yc_ref)):
        col = COL_BRANCHES + n * D_MODEL
        logits = _dot(xn, win_ref[:, col:col + D_MODEL]) + gb_ref[n:n + 1, :]
        term = jax.nn.sigmoid(logits) * _dot(y_ref[...], wbr_ref[n])
        merged = term if merged is None else merged + term
    o_ref[...] = x + _dot(merged.astype(BF16), wout_ref[...])


def _merge(x, ya, yb, yc, norm_g, w_in, gate_b, w_br, w_out, row_tile=WIDE_ROW_TILE):
    rows, d = x.shape
    row_spec = lambda width: pl.BlockSpec((row_tile, width), lambda i: (i, 0))
    return pl.pallas_call(
        _merge_kernel,
        grid=(rows // row_tile,),
        in_specs=[row_spec(D_MODEL), row_spec(BRANCH_D), row_spec(BRANCH_D), row_spec(BRANCH_D),
                  _resident((1, d)), _resident_layer(w_in), _resident(gate_b.shape), _resident_layer(w_br),
                  _resident_layer(w_out)],
        out_specs=row_spec(D_MODEL),
        out_shape=jax.ShapeDtypeStruct(x.shape, F32),
        compiler_params=_params("parallel"),
        name="branch_merge",
    )(x, ya, yb, yc, norm_g.reshape(1, d), w_in.stacked, gate_b, w_br.stacked, w_out.stacked)


def _xattn_kernel(x_ref, g_ref, wq_ref, kv_ref, wo_ref, o_ref):
    x = x_ref[0]
    q = _dot(_rms_norm(x, g_ref[...]).astype(BF16), wq_ref[...]) * (XATTN_DH ** -0.5 * LOG2_E)
    q = q.astype(BF16)
    heads = []
    for h in range(XATTN_HEADS):
        cols = slice(h * XATTN_DH, (h + 1) * XATTN_DH)
        k = kv_ref[0, :, cols]
        v = kv_ref[0, :, XATTN_D + h * XATTN_DH:XATTN_D + (h + 1) * XATTN_DH]
        s = _dot_nt(q[:, cols], k)
        p = jnp.exp2(s - jnp.max(s, axis=-1, keepdims=True))
        o = _dot(p.astype(BF16), v) / jnp.sum(p, axis=-1, keepdims=True)
        heads.append(o.astype(BF16))
    o_ref[0] = x + _dot(jnp.concatenate(heads, axis=-1), wo_ref[...])


def _mem_cross_attention(x, kv, norm_g, w_xq, w_xo, row_tile=WIDE_ROW_TILE):
    b, s, d = x.shape
    m = kv.shape[1]
    return pl.pallas_call(
        _xattn_kernel,
        grid=(b, s // row_tile),
        in_specs=[
            pl.BlockSpec((1, row_tile, d), lambda bi, i: (bi, i, 0)),
            _resident((1, d)), _resident_layer(w_xq),
            pl.BlockSpec((1, m, 2 * XATTN_D), lambda bi, i: (bi, 0, 0)),
            _resident_layer(w_xo),
        ],
        out_specs=pl.BlockSpec((1, row_tile, d), lambda bi, i: (bi, i, 0)),
        out_shape=jax.ShapeDtypeStruct(x.shape, F32),
        compiler_params=_params("parallel", "parallel"),
        name="mem_cross_attention",
    )(x, norm_g.reshape(1, d), w_xq.stacked, kv, w_xo.stacked)


def _ffn_kernel(x_ref, halo_ref, g_ref, wup_ref, cw_ref, cb_ref, wdn_ref, gf_ref, o_ref,
                slab_scr, h_scr, up_scr, act_scr, acc_scr, *, final_norm):
    tile_rows = o_ref.shape[1]
    rows = acc_scr.shape[0]
    groups = rows // F32_SUBLANES
    slabs = slab_scr.shape[0]

    def lanes(j):
        return slice(j * LANES, (j + 1) * LANES)

    for j in range(slabs):
        slab_scr[j, :tile_rows, :] = x_ref[0, :, lanes(j)]
    slab_scr[:, tile_rows:, :] = jnp.zeros((slabs, rows - tile_rows, LANES), F32)
    x = jnp.concatenate(
        [jnp.concatenate([slab_scr[j, pl.ds(gi, F32_SUBLANES, stride=groups), :] for j in range(slabs)], axis=1)
         for gi in range(groups)], axis=0)

    g = g_ref[...]
    halo = _rms_norm(halo_ref[0], g)
    h_scr[:FFN_HALO, :] = jnp.where(pl.program_id(1) > 0, halo, 0.0).astype(BF16)
    h_scr[FFN_HALO:, :] = _rms_norm(x, g).astype(BF16)
    acc_scr[...] = x

    n_chunks = D_FF // FFN_COL_CHUNK
    first_sublane = lax.broadcasted_iota(jnp.int32, (F32_SUBLANES, FFN_COL_CHUNK), 0) == 0

    def cols(c, half):
        start = half * D_FF + c * FFN_COL_CHUNK
        return slice(start, start + FFN_COL_CHUNK)

    def up_proj(c, slot):
        h = h_scr[...]
        for half in range(2):
            up = up_scr.at[slot, half]
            up[...] = _dot(h, wup_ref[:, cols(c, half)])
            prev_tile = up[FFN_HALO - F32_SUBLANES:FFN_HALO, :]
            for shift in range(1, FFN_K):
                wrapped = up[FFN_HALO + rows - shift * F32_SUBLANES:FFN_HALO + rows - (shift - 1) * F32_SUBLANES, :]
                block = jnp.where(first_sublane, pltpu.roll(prev_tile, shift, 0), pltpu.roll(wrapped, 1, 0))
                up[FFN_HALO - shift * F32_SUBLANES:FFN_HALO - (shift - 1) * F32_SUBLANES, :] = block

    def conv(c, slot, half):
        out = cb_ref[:, cols(c, half)]
        for k in range(FFN_K):
            window = up_scr[slot, half, k * F32_SUBLANES:k * F32_SUBLANES + rows, :]
            out = out + cw_ref[k:k + 1, cols(c, half)] * window
        return out

    def gate(c, slot):
        act_scr[slot] = (jax.nn.silu(conv(c, slot, 0)) * conv(c, slot, 1)).astype(BF16)

    def down_proj(c, slot):
        acc_scr[...] += _dot(act_scr[slot], wdn_ref[cols(c, 0), :])

    def stage(c, slot):
        up_proj(c + 1, 1 - slot)
        down_proj(c - 1, 1 - slot)
        gate(c, slot)

    up_proj(0, 0)
    up_proj(1, 1)
    gate(0, 0)
    for c in range(1, n_chunks - 1):
        stage(c, c % 2)
    down_proj(n_chunks - 2, (n_chunks - 2) % 2)
    gate(n_chunks - 1, (n_chunks - 1) % 2)
    down_proj(n_chunks - 1, (n_chunks - 1) % 2)

    for gi in range(groups):
        out = acc_scr[gi * F32_SUBLANES:(gi + 1) * F32_SUBLANES, :]
        if final_norm:
            out = _rms_norm(out, gf_ref[...])
        for j in range(slabs):
            slab_scr[j, pl.ds(gi, F32_SUBLANES, stride=groups), :] = out[:, lanes(j)]
    for j in range(slabs):
        o_ref[0, :, lanes(j)] = slab_scr[j, :tile_rows, :]


def _conv_ffn(x, norm_g, w_up, conv_w, conv_b, w_down, final_g, final_norm, row_tile=FFN_ROW_TILE):
    b, s, d = x.shape
    halo_blocks = row_tile // FFN_HALO
    conv_b = conv_b.reshape(1, 2 * D_FF)
    perm_rows = row_tile + 2 * F32_SUBLANES
    return pl.pallas_call(
        functools.partial(_ffn_kernel, final_norm=final_norm),
        grid=(b, s // row_tile),
        in_specs=[
            pl.BlockSpec((1, row_tile, d), lambda bi, i: (bi, i, 0)),
            pl.BlockSpec((1, FFN_HALO, d), lambda bi, i: (bi, jnp.maximum(i * halo_blocks - 1, 0), 0)),
            _resident((1, d)), _resident_layer(w_up), _resident(conv_w.shape), _resident(conv_b.shape),
            _resident_layer(w_down), _resident((1, d)),
        ],
        out_specs=pl.BlockSpec((1, row_tile, d), lambda bi, i: (bi, i, 0)),
        out_shape=jax.ShapeDtypeStruct(x.shape, F32),
        scratch_shapes=[
            pltpu.VMEM((d // LANES, perm_rows, LANES), F32),
            pltpu.VMEM((FFN_HALO + perm_rows, d), BF16),
            pltpu.VMEM((2, 2, FFN_HALO + perm_rows, FFN_COL_CHUNK), F32),
            pltpu.VMEM((2, perm_rows, FFN_COL_CHUNK), BF16),
            pltpu.VMEM((perm_rows, d), F32),
        ],
        compiler_params=_params("parallel", "parallel"),
        name="conv_ffn",
    )(x, x, norm_g.reshape(1, d), w_up.stacked, conv_w, conv_b, w_down.stacked, final_g.reshape(1, d))


def kernel(x, mem, rel_bias, norm_mix_g, w_in, gate_b, conv_w, conv_b, conv_ln_g, conv_ln_b, lam_q1, lam_k1, lam_q2, lam_k2, subln_g, gmlp_ln_g, gmlp_ln_b, w_s, b_s, w_br, w_out, norm_xattn_g, norm_mem_g, w_xq, w_xkv, w_xo, norm_ffn_g, w_up, ffn_conv_w, ffn_conv_b, w_down, norm_final_g):
    b, s, d = x.shape
    m = mem.shape[1]
    rows = b * s
    bias_tiles = _bias_tiles(rel_bias)
    w_in_l = _to_bf16(w_in, 0)
    for l in range(DEPTH):
        lambda_init = 0.8 - 0.6 * math.exp(-0.3 * l)
        ya, pq, pk, pv, yc = _mixer_in_proj(x, norm_mix_g[l], w_in_l, conv_w[l], conv_b[l], conv_ln_g[l],
                                            conv_ln_b[l], gmlp_ln_g[l], gmlp_ln_b[l], w_s[l], b_s[l])
        side_casts = [(w, l) for w in (w_br, w_out, w_xq, w_xkv, w_xo, w_up, w_down)]
        if l + 1 < DEPTH:
            side_casts.append((w_in, l + 1))
        yb, casts = _diff_attention(pq, pk, pv, bias_tiles, lam_q1[l], lam_k1[l], lam_q2[l], lam_k2[l],
                                    subln_g[l], lambda_init, side_casts)
        w_br_l, w_out_l, w_xq_l, w_xkv_l, w_xo_l, w_up_l, w_down_l = casts[:7]
        x = _merge(x.reshape(rows, d), ya.reshape(rows, BRANCH_D), yb.reshape(rows, BRANCH_D),
                   yc.reshape(rows, BRANCH_D), norm_mix_g[l],
                   w_in_l, gate_b[l], w_br_l, w_out_l).reshape(b, s, d)
        (kv,) = _norm_proj(mem.reshape(b * m, d), norm_mem_g[l], w_xkv_l, (2 * XATTN_D,))
        x = _mem_cross_attention(x, kv.reshape(b, m, 2 * XATTN_D), norm_xattn_g[l], w_xq_l, w_xo_l)
        x = _conv_ffn(x, norm_ffn_g[l], w_up_l, ffn_conv_w[l], ffn_conv_b[l], w_down_l, norm_final_g,
                      final_norm=(l == DEPTH - 1))
        if l + 1 < DEPTH:
            w_in_l = casts[7]
    return x
```
